```python
import jax, jax.numpy as jnp
from jax import lax
import numpy as np

D_MODEL = 1024
BATCH = 16
SEQ = 2048
DEPTH = 1

D_MIX = D_MODEL
GLA_WIDTH = D_MIX // 2
DIFF_WIDTH = D_MIX - GLA_WIDTH
GLA_HEADS = 4
GLA_DV = GLA_WIDTH // GLA_HEADS
GLA_DK = GLA_DV // 2
GLA_RANK = 16
GLA_GATE_NORM = 16.0
GLA_CHUNK = 64
DIFF_HEADS = 4
DIFF_DV = DIFF_WIDTH // DIFF_HEADS
DIFF_DH = DIFF_DV // 2
Q_BLOCK = 128
EPS = 1e-6

SPLIT_SIZES = (
    GLA_HEADS * GLA_DK,
    GLA_HEADS * GLA_DK,
    GLA_WIDTH,
    GLA_WIDTH,
    GLA_RANK,
    DIFF_WIDTH,
    DIFF_WIDTH,
    DIFF_WIDTH,
    DIFF_WIDTH,
)
D_IN = sum(SPLIT_SIZES)

kernel_name = "hybrid_gla_diffattn_alibi_adaln"


def _rmsnorm(x, gain):
    xf = x.astype(jnp.float32)
    y = xf * lax.rsqrt(jnp.mean(xf * xf, axis=-1, keepdims=True) + EPS)
    return (y * gain.astype(jnp.float32)).astype(x.dtype)


def _gla_chunked(q, k, v, log_a):
    dtype = v.dtype
    B, S, H, DK = q.shape
    DV = v.shape[-1]
    N = S // GLA_CHUNK

    def chunks(t):
        return t.astype(jnp.float32).reshape(B, N, GLA_CHUNK, H, t.shape[-1]).transpose(1, 0, 3, 2, 4)

    q = chunks(q) * (DK ** -0.5)
    k, v, log_a = chunks(k), chunks(v), chunks(log_a)
    b = jnp.cumsum(log_a, axis=3)
    b_last = b[:, :, :, -1:, :]
    q_in = q * jnp.exp(b)
    k_in = k * jnp.exp(-b)
    k_st = k * jnp.exp(b_last - b)
    causal = jnp.tril(jnp.ones((GLA_CHUNK, GLA_CHUNK), dtype=bool))
    scores = jnp.einsum('nbhid,nbhjd->nbhij', q_in, k_in)
    intra = jnp.einsum('nbhij,nbhjv->nbhiv', jnp.where(causal, scores, 0.0), v)

    def step(state, inp):
        q_c, k_c, v_c, dec_c = inp
        out = jnp.einsum('bhik,bhkv->bhiv', q_c, state)
        state = dec_c[:, :, 0, :, None] * state + jnp.einsum('bhjk,bhjv->bhkv', k_c, v_c)
        return state, out

    state0 = jnp.zeros((B, H, DK, DV), jnp.float32)
    _, inter = lax.scan(step, state0, (q_in, k_st, v, jnp.exp(b_last)))
    o = (intra + inter).transpose(1, 0, 3, 2, 4).reshape(B, S, H, DV)
    return o.astype(dtype)


def _diff_attention(q, k, v, lam):
    B, S, H, _, DH = q.shape
    DV = v.shape[-1]
    nb = S // Q_BLOCK
    scale = DH ** -0.5
    slopes = 2.0 ** (-8.0 * jnp.arange(1, H + 1, dtype=jnp.float32) / H)
    kf = k.transpose(0, 2, 3, 1, 4)
    vf = v.transpose(0, 2, 1, 3)
    qb = q.reshape(B, nb, Q_BLOCK, H, 2, DH).transpose(1, 0, 3, 4, 2, 5)
    key_pos = jnp.arange(S)

    def block(args):
        q_blk, blk = args
        q_pos = blk * Q_BLOCK + jnp.arange(Q_BLOCK)
        dist = (q_pos[:, None] - key_pos[None, :]).astype(jnp.float32)
        bias = -slopes[:, None, None] * dist
        s = jnp.einsum('bhiqd,bhikd->bhiqk', q_blk, kf).astype(jnp.float32) * scale + bias[None, :, None]
        s = jnp.where(dist >= 0, s, -jnp.inf)
        p = jax.nn.softmax(s, axis=-1)
        w = p[:, :, 0] - lam * p[:, :, 1]
        return jnp.einsum('bhqk,bhkv->bhqv', w.astype(v.dtype), vf)

    out = lax.map(block, (qb, jnp.arange(nb)))
    return out.transpose(1, 0, 3, 2, 4).reshape(B, S, H, DV)


def setup_inputs(seed: int = 0) -> dict:
    key = jax.random.key(seed)
    ks = jax.random.split(key, 18)

    def nrm(k, shape, s):
        return jax.random.normal(k, shape, jnp.float32) * s

    return {
        "x": nrm(ks[0], (BATCH, SEQ, D_MODEL), 1.0),
        "c": nrm(ks[1], (BATCH, D_MODEL), 1.0),
        "w_ada": nrm(ks[2], (DEPTH, D_MODEL, 3 * D_MODEL), 0.5 * D_MODEL ** -0.5),
        "b_ada": nrm(ks[3], (DEPTH, 3 * D_MODEL), 0.02),
        "norm_gain": 1.0 + nrm(ks[4], (DEPTH, D_MODEL), 0.01),
        "w_in": nrm(ks[5], (DEPTH, D_MODEL, D_IN), D_MODEL ** -0.5),
        "w_gla_gate_up": nrm(ks[6], (DEPTH, GLA_RANK, GLA_HEADS * GLA_DK), GLA_RANK ** -0.5),
        "b_gla_gate": nrm(ks[7], (DEPTH, GLA_HEADS * GLA_DK), 0.1),
        "gla_out_gain": 1.0 + nrm(ks[8], (DEPTH, GLA_WIDTH), 0.01),
        "lambda_q1": nrm(ks[9], (DEPTH, DIFF_DH), 0.1),
        "lambda_k1": nrm(ks[10], (DEPTH, DIFF_DH), 0.1),
        "lambda_q2": nrm(ks[11], (DEPTH, DIFF_DH), 0.1),
        "lambda_k2": nrm(ks[12], (DEPTH, DIFF_DH), 0.1),
        "diff_out_gain": 1.0 + nrm(ks[13], (DEPTH, DIFF_WIDTH), 0.01),
        "w_out": nrm(ks[14], (DEPTH, D_MIX, D_MODEL), D_MIX ** -0.5),
        "final_gain": 1.0 + nrm(ks[15], (D_MODEL,), 0.01),
    }


def reference(x, c, w_ada, b_ada, norm_gain, w_in, w_gla_gate_up, b_gla_gate, gla_out_gain,
              lambda_q1, lambda_k1, lambda_q2, lambda_k2, diff_out_gain, w_out, final_gain):
    B, S, _ = x.shape
    split_points = [int(p) for p in np.cumsum(SPLIT_SIZES)[:-1]]
    for l in range(DEPTH):
        mod = jax.nn.silu(c) @ w_ada[l] + b_ada[l]
        shift, scale, gate = jnp.split(mod, 3, axis=-1)
        h = _rmsnorm(x, norm_gain[l]) * (1.0 + scale[:, None, :]) + shift[:, None, :]

        proj = h @ w_in[l]
        gq, gk, gv, gz, gr, dq, dk, dv, dz = jnp.split(proj, split_points, axis=-1)

        gate_logit = (gr @ w_gla_gate_up[l] + b_gla_gate[l]).astype(jnp.float32)
        log_a = jax.nn.log_sigmoid(gate_logit) / GLA_GATE_NORM
        o_gla = _gla_chunked(gq.reshape(B, S, GLA_HEADS, GLA_DK),
                             gk.reshape(B, S, GLA_HEADS, GLA_DK),
                             gv.reshape(B, S, GLA_HEADS, GLA_DV),
                             log_a.reshape(B, S, GLA_HEADS, GLA_DK))
        o_gla = _rmsnorm(o_gla, gla_out_gain[l].reshape(GLA_HEADS, GLA_DV)).reshape(B, S, GLA_WIDTH)
        o_gla = o_gla * jax.nn.silu(gz)

        lam_init = 0.8 - 0.6 * np.exp(-0.3 * l)
        lam = (jnp.exp(jnp.sum(lambda_q1[l] * lambda_k1[l])) - jnp.exp(jnp.sum(lambda_q2[l] * lambda_k2[l]))
               + lam_init).astype(jnp.float32)
        o_diff = _diff_attention(dq.reshape(B, S, DIFF_HEADS, 2, DIFF_DH),
                                 dk.reshape(B, S, DIFF_HEADS, 2, DIFF_DH),
                                 dv.reshape(B, S, DIFF_HEADS, DIFF_DV), lam)
        o_diff = _rmsnorm(o_diff, diff_out_gain[l].reshape(DIFF_HEADS, DIFF_DV)) * (1.0 - lam_init)
        o_diff = o_diff.reshape(B, S, DIFF_WIDTH) * jax.nn.silu(dz)

        mix = jnp.concatenate([o_gla, o_diff], axis=-1)
        x = x + gate[:, None, :] * (mix @ w_out[l])
    return _rmsnorm(x, final_gain)
```

```python
import functools

import jax
import jax.numpy as jnp
import numpy as np
from jax import lax
from jax.experimental import pallas as pl
from jax.experimental.pallas import tpu as pltpu

F32 = jnp.float32
BF16 = jnp.bfloat16

EPS = 1e-6
GLA_HEADS = 4
GLA_DK = 64
GLA_DV = 128
GLA_RANK = 16
GLA_GATE_NORM = 16.0
GLA_CHUNK = 64
DIFF_HEADS = 4
DIFF_DH = 64
DIFF_DV = 128
LANES = 128
RANK_PAD = LANES

ROWS_IN_PROJ = 512
ROWS_GLA = 256
ATTN_TQ = 256
ATTN_TK = 256
ROWS_OUT_PROJ = 512
VMEM_LIMIT = 48 * 1024 * 1024


def _silu(v):
    return v / (1.0 + jnp.exp(-v))


def _log_sigmoid(v):
    return jnp.minimum(v, 0.0) - jnp.log(1.0 + jnp.exp(-jnp.abs(v)))


def _adaln_kernel(c_ref, w_ref, b_ref, o_ref):
    sc = _silu(c_ref[...]).astype(BF16)
    o_ref[...] = jnp.dot(sc, w_ref[...].astype(BF16), preferred_element_type=F32) + b_ref[...]


def _adaln_mod(c, w_ada, b_ada):
    bsz, d = c.shape
    n = w_ada.shape[1]
    tn = 1024
    return pl.pallas_call(
        _adaln_kernel,
        grid=(n // tn,),
        in_specs=[
            pl.BlockSpec((bsz, d), lambda j: (0, 0)),
            pl.BlockSpec((d, tn), lambda j: (0, j)),
            pl.BlockSpec((1, tn), lambda j: (0, j)),
        ],
        out_specs=pl.BlockSpec((bsz, tn), lambda j: (0, j)),
        out_shape=jax.ShapeDtypeStruct((bsz, n), F32),
        compiler_params=pltpu.CompilerParams(dimension_semantics=("arbitrary",)),
        name="adaln_mod",
    )(c, w_ada, b_ada.reshape(1, n))


def _in_proj_kernel(x_ref, shift_ref, scale_ref, gain_ref, w_ref, wup_ref, bg_ref,
                    gqk_ref, gv_ref, gz_ref, la_ref, dq_ref, dk_ref, dv_ref, dz_ref):
    x = x_ref[0]
    rstd = lax.rsqrt(jnp.mean(x * x, axis=-1, keepdims=True) + EPS)
    g = gain_ref[...] * (1.0 + scale_ref[0])
    h = (x * rstd * g + shift_ref[0]).astype(BF16)

    outs = (gqk_ref, gv_ref, gz_ref, dq_ref, dk_ref, dv_ref, dz_ref)
    for i, o_ref in enumerate(outs):
        w = w_ref[:, i * 512:(i + 1) * 512]
        o_ref[0] = jnp.dot(h, w, preferred_element_type=F32).astype(o_ref.dtype)

    gr = jnp.dot(h, w_ref[:, 7 * 512:7 * 512 + RANK_PAD], preferred_element_type=F32)
    logit = jnp.dot(gr, wup_ref[...], preferred_element_type=F32,
                    precision=lax.Precision.HIGHEST) + bg_ref[...]
    la_ref[0] = _log_sigmoid(logit) * (1.0 / GLA_GATE_NORM)


def _in_proj(x, mod3, norm_gain, w_in_r, wup_pad, b_gate):
    bsz, s, d = x.shape
    tm = ROWS_IN_PROJ
    ncol = w_in_r.shape[1]
    hk = GLA_HEADS * GLA_DK
    act = lambda width: pl.BlockSpec((1, tm, width), lambda b, t: (b, t, 0))
    out_shapes = (
        jax.ShapeDtypeStruct((bsz, s, 512), BF16),
        jax.ShapeDtypeStruct((bsz, s, 512), BF16),
        jax.ShapeDtypeStruct((bsz, s, 512), BF16),
        jax.ShapeDtypeStruct((bsz, s, hk), F32),
        jax.ShapeDtypeStruct((bsz, s, 512), BF16),
        jax.ShapeDtypeStruct((bsz, s, 512), BF16),
        jax.ShapeDtypeStruct((bsz, s, 512), BF16),
        jax.ShapeDtypeStruct((bsz, s, 512), BF16),
    )
    return pl.pallas_call(
        _in_proj_kernel,
        grid=(bsz, s // tm),
        in_specs=[
            act(d),
            pl.BlockSpec((1, 1, d), lambda b, t: (b, 0, 0)),
            pl.BlockSpec((1, 1, d), lambda b, t: (b, 0, 1)),
            pl.BlockSpec((1, d), lambda b, t: (0, 0)),
            pl.BlockSpec((d, ncol), lambda b, t: (0, 0)),
            pl.BlockSpec((RANK_PAD, hk), lambda b, t: (0, 0)),
            pl.BlockSpec((1, hk), lambda b, t: (0, 0)),
        ],
        out_specs=(act(512), act(512), act(512), act(hk), act(512), act(512), act(512), act(512)),
        out_shape=out_shapes,
        compiler_params=pltpu.CompilerParams(
            dimension_semantics=("arbitrary", "arbitrary"), vmem_limit_bytes=VMEM_LIMIT),
        name="in_proj",
    )(x, mod3, mod3, norm_gain, w_in_r, wup_pad, b_gate)


def _gla_kernel(qk_ref, v_ref, z_ref, la_ref, gain_ref, o_ref, state_ref):
    tg = qk_ref.shape[1]
    hk = GLA_HEADS * GLA_DK
    c_sz = GLA_CHUNK

    @pl.when(pl.program_id(1) == 0)
    def _():
        state_ref[...] = jnp.zeros_like(state_ref)

    r = lax.broadcasted_iota(jnp.int32, (tg, tg), 0)
    c = lax.broadcasted_iota(jnp.int32, (tg, tg), 1)
    tril = jnp.where((r // c_sz == c // c_sz) & (c <= r), 1.0, 0.0).astype(F32)
    b_all = jnp.dot(tril, la_ref[0], preferred_element_type=F32, precision=lax.Precision.HIGHEST)

    lane_head = lax.broadcasted_iota(jnp.int32, (1, hk), 1) // GLA_DK
    ri = lax.broadcasted_iota(jnp.int32, (c_sz, hk), 0)
    ci = lax.broadcasted_iota(jnp.int32, (c_sz, hk), 1) % c_sz
    causal = ci <= ri
    zero_v = jnp.zeros((c_sz, GLA_DV), BF16)
    zero_s = jnp.zeros((GLA_DK, GLA_DV), BF16)
    gain = gain_ref[...]

    for ch in range(tg // c_sz):
        rows = slice(ch * c_sz, (ch + 1) * c_sz)
        b = b_all[rows]
        b_last = b[c_sz - 1:c_sz]
        q = qk_ref[0, rows, 0:hk].astype(F32)
        k = qk_ref[0, rows, hk:2 * hk].astype(F32)
        v = v_ref[0, rows, :]
        q_in = (q * jnp.exp(b)).astype(BF16)
        k_in = (k * jnp.exp(-b)).astype(BF16)
        k_st = k * jnp.exp(b_last - b)

        k_bd = jnp.concatenate(
            [jnp.where(lane_head == hh, k_in, jnp.zeros_like(k_in)) for hh in range(GLA_HEADS)], axis=0)
        scores = lax.dot_general(q_in, k_bd, (((1,), (1,)), ((), ())), preferred_element_type=F32)
        p = jnp.where(causal, scores, 0.0).astype(BF16)

        v_bd = jnp.concatenate(
            [jnp.concatenate([v[:, hh * GLA_DV:(hh + 1) * GLA_DV] if j == hh else zero_v
                              for j in range(GLA_HEADS)], axis=1) for hh in range(GLA_HEADS)], axis=0)
        s_bd = jnp.concatenate(
            [jnp.concatenate([state_ref[hh].astype(BF16) if j == hh else zero_s
                              for j in range(GLA_HEADS)], axis=1) for hh in range(GLA_HEADS)], axis=0)
        o = (jnp.dot(p, v_bd, preferred_element_type=F32)
             + jnp.dot(q_in, s_bd, preferred_element_type=F32))

        kst_t = k_st.T.astype(BF16)
        dec_col = jnp.exp(b.T[:, c_sz - 1:c_sz])
        for hh in range(GLA_HEADS):
            u = jnp.dot(kst_t[hh * GLA_DK:(hh + 1) * GLA_DK], v[:, hh * GLA_DV:(hh + 1) * GLA_DV],
                        preferred_element_type=F32)
            state_ref[hh] = dec_col[hh * GLA_DK:(hh + 1) * GLA_DK] * state_ref[hh] + u

        z = z_ref[0, rows, :].astype(F32)
        ys = []
        for hh in range(GLA_HEADS):
            oh = o[:, hh * GLA_DV:(hh + 1) * GLA_DV]
            ys.append(oh * lax.rsqrt(jnp.mean(oh * oh, axis=-1, keepdims=True) + EPS))
        y = jnp.concatenate(ys, axis=1) * gain * _silu(z)
        o_ref[0, rows, :] = y.astype(o_ref.dtype)


def _gla(gqk, gv, gz, la, gla_out_gain):
    bsz, s, _ = gqk.shape
    tg = ROWS_GLA
    hk = GLA_HEADS * GLA_DK
    width = GLA_HEADS * GLA_DV
    act = lambda w: pl.BlockSpec((1, tg, w), lambda b, t: (b, t, 0))
    return pl.pallas_call(
        _gla_kernel,
        grid=(bsz, s // tg),
        in_specs=[act(2 * hk), act(width), act(width), act(hk),
                  pl.BlockSpec((1, width), lambda b, t: (0, 0))],
        out_specs=act(width),
        out_shape=jax.ShapeDtypeStruct((bsz, s, width), BF16),
        scratch_shapes=[pltpu.VMEM((GLA_HEADS, GLA_DK, GLA_DV), F32)],
        compiler_params=pltpu.CompilerParams(
            dimension_semantics=("arbitrary", "arbitrary"), vmem_limit_bytes=VMEM_LIMIT),
        name="gla",
    )(gqk, gv, gz, la, gla_out_gain)


def _diff_lambda(lq1_ref, lk1_ref, lq2_ref, lk2_ref, lam_init):
    a = jnp.sum(lq1_ref[...] * lk1_ref[...], axis=-1, keepdims=True)
    b = jnp.sum(lq2_ref[...] * lk2_ref[...], axis=-1, keepdims=True)
    return jnp.exp(a) - jnp.exp(b) + lam_init


def _attn_kernel(q_ref, k_ref, v_ref, z_ref, gain_ref, lq1_ref, lk1_ref, lq2_ref, lk2_ref,
                 o_ref, m_ref, l_ref, acc_ref, *, lam_init):
    tq = q_ref.shape[1]
    tk = ATTN_TK
    h = pl.program_id(1)
    qi = pl.program_id(2)
    slope = jnp.exp2(-2.0 * (h + 1).astype(F32))

    q = q_ref[0]
    first_half = lax.broadcasted_iota(jnp.int32, (1, 2 * DIFF_DH), 1) < DIFF_DH
    zq = jnp.zeros_like(q)
    qs = jnp.concatenate([jnp.where(first_half, q, zq), jnp.where(first_half, zq, q)], axis=0)

    row = lax.broadcasted_iota(jnp.int32, (2 * tq, tk), 0) % tq
    col = lax.broadcasted_iota(jnp.int32, (2 * tq, tk), 1)
    rel = (row - col).astype(F32)
    bias_rel = -slope * rel

    m_ref[...] = jnp.full_like(m_ref, -jnp.inf)
    l_ref[...] = jnp.zeros_like(l_ref)
    acc_ref[...] = jnp.zeros_like(acc_ref)

    def step(kj, masked):
        start = pl.multiple_of(kj * tk, tk)
        kblk = k_ref[0, pl.ds(start, tk), :]
        vblk = v_ref[0, pl.ds(start, tk), :]
        s = lax.dot_general(qs, kblk, (((1,), (1,)), ((), ())), preferred_element_type=F32)
        s = s + bias_rel
        if masked:
            s = jnp.where(rel >= 0.0, s, -jnp.inf)
        off = -slope * (qi * tq - kj * tk).astype(F32)
        m_old = m_ref[...]
        m_new = jnp.maximum(m_old, jnp.max(s, axis=-1, keepdims=True) + off)
        p = jnp.exp(s - (m_new - off))
        alpha = jnp.exp(m_old - m_new)
        l_ref[...] = alpha * l_ref[...] + jnp.sum(p, axis=-1, keepdims=True)
        acc_ref[...] = alpha * acc_ref[...] + jnp.dot(p.astype(BF16), vblk, preferred_element_type=F32)
        m_ref[...] = m_new

    def body(kj, carry):
        step(kj, masked=False)
        return carry

    lax.fori_loop(0, qi, body, 0)
    step(qi, masked=True)

    lam = _diff_lambda(lq1_ref, lk1_ref, lq2_ref, lk2_ref, lam_init)
    o_all = acc_ref[...] / l_ref[...]
    o = o_all[:tq] - lam * o_all[tq:]
    y = o * lax.rsqrt(jnp.mean(o * o, axis=-1, keepdims=True) + EPS)
    y = y * gain_ref[0] * (1.0 - lam_init) * _silu(z_ref[0].astype(F32))
    o_ref[0] = y.astype(o_ref.dtype)


def _diff_attn(dq, dk, dv, dz, gain, lq1, lk1, lq2, lk2, lam_init):
    bsz, s, width = dq.shape
    tq = ATTN_TQ
    assert ATTN_TQ == ATTN_TK
    hw = 2 * DIFF_DH
    qspec = pl.BlockSpec((1, tq, hw), lambda b, h, i: (b, i, h))
    kvspec = pl.BlockSpec((1, s, hw), lambda b, h, i: (b, 0, h))
    lspec = pl.BlockSpec((1, DIFF_DH), lambda b, h, i: (0, 0))
    return pl.pallas_call(
        functools.partial(_attn_kernel, lam_init=lam_init),
        grid=(bsz, DIFF_HEADS, s // tq),
        in_specs=[qspec, kvspec, kvspec, qspec,
                  pl.BlockSpec((1, 1, DIFF_DV), lambda b, h, i: (h, 0, 0)),
                  lspec, lspec, lspec, lspec],
        out_specs=qspec,
        out_shape=jax.ShapeDtypeStruct((bsz, s, width), BF16),
        scratch_shapes=[pltpu.VMEM((2 * tq, 1), F32), pltpu.VMEM((2 * tq, 1), F32),
                        pltpu.VMEM((2 * tq, DIFF_DV), F32)],
        compiler_params=pltpu.CompilerParams(
            dimension_semantics=("arbitrary", "arbitrary", "arbitrary"), vmem_limit_bytes=VMEM_LIMIT),
        name="diff_attn",
    )(dq, dk, dv, dz, gain.reshape(DIFF_HEADS, 1, DIFF_DV), lq1, lk1, lq2, lk2)


def _out_proj_kernel(og_ref, od_ref, x_ref, gate_ref, w_ref, fg_ref, o_ref):
    half = og_ref.shape[2]
    mixw = (jnp.dot(og_ref[0], w_ref[0:half, :], preferred_element_type=F32)
            + jnp.dot(od_ref[0], w_ref[half:2 * half, :], preferred_element_type=F32))
    xn = x_ref[0] + gate_ref[0] * mixw
    y = xn * lax.rsqrt(jnp.mean(xn * xn, axis=-1, keepdims=True) + EPS)
    o_ref[0] = y * fg_ref[...]


def _out_proj(o_gla, o_diff, x, mod3, w_out_bf, final_gain):
    bsz, s, d = x.shape
    tm = ROWS_OUT_PROJ
    half = o_gla.shape[2]
    return pl.pallas_call(
        _out_proj_kernel,
        grid=(bsz, s // tm),
        in_specs=[
            pl.BlockSpec((1, tm, half), lambda b, t: (b, t, 0)),
            pl.BlockSpec((1, tm, half), lambda b, t: (b, t, 0)),
            pl.BlockSpec((1, tm, d), lambda b, t: (b, t, 0)),
            pl.BlockSpec((1, 1, d), lambda b, t: (b, 0, 2)),
            pl.BlockSpec((2 * half, d), lambda b, t: (0, 0)),
            pl.BlockSpec((1, d), lambda b, t: (0, 0)),
        ],
        out_specs=pl.BlockSpec((1, tm, d), lambda b, t: (b, t, 0)),
        out_shape=jax.ShapeDtypeStruct((bsz, s, d), F32),
        compiler_params=pltpu.CompilerParams(
            dimension_semantics=("arbitrary", "arbitrary"), vmem_limit_bytes=VMEM_LIMIT),
        name="out_proj",
    )(o_gla, o_diff, x, mod3, w_out_bf, final_gain)


def _prep_w_in(w_in_l):
    hk = GLA_HEADS * GLA_DK
    sizes = (hk, hk, 512, 512, GLA_RANK, 512, 512, 512, 512)
    offs = np.concatenate([[0], np.cumsum(sizes)])
    gq, gk, gv, gz, gr, dq, dk, dv, dz = [w_in_l[:, offs[i]:offs[i + 1]] for i in range(9)]
    gr_pad = jnp.pad(gr, ((0, 0), (0, RANK_PAD - GLA_RANK)))
    cols = [gq * (GLA_DK ** -0.5), gk, gv, gz, dq * (DIFF_DH ** -0.5), dk, dv, dz, gr_pad]
    return jnp.concatenate(cols, axis=1).astype(BF16)


def kernel(x, c, w_ada, b_ada, norm_gain, w_in, w_gla_gate_up, b_gla_gate, gla_out_gain,
           lambda_q1, lambda_k1, lambda_q2, lambda_k2, diff_out_gain, w_out, final_gain):
    bsz, s, d = x.shape
    depth = w_in.shape[0]
    assert depth == 1, "out_proj applies the final rmsnorm, so exactly one layer is supported"
    for l in range(depth):
        mod = _adaln_mod(c, w_ada[l], b_ada[l])
        mod3 = mod.reshape(bsz, 1, 3 * d)
        w_in_r = _prep_w_in(w_in[l])
        wup_pad = jnp.pad(w_gla_gate_up[l], ((0, RANK_PAD - GLA_RANK), (0, 0)))
        gqk, gv, gz, la, dq, dk, dv, dz = _in_proj(
            x, mod3, norm_gain[l].reshape(1, d), w_in_r, wup_pad, b_gla_gate[l].reshape(1, -1))
        o_gla = _gla(gqk, gv, gz, la, gla_out_gain[l].reshape(1, -1))
        lam_init = float(0.8 - 0.6 * np.exp(-0.3 * l))
        o_diff = _diff_attn(dq, dk, dv, dz, diff_out_gain[l],
                            lambda_q1[l].reshape(1, -1), lambda_k1[l].reshape(1, -1),
                            lambda_q2[l].reshape(1, -1), lambda_k2[l].reshape(1, -1), lam_init)
        x = _out_proj(o_gla, o_diff, x, mod3, w_out[l].astype(BF16), final_gain.reshape(1, d))
    return x
```

```python
import functools
import math

import jax
import jax.numpy as jnp
import numpy as np
from jax import lax
from jax.experimental import pallas as pl
from jax.experimental.pallas import tpu as pltpu

F32 = jnp.float32
BF16 = jnp.bfloat16

EPS = 1e-6
LOG2E = math.log2(math.e)
GLA_HEADS = 4
GLA_DK = 64
GLA_DV = 128
GLA_RANK = 16
GLA_GATE_NORM = 16.0
GLA_CHUNK = 64
DIFF_HEADS = 4
DIFF_DH = 64
DIFF_DV = 128
LANES = 128
RANK_PAD = LANES
SECTION = 512

ROWS_IN_PROJ = 512
ROWS_GLA = 256
ATTN_TQ = 256
ATTN_TK = 256
ROWS_OUT_PROJ = 512
VMEM_LIMIT = 48 * 1024 * 1024


def _silu(v):
    return v / (1.0 + jnp.exp(-v))


def _log_sigmoid(v):
    return jnp.minimum(v, 0.0) - jnp.log(1.0 + jnp.exp(-jnp.abs(v)))


def _adaln_kernel(c_ref, w_ref, b_ref, o_ref):
    sc = _silu(c_ref[...]).astype(BF16)
    o_ref[...] = jnp.dot(sc, w_ref[...].astype(BF16), preferred_element_type=F32) + b_ref[...]


def _adaln_mod(c, w_ada, b_ada):
    bsz, d = c.shape
    n = w_ada.shape[1]
    tn = 1024
    return pl.pallas_call(
        _adaln_kernel,
        grid=(n // tn,),
        in_specs=[
            pl.BlockSpec((bsz, d), lambda j: (0, 0)),
            pl.BlockSpec((d, tn), lambda j: (0, j)),
            pl.BlockSpec((1, tn), lambda j: (0, j)),
        ],
        out_specs=pl.BlockSpec((bsz, tn), lambda j: (0, j)),
        out_shape=jax.ShapeDtypeStruct((bsz, n), F32),
        compiler_params=pltpu.CompilerParams(dimension_semantics=("arbitrary",)),
        name="adaln_mod",
    )(c, w_ada, b_ada.reshape(1, n))


def _in_proj_kernel(x_ref, shift_ref, scale_ref, gain_ref, w_ref, wvt_ref, wup_ref, bg_ref,
                    gqk_ref, gv_ref, gz_ref, dq_ref, dk_ref, dz_ref, dvt_ref, la_ref):
    x = x_ref[0]
    rstd = lax.rsqrt(jnp.mean(x * x, axis=-1, keepdims=True) + EPS)
    g = gain_ref[...] * (1.0 + scale_ref[0])
    h = (x * rstd * g + shift_ref[0]).astype(BF16)

    outs = (gqk_ref, gv_ref, gz_ref, dq_ref, dk_ref, dz_ref)
    for i, o_ref in enumerate(outs):
        w = w_ref[:, i * SECTION:(i + 1) * SECTION]
        o_ref[0] = jnp.dot(h, w, preferred_element_type=F32).astype(o_ref.dtype)

    dvt_ref[0] = lax.dot_general(wvt_ref[...], h, (((1,), (1,)), ((), ())),
                                 preferred_element_type=F32).astype(dvt_ref.dtype)

    gr = jnp.dot(h, w_ref[:, len(outs) * SECTION:len(outs) * SECTION + RANK_PAD],
                 preferred_element_type=F32)
    logit = jnp.dot(gr, wup_ref[...], preferred_element_type=F32,
                    precision=lax.Precision.HIGHEST) + bg_ref[...]
    la_ref[0] = _log_sigmoid(logit) * (1.0 / GLA_GATE_NORM)


def _in_proj(x, mod3, norm_gain, w_in_r, w_dvt, wup_pad, b_gate):
    bsz, s, d = x.shape
    tm = ROWS_IN_PROJ
    ncol = w_in_r.shape[1]
    hk = GLA_HEADS * GLA_DK
    act = lambda width: pl.BlockSpec((1, tm, width), lambda b, t: (b, t, 0))
    sec = jax.ShapeDtypeStruct((bsz, s, SECTION), BF16)
    out_shapes = (
        sec,
        sec,
        sec,
        sec,
        sec,
        sec,
        jax.ShapeDtypeStruct((bsz, SECTION, s), BF16),
        jax.ShapeDtypeStruct((bsz, s, hk), F32),
    )
    return pl.pallas_call(
        _in_proj_kernel,
        grid=(bsz, s // tm),
        in_specs=[
            act(d),
            pl.BlockSpec((1, 1, d), lambda b, t: (b, 0, 0)),
            pl.BlockSpec((1, 1, d), lambda b, t: (b, 0, 1)),
            pl.BlockSpec((1, d), lambda b, t: (0, 0)),
            pl.BlockSpec((d, ncol), lambda b, t: (0, 0)),
            pl.BlockSpec((SECTION, d), lambda b, t: (0, 0)),
            pl.BlockSpec((RANK_PAD, hk), lambda b, t: (0, 0)),
            pl.BlockSpec((1, hk), lambda b, t: (0, 0)),
        ],
        out_specs=(act(SECTION), act(SECTION), act(SECTION), act(SECTION), act(SECTION), act(SECTION),
                   pl.BlockSpec((1, SECTION, tm), lambda b, t: (b, 0, t)), act(hk)),
        out_shape=out_shapes,
        compiler_params=pltpu.CompilerParams(
            dimension_semantics=("arbitrary", "arbitrary"), vmem_limit_bytes=VMEM_LIMIT),
        name="in_proj",
    )(x, mod3, mod3, norm_gain, w_in_r, w_dvt, wup_pad, b_gate)


def _gla_kernel(qk_ref, v_ref, z_ref, la_ref, gain_ref, o_ref, state_ref):
    tg = qk_ref.shape[1]
    hk = GLA_HEADS * GLA_DK
    c_sz = GLA_CHUNK

    @pl.when(pl.program_id(1) == 0)
    def _():
        state_ref[...] = jnp.zeros_like(state_ref)

    r = lax.broadcasted_iota(jnp.int32, (tg, tg), 0)
    c = lax.broadcasted_iota(jnp.int32, (tg, tg), 1)
    tril = jnp.where((r // c_sz == c // c_sz) & (c <= r), 1.0, 0.0).astype(F32)
    b_all = jnp.dot(tril, la_ref[0], preferred_element_type=F32, precision=lax.Precision.HIGHEST)

    lane_head = lax.broadcasted_iota(jnp.int32, (1, hk), 1) // GLA_DK
    ri = lax.broadcasted_iota(jnp.int32, (c_sz, hk), 0)
    ci = lax.broadcasted_iota(jnp.int32, (c_sz, hk), 1) % c_sz
    causal = ci <= ri
    zero_v = jnp.zeros((c_sz, GLA_DV), BF16)
    zero_s = jnp.zeros((GLA_DK, GLA_DV), BF16)
    gain = gain_ref[...]

    for ch in range(tg // c_sz):
        rows = slice(ch * c_sz, (ch + 1) * c_sz)
        b = b_all[rows]
        b_last = b[c_sz - 1:c_sz]
        q = qk_ref[0, rows, 0:hk].astype(F32)
        k = qk_ref[0, rows, hk:2 * hk].astype(F32)
        v = v_ref[0, rows, :]
        q_in = (q * jnp.exp(b)).astype(BF16)
        k_in = (k * jnp.exp(-b)).astype(BF16)
        k_st = k * jnp.exp(b_last - b)

        k_bd = jnp.concatenate(
            [jnp.where(lane_head == hh, k_in, jnp.zeros_like(k_in)) for hh in range(GLA_HEADS)], axis=0)
        scores = lax.dot_general(q_in, k_bd, (((1,), (1,)), ((), ())), preferred_element_type=F32)
        p = jnp.where(causal, scores, 0.0).astype(BF16)

        v_bd = jnp.concatenate(
            [jnp.concatenate([v[:, hh * GLA_DV:(hh + 1) * GLA_DV] if j == hh else zero_v
                              for j in range(GLA_HEADS)], axis=1) for hh in range(GLA_HEADS)], axis=0)
        s_bd = jnp.concatenate(
            [jnp.concatenate([state_ref[hh].astype(BF16) if j == hh else zero_s
                              for j in range(GLA_HEADS)], axis=1) for hh in range(GLA_HEADS)], axis=0)
        o = (jnp.dot(p, v_bd, preferred_element_type=F32)
             + jnp.dot(q_in, s_bd, preferred_element_type=F32))

        kst_t = k_st.T.astype(BF16)
        dec_col = jnp.exp(b.T[:, c_sz - 1:c_sz])
        for hh in range(GLA_HEADS):
            u = jnp.dot(kst_t[hh * GLA_DK:(hh + 1) * GLA_DK], v[:, hh * GLA_DV:(hh + 1) * GLA_DV],
                        preferred_element_type=F32)
            state_ref[hh] = dec_col[hh * GLA_DK:(hh + 1) * GLA_DK] * state_ref[hh] + u

        z = z_ref[0, rows, :].astype(F32)
        ys = []
        for hh in range(GLA_HEADS):
            oh = o[:, hh * GLA_DV:(hh + 1) * GLA_DV]
            ys.append(oh * lax.rsqrt(jnp.mean(oh * oh, axis=-1, keepdims=True) + EPS))
        y = jnp.concatenate(ys, axis=1) * gain * _silu(z)
        o_ref[0, rows, :] = y.astype(o_ref.dtype)


def _gla(gqk, gv, gz, la, gla_out_gain):
    bsz, s, _ = gqk.shape
    tg = ROWS_GLA
    hk = GLA_HEADS * GLA_DK
    width = GLA_HEADS * GLA_DV
    act = lambda w: pl.BlockSpec((1, tg, w), lambda b, t: (b, t, 0))
    return pl.pallas_call(
        _gla_kernel,
        grid=(bsz, s // tg),
        in_specs=[act(2 * hk), act(width), act(width), act(hk),
                  pl.BlockSpec((1, width), lambda b, t: (0, 0))],
        out_specs=act(width),
        out_shape=jax.ShapeDtypeStruct((bsz, s, width), BF16),
        scratch_shapes=[pltpu.VMEM((GLA_HEADS, GLA_DK, GLA_DV), F32)],
        compiler_params=pltpu.CompilerParams(
            dimension_semantics=("arbitrary", "arbitrary"), vmem_limit_bytes=VMEM_LIMIT),
        name="gla",
    )(gqk, gv, gz, la, gla_out_gain)


def _diff_lambda(lq1_ref, lk1_ref, lq2_ref, lk2_ref, lam_init):
    a = jnp.sum(lq1_ref[...] * lk1_ref[...], axis=-1, keepdims=True)
    b = jnp.sum(lq2_ref[...] * lk2_ref[...], axis=-1, keepdims=True)
    return jnp.exp(a) - jnp.exp(b) + lam_init


def _attn_kernel(q_ref, k_ref, vt_ref, z_ref, gain_ref, lq1_ref, lk1_ref, lq2_ref, lk2_ref,
                 o_ref, qs_ref, bias_ref, m_ref, l_ref, acc_ref, *, lam_init):
    tq = q_ref.shape[1]
    tk = ATTN_TK
    h = pl.program_id(1)
    qi = pl.program_id(2)
    slope = jnp.exp2(-2.0 * (h + 1).astype(F32)) * LOG2E

    q = q_ref[0]
    first_half = lax.broadcasted_iota(jnp.int32, (1, 2 * DIFF_DH), 1) < DIFF_DH
    zq = jnp.zeros_like(q)
    qs_ref[0:tq, :] = jnp.where(first_half, q, zq)
    qs_ref[tq:2 * tq, :] = jnp.where(first_half, zq, q)

    def rel_pos():
        key = lax.broadcasted_iota(jnp.int32, (tk, tq), 0)
        qry = lax.broadcasted_iota(jnp.int32, (tk, tq), 1)
        return (qry - key).astype(F32)

    bias_ref[...] = -slope * rel_pos()
    m_ref[...] = jnp.full_like(m_ref, -jnp.inf)
    l_ref[...] = jnp.zeros_like(l_ref)
    acc_ref[...] = jnp.zeros_like(acc_ref)

    def step(kj, masked):
        start = pl.multiple_of(kj * tk, tk)
        kblk = k_ref[0, pl.ds(start, tk), :]
        vt = vt_ref[0, :, pl.ds(start, tk)]
        s = lax.dot_general(kblk, qs_ref[...], (((1,), (1,)), ((), ())), preferred_element_type=F32)
        bias = bias_ref[...]
        if masked:
            bias = jnp.where(rel_pos() >= 0.0, bias, -jnp.inf)
        s = s + jnp.concatenate([bias, bias], axis=1)
        off = -slope * (qi * tq - kj * tk).astype(F32)
        m_old = m_ref[...]
        m_new = jnp.maximum(m_old, jnp.max(s, axis=0, keepdims=True) + off)
        p = jnp.exp2(s - (m_new - off))
        alpha = jnp.exp2(m_old - m_new)
        l_ref[...] = alpha * l_ref[...] + jnp.sum(p, axis=0, keepdims=True)
        acc_ref[...] = alpha * acc_ref[...] + jnp.dot(vt, p.astype(BF16), preferred_element_type=F32)
        m_ref[...] = m_new

    def body(kj, carry):
        step(kj, masked=False)
        return carry

    lax.fori_loop(0, qi, body, 0)
    step(qi, masked=True)

    lam = _diff_lambda(lq1_ref, lk1_ref, lq2_ref, lk2_ref, lam_init)
    o_all = acc_ref[...] / l_ref[...]
    o = (o_all[:, :tq] - lam * o_all[:, tq:]).T
    y = o * lax.rsqrt(jnp.mean(o * o, axis=-1, keepdims=True) + EPS)
    y = y * gain_ref[0] * (1.0 - lam_init) * _silu(z_ref[0].astype(F32))
    o_ref[0] = y.astype(o_ref.dtype)


def _diff_attn(dq, dk, dvt, dz, gain, lq1, lk1, lq2, lk2, lam_init):
    bsz, s, width = dq.shape
    tq = ATTN_TQ
    assert ATTN_TQ == ATTN_TK
    hw = 2 * DIFF_DH
    qspec = pl.BlockSpec((1, tq, hw), lambda b, h, i: (b, i, h))
    kspec = pl.BlockSpec((1, s, hw), lambda b, h, i: (b, 0, h))
    vtspec = pl.BlockSpec((1, DIFF_DV, s), lambda b, h, i: (b, h, 0))
    lspec = pl.BlockSpec((1, DIFF_DH), lambda b, h, i: (0, 0))
    return pl.pallas_call(
        functools.partial(_attn_kernel, lam_init=lam_init),
        grid=(bsz, DIFF_HEADS, s // tq),
        in_specs=[qspec, kspec, vtspec, qspec,
                  pl.BlockSpec((1, 1, DIFF_DV), lambda b, h, i: (h, 0, 0)),
                  lspec, lspec, lspec, lspec],
        out_specs=qspec,
        out_shape=jax.ShapeDtypeStruct((bsz, s, width), BF16),
        scratch_shapes=[pltpu.VMEM((2 * tq, hw), BF16),
                        pltpu.VMEM((ATTN_TK, tq), F32),
                        pltpu.VMEM((1, 2 * tq), F32),
                        pltpu.VMEM((1, 2 * tq), F32),
                        pltpu.VMEM((DIFF_DV, 2 * tq), F32)],
        compiler_params=pltpu.CompilerParams(
            dimension_semantics=("arbitrary", "arbitrary", "arbitrary"), vmem_limit_bytes=VMEM_LIMIT),
        name="diff_attn",
    )(dq, dk, dvt, dz, gain.reshape(DIFF_HEADS, 1, DIFF_DV), lq1, lk1, lq2, lk2)


def _out_proj_kernel(og_ref, od_ref, x_ref, gate_ref, w_ref, fg_ref, o_ref):
    half = og_ref.shape[2]
    mixw = (jnp.dot(og_ref[0], w_ref[0:half, :], preferred_element_type=F32)
            + jnp.dot(od_ref[0], w_ref[half:2 * half, :], preferred_element_type=F32))
    xn = x_ref[0] + gate_ref[0] * mixw
    y = xn * lax.rsqrt(jnp.mean(xn * xn, axis=-1, keepdims=True) + EPS)
    o_ref[0] = y * fg_ref[...]


def _out_proj(o_gla, o_diff, x, mod3, w_out_bf, final_gain):
    bsz, s, d = x.shape
    tm = ROWS_OUT_PROJ
    half = o_gla.shape[2]
    return pl.pallas_call(
        _out_proj_kernel,
        grid=(bsz, s // tm),
        in_specs=[
            pl.BlockSpec((1, tm, half), lambda b, t: (b, t, 0)),
            pl.BlockSpec((1, tm, half), lambda b, t: (b, t, 0)),
            pl.BlockSpec((1, tm, d), lambda b, t: (b, t, 0)),
            pl.BlockSpec((1, 1, d), lambda b, t: (b, 0, 2)),
            pl.BlockSpec((2 * half, d), lambda b, t: (0, 0)),
            pl.BlockSpec((1, d), lambda b, t: (0, 0)),
        ],
        out_specs=pl.BlockSpec((1, tm, d), lambda b, t: (b, t, 0)),
        out_shape=jax.ShapeDtypeStruct((bsz, s, d), F32),
        compiler_params=pltpu.CompilerParams(
            dimension_semantics=("arbitrary", "arbitrary"), vmem_limit_bytes=VMEM_LIMIT),
        name="out_proj",
    )(o_gla, o_diff, x, mod3, w_out_bf, final_gain)


def _prep_w_in(w_in_l):
    hk = GLA_HEADS * GLA_DK
    sizes = (hk, hk, SECTION, SECTION, GLA_RANK, SECTION, SECTION, SECTION, SECTION)
    offs = np.concatenate([[0], np.cumsum(sizes)])
    gq, gk, gv, gz, gr, dq, dk, dv, dz = [w_in_l[:, offs[i]:offs[i + 1]] for i in range(9)]
    gr_pad = jnp.pad(gr, ((0, 0), (0, RANK_PAD - GLA_RANK)))
    cols = [gq * (GLA_DK ** -0.5), gk, gv, gz, dq * (DIFF_DH ** -0.5 * LOG2E), dk, dz, gr_pad]
    return jnp.concatenate(cols, axis=1).astype(BF16), dv.T.astype(BF16)


def kernel(x, c, w_ada, b_ada, norm_gain, w_in, w_gla_gate_up, b_gla_gate, gla_out_gain,
           lambda_q1, lambda_k1, lambda_q2, lambda_k2, diff_out_gain, w_out, final_gain):
    bsz, s, d = x.shape
    depth = w_in.shape[0]
    assert depth == 1, "out_proj applies the final rmsnorm, so exactly one layer is supported"
    for l in range(depth):
        mod = _adaln_mod(c, w_ada[l], b_ada[l])
        mod3 = mod.reshape(bsz, 1, 3 * d)
        w_in_r, w_dvt = _prep_w_in(w_in[l])
        wup_pad = jnp.pad(w_gla_gate_up[l], ((0, RANK_PAD - GLA_RANK), (0, 0)))
        gqk, gv, gz, dq, dk, dz, dvt, la = _in_proj(
            x, mod3, norm_gain[l].reshape(1, d), w_in_r, w_dvt, wup_pad, b_gla_gate[l].reshape(1, -1))
        o_gla = _gla(gqk, gv, gz, la, gla_out_gain[l].reshape(1, -1))
        lam_init = float(0.8 - 0.6 * np.exp(-0.3 * l))
        o_diff = _diff_attn(dq, dk, dvt, dz, diff_out_gain[l],
                            lambda_q1[l].reshape(1, -1), lambda_k1[l].reshape(1, -1),
                            lambda_q2[l].reshape(1, -1), lambda_k2[l].reshape(1, -1), lam_init)
        x = _out_proj(o_gla, o_diff, x, mod3, w_out[l].astype(BF16), final_gain.reshape(1, d))
    return x
```

```python
import functools
import math

import jax
import jax.numpy as jnp
import numpy as np
from jax import lax
from jax.experimental import pallas as pl
from jax.experimental.pallas import tpu as pltpu

F32 = jnp.float32
BF16 = jnp.bfloat16

EPS = 1e-6
LOG2E = math.log2(math.e)
GLA_HEADS = 4
GLA_DK = 64
GLA_DV = 128
GLA_RANK = 16
GLA_GATE_NORM = 16.0
GLA_CHUNK = 64
DIFF_HEADS = 4
DIFF_DH = 64
DIFF_DV = 128
LANES = 128
RANK_PAD = LANES
SECTION = 512
SUM_ROWS = 16

ROWS_IN_PROJ = 512
ROWS_GLA = 256
ATTN_TQ = 256
ATTN_TK = 256
ROWS_OUT_PROJ = 512
VMEM_LIMIT = 48 * 1024 * 1024


def _silu(v):
    return v / (1.0 + jnp.exp(-v))


def _log_sigmoid(v):
    return jnp.minimum(v, 0.0) - jnp.log(1.0 + jnp.exp(-jnp.abs(v)))


def _adaln_kernel(c_ref, w_ref, b_ref, o_ref):
    sc = _silu(c_ref[...]).astype(BF16)
    o_ref[...] = jnp.dot(sc, w_ref[...].astype(BF16), preferred_element_type=F32) + b_ref[...]


def _adaln_mod(c, w_ada, b_ada):
    bsz, d = c.shape
    n = w_ada.shape[1]
    tn = 1024
    return pl.pallas_call(
        _adaln_kernel,
        grid=(n // tn,),
        in_specs=[
            pl.BlockSpec((bsz, d), lambda j: (0, 0)),
            pl.BlockSpec((d, tn), lambda j: (0, j)),
            pl.BlockSpec((1, tn), lambda j: (0, j)),
        ],
        out_specs=pl.BlockSpec((bsz, tn), lambda j: (0, j)),
        out_shape=jax.ShapeDtypeStruct((bsz, n), F32),
        compiler_params=pltpu.CompilerParams(dimension_semantics=("arbitrary",)),
        name="adaln_mod",
    )(c, w_ada, b_ada.reshape(1, n))


def _in_proj_kernel(x_ref, shift_ref, scale_ref, gain_ref, w_ref, wvt_ref, wup_ref, bg_ref,
                    gqk_ref, gv_ref, gz_ref, dq_ref, dk_ref, dz_ref, dvt_ref, la_ref):
    x = x_ref[0]
    rstd = lax.rsqrt(jnp.mean(x * x, axis=-1, keepdims=True) + EPS)
    g = gain_ref[...] * (1.0 + scale_ref[0])
    h = (x * rstd * g + shift_ref[0]).astype(BF16)

    outs = (gqk_ref, gv_ref, gz_ref, dq_ref, dk_ref, dz_ref)
    for i, o_ref in enumerate(outs):
        w = w_ref[:, i * SECTION:(i + 1) * SECTION]
        o_ref[0] = jnp.dot(h, w, preferred_element_type=F32).astype(o_ref.dtype)

    dvt_ref[0] = lax.dot_general(wvt_ref[...], h, (((1,), (1,)), ((), ())),
                                 preferred_element_type=F32).astype(dvt_ref.dtype)

    gr = jnp.dot(h, w_ref[:, len(outs) * SECTION:len(outs) * SECTION + RANK_PAD],
                 preferred_element_type=F32)
    logit = jnp.dot(gr, wup_ref[...], preferred_element_type=F32,
                    precision=lax.Precision.HIGHEST) + bg_ref[...]
    la_ref[0] = _log_sigmoid(logit) * (1.0 / GLA_GATE_NORM)


def _in_proj(x, mod3, norm_gain, w_in_r, w_dvt, wup_pad, b_gate):
    bsz, s, d = x.shape
    tm = ROWS_IN_PROJ
    ncol = w_in_r.shape[1]
    hk = GLA_HEADS * GLA_DK
    act = lambda width: pl.BlockSpec((1, tm, width), lambda b, t: (b, t, 0))
    sec = jax.ShapeDtypeStruct((bsz, s, SECTION), BF16)
    out_shapes = (
        sec,
        sec,
        sec,
        sec,
        sec,
        sec,
        jax.ShapeDtypeStruct((bsz, SECTION, s), BF16),
        jax.ShapeDtypeStruct((bsz, s, hk), F32),
    )
    return pl.pallas_call(
        _in_proj_kernel,
        grid=(bsz, s // tm),
        in_specs=[
            act(d),
            pl.BlockSpec((1, 1, d), lambda b, t: (b, 0, 0)),
            pl.BlockSpec((1, 1, d), lambda b, t: (b, 0, 1)),
            pl.BlockSpec((1, d), lambda b, t: (0, 0)),
            pl.BlockSpec((d, ncol), lambda b, t: (0, 0)),
            pl.BlockSpec((SECTION, d), lambda b, t: (0, 0)),
            pl.BlockSpec((RANK_PAD, hk), lambda b, t: (0, 0)),
            pl.BlockSpec((1, hk), lambda b, t: (0, 0)),
        ],
        out_specs=(act(SECTION), act(SECTION), act(SECTION), act(SECTION), act(SECTION), act(SECTION),
                   pl.BlockSpec((1, SECTION, tm), lambda b, t: (b, 0, t)), act(hk)),
        out_shape=out_shapes,
        compiler_params=pltpu.CompilerParams(
            dimension_semantics=("arbitrary", "arbitrary"), vmem_limit_bytes=VMEM_LIMIT),
        name="in_proj",
    )(x, mod3, mod3, norm_gain, w_in_r, w_dvt, wup_pad, b_gate)


def _gla_kernel(qk_ref, v_ref, z_ref, la_ref, gain_ref, o_ref, state_ref):
    tg = qk_ref.shape[1]
    hk = GLA_HEADS * GLA_DK
    c_sz = GLA_CHUNK

    @pl.when(pl.program_id(1) == 0)
    def _():
        state_ref[...] = jnp.zeros_like(state_ref)

    r = lax.broadcasted_iota(jnp.int32, (tg, tg), 0)
    c = lax.broadcasted_iota(jnp.int32, (tg, tg), 1)
    tril = jnp.where((r // c_sz == c // c_sz) & (c <= r), 1.0, 0.0).astype(F32)
    b_all = jnp.dot(tril, la_ref[0], preferred_element_type=F32, precision=lax.Precision.HIGHEST)

    lane_head = lax.broadcasted_iota(jnp.int32, (1, hk), 1) // GLA_DK
    ri = lax.broadcasted_iota(jnp.int32, (c_sz, hk), 0)
    ci = lax.broadcasted_iota(jnp.int32, (c_sz, hk), 1) % c_sz
    causal = ci <= ri
    zero_v = jnp.zeros((c_sz, GLA_DV), BF16)
    zero_s = jnp.zeros((GLA_DK, GLA_DV), BF16)
    gain = gain_ref[...]

    for ch in range(tg // c_sz):
        rows = slice(ch * c_sz, (ch + 1) * c_sz)
        b = b_all[rows]
        b_last = b[c_sz - 1:c_sz]
        q = qk_ref[0, rows, 0:hk].astype(F32)
        k = qk_ref[0, rows, hk:2 * hk].astype(F32)
        v = v_ref[0, rows, :]
        q_in = (q * jnp.exp(b)).astype(BF16)
        k_in = (k * jnp.exp(-b)).astype(BF16)
        k_st = k * jnp.exp(b_last - b)

        k_bd = jnp.concatenate(
            [jnp.where(lane_head == hh, k_in, jnp.zeros_like(k_in)) for hh in range(GLA_HEADS)], axis=0)
        scores = lax.dot_general(q_in, k_bd, (((1,), (1,)), ((), ())), preferred_element_type=F32)
        p = jnp.where(causal, scores, 0.0).astype(BF16)

        v_bd = jnp.concatenate(
            [jnp.concatenate([v[:, hh * GLA_DV:(hh + 1) * GLA_DV] if j == hh else zero_v
                              for j in range(GLA_HEADS)], axis=1) for hh in range(GLA_HEADS)], axis=0)
        s_bd = jnp.concatenate(
            [jnp.concatenate([state_ref[hh].astype(BF16) if j == hh else zero_s
                              for j in range(GLA_HEADS)], axis=1) for hh in range(GLA_HEADS)], axis=0)
        o = (jnp.dot(p, v_bd, preferred_element_type=F32)
             + jnp.dot(q_in, s_bd, preferred_element_type=F32))

        kst_t = k_st.T.astype(BF16)
        dec_col = jnp.exp(b.T[:, c_sz - 1:c_sz])
        for hh in range(GLA_HEADS):
            u = jnp.dot(kst_t[hh * GLA_DK:(hh + 1) * GLA_DK], v[:, hh * GLA_DV:(hh + 1) * GLA_DV],
                        preferred_element_type=F32)
            state_ref[hh] = dec_col[hh * GLA_DK:(hh + 1) * GLA_DK] * state_ref[hh] + u

        z = z_ref[0, rows, :].astype(F32)
        ys = []
        for hh in range(GLA_HEADS):
            oh = o[:, hh * GLA_DV:(hh + 1) * GLA_DV]
            ys.append(oh * lax.rsqrt(jnp.mean(oh * oh, axis=-1, keepdims=True) + EPS))
        y = jnp.concatenate(ys, axis=1) * gain * _silu(z)
        o_ref[0, rows, :] = y.astype(o_ref.dtype)


def _gla(gqk, gv, gz, la, gla_out_gain):
    bsz, s, _ = gqk.shape
    tg = ROWS_GLA
    hk = GLA_HEADS * GLA_DK
    width = GLA_HEADS * GLA_DV
    act = lambda w: pl.BlockSpec((1, tg, w), lambda b, t: (b, t, 0))
    return pl.pallas_call(
        _gla_kernel,
        grid=(bsz, s // tg),
        in_specs=[act(2 * hk), act(width), act(width), act(hk),
                  pl.BlockSpec((1, width), lambda b, t: (0, 0))],
        out_specs=act(width),
        out_shape=jax.ShapeDtypeStruct((bsz, s, width), BF16),
        scratch_shapes=[pltpu.VMEM((GLA_HEADS, GLA_DK, GLA_DV), F32)],
        compiler_params=pltpu.CompilerParams(
            dimension_semantics=("arbitrary", "arbitrary"), vmem_limit_bytes=VMEM_LIMIT),
        name="gla",
    )(gqk, gv, gz, la, gla_out_gain)


def _diff_lambda(lq1_ref, lk1_ref, lq2_ref, lk2_ref, lam_init):
    a = jnp.sum(lq1_ref[...] * lk1_ref[...], axis=-1, keepdims=True)
    b = jnp.sum(lq2_ref[...] * lk2_ref[...], axis=-1, keepdims=True)
    return jnp.exp(a) - jnp.exp(b) + lam_init


def _attn_spans(nq, tk):
    spans = []
    for qi in range(nq):
        tile = [(qi, 2 * tk * j, 2 * tk, None) for j in range(qi // 2)]
        if qi % 2 == 1:
            tile.append((qi, (qi - 1) * tk, 2 * tk, tk))
        else:
            tile.append((qi, qi * tk, tk, 0))
        for i, sp in enumerate(tile):
            spans.append(sp + (i == 0, i == len(tile) - 1))
    return spans


def _attn_kernel(q_ref, k_ref, vt_ref, z_ref, gain_ref, lq1_ref, lk1_ref, lq2_ref, lk2_ref,
                 o_ref, qs_ref, bias_ref, s0_ref, s1_ref, mb_ref, acc_ref, *, lam_init):
    tq = ATTN_TQ
    tk = ATTN_TK
    nq = q_ref.shape[1] // tq
    h = pl.program_id(1)
    slope = jnp.exp2(-2.0 * (h + 1).astype(F32)) * LOG2E
    s_bufs = (s0_ref, s1_ref)

    first_half = lax.broadcasted_iota(jnp.int32, (1, 2 * DIFF_DH), 1) < DIFF_DH
    for qi in range(nq):
        q = q_ref[0, qi * tq:(qi + 1) * tq, :]
        zq = jnp.zeros_like(q)
        qs_ref[qi, 0:tq, :] = jnp.where(first_half, q, zq)
        qs_ref[qi, tq:2 * tq, :] = jnp.where(first_half, zq, q)

    def rel_pos(nk):
        key = lax.broadcasted_iota(jnp.int32, (nk, tq), 0)
        qry = lax.broadcasted_iota(jnp.int32, (nk, tq), 1)
        return qry - key

    bias_ref[...] = -slope * rel_pos(2 * tk).astype(F32)
    lam = _diff_lambda(lq1_ref, lk1_ref, lq2_ref, lk2_ref, lam_init)
    spans = _attn_spans(nq, tk)

    def span_offset(qi, start):
        return -slope * float(qi * tq - start)

    def scores(t):
        qi, start, nk, diag_off, _, _ = spans[t]
        kblk = k_ref[0, start:start + nk, :]
        s = lax.dot_general(kblk, qs_ref[qi], (((1,), (1,)), ((), ())), preferred_element_type=F32)
        bias = bias_ref[0:nk, :]
        if diag_off is not None:
            bias = jnp.where(rel_pos(nk) + diag_off >= 0, bias, -jnp.inf)
        s = s + jnp.concatenate([bias, bias], axis=1)
        s_bufs[t % 2][0:nk, :] = s
        mb_ref[t % 2] = jnp.max(s, axis=0, keepdims=True) + span_offset(qi, start)

    def update(t, m_old):
        qi, start, nk, _, first, last = spans[t]
        s = s_bufs[t % 2][0:nk, :]
        m_blk = mb_ref[t % 2]
        m_new = m_blk if first else jnp.maximum(m_old, m_blk)
        p = jnp.exp2(s - (m_new - span_offset(qi, start))).astype(BF16)
        vt_ones = jnp.concatenate([vt_ref[0, :, start:start + nk], jnp.ones((SUM_ROWS, nk), BF16)], axis=0)
        pv = jnp.dot(vt_ones, p, preferred_element_type=F32)
        if first:
            acc_ref[...] = pv
        else:
            acc_ref[...] = jnp.exp2(m_old - m_new) * acc_ref[...] + pv
        if last:
            rows = slice(qi * tq, (qi + 1) * tq)
            o_all = acc_ref[0:DIFF_DV, :] / acc_ref[DIFF_DV:DIFF_DV + 1, :]
            o = (o_all[:, :tq] - lam * o_all[:, tq:]).T
            y = o * lax.rsqrt(jnp.mean(o * o, axis=-1, keepdims=True) + EPS)
            y = y * gain_ref[0] * (1.0 - lam_init) * _silu(z_ref[0, rows, :].astype(F32))
            o_ref[0, rows, :] = y.astype(o_ref.dtype)
        return m_new

    scores(0)
    m = None
    for t in range(len(spans)):
        if t + 1 < len(spans):
            scores(t + 1)
        m = update(t, m)


def _diff_attn(dq, dk, dvt, dz, gain, lq1, lk1, lq2, lk2, lam_init):
    bsz, s, width = dq.shape
    tq = ATTN_TQ
    assert ATTN_TQ == ATTN_TK
    hw = 2 * DIFF_DH
    rowspec = pl.BlockSpec((1, s, hw), lambda b, h: (b, 0, h))
    vtspec = pl.BlockSpec((1, DIFF_DV, s), lambda b, h: (b, h, 0))
    lspec = pl.BlockSpec((1, DIFF_DH), lambda b, h: (0, 0))
    return pl.pallas_call(
        functools.partial(_attn_kernel, lam_init=lam_init),
        grid=(bsz, DIFF_HEADS),
        in_specs=[rowspec, rowspec, vtspec, rowspec,
                  pl.BlockSpec((1, 1, DIFF_DV), lambda b, h: (h, 0, 0)),
                  lspec, lspec, lspec, lspec],
        out_specs=rowspec,
        out_shape=jax.ShapeDtypeStruct((bsz, s, width), BF16),
        scratch_shapes=[pltpu.VMEM((s // tq, 2 * tq, hw), BF16),
                        pltpu.VMEM((2 * ATTN_TK, tq), F32),
                        pltpu.VMEM((2 * ATTN_TK, 2 * tq), F32),
                        pltpu.VMEM((2 * ATTN_TK, 2 * tq), F32),
                        pltpu.VMEM((2, 1, 2 * tq), F32),
                        pltpu.VMEM((DIFF_DV + SUM_ROWS, 2 * tq), F32)],
        compiler_params=pltpu.CompilerParams(
            dimension_semantics=("arbitrary", "arbitrary"), vmem_limit_bytes=VMEM_LIMIT),
        name="diff_attn",
    )(dq, dk, dvt, dz, gain.reshape(DIFF_HEADS, 1, DIFF_DV), lq1, lk1, lq2, lk2)


def _out_proj_kernel(og_ref, od_ref, x_ref, gate_ref, w_ref, fg_ref, o_ref):
    half = og_ref.shape[2]
    mixw = (jnp.dot(og_ref[0], w_ref[0:half, :], preferred_element_type=F32)
            + jnp.dot(od_ref[0], w_ref[half:2 * half, :], preferred_element_type=F32))
    xn = x_ref[0] + gate_ref[0] * mixw
    y = xn * lax.rsqrt(jnp.mean(xn * xn, axis=-1, keepdims=True) + EPS)
    o_ref[0] = y * fg_ref[...]


def _out_proj(o_gla, o_diff, x, mod3, w_out_bf, final_gain):
    bsz, s, d = x.shape
    tm = ROWS_OUT_PROJ
    half = o_gla.shape[2]
    return pl.pallas_call(
        _out_proj_kernel,
        grid=(bsz, s // tm),
        in_specs=[
            pl.BlockSpec((1, tm, half), lambda b, t: (b, t, 0)),
            pl.BlockSpec((1, tm, half), lambda b, t: (b, t, 0)),
            pl.BlockSpec((1, tm, d), lambda b, t: (b, t, 0)),
            pl.BlockSpec((1, 1, d), lambda b, t: (b, 0, 2)),
            pl.BlockSpec((2 * half, d), lambda b, t: (0, 0)),
            pl.BlockSpec((1, d), lambda b, t: (0, 0)),
        ],
        out_specs=pl.BlockSpec((1, tm, d), lambda b, t: (b, t, 0)),
        out_shape=jax.ShapeDtypeStruct((bsz, s, d), F32),
        compiler_params=pltpu.CompilerParams(
            dimension_semantics=("arbitrary", "arbitrary"), vmem_limit_bytes=VMEM_LIMIT),
        name="out_proj",
    )(o_gla, o_diff, x, mod3, w_out_bf, final_gain)


def _prep_w_in(w_in_l):
    hk = GLA_HEADS * GLA_DK
    sizes = (hk, hk, SECTION, SECTION, GLA_RANK, SECTION, SECTION, SECTION, SECTION)
    offs = np.concatenate([[0], np.cumsum(sizes)])
    gq, gk, gv, gz, gr, dq, dk, dv, dz = [w_in_l[:, offs[i]:offs[i + 1]] for i in range(9)]
    gr_pad = jnp.pad(gr, ((0, 0), (0, RANK_PAD - GLA_RANK)))
    cols = [gq * (GLA_DK ** -0.5), gk, gv, gz, dq * (DIFF_DH ** -0.5 * LOG2E), dk, dz, gr_pad]
    return jnp.concatenate(cols, axis=1).astype(BF16), dv.T.astype(BF16)


def kernel(x, c, w_ada, b_ada, norm_gain, w_in, w_gla_gate_up, b_gla_gate, gla_out_gain,
           lambda_q1, lambda_k1, lambda_q2, lambda_k2, diff_out_gain, w_out, final_gain):
    bsz, s, d = x.shape
    depth = w_in.shape[0]
    assert depth == 1, "out_proj applies the final rmsnorm, so exactly one layer is supported"
    for l in range(depth):
        mod = _adaln_mod(c, w_ada[l], b_ada[l])
        mod3 = mod.reshape(bsz, 1, 3 * d)
        w_in_r, w_dvt = _prep_w_in(w_in[l])
        wup_pad = jnp.pad(w_gla_gate_up[l], ((0, RANK_PAD - GLA_RANK), (0, 0)))
        gqk, gv, gz, dq, dk, dz, dvt, la = _in_proj(
            x, mod3, norm_gain[l].reshape(1, d), w_in_r, w_dvt, wup_pad, b_gla_gate[l].reshape(1, -1))
        o_gla = _gla(gqk, gv, gz, la, gla_out_gain[l].reshape(1, -1))
        lam_init = float(0.8 - 0.6 * np.exp(-0.3 * l))
        o_diff = _diff_attn(dq, dk, dvt, dz, diff_out_gain[l],
                            lambda_q1[l].reshape(1, -1), lambda_k1[l].reshape(1, -1),
                            lambda_q2[l].reshape(1, -1), lambda_k2[l].reshape(1, -1), lam_init)
        x = _out_proj(o_gla, o_diff, x, mod3, w_out[l].astype(BF16), final_gain.reshape(1, d))
    return x
```

```python
import functools
import math

import jax
import jax.numpy as jnp
import numpy as np
from jax import lax
from jax.experimental import pallas as pl
from jax.experimental.pallas import tpu as pltpu

F32 = jnp.float32
BF16 = jnp.bfloat16

EPS = 1e-6
LOG2E = math.log2(math.e)
GLA_HEADS = 4
GLA_DK = 64
GLA_DV = 128
GLA_RANK = 16
GLA_GATE_NORM = 16.0
GLA_CHUNK = 64
DIFF_HEADS = 4
DIFF_DH = 64
DIFF_DV = 128
LANES = 128
RANK_PAD = LANES
SECTION = 512
SUM_ROWS = 16

ROWS_IN_PROJ = 512
ROWS_GLA = 512
GLA_CUMSUM_ROWS = 256
ATTN_TQ = 256
ATTN_TK = 256
ROWS_OUT_PROJ = 512
VMEM_LIMIT = 48 * 1024 * 1024


def _silu(v):
    return v / (1.0 + jnp.exp(-v))


def _log_sigmoid(v):
    return jnp.minimum(v, 0.0) - jnp.log(1.0 + jnp.exp(-jnp.abs(v)))


def _adaln_kernel(c_ref, w_ref, b_ref, o_ref):
    sc = _silu(c_ref[...]).astype(BF16)
    o_ref[...] = jnp.dot(sc, w_ref[...].astype(BF16), preferred_element_type=F32) + b_ref[...]


def _adaln_mod(c, w_ada, b_ada):
    bsz, d = c.shape
    n = w_ada.shape[1]
    tn = 1024
    return pl.pallas_call(
        _adaln_kernel,
        grid=(n // tn,),
        in_specs=[
            pl.BlockSpec((bsz, d), lambda j: (0, 0)),
            pl.BlockSpec((d, tn), lambda j: (0, j)),
            pl.BlockSpec((1, tn), lambda j: (0, j)),
        ],
        out_specs=pl.BlockSpec((bsz, tn), lambda j: (0, j)),
        out_shape=jax.ShapeDtypeStruct((bsz, n), F32),
        compiler_params=pltpu.CompilerParams(dimension_semantics=("arbitrary",)),
        name="adaln_mod",
    )(c, w_ada, b_ada.reshape(1, n))


def _in_proj_kernel(x_ref, shift_ref, scale_ref, gain_ref, w_ref, wvt_ref, wup_ref, bg_ref,
                    gqk_ref, gv_ref, gz_ref, dq_ref, dk_ref, dz_ref, dvt_ref, la_ref):
    x = x_ref[0]
    rstd = lax.rsqrt(jnp.mean(x * x, axis=-1, keepdims=True) + EPS)
    g = gain_ref[...] * (1.0 + scale_ref[0])
    h = (x * rstd * g + shift_ref[0]).astype(BF16)

    outs = (gqk_ref, gv_ref, gz_ref, dq_ref, dk_ref, dz_ref)
    for i, o_ref in enumerate(outs):
        w = w_ref[:, i * SECTION:(i + 1) * SECTION]
        sec = jnp.dot(h, w, preferred_element_type=F32)
        if o_ref is gz_ref or o_ref is dz_ref:
            sec = _silu(sec)
        o_ref[0] = sec.astype(o_ref.dtype)

    dvt_ref[0] = lax.dot_general(wvt_ref[...], h, (((1,), (1,)), ((), ())),
                                 preferred_element_type=F32).astype(dvt_ref.dtype)

    gr = jnp.dot(h, w_ref[:, len(outs) * SECTION:len(outs) * SECTION + RANK_PAD],
                 preferred_element_type=F32)
    logit = jnp.dot(gr.astype(BF16), wup_ref[...], preferred_element_type=F32) + bg_ref[...]
    la_ref[0] = (_log_sigmoid(logit) * (1.0 / GLA_GATE_NORM)).astype(la_ref.dtype)


def _in_proj(x, mod3, norm_gain, w_in_r, w_dvt, wup_pad, b_gate):
    bsz, s, d = x.shape
    tm = ROWS_IN_PROJ
    ncol = w_in_r.shape[1]
    hk = GLA_HEADS * GLA_DK
    act = lambda width: pl.BlockSpec((1, tm, width), lambda b, t: (b, t, 0))
    sec = jax.ShapeDtypeStruct((bsz, s, SECTION), BF16)
    out_shapes = (
        sec,
        sec,
        sec,
        sec,
        sec,
        sec,
        jax.ShapeDtypeStruct((bsz, SECTION, s), BF16),
        jax.ShapeDtypeStruct((bsz, s, hk), BF16),
    )
    return pl.pallas_call(
        _in_proj_kernel,
        grid=(bsz, s // tm),
        in_specs=[
            act(d),
            pl.BlockSpec((1, 1, d), lambda b, t: (b, 0, 0)),
            pl.BlockSpec((1, 1, d), lambda b, t: (b, 0, 1)),
            pl.BlockSpec((1, d), lambda b, t: (0, 0)),
            pl.BlockSpec((d, ncol), lambda b, t: (0, 0)),
            pl.BlockSpec((SECTION, d), lambda b, t: (0, 0)),
            pl.BlockSpec((RANK_PAD, hk), lambda b, t: (0, 0)),
            pl.BlockSpec((1, hk), lambda b, t: (0, 0)),
        ],
        out_specs=(act(SECTION), act(SECTION), act(SECTION), act(SECTION), act(SECTION), act(SECTION),
                   pl.BlockSpec((1, SECTION, tm), lambda b, t: (b, 0, t)), act(hk)),
        out_shape=out_shapes,
        compiler_params=pltpu.CompilerParams(
            dimension_semantics=("arbitrary", "arbitrary"), vmem_limit_bytes=VMEM_LIMIT),
        name="in_proj",
    )(x, mod3, mod3, norm_gain, w_in_r, w_dvt, wup_pad, b_gate)


def _gla_kernel(qk_ref, v_ref, za_ref, la_ref, gain_ref, tril_ref, chunk_ind_ref, ones_bd_ref,
                o_ref, state_ref):
    tg = qk_ref.shape[1]
    hk = GLA_HEADS * GLA_DK
    c_sz = GLA_CHUNK
    n_ch = tg // c_sz

    @pl.when(pl.program_id(1) == 0)
    def _():
        state_ref[...] = jnp.zeros_like(state_ref)

    la = la_ref[0]
    grp = tril_ref.shape[0]
    b_all = jnp.concatenate(
        [jnp.dot(tril_ref[...], la[g * grp:(g + 1) * grp], preferred_element_type=F32)
         for g in range(tg // grp)], axis=0)
    b_last = jnp.concatenate(
        [jnp.broadcast_to(b_all[(ch + 1) * c_sz - 1:(ch + 1) * c_sz], (c_sz, hk)) for ch in range(n_ch)], axis=0)
    q = qk_ref[0, :, 0:hk]
    k = qk_ref[0, :, hk:2 * hk]
    q_in = q * jnp.exp(b_all).astype(BF16)
    k_in = k * jnp.exp(-b_all).astype(BF16)
    k_st = k * jnp.exp(b_last - b_all).astype(BF16)

    lane_head = lax.broadcasted_iota(jnp.int32, (1, hk), 1) // GLA_DK
    ri = lax.broadcasted_iota(jnp.int32, (GLA_HEADS * c_sz, c_sz), 0) % c_sz
    ci = lax.broadcasted_iota(jnp.int32, (GLA_HEADS * c_sz, c_sz), 1)
    causal = ci <= ri
    dec_all = jnp.concatenate(
        [jnp.exp(lax.dot_general(la[g * grp:(g + 1) * grp], chunk_ind_ref[...], (((0,), (0,)), ((), ())),
                                 preferred_element_type=F32)) for g in range(tg // grp)], axis=1)

    chunk_rows = [slice(ch * c_sz, (ch + 1) * c_sz) for ch in range(n_ch)]
    head_rows = [slice(hh * c_sz, (hh + 1) * c_sz) for hh in range(GLA_HEADS)]
    head_cols = [slice(hh * GLA_DV, (hh + 1) * GLA_DV) for hh in range(GLA_HEADS)]
    vs = [v_ref[0, rows, :] for rows in chunk_rows]

    us = []
    for ch, rows in enumerate(chunk_rows):
        kst_t = k_st[rows].T
        us.append(jnp.concatenate(
            [jnp.dot(kst_t[hr], vs[ch][:, vc], preferred_element_type=F32)
             for hr, vc in zip(head_rows, head_cols)], axis=0))
    state = state_ref[...]
    states = []
    for ch in range(n_ch):
        states.append(state.astype(BF16))
        state = dec_all[:, ch * GLA_DV:(ch + 1) * GLA_DV] * state + us[ch]
    state_ref[...] = state
    boths = []
    for ch, rows in enumerate(chunk_rows):
        q_c = q_in[rows]
        qm = jnp.concatenate(
            [jnp.where(lane_head == hh, q_c, jnp.zeros_like(q_c)) for hh in range(GLA_HEADS)], axis=0)
        rhs = jnp.concatenate([states[ch], k_in[rows].T], axis=1)
        boths.append(jnp.dot(qm, rhs, preferred_element_type=F32))
    o_rows = []
    for ch in range(n_ch):
        inter = boths[ch][:, 0:GLA_DV]
        p = jnp.where(causal, boths[ch][:, GLA_DV:GLA_DV + c_sz], 0.0).astype(BF16)
        o_rows.append(jnp.concatenate(
            [jnp.dot(p[hr], vs[ch][:, vc], preferred_element_type=F32) + inter[hr]
             for hr, vc in zip(head_rows, head_cols)], axis=1))

    o = jnp.concatenate(o_rows, axis=0)
    ms = jnp.dot((o * o).astype(BF16), ones_bd_ref[...], preferred_element_type=F32) * (1.0 / GLA_DV)
    y = o * lax.rsqrt(ms + EPS) * gain_ref[...] * za_ref[0].astype(F32)
    o_ref[0] = y.astype(o_ref.dtype)


def _gla(gqk, gv, gza, la, gla_out_gain):
    bsz, s, _ = gqk.shape
    tg = ROWS_GLA
    hk = GLA_HEADS * GLA_DK
    width = GLA_HEADS * GLA_DV
    grp = GLA_CUMSUM_ROWS
    idx = np.arange(grp)
    tril = ((idx[:, None] // GLA_CHUNK == idx[None, :] // GLA_CHUNK) & (idx[None, :] <= idx[:, None]))
    chunk_ind = idx[:, None] // GLA_CHUNK == np.arange(grp // GLA_CHUNK * GLA_DV)[None, :] // GLA_DV
    col = np.arange(width)
    ones_bd = col[:, None] // GLA_DV == col[None, :] // GLA_DV
    act = lambda w: pl.BlockSpec((1, tg, w), lambda b, t: (b, t, 0))
    const = lambda shape: pl.BlockSpec(shape, lambda b, t: (0, 0))
    return pl.pallas_call(
        _gla_kernel,
        grid=(bsz, s // tg),
        in_specs=[act(2 * hk), act(width), act(width), act(hk),
                  const((1, width)), const(tril.shape), const(chunk_ind.shape), const((width, width))],
        out_specs=act(width),
        out_shape=jax.ShapeDtypeStruct((bsz, s, width), BF16),
        scratch_shapes=[pltpu.VMEM((hk, GLA_DV), F32)],
        compiler_params=pltpu.CompilerParams(
            dimension_semantics=("arbitrary", "arbitrary"), vmem_limit_bytes=VMEM_LIMIT),
        name="gla",
    )(gqk, gv, gza, la, gla_out_gain, jnp.asarray(tril, BF16), jnp.asarray(chunk_ind, BF16),
      jnp.asarray(ones_bd, BF16))


def _diff_lambda(lq1_ref, lk1_ref, lq2_ref, lk2_ref, lam_init):
    a = jnp.sum(lq1_ref[...] * lk1_ref[...], axis=-1, keepdims=True)
    b = jnp.sum(lq2_ref[...] * lk2_ref[...], axis=-1, keepdims=True)
    return jnp.exp(a) - jnp.exp(b) + lam_init


def _attn_spans(nq, tk):
    spans = []
    for qi in range(nq):
        tile = [(qi, 2 * tk * j, 2 * tk, None) for j in range(qi // 2)]
        if qi % 2 == 1:
            tile.append((qi, (qi - 1) * tk, 2 * tk, tk))
        else:
            tile.append((qi, qi * tk, tk, 0))
        for i, sp in enumerate(tile):
            spans.append(sp + (i == 0, i == len(tile) - 1))
    return spans


def _attn_kernel(q_ref, k_ref, vt_ref, z_ref, gain_ref, lq1_ref, lk1_ref, lq2_ref, lk2_ref,
                 o_ref, qs_ref, bias_ref, s0_ref, s1_ref, mb_ref, acc_ref, *, lam_init):
    tq = ATTN_TQ
    tk = ATTN_TK
    nq = q_ref.shape[1] // tq
    h = pl.program_id(1)
    slope = jnp.exp2(-2.0 * (h + 1).astype(F32)) * LOG2E
    s_bufs = (s0_ref, s1_ref)

    first_half = lax.broadcasted_iota(jnp.int32, (1, 2 * DIFF_DH), 1) < DIFF_DH
    for qi in range(nq):
        q = q_ref[0, qi * tq:(qi + 1) * tq, :]
        zq = jnp.zeros_like(q)
        qs_ref[qi, 0:tq, :] = jnp.where(first_half, q, zq)
        qs_ref[qi, tq:2 * tq, :] = jnp.where(first_half, zq, q)

    def rel_pos(nk):
        key = lax.broadcasted_iota(jnp.int32, (nk, tq), 0)
        qry = lax.broadcasted_iota(jnp.int32, (nk, tq), 1)
        return qry - key

    bias_ref[...] = -slope * rel_pos(2 * tk).astype(F32)
    lam = _diff_lambda(lq1_ref, lk1_ref, lq2_ref, lk2_ref, lam_init)
    spans = _attn_spans(nq, tk)

    def span_offset(qi, start):
        return -slope * float(qi * tq - start)

    def scores(t):
        qi, start, nk, diag_off, _, _ = spans[t]
        kblk = k_ref[0, start:start + nk, :]
        s = lax.dot_general(kblk, qs_ref[qi], (((1,), (1,)), ((), ())), preferred_element_type=F32)
        bias = bias_ref[0:nk, :]
        if diag_off is not None:
            bias = jnp.where(rel_pos(nk) + diag_off >= 0, bias, -jnp.inf)
        s = s + jnp.concatenate([bias, bias], axis=1)
        s_bufs[t % 2][0:nk, :] = s
        mb_ref[t % 2] = jnp.max(s, axis=0, keepdims=True) + span_offset(qi, start)

    def update(t, m_old):
        qi, start, nk, _, first, last = spans[t]
        s = s_bufs[t % 2][0:nk, :]
        m_blk = mb_ref[t % 2]
        m_new = m_blk if first else jnp.maximum(m_old, m_blk)
        p = jnp.exp2(s - (m_new - span_offset(qi, start))).astype(BF16)
        vt_ones = jnp.concatenate([vt_ref[0, :, start:start + nk], jnp.ones((SUM_ROWS, nk), BF16)], axis=0)
        pv = jnp.dot(vt_ones, p, preferred_element_type=F32)
        if first:
            acc_ref[...] = pv
        else:
            acc_ref[...] = jnp.exp2(m_old - m_new) * acc_ref[...] + pv
        if last:
            rows = slice(qi * tq, (qi + 1) * tq)
            o_all = acc_ref[0:DIFF_DV, :] / acc_ref[DIFF_DV:DIFF_DV + 1, :]
            o = (o_all[:, :tq] - lam * o_all[:, tq:]).T
            y = o * lax.rsqrt(jnp.mean(o * o, axis=-1, keepdims=True) + EPS)
            y = y * gain_ref[0] * (1.0 - lam_init) * z_ref[0, rows, :].astype(F32)
            o_ref[0, rows, :] = y.astype(o_ref.dtype)
        return m_new

    scores(0)
    m = None
    for t in range(len(spans)):
        if t + 1 < len(spans):
            scores(t + 1)
        m = update(t, m)


def _diff_attn(dq, dk, dvt, dz, gain, lq1, lk1, lq2, lk2, lam_init):
    bsz, s, width = dq.shape
    tq = ATTN_TQ
    assert ATTN_TQ == ATTN_TK
    hw = 2 * DIFF_DH
    rowspec = pl.BlockSpec((1, s, hw), lambda b, h: (b, 0, h))
    vtspec = pl.BlockSpec((1, DIFF_DV, s), lambda b, h: (b, h, 0))
    lspec = pl.BlockSpec((1, DIFF_DH), lambda b, h: (0, 0))
    return pl.pallas_call(
        functools.partial(_attn_kernel, lam_init=lam_init),
        grid=(bsz, DIFF_HEADS),
        in_specs=[rowspec, rowspec, vtspec, rowspec,
                  pl.BlockSpec((1, 1, DIFF_DV), lambda b, h: (h, 0, 0)),
                  lspec, lspec, lspec, lspec],
        out_specs=rowspec,
        out_shape=jax.ShapeDtypeStruct((bsz, s, width), BF16),
        scratch_shapes=[pltpu.VMEM((s // tq, 2 * tq, hw), BF16),
                        pltpu.VMEM((2 * ATTN_TK, tq), F32),
                        pltpu.VMEM((2 * ATTN_TK, 2 * tq), F32),
                        pltpu.VMEM((2 * ATTN_TK, 2 * tq), F32),
                        pltpu.VMEM((2, 1, 2 * tq), F32),
                        pltpu.VMEM((DIFF_DV + SUM_ROWS, 2 * tq), F32)],
        compiler_params=pltpu.CompilerParams(
            dimension_semantics=("arbitrary", "arbitrary"), vmem_limit_bytes=VMEM_LIMIT),
        name="diff_attn",
    )(dq, dk, dvt, dz, gain.reshape(DIFF_HEADS, 1, DIFF_DV), lq1, lk1, lq2, lk2)


def _out_proj_kernel(og_ref, od_ref, x_ref, gate_ref, w_ref, fg_ref, o_ref):
    half = og_ref.shape[2]
    mixw = (jnp.dot(og_ref[0], w_ref[0:half, :], preferred_element_type=F32)
            + jnp.dot(od_ref[0], w_ref[half:2 * half, :], preferred_element_type=F32))
    xn = x_ref[0] + gate_ref[0] * mixw
    y = xn * lax.rsqrt(jnp.mean(xn * xn, axis=-1, keepdims=True) + EPS)
    o_ref[0] = y * fg_ref[...]


def _out_proj(o_gla, o_diff, x, mod3, w_out_bf, final_gain):
    bsz, s, d = x.shape
    tm = ROWS_OUT_PROJ
    half = o_gla.shape[2]
    return pl.pallas_call(
        _out_proj_kernel,
        grid=(bsz, s // tm),
        in_specs=[
            pl.BlockSpec((1, tm, half), lambda b, t: (b, t, 0)),
            pl.BlockSpec((1, tm, half), lambda b, t: (b, t, 0)),
            pl.BlockSpec((1, tm, d), lambda b, t: (b, t, 0)),
            pl.BlockSpec((1, 1, d), lambda b, t: (b, 0, 2)),
            pl.BlockSpec((2 * half, d), lambda b, t: (0, 0)),
            pl.BlockSpec((1, d), lambda b, t: (0, 0)),
        ],
        out_specs=pl.BlockSpec((1, tm, d), lambda b, t: (b, t, 0)),
        out_shape=jax.ShapeDtypeStruct((bsz, s, d), F32),
        compiler_params=pltpu.CompilerParams(
            dimension_semantics=("arbitrary", "arbitrary"), vmem_limit_bytes=VMEM_LIMIT),
        name="out_proj",
    )(o_gla, o_diff, x, mod3, w_out_bf, final_gain)


def _prep_w_in(w_in_l):
    hk = GLA_HEADS * GLA_DK
    sizes = (hk, hk, SECTION, SECTION, GLA_RANK, SECTION, SECTION, SECTION, SECTION)
    offs = np.concatenate([[0], np.cumsum(sizes)])
    gq, gk, gv, gz, gr, dq, dk, dv, dz = [w_in_l[:, offs[i]:offs[i + 1]] for i in range(9)]
    gr_pad = jnp.pad(gr, ((0, 0), (0, RANK_PAD - GLA_RANK)))
    cols = [gq * (GLA_DK ** -0.5), gk, gv, gz, dq * (DIFF_DH ** -0.5 * LOG2E), dk, dz, gr_pad]
    return jnp.concatenate(cols, axis=1).astype(BF16), dv.T.astype(BF16)


def kernel(x, c, w_ada, b_ada, norm_gain, w_in, w_gla_gate_up, b_gla_gate, gla_out_gain,
           lambda_q1, lambda_k1, lambda_q2, lambda_k2, diff_out_gain, w_out, final_gain):
    bsz, s, d = x.shape
    depth = w_in.shape[0]
    assert depth == 1, "out_proj applies the final rmsnorm, so exactly one layer is supported"
    for l in range(depth):
        mod = _adaln_mod(c, w_ada[l], b_ada[l])
        mod3 = mod.reshape(bsz, 1, 3 * d)
        w_in_r, w_dvt = _prep_w_in(w_in[l])
        wup_pad = jnp.pad(w_gla_gate_up[l], ((0, RANK_PAD - GLA_RANK), (0, 0))).astype(BF16)
        gqk, gv, gz, dq, dk, dz, dvt, la = _in_proj(
            x, mod3, norm_gain[l].reshape(1, d), w_in_r, w_dvt, wup_pad, b_gla_gate[l].reshape(1, -1))
        o_gla = _gla(gqk, gv, gz, la, gla_out_gain[l].reshape(1, -1))
        lam_init = float(0.8 - 0.6 * np.exp(-0.3 * l))
        o_diff = _diff_attn(dq, dk, dvt, dz, diff_out_gain[l],
                            lambda_q1[l].reshape(1, -1), lambda_k1[l].reshape(1, -1),
                            lambda_q2[l].reshape(1, -1), lambda_k2[l].reshape(1, -1), lam_init)
        x = _out_proj(o_gla, o_diff, x, mod3, w_out[l].astype(BF16), final_gain.reshape(1, d))
    return x
```

```python
import functools
import math

import jax
import jax.numpy as jnp
import numpy as np
from jax import lax
from jax.experimental import pallas as pl
from jax.experimental.pallas import tpu as pltpu

F32 = jnp.float32
BF16 = jnp.bfloat16

EPS = 1e-6
LOG2E = math.log2(math.e)
GLA_HEADS = 4
GLA_DK = 64
GLA_DV = 128
GLA_RANK = 16
GLA_GATE_NORM = 16.0
GLA_CHUNK = 64
DIFF_HEADS = 4
DIFF_DH = 64
DIFF_DV = 128
LANES = 128
RANK_PAD = LANES
SECTION = 512
SUM_ROWS = 16

ROWS_IN_PROJ = 512
ROWS_GLA = 512
GLA_CUMSUM_ROWS = 256
ATTN_TQ = 256
ATTN_TK = 256
ATTN_AHEAD = 2
ROWS_OUT_PROJ = 512
VMEM_LIMIT = 48 * 1024 * 1024


def _silu(v):
    return v / (1.0 + jnp.exp(-v))


def _log_sigmoid(v):
    return jnp.minimum(v, 0.0) - jnp.log(1.0 + jnp.exp(-jnp.abs(v)))


def _adaln_kernel(c_ref, w_ref, b_ref, o_ref):
    sc = _silu(c_ref[...]).astype(BF16)
    o_ref[...] = jnp.dot(sc, w_ref[...].astype(BF16), preferred_element_type=F32) + b_ref[...]


def _adaln_mod(c, w_ada, b_ada):
    bsz, d = c.shape
    n = w_ada.shape[1]
    tn = 1024
    return pl.pallas_call(
        _adaln_kernel,
        grid=(n // tn,),
        in_specs=[
            pl.BlockSpec((bsz, d), lambda j: (0, 0)),
            pl.BlockSpec((d, tn), lambda j: (0, j)),
            pl.BlockSpec((1, tn), lambda j: (0, j)),
        ],
        out_specs=pl.BlockSpec((bsz, tn), lambda j: (0, j)),
        out_shape=jax.ShapeDtypeStruct((bsz, n), F32),
        compiler_params=pltpu.CompilerParams(dimension_semantics=("arbitrary",)),
        name="adaln_mod",
    )(c, w_ada, b_ada.reshape(1, n))


def _in_proj_kernel(x_ref, shift_ref, scale_ref, gain_ref, w_ref, wvt_ref, wup_ref, bg_ref,
                    gqk_ref, gv_ref, gz_ref, dq_ref, dk_ref, dz_ref, dvt_ref, la_ref):
    x = x_ref[0]
    rstd = lax.rsqrt(jnp.mean(x * x, axis=-1, keepdims=True) + EPS)
    g = gain_ref[...] * (1.0 + scale_ref[0])
    h = (x * rstd * g + shift_ref[0]).astype(BF16)

    outs = (gqk_ref, gv_ref, gz_ref, dq_ref, dk_ref, dz_ref)
    for i, o_ref in enumerate(outs):
        w = w_ref[:, i * SECTION:(i + 1) * SECTION]
        sec = jnp.dot(h, w, preferred_element_type=F32)
        if o_ref is gz_ref or o_ref is dz_ref:
            sec = _silu(sec)
        o_ref[0] = sec.astype(o_ref.dtype)

    dvt_ref[0] = lax.dot_general(wvt_ref[...], h, (((1,), (1,)), ((), ())),
                                 preferred_element_type=F32).astype(dvt_ref.dtype)

    gr = jnp.dot(h, w_ref[:, len(outs) * SECTION:len(outs) * SECTION + RANK_PAD],
                 preferred_element_type=F32)
    logit = jnp.dot(gr.astype(BF16), wup_ref[...], preferred_element_type=F32) + bg_ref[...]
    la_ref[0] = (_log_sigmoid(logit) * (1.0 / GLA_GATE_NORM)).astype(la_ref.dtype)


def _in_proj(x, mod3, norm_gain, w_in_r, w_dvt, wup_pad, b_gate):
    bsz, s, d = x.shape
    tm = ROWS_IN_PROJ
    ncol = w_in_r.shape[1]
    hk = GLA_HEADS * GLA_DK
    act = lambda width: pl.BlockSpec((1, tm, width), lambda b, t: (b, t, 0))
    sec = jax.ShapeDtypeStruct((bsz, s, SECTION), BF16)
    out_shapes = (
        sec,
        sec,
        sec,
        sec,
        sec,
        sec,
        jax.ShapeDtypeStruct((bsz, SECTION, s), BF16),
        jax.ShapeDtypeStruct((bsz, s, hk), BF16),
    )
    return pl.pallas_call(
        _in_proj_kernel,
        grid=(bsz, s // tm),
        in_specs=[
            act(d),
            pl.BlockSpec((1, 1, d), lambda b, t: (b, 0, 0)),
            pl.BlockSpec((1, 1, d), lambda b, t: (b, 0, 1)),
            pl.BlockSpec((1, d), lambda b, t: (0, 0)),
            pl.BlockSpec((d, ncol), lambda b, t: (0, 0)),
            pl.BlockSpec((SECTION, d), lambda b, t: (0, 0)),
            pl.BlockSpec((RANK_PAD, hk), lambda b, t: (0, 0)),
            pl.BlockSpec((1, hk), lambda b, t: (0, 0)),
        ],
        out_specs=(act(SECTION), act(SECTION), act(SECTION), act(SECTION), act(SECTION), act(SECTION),
                   pl.BlockSpec((1, SECTION, tm), lambda b, t: (b, 0, t)), act(hk)),
        out_shape=out_shapes,
        compiler_params=pltpu.CompilerParams(
            dimension_semantics=("arbitrary", "arbitrary"), vmem_limit_bytes=VMEM_LIMIT),
        name="in_proj",
    )(x, mod3, mod3, norm_gain, w_in_r, w_dvt, wup_pad, b_gate)


def _gla_kernel(qk_ref, v_ref, za_ref, la_ref, gain_ref, tril_ref, chunk_ind_ref, ones_bd_ref,
                o_ref, state_ref):
    tg = qk_ref.shape[1]
    hk = GLA_HEADS * GLA_DK
    c_sz = GLA_CHUNK
    n_ch = tg // c_sz

    @pl.when(pl.program_id(1) == 0)
    def _():
        state_ref[...] = jnp.zeros_like(state_ref)

    la = la_ref[0]
    grp = tril_ref.shape[0]
    b_all = jnp.concatenate(
        [jnp.dot(tril_ref[...], la[g * grp:(g + 1) * grp], preferred_element_type=F32)
         for g in range(tg // grp)], axis=0)
    b_last = jnp.concatenate(
        [jnp.broadcast_to(b_all[(ch + 1) * c_sz - 1:(ch + 1) * c_sz], (c_sz, hk)) for ch in range(n_ch)], axis=0)
    q = qk_ref[0, :, 0:hk]
    k = qk_ref[0, :, hk:2 * hk]
    q_in = q * jnp.exp(b_all).astype(BF16)
    k_in = k * jnp.exp(-b_all).astype(BF16)
    k_st = k * jnp.exp(b_last - b_all).astype(BF16)

    lane_head = lax.broadcasted_iota(jnp.int32, (1, hk), 1) // GLA_DK
    ri = lax.broadcasted_iota(jnp.int32, (GLA_HEADS * c_sz, c_sz), 0) % c_sz
    ci = lax.broadcasted_iota(jnp.int32, (GLA_HEADS * c_sz, c_sz), 1)
    causal = ci <= ri
    dec_all = jnp.concatenate(
        [jnp.exp(lax.dot_general(la[g * grp:(g + 1) * grp], chunk_ind_ref[...], (((0,), (0,)), ((), ())),
                                 preferred_element_type=F32)) for g in range(tg // grp)], axis=1)

    chunk_rows = [slice(ch * c_sz, (ch + 1) * c_sz) for ch in range(n_ch)]
    head_rows = [slice(hh * c_sz, (hh + 1) * c_sz) for hh in range(GLA_HEADS)]
    head_cols = [slice(hh * GLA_DV, (hh + 1) * GLA_DV) for hh in range(GLA_HEADS)]
    vs = [v_ref[0, rows, :] for rows in chunk_rows]

    us = []
    for ch, rows in enumerate(chunk_rows):
        kst_t = k_st[rows].T
        us.append(jnp.concatenate(
            [jnp.dot(kst_t[hr], vs[ch][:, vc], preferred_element_type=F32)
             for hr, vc in zip(head_rows, head_cols)], axis=0))
    state = state_ref[...]
    states = []
    for ch in range(n_ch):
        states.append(state.astype(BF16))
        state = dec_all[:, ch * GLA_DV:(ch + 1) * GLA_DV] * state + us[ch]
    state_ref[...] = state
    boths = []
    for ch, rows in enumerate(chunk_rows):
        q_c = q_in[rows]
        qm = jnp.concatenate(
            [jnp.where(lane_head == hh, q_c, jnp.zeros_like(q_c)) for hh in range(GLA_HEADS)], axis=0)
        rhs = jnp.concatenate([states[ch], k_in[rows].T], axis=1)
        boths.append(jnp.dot(qm, rhs, preferred_element_type=F32))
    o_rows = []
    for ch in range(n_ch):
        inter = boths[ch][:, 0:GLA_DV]
        p = jnp.where(causal, boths[ch][:, GLA_DV:GLA_DV + c_sz], 0.0).astype(BF16)
        o_rows.append(jnp.concatenate(
            [jnp.dot(p[hr], vs[ch][:, vc], preferred_element_type=F32) + inter[hr]
             for hr, vc in zip(head_rows, head_cols)], axis=1))

    o = jnp.concatenate(o_rows, axis=0)
    ms = jnp.dot((o * o).astype(BF16), ones_bd_ref[...], preferred_element_type=F32) * (1.0 / GLA_DV)
    y = o * lax.rsqrt(ms + EPS) * gain_ref[...] * za_ref[0].astype(F32)
    o_ref[0] = y.astype(o_ref.dtype)


def _gla(gqk, gv, gza, la, gla_out_gain):
    bsz, s, _ = gqk.shape
    tg = ROWS_GLA
    hk = GLA_HEADS * GLA_DK
    width = GLA_HEADS * GLA_DV
    grp = GLA_CUMSUM_ROWS
    idx = np.arange(grp)
    tril = ((idx[:, None] // GLA_CHUNK == idx[None, :] // GLA_CHUNK) & (idx[None, :] <= idx[:, None]))
    chunk_ind = idx[:, None] // GLA_CHUNK == np.arange(grp // GLA_CHUNK * GLA_DV)[None, :] // GLA_DV
    col = np.arange(width)
    ones_bd = col[:, None] // GLA_DV == col[None, :] // GLA_DV
    act = lambda w: pl.BlockSpec((1, tg, w), lambda b, t: (b, t, 0))
    const = lambda shape: pl.BlockSpec(shape, lambda b, t: (0, 0))
    return pl.pallas_call(
        _gla_kernel,
        grid=(bsz, s // tg),
        in_specs=[act(2 * hk), act(width), act(width), act(hk),
                  const((1, width)), const(tril.shape), const(chunk_ind.shape), const((width, width))],
        out_specs=act(width),
        out_shape=jax.ShapeDtypeStruct((bsz, s, width), BF16),
        scratch_shapes=[pltpu.VMEM((hk, GLA_DV), F32)],
        compiler_params=pltpu.CompilerParams(
            dimension_semantics=("arbitrary", "arbitrary"), vmem_limit_bytes=VMEM_LIMIT),
        name="gla",
    )(gqk, gv, gza, la, gla_out_gain, jnp.asarray(tril, BF16), jnp.asarray(chunk_ind, BF16),
      jnp.asarray(ones_bd, BF16))


def _diff_lambda(lq1_ref, lk1_ref, lq2_ref, lk2_ref, lam_init):
    a = jnp.sum(lq1_ref[...] * lk1_ref[...], axis=-1, keepdims=True)
    b = jnp.sum(lq2_ref[...] * lk2_ref[...], axis=-1, keepdims=True)
    return jnp.exp(a) - jnp.exp(b) + lam_init


def _attn_spans(nq, tk):
    spans = []
    for qi in range(nq):
        tile = [(qi, 2 * tk * j, 2 * tk, None) for j in range(qi // 2)]
        if qi % 2 == 1:
            tile.append((qi, (qi - 1) * tk, 2 * tk, tk))
        else:
            tile.append((qi, qi * tk, tk, 0))
        for i, sp in enumerate(tile):
            spans.append(sp + (i == 0, i == len(tile) - 1))
    return spans


def _attn_bias_tables(tq, tk):
    def split(x, terms):
        parts = []
        for _ in range(terms):
            hi = x.astype(jnp.bfloat16)
            parts.append(hi)
            x = x - hi.astype(np.float32)
        return parts

    key = np.arange(2 * tk, dtype=np.float32)
    kaug = np.zeros((2 * tk, LANES), np.float32)
    kaug[:, 0] = kaug[:, 1] = key % 256
    kaug[:, 2] = kaug[:, 3] = key - key % 256
    kaug[:, 4:7] = 1.0
    slopes = (2.0 ** (-8.0 * np.arange(1, DIFF_HEADS + 1) / DIFF_HEADS) * LOG2E).astype(np.float32)
    qry = np.tile(np.arange(tq, dtype=np.float32), 2)
    qaug = np.zeros((DIFF_HEADS, 2 * tq, LANES), jnp.bfloat16)
    for hh in range(DIFF_HEADS):
        s_hi, s_lo = split(slopes[hh:hh + 1], 2)
        qaug[hh, :, 0] = qaug[hh, :, 2] = s_hi
        qaug[hh, :, 1] = qaug[hh, :, 3] = s_lo
        for lane, part in zip((4, 5, 6), split(-slopes[hh] * qry, 3)):
            qaug[hh, :, lane] = part
    slope_rows = np.broadcast_to(slopes[:, None, None], (DIFF_HEADS, 1, LANES))
    return jnp.asarray(kaug, BF16), jnp.asarray(qaug), jnp.asarray(slope_rows, F32)


def _attn_kernel(q_ref, k_ref, vt_ref, z_ref, gain_ref, kaug_ref, qaug_ref, slope_ref,
                 lq1_ref, lk1_ref, lq2_ref, lk2_ref,
                 o_ref, qs_ref, s_ref, mb_ref, acc_ref, *, lam_init):
    tq = ATTN_TQ
    tk = ATTN_TK
    hw = 2 * DIFF_DH
    nq = q_ref.shape[1] // tq
    slope = slope_ref[0][:, 0:1]
    n_buf = ATTN_AHEAD + 1

    first_half = lax.broadcasted_iota(jnp.int32, (1, hw), 1) < DIFF_DH
    for qi in range(nq):
        q = q_ref[0, qi * tq:(qi + 1) * tq, :]
        zq = jnp.zeros_like(q)
        qs_ref[qi, 0:tq, 0:hw] = jnp.where(first_half, q, zq)
        qs_ref[qi, tq:2 * tq, 0:hw] = jnp.where(first_half, zq, q)
        qs_ref[qi, :, hw:hw + LANES] = qaug_ref[0]

    def rel_pos(nk):
        key = lax.broadcasted_iota(jnp.int32, (nk, 2 * tq), 0)
        qry = lax.broadcasted_iota(jnp.int32, (nk, 2 * tq), 1) % tq
        return qry - key

    lam = _diff_lambda(lq1_ref, lk1_ref, lq2_ref, lk2_ref, lam_init)
    spans = _attn_spans(nq, tk)

    def span_offset(qi, start):
        return -slope * float(qi * tq - start)

    def scores(t):
        qi, start, nk, diag_off, _, _ = spans[t]
        kblk = jnp.concatenate([k_ref[0, start:start + nk, :], kaug_ref[0:nk, :]], axis=1)
        s = lax.dot_general(kblk, qs_ref[qi], (((1,), (1,)), ((), ())), preferred_element_type=F32)
        if diag_off is not None:
            keep = rel_pos(tk) + (diag_off - (nk - tk)) >= 0
            tail = jnp.where(keep, s[nk - tk:nk], -jnp.inf)
            s = tail if nk == tk else jnp.concatenate([s[0:nk - tk], tail], axis=0)
        s_ref[t % n_buf, 0:nk, :] = s
        mb_ref[t % n_buf] = jnp.max(s, axis=0, keepdims=True) + span_offset(qi, start)

    def update(t, m_old):
        qi, start, nk, _, first, last = spans[t]
        s = s_ref[t % n_buf, 0:nk, :]
        m_blk = mb_ref[t % n_buf]
        m_new = m_blk if first else jnp.maximum(m_old, m_blk)
        p = jnp.exp2(s - (m_new - span_offset(qi, start))).astype(BF16)
        vt_ones = jnp.concatenate([vt_ref[0, :, start:start + nk], jnp.ones((SUM_ROWS, nk), BF16)], axis=0)
        pv = jnp.dot(vt_ones, p, preferred_element_type=F32)
        if first:
            acc_ref[...] = pv
        else:
            acc_ref[...] = jnp.exp2(m_old - m_new) * acc_ref[...] + pv
        if last:
            rows = slice(qi * tq, (qi + 1) * tq)
            o_all = acc_ref[0:DIFF_DV, :] / acc_ref[DIFF_DV:DIFF_DV + 1, :]
            o = (o_all[:, :tq] - lam * o_all[:, tq:]).T
            y = o * lax.rsqrt(jnp.mean(o * o, axis=-1, keepdims=True) + EPS)
            y = y * gain_ref[0] * (1.0 - lam_init) * z_ref[0, rows, :].astype(F32)
            o_ref[0, rows, :] = y.astype(o_ref.dtype)
        return m_new

    for t in range(min(ATTN_AHEAD, len(spans))):
        scores(t)
    m = None
    for t in range(len(spans)):
        if t + ATTN_AHEAD < len(spans):
            scores(t + ATTN_AHEAD)
        m = update(t, m)


def _diff_attn(dq, dk, dvt, dz, gain, lq1, lk1, lq2, lk2, lam_init):
    bsz, s, width = dq.shape
    tq = ATTN_TQ
    assert ATTN_TQ == ATTN_TK
    hw = 2 * DIFF_DH
    rowspec = pl.BlockSpec((1, s, hw), lambda b, h: (b, 0, h))
    vtspec = pl.BlockSpec((1, DIFF_DV, s), lambda b, h: (b, h, 0))
    lspec = pl.BlockSpec((1, DIFF_DH), lambda b, h: (0, 0))
    kaug, qaug, slope_rows = _attn_bias_tables(tq, ATTN_TK)
    per_head = lambda rows: pl.BlockSpec((1, rows, LANES), lambda b, h: (h, 0, 0))
    return pl.pallas_call(
        functools.partial(_attn_kernel, lam_init=lam_init),
        grid=(bsz, DIFF_HEADS),
        in_specs=[rowspec, rowspec, vtspec, rowspec, per_head(1),
                  pl.BlockSpec(kaug.shape, lambda b, h: (0, 0)), per_head(2 * tq), per_head(1),
                  lspec, lspec, lspec, lspec],
        out_specs=rowspec,
        out_shape=jax.ShapeDtypeStruct((bsz, s, width), BF16),
        scratch_shapes=[pltpu.VMEM((s // tq, 2 * tq, hw + LANES), BF16),
                        pltpu.VMEM((ATTN_AHEAD + 1, 2 * ATTN_TK, 2 * tq), F32),
                        pltpu.VMEM((ATTN_AHEAD + 1, 1, 2 * tq), F32),
                        pltpu.VMEM((DIFF_DV + SUM_ROWS, 2 * tq), F32)],
        compiler_params=pltpu.CompilerParams(
            dimension_semantics=("arbitrary", "arbitrary"), vmem_limit_bytes=VMEM_LIMIT),
        name="diff_attn",
    )(dq, dk, dvt, dz, gain.reshape(DIFF_HEADS, 1, DIFF_DV), kaug, qaug, slope_rows, lq1, lk1, lq2, lk2)


def _out_proj_kernel(og_ref, od_ref, x_ref, gate_ref, w_ref, fg_ref, o_ref):
    half = og_ref.shape[2]
    mixw = (jnp.dot(og_ref[0], w_ref[0:half, :], preferred_element_type=F32)
            + jnp.dot(od_ref[0], w_ref[half:2 * half, :], preferred_element_type=F32))
    xn = x_ref[0] + gate_ref[0] * mixw
    y = xn * lax.rsqrt(jnp.mean(xn * xn, axis=-1, keepdims=True) + EPS)
    o_ref[0] = y * fg_ref[...]


def _out_proj(o_gla, o_diff, x, mod3, w_out_bf, final_gain):
    bsz, s, d = x.shape
    tm = ROWS_OUT_PROJ
    half = o_gla.shape[2]
    return pl.pallas_call(
        _out_proj_kernel,
        grid=(bsz, s // tm),
        in_specs=[
            pl.BlockSpec((1, tm, half), lambda b, t: (b, t, 0)),
            pl.BlockSpec((1, tm, half), lambda b, t: (b, t, 0)),
            pl.BlockSpec((1, tm, d), lambda b, t: (b, t, 0)),
            pl.BlockSpec((1, 1, d), lambda b, t: (b, 0, 2)),
            pl.BlockSpec((2 * half, d), lambda b, t: (0, 0)),
            pl.BlockSpec((1, d), lambda b, t: (0, 0)),
        ],
        out_specs=pl.BlockSpec((1, tm, d), lambda b, t: (b, t, 0)),
        out_shape=jax.ShapeDtypeStruct((bsz, s, d), F32),
        compiler_params=pltpu.CompilerParams(
            dimension_semantics=("arbitrary", "arbitrary"), vmem_limit_bytes=VMEM_LIMIT),
        name="out_proj",
    )(o_gla, o_diff, x, mod3, w_out_bf, final_gain)


def _prep_w_in(w_in_l):
    hk = GLA_HEADS * GLA_DK
    sizes = (hk, hk, SECTION, SECTION, GLA_RANK, SECTION, SECTION, SECTION, SECTION)
    offs = np.concatenate([[0], np.cumsum(sizes)])
    gq, gk, gv, gz, gr, dq, dk, dv, dz = [w_in_l[:, offs[i]:offs[i + 1]] for i in range(9)]
    gr_pad = jnp.pad(gr, ((0, 0), (0, RANK_PAD - GLA_RANK)))
    cols = [gq * (GLA_DK ** -0.5), gk, gv, gz, dq * (DIFF_DH ** -0.5 * LOG2E), dk, dz, gr_pad]
    return jnp.concatenate(cols, axis=1).astype(BF16), dv.T.astype(BF16)


def kernel(x, c, w_ada, b_ada, norm_gain, w_in, w_gla_gate_up, b_gla_gate, gla_out_gain,
           lambda_q1, lambda_k1, lambda_q2, lambda_k2, diff_out_gain, w_out, final_gain):
    bsz, s, d = x.shape
    depth = w_in.shape[0]
    assert depth == 1, "out_proj applies the final rmsnorm, so exactly one layer is supported"
    for l in range(depth):
        mod = _adaln_mod(c, w_ada[l], b_ada[l])
        mod3 = mod.reshape(bsz, 1, 3 * d)
        w_in_r, w_dvt = _prep_w_in(w_in[l])
        wup_pad = jnp.pad(w_gla_gate_up[l], ((0, RANK_PAD - GLA_RANK), (0, 0))).astype(BF16)
        gqk, gv, gz, dq, dk, dz, dvt, la = _in_proj(
            x, mod3, norm_gain[l].reshape(1, d), w_in_r, w_dvt, wup_pad, b_gla_gate[l].reshape(1, -1))
        o_gla = _gla(gqk, gv, gz, la, gla_out_gain[l].reshape(1, -1))
        lam_init = float(0.8 - 0.6 * np.exp(-0.3 * l))
        o_diff = _diff_attn(dq, dk, dvt, dz, diff_out_gain[l],
                            lambda_q1[l].reshape(1, -1), lambda_k1[l].reshape(1, -1),
                            lambda_q2[l].reshape(1, -1), lambda_k2[l].reshape(1, -1), lam_init)
        x = _out_proj(o_gla, o_diff, x, mod3, w_out[l].astype(BF16), final_gain.reshape(1, d))
    return x
```

```python
import functools
import math

import jax
import jax.numpy as jnp
import numpy as np
from jax import lax
from jax.experimental import pallas as pl
from jax.experimental.pallas import tpu as pltpu

F32 = jnp.float32
BF16 = jnp.bfloat16

EPS = 1e-6
LOG2E = math.log2(math.e)
GLA_HEADS = 4
GLA_DK = 64
GLA_DV = 128
GLA_RANK = 16
GLA_GATE_NORM = 16.0
GLA_CHUNK = 64
DIFF_HEADS = 4
DIFF_DH = 64
DIFF_DV = 128
LANES = 128
RANK_PAD = LANES
SECTION = 512
SUM_ROWS = 16

ROWS_IN_PROJ = 512
IN_PROJ_SUB_ROWS = 512
ROWS_GLA = 512
GLA_CUMSUM_ROWS = 256
ATTN_TQ = 256
ATTN_TK = 256
ATTN_AHEAD = 2
ATTN_SPAN_TILES = 2
ROWS_OUT_PROJ = 1024
VMEM_LIMIT = 48 * 1024 * 1024


def _silu(v):
    return v / (1.0 + jnp.exp(-v))


def _log_sigmoid(v):
    return jnp.minimum(v, 0.0) - jnp.log(1.0 + jnp.exp(-jnp.abs(v)))


def _adaln_kernel(c_ref, w_ref, b_ref, o_ref):
    sc = _silu(c_ref[...]).astype(BF16)
    o_ref[...] = jnp.dot(sc, w_ref[...].astype(BF16), preferred_element_type=F32) + b_ref[...]


def _adaln_mod(c, w_ada, b_ada):
    bsz, d = c.shape
    n = w_ada.shape[1]
    tn = 1024
    return pl.pallas_call(
        _adaln_kernel,
        grid=(n // tn,),
        in_specs=[
            pl.BlockSpec((bsz, d), lambda j: (0, 0)),
            pl.BlockSpec((d, tn), lambda j: (0, j)),
            pl.BlockSpec((1, tn), lambda j: (0, j)),
        ],
        out_specs=pl.BlockSpec((bsz, tn), lambda j: (0, j)),
        out_shape=jax.ShapeDtypeStruct((bsz, n), F32),
        compiler_params=pltpu.CompilerParams(dimension_semantics=("arbitrary",)),
        name="adaln_mod",
    )(c, w_ada, b_ada.reshape(1, n))


def _in_proj_kernel(x_ref, shift_ref, scale_ref, gain_ref, w_ref, wup_ref, bg_ref,
                    gqk_ref, gv_ref, gz_ref, dq_ref, dk_ref, dz_ref, dvt_ref, la_ref,
                    wvt_ref):
    n_sec = 7
    dv_sec = 6
    tm = x_ref.shape[1]
    sub = IN_PROJ_SUB_ROWS

    @pl.when((pl.program_id(0) == 0) & (pl.program_id(1) == 0))
    def _():
        wvt_ref[...] = w_ref[:, dv_sec * SECTION:(dv_sec + 1) * SECTION].T

    g = gain_ref[...] * (1.0 + scale_ref[0])
    for r in range(tm // sub):
        rows = slice(r * sub, (r + 1) * sub)
        x = x_ref[0, rows, :]
        rstd = lax.rsqrt(jnp.mean(x * x, axis=-1, keepdims=True) + EPS)
        h = (x * rstd * g + shift_ref[0]).astype(BF16)

        outs = (gqk_ref, gv_ref, gz_ref, dq_ref, dk_ref, dz_ref)
        for j, o_ref in enumerate(outs):
            w = w_ref[:, j * SECTION:(j + 1) * SECTION]
            sec = jnp.dot(h, w, preferred_element_type=F32)
            if o_ref is gz_ref or o_ref is dz_ref:
                sec = _silu(sec)
            o_ref[0, rows, :] = sec.astype(o_ref.dtype)

        dvt_ref[0, :, rows] = lax.dot_general(wvt_ref[...], h, (((1,), (1,)), ((), ())),
                                              preferred_element_type=F32).astype(dvt_ref.dtype)

        gr = jnp.dot(h, w_ref[:, n_sec * SECTION:n_sec * SECTION + RANK_PAD],
                     preferred_element_type=F32)
        logit = jnp.dot(gr.astype(BF16), wup_ref[...], preferred_element_type=F32) + bg_ref[...]
        la_ref[0, rows, :] = (_log_sigmoid(logit) * (1.0 / GLA_GATE_NORM)).astype(la_ref.dtype)


def _in_proj(x, mod3, norm_gain, w_in_r, wup_pad, b_gate):
    bsz, s, d = x.shape
    tm = ROWS_IN_PROJ
    ncol = w_in_r.shape[1]
    hk = GLA_HEADS * GLA_DK
    act = lambda width: pl.BlockSpec((1, tm, width), lambda b, t: (b, t, 0))
    sec = jax.ShapeDtypeStruct((bsz, s, SECTION), BF16)
    out_shapes = (
        sec,
        sec,
        sec,
        sec,
        sec,
        sec,
        jax.ShapeDtypeStruct((bsz, SECTION, s), BF16),
        jax.ShapeDtypeStruct((bsz, s, hk), BF16),
    )
    const = lambda shape: pl.BlockSpec(shape, lambda b, t: (0, 0))
    return pl.pallas_call(
        _in_proj_kernel,
        grid=(bsz, s // tm),
        in_specs=[
            act(d),
            pl.BlockSpec((1, 1, d), lambda b, t: (b, 0, 0)),
            pl.BlockSpec((1, 1, d), lambda b, t: (b, 0, 1)),
            const((1, d)), const((d, ncol)), const((RANK_PAD, hk)), const((1, hk)),
        ],
        out_specs=(act(SECTION), act(SECTION), act(SECTION), act(SECTION), act(SECTION), act(SECTION),
                   pl.BlockSpec((1, SECTION, tm), lambda b, t: (b, 0, t)), act(hk)),
        out_shape=out_shapes,
        scratch_shapes=[pltpu.VMEM((SECTION, d), BF16)],
        compiler_params=pltpu.CompilerParams(
            dimension_semantics=("arbitrary", "arbitrary"), vmem_limit_bytes=VMEM_LIMIT),
        name="in_proj",
    )(x, mod3, mod3, norm_gain, w_in_r, wup_pad, b_gate)


def _gla_kernel(qk_ref, v_ref, za_ref, la_ref, gain_ref, tril_ref, chunk_ind_ref, ones_bd_ref,
                o_ref, state_ref):
    tg = qk_ref.shape[1]
    hk = GLA_HEADS * GLA_DK
    c_sz = GLA_CHUNK
    n_ch = tg // c_sz

    @pl.when(pl.program_id(1) == 0)
    def _():
        state_ref[...] = jnp.zeros_like(state_ref)

    la = la_ref[0]
    grp = tril_ref.shape[0]
    b_all = jnp.concatenate(
        [jnp.dot(tril_ref[...], la[g * grp:(g + 1) * grp], preferred_element_type=F32)
         for g in range(tg // grp)], axis=0)
    b_last = jnp.concatenate(
        [jnp.broadcast_to(b_all[(ch + 1) * c_sz - 1:(ch + 1) * c_sz], (c_sz, hk)) for ch in range(n_ch)], axis=0)
    q = qk_ref[0, :, 0:hk]
    k = qk_ref[0, :, hk:2 * hk]
    q_in = q * jnp.exp(b_all).astype(BF16)
    k_in = k * jnp.exp(-b_all).astype(BF16)
    k_st = k * jnp.exp(b_last - b_all).astype(BF16)

    lane_head = lax.broadcasted_iota(jnp.int32, (1, hk), 1) // GLA_DK
    ri = lax.broadcasted_iota(jnp.int32, (GLA_HEADS * c_sz, c_sz), 0) % c_sz
    ci = lax.broadcasted_iota(jnp.int32, (GLA_HEADS * c_sz, c_sz), 1)
    causal = ci <= ri
    dec_all = jnp.concatenate(
        [jnp.exp(lax.dot_general(la[g * grp:(g + 1) * grp], chunk_ind_ref[...], (((0,), (0,)), ((), ())),
                                 preferred_element_type=F32)) for g in range(tg // grp)], axis=1)

    chunk_rows = [slice(ch * c_sz, (ch + 1) * c_sz) for ch in range(n_ch)]
    head_rows = [slice(hh * c_sz, (hh + 1) * c_sz) for hh in range(GLA_HEADS)]
    head_cols = [slice(hh * GLA_DV, (hh + 1) * GLA_DV) for hh in range(GLA_HEADS)]
    vs = [v_ref[0, rows, :] for rows in chunk_rows]

    us = []
    for ch, rows in enumerate(chunk_rows):
        kst_t = k_st[rows].T
        us.append(jnp.concatenate(
            [jnp.dot(kst_t[hr], vs[ch][:, vc], preferred_element_type=F32)
             for hr, vc in zip(head_rows, head_cols)], axis=0))
    state = state_ref[...]
    states = []
    for ch in range(n_ch):
        states.append(state.astype(BF16))
        state = dec_all[:, ch * GLA_DV:(ch + 1) * GLA_DV] * state + us[ch]
    state_ref[...] = state
    boths = []
    for ch, rows in enumerate(chunk_rows):
        q_c = q_in[rows]
        qm = jnp.concatenate(
            [jnp.where(lane_head == hh, q_c, jnp.zeros_like(q_c)) for hh in range(GLA_HEADS)], axis=0)
        rhs = jnp.concatenate([states[ch], k_in[rows].T], axis=1)
        boths.append(jnp.dot(qm, rhs, preferred_element_type=F32))
    o_rows = []
    for ch in range(n_ch):
        inter = boths[ch][:, 0:GLA_DV]
        p = jnp.where(causal, boths[ch][:, GLA_DV:GLA_DV + c_sz], 0.0).astype(BF16)
        o_rows.append(jnp.concatenate(
            [jnp.dot(p[hr], vs[ch][:, vc], preferred_element_type=F32) + inter[hr]
             for hr, vc in zip(head_rows, head_cols)], axis=1))

    o = jnp.concatenate(o_rows, axis=0)
    ms = jnp.dot((o * o).astype(BF16), ones_bd_ref[...], preferred_element_type=F32) * (1.0 / GLA_DV)
    y = o * lax.rsqrt(ms + EPS) * gain_ref[...] * za_ref[0].astype(F32)
    o_ref[0] = y.astype(o_ref.dtype)


def _gla(gqk, gv, gza, la, gla_out_gain):
    bsz, s, _ = gqk.shape
    tg = ROWS_GLA
    hk = GLA_HEADS * GLA_DK
    width = GLA_HEADS * GLA_DV
    grp = GLA_CUMSUM_ROWS
    idx = np.arange(grp)
    tril = ((idx[:, None] // GLA_CHUNK == idx[None, :] // GLA_CHUNK) & (idx[None, :] <= idx[:, None]))
    chunk_ind = idx[:, None] // GLA_CHUNK == np.arange(grp // GLA_CHUNK * GLA_DV)[None, :] // GLA_DV
    col = np.arange(width)
    ones_bd = col[:, None] // GLA_DV == col[None, :] // GLA_DV
    act = lambda w: pl.BlockSpec((1, tg, w), lambda b, t: (b, t, 0))
    const = lambda shape: pl.BlockSpec(shape, lambda b, t: (0, 0))
    return pl.pallas_call(
        _gla_kernel,
        grid=(bsz, s // tg),
        in_specs=[act(2 * hk), act(width), act(width), act(hk),
                  const((1, width)), const(tril.shape), const(chunk_ind.shape), const((width, width))],
        out_specs=act(width),
        out_shape=jax.ShapeDtypeStruct((bsz, s, width), BF16),
        scratch_shapes=[pltpu.VMEM((hk, GLA_DV), F32)],
        compiler_params=pltpu.CompilerParams(
            dimension_semantics=("arbitrary", "arbitrary"), vmem_limit_bytes=VMEM_LIMIT),
        name="gla",
    )(gqk, gv, gza, la, gla_out_gain, jnp.asarray(tril, BF16), jnp.asarray(chunk_ind, BF16),
      jnp.asarray(ones_bd, BF16))


def _diff_lambda(lq1_ref, lk1_ref, lq2_ref, lk2_ref, lam_init):
    a = jnp.sum(lq1_ref[...] * lk1_ref[...], axis=-1, keepdims=True)
    b = jnp.sum(lq2_ref[...] * lk2_ref[...], axis=-1, keepdims=True)
    return jnp.exp(a) - jnp.exp(b) + lam_init


def _attn_spans(nq, tk):
    spans = []
    for qi in range(nq):
        if ATTN_SPAN_TILES == 1:
            tile = [(qi, tk * j, tk, None) for j in range(qi)] + [(qi, qi * tk, tk, 0)]
        else:
            tile = [(qi, 2 * tk * j, 2 * tk, None) for j in range(qi // 2)]
            if qi % 2 == 1:
                tile.append((qi, (qi - 1) * tk, 2 * tk, tk))
            else:
                tile.append((qi, qi * tk, tk, 0))
        for i, sp in enumerate(tile):
            spans.append(sp + (i == 0, i == len(tile) - 1))
    return spans


def _attn_bias_tables(tq, tk):
    def split(x, terms):
        parts = []
        for _ in range(terms):
            hi = x.astype(jnp.bfloat16)
            parts.append(hi)
            x = x - hi.astype(np.float32)
        return parts

    key = np.arange(2 * tk, dtype=np.float32)
    kaug = np.zeros((2 * tk, LANES), np.float32)
    kaug[:, 0] = kaug[:, 1] = key % 256
    kaug[:, 2] = kaug[:, 3] = key - key % 256
    kaug[:, 4:7] = 1.0
    slopes = (2.0 ** (-8.0 * np.arange(1, DIFF_HEADS + 1) / DIFF_HEADS) * LOG2E).astype(np.float32)
    qry = np.tile(np.arange(tq, dtype=np.float32), 2)
    qaug = np.zeros((DIFF_HEADS, 2 * tq, LANES), jnp.bfloat16)
    for hh in range(DIFF_HEADS):
        s_hi, s_lo = split(slopes[hh:hh + 1], 2)
        qaug[hh, :, 0] = qaug[hh, :, 2] = s_hi
        qaug[hh, :, 1] = qaug[hh, :, 3] = s_lo
        for lane, part in zip((4, 5, 6), split(-slopes[hh] * qry, 3)):
            qaug[hh, :, lane] = part
    slope_rows = np.broadcast_to(slopes[:, None, None], (DIFF_HEADS, 1, LANES))
    return jnp.asarray(kaug, BF16), jnp.asarray(qaug), jnp.asarray(slope_rows, F32)


def _attn_kernel(q_ref, k_ref, vt_ref, z_ref, gain_ref, kaug_ref, qaug_ref, slope_ref,
                 lq1_ref, lk1_ref, lq2_ref, lk2_ref,
                 o_ref, qs_ref, s_ref, mb_ref, acc_ref, *, lam_init):
    tq = ATTN_TQ
    tk = ATTN_TK
    hw = 2 * DIFF_DH
    nq = q_ref.shape[1] // tq
    slope = slope_ref[0][:, 0:1]
    n_buf = ATTN_AHEAD + 1

    first_half = lax.broadcasted_iota(jnp.int32, (1, hw), 1) < DIFF_DH
    for qi in range(nq):
        q = q_ref[0, qi * tq:(qi + 1) * tq, :]
        zq = jnp.zeros_like(q)
        qs_ref[qi, 0:tq, 0:hw] = jnp.where(first_half, q, zq)
        qs_ref[qi, tq:2 * tq, 0:hw] = jnp.where(first_half, zq, q)
        qs_ref[qi, :, hw:hw + LANES] = qaug_ref[0]

    def rel_pos(nk):
        key = lax.broadcasted_iota(jnp.int32, (nk, 2 * tq), 0)
        qry = lax.broadcasted_iota(jnp.int32, (nk, 2 * tq), 1) % tq
        return qry - key

    lam = _diff_lambda(lq1_ref, lk1_ref, lq2_ref, lk2_ref, lam_init)
    spans = _attn_spans(nq, tk)

    def span_offset(qi, start):
        return -slope * float(qi * tq - start)

    def scores(t):
        qi, start, nk, diag_off, _, _ = spans[t]
        kblk = jnp.concatenate([k_ref[0, start:start + nk, :], kaug_ref[0:nk, :]], axis=1)
        s = lax.dot_general(kblk, qs_ref[qi], (((1,), (1,)), ((), ())), preferred_element_type=F32)
        if diag_off is not None:
            keep = rel_pos(tk) + (diag_off - (nk - tk)) >= 0
            tail = jnp.where(keep, s[nk - tk:nk], -jnp.inf)
            s = tail if nk == tk else jnp.concatenate([s[0:nk - tk], tail], axis=0)
        s_ref[t % n_buf, 0:nk, :] = s
        mb_ref[t % n_buf] = jnp.max(s, axis=0, keepdims=True) + span_offset(qi, start)

    def update(t, m_old):
        qi, start, nk, _, first, last = spans[t]
        s = s_ref[t % n_buf, 0:nk, :]
        m_blk = mb_ref[t % n_buf]
        m_new = m_blk if first else jnp.maximum(m_old, m_blk)
        p = jnp.exp2(s - (m_new - span_offset(qi, start))).astype(BF16)
        vt_ones = jnp.concatenate([vt_ref[0, :, start:start + nk], jnp.ones((SUM_ROWS, nk), BF16)], axis=0)
        pv = jnp.dot(vt_ones, p, preferred_element_type=F32)
        if first:
            acc_ref[...] = pv
        else:
            acc_ref[...] = jnp.exp2(m_old - m_new) * acc_ref[...] + pv
        if last:
            rows = slice(qi * tq, (qi + 1) * tq)
            o_all = acc_ref[0:DIFF_DV, :] / acc_ref[DIFF_DV:DIFF_DV + 1, :]
            o = (o_all[:, :tq] - lam * o_all[:, tq:]).T
            y = o * lax.rsqrt(jnp.mean(o * o, axis=-1, keepdims=True) + EPS)
            y = y * gain_ref[0] * (1.0 - lam_init) * z_ref[0, rows, :].astype(F32)
            o_ref[0, rows, :] = y.astype(o_ref.dtype)
        return m_new

    for t in range(min(ATTN_AHEAD, len(spans))):
        scores(t)
    m = None
    for t in range(len(spans)):
        if t + ATTN_AHEAD < len(spans):
            scores(t + ATTN_AHEAD)
        m = update(t, m)


def _diff_attn(dq, dk, dvt, dz, gain, lq1, lk1, lq2, lk2, lam_init):
    bsz, s, width = dq.shape
    tq = ATTN_TQ
    assert ATTN_TQ == ATTN_TK
    hw = 2 * DIFF_DH
    rowspec = pl.BlockSpec((1, s, hw), lambda b, h: (b, 0, h))
    vtspec = pl.BlockSpec((1, DIFF_DV, s), lambda b, h: (b, h, 0))
    lspec = pl.BlockSpec((1, DIFF_DH), lambda b, h: (0, 0))
    kaug, qaug, slope_rows = _attn_bias_tables(tq, ATTN_TK)
    per_head = lambda rows: pl.BlockSpec((1, rows, LANES), lambda b, h: (h, 0, 0))
    return pl.pallas_call(
        functools.partial(_attn_kernel, lam_init=lam_init),
        grid=(bsz, DIFF_HEADS),
        in_specs=[rowspec, rowspec, vtspec, rowspec, per_head(1),
                  pl.BlockSpec(kaug.shape, lambda b, h: (0, 0)), per_head(2 * tq), per_head(1),
                  lspec, lspec, lspec, lspec],
        out_specs=rowspec,
        out_shape=jax.ShapeDtypeStruct((bsz, s, width), BF16),
        scratch_shapes=[pltpu.VMEM((s // tq, 2 * tq, hw + LANES), BF16),
                        pltpu.VMEM((ATTN_AHEAD + 1, 2 * ATTN_TK, 2 * tq), F32),
                        pltpu.VMEM((ATTN_AHEAD + 1, 1, 2 * tq), F32),
                        pltpu.VMEM((DIFF_DV + SUM_ROWS, 2 * tq), F32)],
        compiler_params=pltpu.CompilerParams(
            dimension_semantics=("arbitrary", "arbitrary"), vmem_limit_bytes=VMEM_LIMIT),
        name="diff_attn",
    )(dq, dk, dvt, dz, gain.reshape(DIFF_HEADS, 1, DIFF_DV), kaug, qaug, slope_rows, lq1, lk1, lq2, lk2)


def _out_proj_kernel(og_ref, od_ref, x_ref, gate_ref, w_ref, fg_ref, o_ref):
    half = og_ref.shape[2]
    mixw = (jnp.dot(og_ref[0], w_ref[0:half, :], preferred_element_type=F32)
            + jnp.dot(od_ref[0], w_ref[half:2 * half, :], preferred_element_type=F32))
    xn = x_ref[0] + gate_ref[0] * mixw
    y = xn * lax.rsqrt(jnp.mean(xn * xn, axis=-1, keepdims=True) + EPS)
    o_ref[0] = y * fg_ref[...]


def _out_proj(o_gla, o_diff, x, mod3, w_out_bf, final_gain):
    bsz, s, d = x.shape
    tm = ROWS_OUT_PROJ
    half = o_gla.shape[2]
    return pl.pallas_call(
        _out_proj_kernel,
        grid=(bsz, s // tm),
        in_specs=[
            pl.BlockSpec((1, tm, half), lambda b, t: (b, t, 0)),
            pl.BlockSpec((1, tm, half), lambda b, t: (b, t, 0)),
            pl.BlockSpec((1, tm, d), lambda b, t: (b, t, 0)),
            pl.BlockSpec((1, 1, d), lambda b, t: (b, 0, 2)),
            pl.BlockSpec((2 * half, d), lambda b, t: (0, 0)),
            pl.BlockSpec((1, d), lambda b, t: (0, 0)),
        ],
        out_specs=pl.BlockSpec((1, tm, d), lambda b, t: (b, t, 0)),
        out_shape=jax.ShapeDtypeStruct((bsz, s, d), F32),
        compiler_params=pltpu.CompilerParams(
            dimension_semantics=("arbitrary", "arbitrary"), vmem_limit_bytes=VMEM_LIMIT),
        name="out_proj",
    )(o_gla, o_diff, x, mod3, w_out_bf, final_gain)


def _prep_w_in(w_in_l):
    hk = GLA_HEADS * GLA_DK
    sizes = (hk, hk, SECTION, SECTION, GLA_RANK, SECTION, SECTION, SECTION, SECTION)
    offs = np.concatenate([[0], np.cumsum(sizes)])
    gq, gk, gv, gz, gr, dq, dk, dv, dz = [w_in_l[:, offs[i]:offs[i + 1]] for i in range(9)]
    gr_pad = jnp.pad(gr, ((0, 0), (0, RANK_PAD - GLA_RANK)))
    cols = [gq * (GLA_DK ** -0.5), gk, gv, gz, dq * (DIFF_DH ** -0.5 * LOG2E), dk, dz, dv, gr_pad]
    return jnp.concatenate(cols, axis=1).astype(BF16)


def kernel(x, c, w_ada, b_ada, norm_gain, w_in, w_gla_gate_up, b_gla_gate, gla_out_gain,
           lambda_q1, lambda_k1, lambda_q2, lambda_k2, diff_out_gain, w_out, final_gain):
    bsz, s, d = x.shape
    depth = w_in.shape[0]
    assert depth == 1, "out_proj applies the final rmsnorm, so exactly one layer is supported"
    for l in range(depth):
        mod = _adaln_mod(c, w_ada[l], b_ada[l])
        mod3 = mod.reshape(bsz, 1, 3 * d)
        w_in_r = _prep_w_in(w_in[l])
        wup_pad = jnp.pad(w_gla_gate_up[l], ((0, RANK_PAD - GLA_RANK), (0, 0))).astype(BF16)
        gqk, gv, gz, dq, dk, dz, dvt, la = _in_proj(
            x, mod3, norm_gain[l].reshape(1, d), w_in_r, wup_pad, b_gla_gate[l].reshape(1, -1))
        o_gla = _gla(gqk, gv, gz, la, gla_out_gain[l].reshape(1, -1))
        lam_init = float(0.8 - 0.6 * np.exp(-0.3 * l))
        o_diff = _diff_attn(dq, dk, dvt, dz, diff_out_gain[l],
                            lambda_q1[l].reshape(1, -1), lambda_k1[l].reshape(1, -1),
                            lambda_q2[l].reshape(1, -1), lambda_k2[l].reshape(1, -1), lam_init)
        x = _out_proj(o_gla, o_diff, x, mod3, w_out[l].astype(BF16), final_gain.reshape(1, d))
    return x
```

```python
import functools
import math

import jax
import jax.numpy as jnp
import numpy as np
from jax import lax
from jax.experimental import pallas as pl
from jax.experimental.pallas import tpu as pltpu

F32 = jnp.float32
BF16 = jnp.bfloat16

EPS = 1e-6
LOG2E = math.log2(math.e)
GLA_HEADS = 4
GLA_DK = 64
GLA_DV = 128
GLA_RANK = 16
GLA_GATE_NORM = 16.0
GLA_CHUNK = 64
DIFF_HEADS = 4
DIFF_DH = 64
DIFF_DV = 128
LANES = 128
RANK_PAD = LANES
SECTION = 512
SUM_ROWS = 16

ROWS_IN_PROJ = 512
IN_PROJ_SUB_ROWS = 512
ROWS_GLA = 512
GLA_CUMSUM_ROWS = 256
ATTN_TQ = 256
ATTN_TK = 256
ATTN_AHEAD = 2
ATTN_SPAN_TILES = 2
BOUND_SLACK = 1.01
FAST_PATH_MIN_DENOM = 2.0 ** -90
ROWS_OUT_PROJ = 1024
VMEM_LIMIT = 48 * 1024 * 1024


def _silu(v):
    return v / (1.0 + jnp.exp(-v))


def _log_sigmoid(v):
    return jnp.minimum(v, 0.0) - jnp.log(1.0 + jnp.exp(-jnp.abs(v)))


def _adaln_kernel(c_ref, w_ref, b_ref, o_ref):
    sc = _silu(c_ref[...]).astype(BF16)
    o_ref[...] = jnp.dot(sc, w_ref[0].astype(BF16), preferred_element_type=F32) + b_ref[0]


def _adaln_mod(c, w_ada, b_ada, layer):
    bsz, d = c.shape
    n = w_ada.shape[2]
    tn = 1024
    return pl.pallas_call(
        _adaln_kernel,
        grid=(n // tn,),
        in_specs=[
            pl.BlockSpec((bsz, d), lambda j: (0, 0)),
            pl.BlockSpec((1, d, tn), lambda j: (layer, 0, j)),
            pl.BlockSpec((1, 1, tn), lambda j: (layer, 0, j)),
        ],
        out_specs=pl.BlockSpec((bsz, tn), lambda j: (0, j)),
        out_shape=jax.ShapeDtypeStruct((bsz, n), F32),
        compiler_params=pltpu.CompilerParams(dimension_semantics=("arbitrary",)),
        name="adaln_mod",
    )(c, w_ada, b_ada.reshape(b_ada.shape[0], 1, n))


def _in_proj_kernel(x_ref, shift_ref, scale_ref, gain_ref, w_ref, wup_ref, bg_ref,
                    gqk_ref, gv_ref, gz_ref, dq_ref, dk_ref, dz_ref, dvt_ref, la_ref,
                    wvt_ref):
    n_sec = 7
    dv_sec = 6
    tm = x_ref.shape[1]
    sub = IN_PROJ_SUB_ROWS

    @pl.when((pl.program_id(0) == 0) & (pl.program_id(1) == 0))
    def _():
        wvt_ref[...] = w_ref[:, dv_sec * SECTION:(dv_sec + 1) * SECTION].T

    g = gain_ref[...] * (1.0 + scale_ref[0])
    for r in range(tm // sub):
        rows = slice(r * sub, (r + 1) * sub)
        x = x_ref[0, rows, :]
        rstd = lax.rsqrt(jnp.mean(x * x, axis=-1, keepdims=True) + EPS)
        h = (x * rstd * g + shift_ref[0]).astype(BF16)

        outs = (gqk_ref, gv_ref, gz_ref, dq_ref, dk_ref, dz_ref)
        for j, o_ref in enumerate(outs):
            w = w_ref[:, j * SECTION:(j + 1) * SECTION]
            sec = jnp.dot(h, w, preferred_element_type=F32)
            if o_ref is gz_ref or o_ref is dz_ref:
                sec = _silu(sec)
            o_ref[0, rows, :] = sec.astype(o_ref.dtype)

        dvt_ref[0, :, rows] = lax.dot_general(wvt_ref[...], h, (((1,), (1,)), ((), ())),
                                              preferred_element_type=F32).astype(dvt_ref.dtype)

        gr = jnp.dot(h, w_ref[:, n_sec * SECTION:n_sec * SECTION + RANK_PAD],
                     preferred_element_type=F32)
        logit = jnp.dot(gr.astype(BF16), wup_ref[...], preferred_element_type=F32) + bg_ref[...]
        la_ref[0, rows, :] = (_log_sigmoid(logit) * (1.0 / GLA_GATE_NORM)).astype(la_ref.dtype)


def _in_proj(x, mod3, norm_gain, w_in_r, wup_pad, b_gate):
    bsz, s, d = x.shape
    tm = ROWS_IN_PROJ
    ncol = w_in_r.shape[1]
    hk = GLA_HEADS * GLA_DK
    act = lambda width: pl.BlockSpec((1, tm, width), lambda b, t: (b, t, 0))
    sec = jax.ShapeDtypeStruct((bsz, s, SECTION), BF16)
    out_shapes = (
        sec,
        sec,
        sec,
        sec,
        sec,
        sec,
        jax.ShapeDtypeStruct((bsz, SECTION, s), BF16),
        jax.ShapeDtypeStruct((bsz, s, hk), BF16),
    )
    const = lambda shape: pl.BlockSpec(shape, lambda b, t: (0, 0))
    return pl.pallas_call(
        _in_proj_kernel,
        grid=(bsz, s // tm),
        in_specs=[
            act(d),
            pl.BlockSpec((1, 1, d), lambda b, t: (b, 0, 0)),
            pl.BlockSpec((1, 1, d), lambda b, t: (b, 0, 1)),
            const((1, d)), const((d, ncol)), const((RANK_PAD, hk)), const((1, hk)),
        ],
        out_specs=(act(SECTION), act(SECTION), act(SECTION), act(SECTION), act(SECTION), act(SECTION),
                   pl.BlockSpec((1, SECTION, tm), lambda b, t: (b, 0, t)), act(hk)),
        out_shape=out_shapes,
        scratch_shapes=[pltpu.VMEM((SECTION, d), BF16)],
        compiler_params=pltpu.CompilerParams(
            dimension_semantics=("arbitrary", "arbitrary"), vmem_limit_bytes=VMEM_LIMIT),
        name="in_proj",
    )(x, mod3, mod3, norm_gain, w_in_r, wup_pad, b_gate)


def _gla_kernel(qk_ref, v_ref, za_ref, la_ref, gain_ref, tril_ref, chunk_ind_ref, ones_bd_ref,
                o_ref, state_ref):
    tg = qk_ref.shape[1]
    hk = GLA_HEADS * GLA_DK
    c_sz = GLA_CHUNK
    n_ch = tg // c_sz

    @pl.when(pl.program_id(1) == 0)
    def _():
        state_ref[...] = jnp.zeros_like(state_ref)

    la = la_ref[0]
    grp = tril_ref.shape[0]
    b_all = jnp.concatenate(
        [jnp.dot(tril_ref[...], la[g * grp:(g + 1) * grp], preferred_element_type=F32)
         for g in range(tg // grp)], axis=0)
    b_last = jnp.concatenate(
        [jnp.broadcast_to(b_all[(ch + 1) * c_sz - 1:(ch + 1) * c_sz], (c_sz, hk)) for ch in range(n_ch)], axis=0)
    q = qk_ref[0, :, 0:hk]
    k = qk_ref[0, :, hk:2 * hk]
    q_in = q * jnp.exp(b_all).astype(BF16)
    k_in = k * jnp.exp(-b_all).astype(BF16)
    k_st = k * jnp.exp(b_last - b_all).astype(BF16)

    lane_head = lax.broadcasted_iota(jnp.int32, (1, hk), 1) // GLA_DK
    ri = lax.broadcasted_iota(jnp.int32, (GLA_HEADS * c_sz, c_sz), 0) % c_sz
    ci = lax.broadcasted_iota(jnp.int32, (GLA_HEADS * c_sz, c_sz), 1)
    causal = ci <= ri
    dec_all = jnp.concatenate(
        [jnp.exp(lax.dot_general(la[g * grp:(g + 1) * grp], chunk_ind_ref[...], (((0,), (0,)), ((), ())),
                                 preferred_element_type=F32)) for g in range(tg // grp)], axis=1)

    chunk_rows = [slice(ch * c_sz, (ch + 1) * c_sz) for ch in range(n_ch)]
    head_rows = [slice(hh * c_sz, (hh + 1) * c_sz) for hh in range(GLA_HEADS)]
    head_cols = [slice(hh * GLA_DV, (hh + 1) * GLA_DV) for hh in range(GLA_HEADS)]
    vs = [v_ref[0, rows, :] for rows in chunk_rows]

    us = []
    for ch, rows in enumerate(chunk_rows):
        kst_t = k_st[rows].T
        us.append(jnp.concatenate(
            [jnp.dot(kst_t[hr], vs[ch][:, vc], preferred_element_type=F32)
             for hr, vc in zip(head_rows, head_cols)], axis=0))
    state = state_ref[...]
    states = []
    for ch in range(n_ch):
        states.append(state.astype(BF16))
        state = dec_all[:, ch * GLA_DV:(ch + 1) * GLA_DV] * state + us[ch]
    state_ref[...] = state
    boths = []
    for ch, rows in enumerate(chunk_rows):
        q_c = q_in[rows]
        qm = jnp.concatenate(
            [jnp.where(lane_head == hh, q_c, jnp.zeros_like(q_c)) for hh in range(GLA_HEADS)], axis=0)
        rhs = jnp.concatenate([states[ch], k_in[rows].T], axis=1)
        boths.append(jnp.dot(qm, rhs, preferred_element_type=F32))
    o_rows = []
    for ch in range(n_ch):
        inter = boths[ch][:, 0:GLA_DV]
        p = jnp.where(causal, boths[ch][:, GLA_DV:GLA_DV + c_sz], 0.0).astype(BF16)
        o_rows.append(jnp.concatenate(
            [jnp.dot(p[hr], vs[ch][:, vc], preferred_element_type=F32) + inter[hr]
             for hr, vc in zip(head_rows, head_cols)], axis=1))

    o = jnp.concatenate(o_rows, axis=0)
    ms = jnp.dot((o * o).astype(BF16), ones_bd_ref[...], preferred_element_type=F32) * (1.0 / GLA_DV)
    y = o * lax.rsqrt(ms + EPS) * gain_ref[...] * za_ref[0].astype(F32)
    o_ref[0] = y.astype(o_ref.dtype)


def _gla(gqk, gv, gza, la, gla_out_gain):
    bsz, s, _ = gqk.shape
    tg = ROWS_GLA
    hk = GLA_HEADS * GLA_DK
    width = GLA_HEADS * GLA_DV
    grp = GLA_CUMSUM_ROWS
    idx = np.arange(grp)
    tril = ((idx[:, None] // GLA_CHUNK == idx[None, :] // GLA_CHUNK) & (idx[None, :] <= idx[:, None]))
    chunk_ind = idx[:, None] // GLA_CHUNK == np.arange(grp // GLA_CHUNK * GLA_DV)[None, :] // GLA_DV
    col = np.arange(width)
    ones_bd = col[:, None] // GLA_DV == col[None, :] // GLA_DV
    act = lambda w: pl.BlockSpec((1, tg, w), lambda b, t: (b, t, 0))
    const = lambda shape: pl.BlockSpec(shape, lambda b, t: (0, 0))
    return pl.pallas_call(
        _gla_kernel,
        grid=(bsz, s // tg),
        in_specs=[act(2 * hk), act(width), act(width), act(hk),
                  const((1, width)), const(tril.shape), const(chunk_ind.shape), const((width, width))],
        out_specs=act(width),
        out_shape=jax.ShapeDtypeStruct((bsz, s, width), BF16),
        scratch_shapes=[pltpu.VMEM((hk, GLA_DV), F32)],
        compiler_params=pltpu.CompilerParams(
            dimension_semantics=("arbitrary", "arbitrary"), vmem_limit_bytes=VMEM_LIMIT),
        name="gla",
    )(gqk, gv, gza, la, gla_out_gain, jnp.asarray(tril, BF16), jnp.asarray(chunk_ind, BF16),
      jnp.asarray(ones_bd, BF16))


def _diff_lambda(lq1_ref, lk1_ref, lq2_ref, lk2_ref, lam_init):
    a = jnp.sum(lq1_ref[...] * lk1_ref[...], axis=-1, keepdims=True)
    b = jnp.sum(lq2_ref[...] * lk2_ref[...], axis=-1, keepdims=True)
    return jnp.exp(a) - jnp.exp(b) + lam_init


def _attn_spans(nq, tk):
    spans = []
    for qi in range(nq):
        if ATTN_SPAN_TILES == 1:
            tile = [(qi, tk * j, tk, None) for j in range(qi)] + [(qi, qi * tk, tk, 0)]
        else:
            tile = [(qi, 2 * tk * j, 2 * tk, None) for j in range(qi // 2)]
            if qi % 2 == 1:
                tile.append((qi, (qi - 1) * tk, 2 * tk, tk))
            else:
                tile.append((qi, qi * tk, tk, 0))
        for i, sp in enumerate(tile):
            spans.append(sp + (i == 0, i == len(tile) - 1))
    return spans


def _attn_bias_tables(tq, tk):
    def split(x, terms):
        parts = []
        for _ in range(terms):
            hi = x.astype(jnp.bfloat16)
            parts.append(hi)
            x = x - hi.astype(np.float32)
        return parts

    key = np.arange(2 * tk, dtype=np.float32)
    kaug = np.zeros((2 * tk, LANES), np.float32)
    kaug[:, 0] = kaug[:, 1] = key % 256
    kaug[:, 2] = kaug[:, 3] = key - key % 256
    kaug[:, 4:7] = 1.0
    slopes = (2.0 ** (-8.0 * np.arange(1, DIFF_HEADS + 1) / DIFF_HEADS) * LOG2E).astype(np.float32)
    qry = np.tile(np.arange(tq, dtype=np.float32), 2)
    qaug = np.zeros((DIFF_HEADS, 2 * tq, LANES), jnp.bfloat16)
    for hh in range(DIFF_HEADS):
        s_hi, s_lo = split(slopes[hh:hh + 1], 2)
        qaug[hh, :, 0] = qaug[hh, :, 2] = s_hi
        qaug[hh, :, 1] = qaug[hh, :, 3] = s_lo
        for lane, part in zip((4, 5, 6), split(-slopes[hh] * qry, 3)):
            qaug[hh, :, lane] = part
    slope_rows = np.broadcast_to(slopes[:, None, None], (DIFF_HEADS, 1, LANES))
    return jnp.asarray(kaug, BF16), jnp.asarray(qaug), jnp.asarray(slope_rows, F32)


def _attn_kernel(q_ref, k_ref, vt_ref, z_ref, gain_ref, kaug_ref, qaug_ref, slope_ref,
                 lq1_ref, lk1_ref, lq2_ref, lk2_ref,
                 o_ref, qs_ref, s_ref, mb_ref, acc_ref, *, lam_init):
    tq = ATTN_TQ
    tk = ATTN_TK
    hw = 2 * DIFF_DH
    nq = q_ref.shape[1] // tq
    slope = slope_ref[0][:, 0:1]
    n_buf = ATTN_AHEAD + 1

    first_half = lax.broadcasted_iota(jnp.int32, (1, hw), 1) < DIFF_DH
    for qi in range(nq):
        q = q_ref[0, qi * tq:(qi + 1) * tq, :]
        zq = jnp.zeros_like(q)
        qs_ref[qi, 0:tq, 0:hw] = jnp.where(first_half, q, zq)
        qs_ref[qi, tq:2 * tq, 0:hw] = jnp.where(first_half, zq, q)
        qs_ref[qi, :, hw:hw + LANES] = qaug_ref[0]

    def rel_pos(nk):
        key = lax.broadcasted_iota(jnp.int32, (nk, 2 * tq), 0)
        qry = lax.broadcasted_iota(jnp.int32, (nk, 2 * tq), 1) % tq
        return qry - key

    lam = _diff_lambda(lq1_ref, lk1_ref, lq2_ref, lk2_ref, lam_init)
    spans = _attn_spans(nq, tk)

    def span_offset(qi, start):
        return -slope * float(qi * tq - start)

    def masked_scores(t):
        qi, start, nk, diag_off, _, _ = spans[t]
        kblk = jnp.concatenate([k_ref[0, start:start + nk, :], kaug_ref[0:nk, :]], axis=1)
        s = lax.dot_general(kblk, qs_ref[qi], (((1,), (1,)), ((), ())), preferred_element_type=F32)
        if diag_off is not None:
            keep = rel_pos(tk) + (diag_off - (nk - tk)) >= 0
            tail = jnp.where(keep, s[nk - tk:nk], -jnp.inf)
            s = tail if nk == tk else jnp.concatenate([s[0:nk - tk], tail], axis=0)
        return s

    def values_and_ones(t):
        _, start, nk, _, _, _ = spans[t]
        return jnp.concatenate([vt_ref[0, :, start:start + nk], jnp.ones((SUM_ROWS, nk), BF16)], axis=0)

    def finish(qi, acc):
        rows = slice(qi * tq, (qi + 1) * tq)
        o_all = acc[0:DIFF_DV, :] / acc[DIFF_DV:DIFF_DV + 1, :]
        o = (o_all[:, :tq] - lam * o_all[:, tq:]).T
        y = o * lax.rsqrt(jnp.mean(o * o, axis=-1, keepdims=True) + EPS)
        y = y * gain_ref[0] * (1.0 - lam_init) * z_ref[0, rows, :].astype(F32)
        o_ref[0, rows, :] = y.astype(o_ref.dtype)

    k_all = k_ref[0]
    r_i = lax.broadcasted_iota(jnp.int32, (hw, LANES), 0)
    c_i = lax.broadcasted_iota(jnp.int32, (hw, LANES), 1)
    half_sel = jnp.where(((c_i == 0) & (r_i < DIFF_DH)) | ((c_i == 1) & (r_i >= DIFF_DH)), 1.0, 0.0).astype(BF16)
    k_norm2 = jnp.max(jnp.dot(k_all * k_all, half_sel, preferred_element_type=F32), axis=0, keepdims=True)
    k_max = jnp.sqrt(k_norm2)
    map_cols = lax.broadcasted_iota(jnp.int32, (1, 2 * tq), 1) < tq
    k_max_row = jnp.where(map_cols, k_max[:, 0:1], k_max[:, 1:2])
    ones_rows = jnp.ones((8, hw), BF16)
    bounds = []
    for qi in range(nq):
        q_sq = qs_ref[qi, :, 0:hw]
        q_norm2 = lax.dot_general(ones_rows, q_sq * q_sq, (((1,), (1,)), ((), ())),
                                  preferred_element_type=F32)[0:1]
        bounds.append(jnp.sqrt(q_norm2) * k_max_row * BOUND_SLACK)
    denom_min = None
    acc = None
    s_next = masked_scores(0)
    for t, (qi, start, _, _, first, last) in enumerate(spans):
        s = s_next
        if t + 1 < len(spans):
            s_next = masked_scores(t + 1)
        p = jnp.exp2(s - (bounds[qi] - span_offset(qi, start))).astype(BF16)
        pv = jnp.dot(values_and_ones(t), p, preferred_element_type=F32)
        acc = pv if first else acc + pv
        if last:
            denom = acc[DIFF_DV:DIFF_DV + 1, :]
            denom_min = denom if denom_min is None else jnp.minimum(denom_min, denom)
            finish(qi, acc)
    fast_ok = jnp.min(denom_min) >= FAST_PATH_MIN_DENOM

    def scores(t):
        qi, start, nk, _, _, _ = spans[t]
        s = masked_scores(t)
        s_ref[t % n_buf, 0:nk, :] = s
        mb_ref[t % n_buf] = jnp.max(s, axis=0, keepdims=True) + span_offset(qi, start)

    def update(t, m_old):
        qi, start, nk, _, first, last = spans[t]
        s = s_ref[t % n_buf, 0:nk, :]
        m_blk = mb_ref[t % n_buf]
        m_new = m_blk if first else jnp.maximum(m_old, m_blk)
        p = jnp.exp2(s - (m_new - span_offset(qi, start))).astype(BF16)
        pv = jnp.dot(values_and_ones(t), p, preferred_element_type=F32)
        if first:
            acc_ref[...] = pv
        else:
            acc_ref[...] = jnp.exp2(m_old - m_new) * acc_ref[...] + pv
        if last:
            finish(qi, acc_ref[...])
        return m_new

    @pl.when(jnp.logical_not(fast_ok))
    def _():
        for t in range(min(ATTN_AHEAD, len(spans))):
            scores(t)
        m = None
        for t in range(len(spans)):
            if t + ATTN_AHEAD < len(spans):
                scores(t + ATTN_AHEAD)
            m = update(t, m)


def _diff_attn(dq, dk, dvt, dz, gain, lq1, lk1, lq2, lk2, lam_init):
    bsz, s, width = dq.shape
    tq = ATTN_TQ
    assert ATTN_TQ == ATTN_TK
    hw = 2 * DIFF_DH
    rowspec = pl.BlockSpec((1, s, hw), lambda b, h: (b, 0, h))
    vtspec = pl.BlockSpec((1, DIFF_DV, s), lambda b, h: (b, h, 0))
    lspec = pl.BlockSpec((1, DIFF_DH), lambda b, h: (0, 0))
    kaug, qaug, slope_rows = _attn_bias_tables(tq, ATTN_TK)
    per_head = lambda rows: pl.BlockSpec((1, rows, LANES), lambda b, h: (h, 0, 0))
    return pl.pallas_call(
        functools.partial(_attn_kernel, lam_init=lam_init),
        grid=(bsz, DIFF_HEADS),
        in_specs=[rowspec, rowspec, vtspec, rowspec, per_head(1),
                  pl.BlockSpec(kaug.shape, lambda b, h: (0, 0)), per_head(2 * tq), per_head(1),
                  lspec, lspec, lspec, lspec],
        out_specs=rowspec,
        out_shape=jax.ShapeDtypeStruct((bsz, s, width), BF16),
        scratch_shapes=[pltpu.VMEM((s // tq, 2 * tq, hw + LANES), BF16),
                        pltpu.VMEM((ATTN_AHEAD + 1, 2 * ATTN_TK, 2 * tq), F32),
                        pltpu.VMEM((ATTN_AHEAD + 1, 1, 2 * tq), F32),
                        pltpu.VMEM((DIFF_DV + SUM_ROWS, 2 * tq), F32)],
        compiler_params=pltpu.CompilerParams(
            dimension_semantics=("arbitrary", "arbitrary"), vmem_limit_bytes=VMEM_LIMIT),
        name="diff_attn",
    )(dq, dk, dvt, dz, gain.reshape(DIFF_HEADS, 1, DIFF_DV), kaug, qaug, slope_rows, lq1, lk1, lq2, lk2)


def _out_proj_kernel(og_ref, od_ref, x_ref, gate_ref, w_ref, fg_ref, o_ref):
    half = og_ref.shape[2]
    mixw = (jnp.dot(og_ref[0], w_ref[0:half, :], preferred_element_type=F32)
            + jnp.dot(od_ref[0], w_ref[half:2 * half, :], preferred_element_type=F32))
    xn = x_ref[0] + gate_ref[0] * mixw
    y = xn * lax.rsqrt(jnp.mean(xn * xn, axis=-1, keepdims=True) + EPS)
    o_ref[0] = y * fg_ref[...]


def _out_proj(o_gla, o_diff, x, mod3, w_out_bf, final_gain):
    bsz, s, d = x.shape
    tm = ROWS_OUT_PROJ
    half = o_gla.shape[2]
    return pl.pallas_call(
        _out_proj_kernel,
        grid=(bsz, s // tm),
        in_specs=[
            pl.BlockSpec((1, tm, half), lambda b, t: (b, t, 0)),
            pl.BlockSpec((1, tm, half), lambda b, t: (b, t, 0)),
            pl.BlockSpec((1, tm, d), lambda b, t: (b, t, 0)),
            pl.BlockSpec((1, 1, d), lambda b, t: (b, 0, 2)),
            pl.BlockSpec((2 * half, d), lambda b, t: (0, 0)),
            pl.BlockSpec((1, d), lambda b, t: (0, 0)),
        ],
        out_specs=pl.BlockSpec((1, tm, d), lambda b, t: (b, t, 0)),
        out_shape=jax.ShapeDtypeStruct((bsz, s, d), F32),
        compiler_params=pltpu.CompilerParams(
            dimension_semantics=("arbitrary", "arbitrary"), vmem_limit_bytes=VMEM_LIMIT),
        name="out_proj",
    )(o_gla, o_diff, x, mod3, w_out_bf, final_gain)


def _prep_w_in(w_in_l):
    hk = GLA_HEADS * GLA_DK
    sizes = (hk, hk, SECTION, SECTION, GLA_RANK, SECTION, SECTION, SECTION, SECTION)
    offs = np.concatenate([[0], np.cumsum(sizes)])
    gq, gk, gv, gz, gr, dq, dk, dv, dz = [w_in_l[:, offs[i]:offs[i + 1]] for i in range(9)]
    gr_pad = jnp.pad(gr, ((0, 0), (0, RANK_PAD - GLA_RANK)))
    cols = [gq * (GLA_DK ** -0.5), gk, gv, gz, dq * (DIFF_DH ** -0.5 * LOG2E), dk, dz, dv, gr_pad]
    return jnp.concatenate(cols, axis=1).astype(BF16)


def kernel(x, c, w_ada, b_ada, norm_gain, w_in, w_gla_gate_up, b_gla_gate, gla_out_gain,
           lambda_q1, lambda_k1, lambda_q2, lambda_k2, diff_out_gain, w_out, final_gain):
    bsz, s, d = x.shape
    depth = w_in.shape[0]
    assert depth == 1, "out_proj applies the final rmsnorm, so exactly one layer is supported"
    for l in range(depth):
        mod = _adaln_mod(c, w_ada, b_ada, l)
        mod3 = mod.reshape(bsz, 1, 3 * d)
        w_in_r = _prep_w_in(w_in[l])
        wup_pad = jnp.pad(w_gla_gate_up[l], ((0, RANK_PAD - GLA_RANK), (0, 0))).astype(BF16)
        gqk, gv, gz, dq, dk, dz, dvt, la = _in_proj(
            x, mod3, norm_gain[l].reshape(1, d), w_in_r, wup_pad, b_gla_gate[l].reshape(1, -1))
        o_gla = _gla(gqk, gv, gz, la, gla_out_gain[l].reshape(1, -1))
        lam_init = float(0.8 - 0.6 * np.exp(-0.3 * l))
        o_diff = _diff_attn(dq, dk, dvt, dz, diff_out_gain[l],
                            lambda_q1[l].reshape(1, -1), lambda_k1[l].reshape(1, -1),
                            lambda_q2[l].reshape(1, -1), lambda_k2[l].reshape(1, -1), lam_init)
        x = _out_proj(o_gla, o_diff, x, mod3, w_out[l].astype(BF16), final_gain.reshape(1, d))
    return x
```

```python
import functools
import math

import jax
import jax.numpy as jnp
import numpy as np
from jax import lax
from jax.experimental import pallas as pl
from jax.experimental.pallas import tpu as pltpu

F32 = jnp.float32
BF16 = jnp.bfloat16

EPS = 1e-6
LOG2E = math.log2(math.e)
GLA_HEADS = 4
GLA_DK = 64
GLA_DV = 128
GLA_RANK = 16
GLA_GATE_NORM = 16.0
GLA_CHUNK = 64
DIFF_HEADS = 4
DIFF_DH = 64
DIFF_DV = 128
LANES = 128
RANK_PAD = LANES
SECTION = 512
SUM_ROWS = 16

ROWS_IN_PROJ = 512
IN_PROJ_SUB_ROWS = 512
ROWS_GLA = 512
GLA_CUMSUM_ROWS = 256
ATTN_TQ = 256
ATTN_TK = 256
ATTN_AHEAD = 2
ATTN_SPAN_TILES = 2
BOUND_SLACK = 1.01
FAST_PATH_MIN_DENOM = 2.0 ** -90
ROWS_OUT_PROJ = 1024
VMEM_LIMIT = 48 * 1024 * 1024


def _silu(v):
    return v / (1.0 + jnp.exp(-v))


def _log_sigmoid(v):
    return jnp.minimum(v, 0.0) - jnp.log(1.0 + jnp.exp(-jnp.abs(v)))


def _adaln_kernel(c_ref, w_ref, b_ref, o_ref):
    sc = _silu(c_ref[...]).astype(BF16)
    o_ref[...] = jnp.dot(sc, w_ref[0].astype(BF16), preferred_element_type=F32) + b_ref[0]


def _adaln_mod(c, w_ada, b_ada, layer):
    bsz, d = c.shape
    n = w_ada.shape[2]
    tn = 1024
    return pl.pallas_call(
        _adaln_kernel,
        grid=(n // tn,),
        in_specs=[
            pl.BlockSpec((bsz, d), lambda j: (0, 0)),
            pl.BlockSpec((1, d, tn), lambda j: (layer, 0, j)),
            pl.BlockSpec((1, 1, tn), lambda j: (layer, 0, j)),
        ],
        out_specs=pl.BlockSpec((bsz, tn), lambda j: (0, j)),
        out_shape=jax.ShapeDtypeStruct((bsz, n), F32),
        compiler_params=pltpu.CompilerParams(dimension_semantics=("arbitrary",)),
        name="adaln_mod",
    )(c, w_ada, b_ada.reshape(b_ada.shape[0], 1, n))


def _in_proj_kernel(x_ref, shift_ref, scale_ref, gain_ref, w_ref, wup_ref, bg_ref,
                    gqk_ref, gv_ref, gz_ref, dq_ref, dk_ref, dz_ref, dvt_ref, la_ref,
                    wvt_ref):
    n_sec = 7
    dv_sec = 6
    tm = x_ref.shape[1]
    sub = IN_PROJ_SUB_ROWS

    @pl.when((pl.program_id(0) == 0) & (pl.program_id(1) == 0))
    def _():
        wvt_ref[...] = w_ref[:, dv_sec * SECTION:(dv_sec + 1) * SECTION].T

    g = gain_ref[...] * (1.0 + scale_ref[0])
    for r in range(tm // sub):
        rows = slice(r * sub, (r + 1) * sub)
        x = x_ref[0, rows, :]
        rstd = lax.rsqrt(jnp.mean(x * x, axis=-1, keepdims=True) + EPS)
        h = (x * rstd * g + shift_ref[0]).astype(BF16)

        outs = (gqk_ref, gv_ref, gz_ref, dq_ref, dk_ref, dz_ref)
        for j, o_ref in enumerate(outs):
            w = w_ref[:, j * SECTION:(j + 1) * SECTION]
            sec = jnp.dot(h, w, preferred_element_type=F32)
            if o_ref is gz_ref or o_ref is dz_ref:
                sec = _silu(sec)
            o_ref[0, rows, :] = sec.astype(o_ref.dtype)

        dvt_ref[0, :, rows] = lax.dot_general(wvt_ref[...], h, (((1,), (1,)), ((), ())),
                                              preferred_element_type=F32).astype(dvt_ref.dtype)

        gr = jnp.dot(h, w_ref[:, n_sec * SECTION:n_sec * SECTION + RANK_PAD],
                     preferred_element_type=F32)
        logit = jnp.dot(gr.astype(BF16), wup_ref[...], preferred_element_type=F32) + bg_ref[...]
        la_ref[0, rows, :] = (_log_sigmoid(logit) * (1.0 / GLA_GATE_NORM)).astype(la_ref.dtype)


def _in_proj(x, mod3, norm_gain, w_in_r, wup_pad, b_gate):
    bsz, s, d = x.shape
    tm = ROWS_IN_PROJ
    ncol = w_in_r.shape[1]
    hk = GLA_HEADS * GLA_DK
    act = lambda width: pl.BlockSpec((1, tm, width), lambda b, t: (b, t, 0))
    sec = jax.ShapeDtypeStruct((bsz, s, SECTION), BF16)
    out_shapes = (
        sec,
        sec,
        sec,
        sec,
        sec,
        sec,
        jax.ShapeDtypeStruct((bsz, SECTION, s), BF16),
        jax.ShapeDtypeStruct((bsz, s, hk), BF16),
    )
    const = lambda shape: pl.BlockSpec(shape, lambda b, t: (0, 0))
    return pl.pallas_call(
        _in_proj_kernel,
        grid=(bsz, s // tm),
        in_specs=[
            act(d),
            pl.BlockSpec((1, 1, d), lambda b, t: (b, 0, 0)),
            pl.BlockSpec((1, 1, d), lambda b, t: (b, 0, 1)),
            const((1, d)), const((d, ncol)), const((RANK_PAD, hk)), const((1, hk)),
        ],
        out_specs=(act(SECTION), act(SECTION), act(SECTION), act(SECTION), act(SECTION), act(SECTION),
                   pl.BlockSpec((1, SECTION, tm), lambda b, t: (b, 0, t)), act(hk)),
        out_shape=out_shapes,
        scratch_shapes=[pltpu.VMEM((SECTION, d), BF16)],
        compiler_params=pltpu.CompilerParams(
            dimension_semantics=("arbitrary", "arbitrary"), vmem_limit_bytes=VMEM_LIMIT),
        name="in_proj",
    )(x, mod3, mod3, norm_gain, w_in_r, wup_pad, b_gate)


def _gla_kernel(qk_ref, v_ref, za_ref, la_ref, gain_ref, tril_ref, chunk_ind_ref, ones_bd_ref,
                o_ref, state_ref):
    tg = qk_ref.shape[1]
    hk = GLA_HEADS * GLA_DK
    c_sz = GLA_CHUNK
    n_ch = tg // c_sz

    @pl.when(pl.program_id(1) == 0)
    def _():
        state_ref[...] = jnp.zeros_like(state_ref)

    la = la_ref[0]
    grp = tril_ref.shape[0]
    b_all = jnp.concatenate(
        [jnp.dot(tril_ref[...], la[g * grp:(g + 1) * grp], preferred_element_type=F32)
         for g in range(tg // grp)], axis=0)
    b_last = jnp.concatenate(
        [jnp.broadcast_to(b_all[(ch + 1) * c_sz - 1:(ch + 1) * c_sz], (c_sz, hk)) for ch in range(n_ch)], axis=0)
    q = qk_ref[0, :, 0:hk]
    k = qk_ref[0, :, hk:2 * hk]
    q_in = q * jnp.exp(b_all).astype(BF16)
    k_in = k * jnp.exp(-b_all).astype(BF16)
    k_st = k * jnp.exp(b_last - b_all).astype(BF16)

    lane_head = lax.broadcasted_iota(jnp.int32, (1, hk), 1) // GLA_DK
    ri = lax.broadcasted_iota(jnp.int32, (GLA_HEADS * c_sz, c_sz), 0) % c_sz
    ci = lax.broadcasted_iota(jnp.int32, (GLA_HEADS * c_sz, c_sz), 1)
    causal = ci <= ri
    dec_all = jnp.concatenate(
        [jnp.exp(lax.dot_general(la[g * grp:(g + 1) * grp], chunk_ind_ref[...], (((0,), (0,)), ((), ())),
                                 preferred_element_type=F32)) for g in range(tg // grp)], axis=1)

    chunk_rows = [slice(ch * c_sz, (ch + 1) * c_sz) for ch in range(n_ch)]
    head_rows = [slice(hh * c_sz, (hh + 1) * c_sz) for hh in range(GLA_HEADS)]
    head_cols = [slice(hh * GLA_DV, (hh + 1) * GLA_DV) for hh in range(GLA_HEADS)]
    vs = [v_ref[0, rows, :] for rows in chunk_rows]

    us = []
    for ch, rows in enumerate(chunk_rows):
        kst_t = k_st[rows].T
        us.append(jnp.concatenate(
            [jnp.dot(kst_t[hr], vs[ch][:, vc], preferred_element_type=F32)
             for hr, vc in zip(head_rows, head_cols)], axis=0))
    state = state_ref[...]
    states = []
    for ch in range(n_ch):
        states.append(state.astype(BF16))
        state = dec_all[:, ch * GLA_DV:(ch + 1) * GLA_DV] * state + us[ch]
    state_ref[...] = state
    boths = []
    for ch, rows in enumerate(chunk_rows):
        q_c = q_in[rows]
        qm = jnp.concatenate(
            [jnp.where(lane_head == hh, q_c, jnp.zeros_like(q_c)) for hh in range(GLA_HEADS)], axis=0)
        rhs = jnp.concatenate([states[ch], k_in[rows].T], axis=1)
        boths.append(jnp.dot(qm, rhs, preferred_element_type=F32))
    o_rows = []
    for ch in range(n_ch):
        inter = boths[ch][:, 0:GLA_DV]
        p = jnp.where(causal, boths[ch][:, GLA_DV:GLA_DV + c_sz], 0.0).astype(BF16)
        o_rows.append(jnp.concatenate(
            [jnp.dot(p[hr], vs[ch][:, vc], preferred_element_type=F32) + inter[hr]
             for hr, vc in zip(head_rows, head_cols)], axis=1))

    o = jnp.concatenate(o_rows, axis=0)
    ms = jnp.dot((o * o).astype(BF16), ones_bd_ref[...], preferred_element_type=F32) * (1.0 / GLA_DV)
    y = o * lax.rsqrt(ms + EPS) * gain_ref[...] * za_ref[0].astype(F32)
    o_ref[0] = y.astype(o_ref.dtype)


def _gla(gqk, gv, gza, la, gla_out_gain):
    bsz, s, _ = gqk.shape
    tg = ROWS_GLA
    hk = GLA_HEADS * GLA_DK
    width = GLA_HEADS * GLA_DV
    grp = GLA_CUMSUM_ROWS
    idx = np.arange(grp)
    tril = ((idx[:, None] // GLA_CHUNK == idx[None, :] // GLA_CHUNK) & (idx[None, :] <= idx[:, None]))
    chunk_ind = idx[:, None] // GLA_CHUNK == np.arange(grp // GLA_CHUNK * GLA_DV)[None, :] // GLA_DV
    col = np.arange(width)
    ones_bd = col[:, None] // GLA_DV == col[None, :] // GLA_DV
    act = lambda w: pl.BlockSpec((1, tg, w), lambda b, t: (b, t, 0))
    const = lambda shape: pl.BlockSpec(shape, lambda b, t: (0, 0))
    return pl.pallas_call(
        _gla_kernel,
        grid=(bsz, s // tg),
        in_specs=[act(2 * hk), act(width), act(width), act(hk),
                  const((1, width)), const(tril.shape), const(chunk_ind.shape), const((width, width))],
        out_specs=act(width),
        out_shape=jax.ShapeDtypeStruct((bsz, s, width), BF16),
        scratch_shapes=[pltpu.VMEM((hk, GLA_DV), F32)],
        compiler_params=pltpu.CompilerParams(
            dimension_semantics=("arbitrary", "arbitrary"), vmem_limit_bytes=VMEM_LIMIT),
        name="gla",
    )(gqk, gv, gza, la, gla_out_gain, jnp.asarray(tril, BF16), jnp.asarray(chunk_ind, BF16),
      jnp.asarray(ones_bd, BF16))


def _diff_lambda(lq1_ref, lk1_ref, lq2_ref, lk2_ref, lam_init):
    a = jnp.sum(lq1_ref[...] * lk1_ref[...], axis=-1, keepdims=True)
    b = jnp.sum(lq2_ref[...] * lk2_ref[...], axis=-1, keepdims=True)
    return jnp.exp(a) - jnp.exp(b) + lam_init


def _attn_spans(nq, tk):
    spans = []
    for qi in range(nq):
        if ATTN_SPAN_TILES == 1:
            tile = [(qi, tk * j, tk, None) for j in range(qi)] + [(qi, qi * tk, tk, 0)]
        else:
            tile = [(qi, 2 * tk * j, 2 * tk, None) for j in range(qi // 2)]
            if qi % 2 == 1:
                tile.append((qi, (qi - 1) * tk, 2 * tk, tk))
            else:
                tile.append((qi, qi * tk, tk, 0))
        for i, sp in enumerate(tile):
            spans.append(sp + (i == 0, i == len(tile) - 1))
    return spans


def _attn_bias_tables(tq, tk):
    def split(x, terms):
        parts = []
        for _ in range(terms):
            hi = x.astype(jnp.bfloat16)
            parts.append(hi)
            x = x - hi.astype(np.float32)
        return parts

    key = np.arange(2 * tk, dtype=np.float32)
    kaug = np.zeros((2 * tk, LANES), np.float32)
    kaug[:, 0] = kaug[:, 1] = key % 256
    kaug[:, 2] = kaug[:, 3] = key - key % 256
    kaug[:, 4:7] = 1.0
    slopes = (2.0 ** (-8.0 * np.arange(1, DIFF_HEADS + 1) / DIFF_HEADS) * LOG2E).astype(np.float32)
    qry = np.tile(np.arange(tq, dtype=np.float32), 2)
    qaug = np.zeros((DIFF_HEADS, 2 * tq, LANES), jnp.bfloat16)
    for hh in range(DIFF_HEADS):
        s_hi, s_lo = split(slopes[hh:hh + 1], 2)
        qaug[hh, :, 0] = qaug[hh, :, 2] = s_hi
        qaug[hh, :, 1] = qaug[hh, :, 3] = s_lo
        for lane, part in zip((4, 5, 6), split(-slopes[hh] * qry, 3)):
            qaug[hh, :, lane] = part
    slope_rows = np.broadcast_to(slopes[:, None, None], (DIFF_HEADS, 1, LANES))
    return jnp.asarray(kaug, BF16), jnp.asarray(qaug), jnp.asarray(slope_rows, F32)


def _attn_kernel(q_ref, k_ref, vt_ref, z_ref, gain_ref, kaug_ref, qaug_ref, slope_ref,
                 lq1_ref, lk1_ref, lq2_ref, lk2_ref,
                 o_ref, qs_ref, s_ref, mb_ref, acc_ref, *, lam_init):
    tq = ATTN_TQ
    tk = ATTN_TK
    hw = 2 * DIFF_DH
    nq = q_ref.shape[1] // tq
    slope = slope_ref[0][:, 0:1]
    n_buf = ATTN_AHEAD + 1

    first_half = lax.broadcasted_iota(jnp.int32, (1, hw), 1) < DIFF_DH
    for qi in range(nq):
        q = q_ref[0, qi * tq:(qi + 1) * tq, :]
        zq = jnp.zeros_like(q)
        qs_ref[qi, 0:tq, 0:hw] = jnp.where(first_half, q, zq)
        qs_ref[qi, tq:2 * tq, 0:hw] = jnp.where(first_half, zq, q)
        qs_ref[qi, :, hw:hw + LANES] = qaug_ref[0]

    def rel_pos(nk):
        key = lax.broadcasted_iota(jnp.int32, (nk, 2 * tq), 0)
        qry = lax.broadcasted_iota(jnp.int32, (nk, 2 * tq), 1) % tq
        return qry - key

    lam = _diff_lambda(lq1_ref, lk1_ref, lq2_ref, lk2_ref, lam_init)
    spans = _attn_spans(nq, tk)

    def span_offset(qi, start):
        return -slope * float(qi * tq - start)

    def masked_scores(t):
        qi, start, nk, diag_off, _, _ = spans[t]
        kblk = jnp.concatenate([k_ref[0, start:start + nk, :], kaug_ref[0:nk, :]], axis=1)
        s = lax.dot_general(kblk, qs_ref[qi], (((1,), (1,)), ((), ())), preferred_element_type=F32)
        if diag_off is not None:
            keep = rel_pos(tk) + (diag_off - (nk - tk)) >= 0
            tail = jnp.where(keep, s[nk - tk:nk], -jnp.inf)
            s = tail if nk == tk else jnp.concatenate([s[0:nk - tk], tail], axis=0)
        return s

    def values_and_ones(t):
        _, start, nk, _, _, _ = spans[t]
        return jnp.concatenate([vt_ref[0, :, start:start + nk], jnp.ones((SUM_ROWS, nk), BF16)], axis=0)

    def finish(qi, acc):
        rows = slice(qi * tq, (qi + 1) * tq)
        o_all = acc[0:DIFF_DV, :] / acc[DIFF_DV:DIFF_DV + 1, :]
        o = (o_all[:, :tq] - lam * o_all[:, tq:]).T
        y = o * lax.rsqrt(jnp.mean(o * o, axis=-1, keepdims=True) + EPS)
        y = y * gain_ref[0] * (1.0 - lam_init) * z_ref[0, rows, :].astype(F32)
        o_ref[0, rows, :] = y.astype(o_ref.dtype)

    k_all = k_ref[0]
    r_i = lax.broadcasted_iota(jnp.int32, (hw, LANES), 0)
    c_i = lax.broadcasted_iota(jnp.int32, (hw, LANES), 1)
    half_sel = jnp.where(((c_i == 0) & (r_i < DIFF_DH)) | ((c_i == 1) & (r_i >= DIFF_DH)), 1.0, 0.0).astype(BF16)
    k_norm2 = jnp.max(jnp.dot(k_all * k_all, half_sel, preferred_element_type=F32), axis=0, keepdims=True)
    k_max = jnp.sqrt(k_norm2)
    map_cols = lax.broadcasted_iota(jnp.int32, (1, 2 * tq), 1) < tq
    k_max_row = jnp.where(map_cols, k_max[:, 0:1], k_max[:, 1:2])
    ones_rows = jnp.ones((8, hw), BF16)
    bounds = []
    for qi in range(nq):
        q_sq = qs_ref[qi, :, 0:hw]
        q_norm2 = lax.dot_general(ones_rows, q_sq * q_sq, (((1,), (1,)), ((), ())),
                                  preferred_element_type=F32)[0:1]
        bounds.append(jnp.sqrt(q_norm2) * k_max_row * BOUND_SLACK)
    denom_min = None
    acc = None
    s_next = masked_scores(0)
    for t, (qi, start, _, _, first, last) in enumerate(spans):
        s = s_next
        if t + 1 < len(spans):
            s_next = masked_scores(t + 1)
        p = jnp.exp2(s - (bounds[qi] - span_offset(qi, start))).astype(BF16)
        pv = jnp.dot(values_and_ones(t), p, preferred_element_type=F32)
        acc = pv if first else acc + pv
        if last:
            denom = acc[DIFF_DV:DIFF_DV + 1, :]
            denom_min = denom if denom_min is None else jnp.minimum(denom_min, denom)
            finish(qi, acc)
    fast_ok = jnp.min(denom_min) >= FAST_PATH_MIN_DENOM

    def scores(t):
        qi, start, nk, _, _, _ = spans[t]
        s = masked_scores(t)
        s_ref[t % n_buf, 0:nk, :] = s
        mb_ref[t % n_buf] = jnp.max(s, axis=0, keepdims=True) + span_offset(qi, start)

    def update(t, m_old):
        qi, start, nk, _, first, last = spans[t]
        s = s_ref[t % n_buf, 0:nk, :]
        m_blk = mb_ref[t % n_buf]
        m_new = m_blk if first else jnp.maximum(m_old, m_blk)
        p = jnp.exp2(s - (m_new - span_offset(qi, start))).astype(BF16)
        pv = jnp.dot(values_and_ones(t), p, preferred_element_type=F32)
        if first:
            acc_ref[...] = pv
        else:
            acc_ref[...] = jnp.exp2(m_old - m_new) * acc_ref[...] + pv
        if last:
            finish(qi, acc_ref[...])
        return m_new

    @pl.when(jnp.logical_not(fast_ok))
    def _():
        for t in range(min(ATTN_AHEAD, len(spans))):
            scores(t)
        m = None
        for t in range(len(spans)):
            if t + ATTN_AHEAD < len(spans):
                scores(t + ATTN_AHEAD)
            m = update(t, m)


def _diff_attn(dq, dk, dvt, dz, gain, lq1, lk1, lq2, lk2, lam_init):
    bsz, s, width = dq.shape
    tq = ATTN_TQ
    assert ATTN_TQ == ATTN_TK
    hw = 2 * DIFF_DH
    rowspec = pl.BlockSpec((1, s, hw), lambda b, h: (b, 0, h))
    vtspec = pl.BlockSpec((1, DIFF_DV, s), lambda b, h: (b, h, 0))
    lspec = pl.BlockSpec((1, DIFF_DH), lambda b, h: (0, 0))
    kaug, qaug, slope_rows = _attn_bias_tables(tq, ATTN_TK)
    per_head = lambda rows: pl.BlockSpec((1, rows, LANES), lambda b, h: (h, 0, 0))
    return pl.pallas_call(
        functools.partial(_attn_kernel, lam_init=lam_init),
        grid=(bsz, DIFF_HEADS),
        in_specs=[rowspec, rowspec, vtspec, rowspec, per_head(1),
                  pl.BlockSpec(kaug.shape, lambda b, h: (0, 0)), per_head(2 * tq), per_head(1),
                  lspec, lspec, lspec, lspec],
        out_specs=rowspec,
        out_shape=jax.ShapeDtypeStruct((bsz, s, width), BF16),
        scratch_shapes=[pltpu.VMEM((s // tq, 2 * tq, hw + LANES), BF16),
                        pltpu.VMEM((ATTN_AHEAD + 1, 2 * ATTN_TK, 2 * tq), F32),
                        pltpu.VMEM((ATTN_AHEAD + 1, 1, 2 * tq), F32),
                        pltpu.VMEM((DIFF_DV + SUM_ROWS, 2 * tq), F32)],
        compiler_params=pltpu.CompilerParams(
            dimension_semantics=("arbitrary", "arbitrary"), vmem_limit_bytes=VMEM_LIMIT),
        name="diff_attn",
    )(dq, dk, dvt, dz, gain.reshape(DIFF_HEADS, 1, DIFF_DV), kaug, qaug, slope_rows, lq1, lk1, lq2, lk2)


def _out_proj_kernel(og_ref, od_ref, x_ref, gate_ref, w_ref, fg_ref, o_ref):
    half = og_ref.shape[2]
    mixw = (jnp.dot(og_ref[0], w_ref[0:half, :], preferred_element_type=F32)
            + jnp.dot(od_ref[0], w_ref[half:2 * half, :], preferred_element_type=F32))
    xn = x_ref[0] + gate_ref[0] * mixw
    y = xn * lax.rsqrt(jnp.mean(xn * xn, axis=-1, keepdims=True) + EPS)
    o_ref[0] = y * fg_ref[...]


def _out_proj(o_gla, o_diff, x, mod3, w_out_bf, final_gain):
    bsz, s, d = x.shape
    tm = ROWS_OUT_PROJ
    half = o_gla.shape[2]
    return pl.pallas_call(
        _out_proj_kernel,
        grid=(bsz, s // tm),
        in_specs=[
            pl.BlockSpec((1, tm, half), lambda b, t: (b, t, 0)),
            pl.BlockSpec((1, tm, half), lambda b, t: (b, t, 0)),
            pl.BlockSpec((1, tm, d), lambda b, t: (b, t, 0)),
            pl.BlockSpec((1, 1, d), lambda b, t: (b, 0, 2)),
            pl.BlockSpec((2 * half, d), lambda b, t: (0, 0)),
            pl.BlockSpec((1, d), lambda b, t: (0, 0)),
        ],
        out_specs=pl.BlockSpec((1, tm, d), lambda b, t: (b, t, 0)),
        out_shape=jax.ShapeDtypeStruct((bsz, s, d), F32),
        compiler_params=pltpu.CompilerParams(
            dimension_semantics=("arbitrary", "arbitrary"), vmem_limit_bytes=VMEM_LIMIT),
        name="out_proj",
    )(o_gla, o_diff, x, mod3, w_out_bf, final_gain)


def _w_in_layout():
    hk = GLA_HEADS * GLA_DK
    names = ("gq", "gk", "gv", "gz", "gr", "dq", "dk", "dv", "dz")
    sizes = (hk, hk, SECTION, SECTION, GLA_RANK, SECTION, SECTION, SECTION, SECTION)
    src = dict(zip(names, np.concatenate([[0], np.cumsum(sizes)[:-1]]).tolist()))
    width = dict(zip(names, sizes))
    scale = {"gq": GLA_DK ** -0.5, "dq": DIFF_DH ** -0.5 * LOG2E}
    pieces, dst = [], 0
    for name in ("gq", "gk", "gv", "gz", "dq", "dk", "dz", "dv", "gr"):
        pieces.append((src[name], dst, width[name], scale.get(name)))
        dst += width[name]
    return pieces, dst + RANK_PAD - GLA_RANK


def _w_prep_kernel(wt_ref, o_ref):
    pieces, _ = _w_in_layout()
    for src, dst, width, scale in pieces:
        piece = wt_ref[0, src:src + width, :]
        if scale is not None:
            piece = piece * scale
        if width < LANES:
            piece = jnp.concatenate([piece, jnp.zeros((LANES - width, piece.shape[1]), piece.dtype)], axis=0)
        o_ref[:, dst:dst + piece.shape[0]] = piece.T.astype(o_ref.dtype)


def _prep_w_in(w_in, layer):
    _, d, n_src = w_in.shape
    _, n_dst = _w_in_layout()
    rows = 256
    return pl.pallas_call(
        _w_prep_kernel,
        grid=(d // rows,),
        in_specs=[pl.BlockSpec((1, n_src, rows), lambda i: (layer, 0, i))],
        out_specs=pl.BlockSpec((rows, n_dst), lambda i: (i, 0)),
        out_shape=jax.ShapeDtypeStruct((d, n_dst), BF16),
        compiler_params=pltpu.CompilerParams(dimension_semantics=("arbitrary",), vmem_limit_bytes=VMEM_LIMIT),
        name="w_in_prep",
    )(jnp.swapaxes(w_in, 1, 2))


def kernel(x, c, w_ada, b_ada, norm_gain, w_in, w_gla_gate_up, b_gla_gate, gla_out_gain,
           lambda_q1, lambda_k1, lambda_q2, lambda_k2, diff_out_gain, w_out, final_gain):
    bsz, s, d = x.shape
    depth = w_in.shape[0]
    assert depth == 1, "out_proj applies the final rmsnorm, so exactly one layer is supported"
    for l in range(depth):
        mod = _adaln_mod(c, w_ada, b_ada, l)
        mod3 = mod.reshape(bsz, 1, 3 * d)
        w_in_r = _prep_w_in(w_in, l)
        wup_pad = jnp.pad(w_gla_gate_up[l], ((0, RANK_PAD - GLA_RANK), (0, 0))).astype(BF16)
        gqk, gv, gz, dq, dk, dz, dvt, la = _in_proj(
            x, mod3, norm_gain[l].reshape(1, d), w_in_r, wup_pad, b_gla_gate[l].reshape(1, -1))
        o_gla = _gla(gqk, gv, gz, la, gla_out_gain[l].reshape(1, -1))
        lam_init = float(0.8 - 0.6 * np.exp(-0.3 * l))
        o_diff = _diff_attn(dq, dk, dvt, dz, diff_out_gain[l],
                            lambda_q1[l].reshape(1, -1), lambda_k1[l].reshape(1, -1),
                            lambda_q2[l].reshape(1, -1), lambda_k2[l].reshape(1, -1), lam_init)
        x = _out_proj(o_gla, o_diff, x, mod3, w_out[l].astype(BF16), final_gain.reshape(1, d))
    return x
```

```python
import functools
import math

import jax
import jax.numpy as jnp
import numpy as np
from jax import lax
from jax.experimental import pallas as pl
from jax.experimental.pallas import tpu as pltpu

F32 = jnp.float32
BF16 = jnp.bfloat16

EPS = 1e-6
LOG2E = math.log2(math.e)
GLA_HEADS = 4
GLA_DK = 64
GLA_DV = 128
GLA_RANK = 16
GLA_GATE_NORM = 16.0
GLA_CHUNK = 64
DIFF_HEADS = 4
DIFF_DH = 64
DIFF_DV = 128
LANES = 128
RANK_PAD = LANES
SECTION = 512
SUM_ROWS = 16

ROWS_IN_PROJ = 512
ROWS_GLA = 512
GLA_CUMSUM_ROWS = 256
ATTN_TQ = 256
ATTN_TK = 256
ATTN_AHEAD = 2
ATTN_SPAN_TILES = 2
BOUND_SLACK = 1.01
FAST_PATH_MIN_DENOM = 2.0 ** -90
ROWS_OUT_PROJ = 1024
VMEM_LIMIT = 48 * 1024 * 1024


def _silu(v):
    return v / (1.0 + jnp.exp(-v))


def _log_sigmoid(v):
    return jnp.minimum(v, 0.0) - jnp.log(1.0 + jnp.exp(-jnp.abs(v)))


def _adaln_kernel(c_ref, w_ref, b_ref, o_ref):
    sc = _silu(c_ref[...]).astype(BF16)
    o_ref[...] = jnp.dot(sc, w_ref[0].astype(BF16), preferred_element_type=F32) + b_ref[0]


def _adaln_mod(c, w_ada, b_ada, layer):
    bsz, d = c.shape
    n = w_ada.shape[2]
    tn = 1024
    return pl.pallas_call(
        _adaln_kernel,
        grid=(n // tn,),
        in_specs=[
            pl.BlockSpec((bsz, d), lambda j: (0, 0)),
            pl.BlockSpec((1, d, tn), lambda j: (layer, 0, j)),
            pl.BlockSpec((1, 1, tn), lambda j: (layer, 0, j)),
        ],
        out_specs=pl.BlockSpec((bsz, tn), lambda j: (0, j)),
        out_shape=jax.ShapeDtypeStruct((bsz, n), F32),
        compiler_params=pltpu.CompilerParams(dimension_semantics=("arbitrary",)),
        name="adaln_mod",
    )(c, w_ada, b_ada.reshape(b_ada.shape[0], 1, n))


def _shift_proj_kernel(shift_ref, w_ref, o_ref):
    o_ref[...] = jnp.dot(shift_ref[...].astype(BF16), w_ref[...], preferred_element_type=F32)


def _shift_proj(mod, w_in_r):
    bsz = mod.shape[0]
    d, ncol = w_in_r.shape
    return pl.pallas_call(
        _shift_proj_kernel,
        grid=(1,),
        in_specs=[pl.BlockSpec((bsz, d), lambda i: (0, 0)), pl.BlockSpec((d, ncol), lambda i: (0, 0))],
        out_specs=pl.BlockSpec((bsz, ncol), lambda i: (0, 0)),
        out_shape=jax.ShapeDtypeStruct((bsz, ncol), F32),
        compiler_params=pltpu.CompilerParams(dimension_semantics=("arbitrary",), vmem_limit_bytes=VMEM_LIMIT),
        name="shift_proj",
    )(mod, w_in_r)


def _in_proj_kernel(x_ref, shift_ref, scale_ref, gain_ref, sw_ref, w_ref, wup_ref, bg_ref,
                    gqk_ref, gv_ref, gz_ref, dq_ref, dk_ref, dz_ref, dvt_ref, la_ref,
                    wvt_ref):
    n_sec = 7
    dv_sec = 6

    @pl.when((pl.program_id(0) == 0) & (pl.program_id(1) == 0))
    def _():
        wvt_ref[...] = w_ref[:, dv_sec * SECTION:(dv_sec + 1) * SECTION].T

    x = x_ref[0]
    g = gain_ref[...] * (1.0 + scale_ref[0])
    xg = (x * g).astype(BF16)
    rstd = lax.rsqrt(jnp.mean(x * x, axis=-1, keepdims=True) + EPS)
    rstd_sec = pltpu.repeat(jnp.broadcast_to(rstd, (x.shape[0], LANES)), SECTION // LANES, axis=1)

    outs = (gqk_ref, gv_ref, gz_ref, dq_ref, dk_ref, dz_ref)
    for j, o_ref in enumerate(outs):
        cols = slice(j * SECTION, (j + 1) * SECTION)
        sec = jnp.dot(xg, w_ref[:, cols], preferred_element_type=F32) * rstd_sec + sw_ref[0][:, cols]
        if o_ref is gz_ref or o_ref is dz_ref:
            sec = _silu(sec)
        o_ref[0] = sec.astype(o_ref.dtype)

    h = (x * rstd * g + shift_ref[0]).astype(BF16)
    dvt_ref[0] = lax.dot_general(wvt_ref[...], h, (((1,), (1,)), ((), ())),
                                 preferred_element_type=F32).astype(dvt_ref.dtype)
    gr = jnp.dot(h, w_ref[:, n_sec * SECTION:n_sec * SECTION + RANK_PAD],
                 preferred_element_type=F32)
    logit = jnp.dot(gr.astype(BF16), wup_ref[...], preferred_element_type=F32) + bg_ref[...]
    la_ref[0] = (_log_sigmoid(logit) * (1.0 / GLA_GATE_NORM)).astype(la_ref.dtype)


def _in_proj(x, mod3, norm_gain, shift_w, w_in_r, wup_pad, b_gate):
    bsz, s, d = x.shape
    tm = ROWS_IN_PROJ
    ncol = w_in_r.shape[1]
    hk = GLA_HEADS * GLA_DK
    act = lambda width: pl.BlockSpec((1, tm, width), lambda b, t: (b, t, 0))
    sec = jax.ShapeDtypeStruct((bsz, s, SECTION), BF16)
    out_shapes = (
        sec,
        sec,
        sec,
        sec,
        sec,
        sec,
        jax.ShapeDtypeStruct((bsz, SECTION, s), BF16),
        jax.ShapeDtypeStruct((bsz, s, hk), BF16),
    )
    const = lambda shape: pl.BlockSpec(shape, lambda b, t: (0, 0))
    return pl.pallas_call(
        _in_proj_kernel,
        grid=(bsz, s // tm),
        in_specs=[
            act(d),
            pl.BlockSpec((1, 1, d), lambda b, t: (b, 0, 0)),
            pl.BlockSpec((1, 1, d), lambda b, t: (b, 0, 1)),
            const((1, d)),
            pl.BlockSpec((1, 1, ncol), lambda b, t: (b, 0, 0)),
            const((d, ncol)), const((RANK_PAD, hk)), const((1, hk)),
        ],
        out_specs=(act(SECTION), act(SECTION), act(SECTION), act(SECTION), act(SECTION), act(SECTION),
                   pl.BlockSpec((1, SECTION, tm), lambda b, t: (b, 0, t)), act(hk)),
        out_shape=out_shapes,
        scratch_shapes=[pltpu.VMEM((SECTION, d), BF16)],
        compiler_params=pltpu.CompilerParams(
            dimension_semantics=("arbitrary", "arbitrary"), vmem_limit_bytes=VMEM_LIMIT),
        name="in_proj",
    )(x, mod3, mod3, norm_gain, shift_w, w_in_r, wup_pad, b_gate)


def _gla_kernel(qk_ref, v_ref, za_ref, la_ref, gain_ref, tril_ref, chunk_ind_ref, ones_bd_ref,
                o_ref, state_ref):
    tg = qk_ref.shape[1]
    hk = GLA_HEADS * GLA_DK
    c_sz = GLA_CHUNK
    n_ch = tg // c_sz

    @pl.when(pl.program_id(1) == 0)
    def _():
        state_ref[...] = jnp.zeros_like(state_ref)

    la = la_ref[0]
    grp = tril_ref.shape[0]
    b_all = jnp.concatenate(
        [jnp.dot(tril_ref[...], la[g * grp:(g + 1) * grp], preferred_element_type=F32)
         for g in range(tg // grp)], axis=0)
    b_last = jnp.concatenate(
        [jnp.broadcast_to(b_all[(ch + 1) * c_sz - 1:(ch + 1) * c_sz], (c_sz, hk)) for ch in range(n_ch)], axis=0)
    q = qk_ref[0, :, 0:hk]
    k = qk_ref[0, :, hk:2 * hk]
    q_in = q * jnp.exp(b_all).astype(BF16)
    k_in = k * jnp.exp(-b_all).astype(BF16)
    k_st = k * jnp.exp(b_last - b_all).astype(BF16)

    lane_head = lax.broadcasted_iota(jnp.int32, (1, hk), 1) // GLA_DK
    ri = lax.broadcasted_iota(jnp.int32, (GLA_HEADS * c_sz, c_sz), 0) % c_sz
    ci = lax.broadcasted_iota(jnp.int32, (GLA_HEADS * c_sz, c_sz), 1)
    causal = ci <= ri
    dec_all = jnp.concatenate(
        [jnp.exp(lax.dot_general(la[g * grp:(g + 1) * grp], chunk_ind_ref[...], (((0,), (0,)), ((), ())),
                                 preferred_element_type=F32)) for g in range(tg // grp)], axis=1)

    chunk_rows = [slice(ch * c_sz, (ch + 1) * c_sz) for ch in range(n_ch)]
    head_rows = [slice(hh * c_sz, (hh + 1) * c_sz) for hh in range(GLA_HEADS)]
    head_cols = [slice(hh * GLA_DV, (hh + 1) * GLA_DV) for hh in range(GLA_HEADS)]
    vs = [v_ref[0, rows, :] for rows in chunk_rows]

    us = []
    for ch, rows in enumerate(chunk_rows):
        kst_t = k_st[rows].T
        us.append(jnp.concatenate(
            [jnp.dot(kst_t[hr], vs[ch][:, vc], preferred_element_type=F32)
             for hr, vc in zip(head_rows, head_cols)], axis=0))
    state = state_ref[...]
    states = []
    for ch in range(n_ch):
        states.append(state.astype(BF16))
        state = dec_all[:, ch * GLA_DV:(ch + 1) * GLA_DV] * state + us[ch]
    state_ref[...] = state
    boths = []
    for ch, rows in enumerate(chunk_rows):
        q_c = q_in[rows]
        qm = jnp.concatenate(
            [jnp.where(lane_head == hh, q_c, jnp.zeros_like(q_c)) for hh in range(GLA_HEADS)], axis=0)
        rhs = jnp.concatenate([states[ch], k_in[rows].T], axis=1)
        boths.append(jnp.dot(qm, rhs, preferred_element_type=F32))
    o_rows = []
    for ch in range(n_ch):
        inter = boths[ch][:, 0:GLA_DV]
        p = jnp.where(causal, boths[ch][:, GLA_DV:GLA_DV + c_sz], 0.0).astype(BF16)
        o_rows.append(jnp.concatenate(
            [jnp.dot(p[hr], vs[ch][:, vc], preferred_element_type=F32) + inter[hr]
             for hr, vc in zip(head_rows, head_cols)], axis=1))

    o = jnp.concatenate(o_rows, axis=0)
    ms = jnp.dot((o * o).astype(BF16), ones_bd_ref[...], preferred_element_type=F32) * (1.0 / GLA_DV)
    y = o * lax.rsqrt(ms + EPS) * gain_ref[...] * za_ref[0].astype(F32)
    o_ref[0] = y.astype(o_ref.dtype)


def _gla(gqk, gv, gza, la, gla_out_gain):
    bsz, s, _ = gqk.shape
    tg = ROWS_GLA
    hk = GLA_HEADS * GLA_DK
    width = GLA_HEADS * GLA_DV
    grp = GLA_CUMSUM_ROWS
    idx = np.arange(grp)
    tril = ((idx[:, None] // GLA_CHUNK == idx[None, :] // GLA_CHUNK) & (idx[None, :] <= idx[:, None]))
    chunk_ind = idx[:, None] // GLA_CHUNK == np.arange(grp // GLA_CHUNK * GLA_DV)[None, :] // GLA_DV
    col = np.arange(width)
    ones_bd = col[:, None] // GLA_DV == col[None, :] // GLA_DV
    act = lambda w: pl.BlockSpec((1, tg, w), lambda b, t: (b, t, 0))
    const = lambda shape: pl.BlockSpec(shape, lambda b, t: (0, 0))
    return pl.pallas_call(
        _gla_kernel,
        grid=(bsz, s // tg),
        in_specs=[act(2 * hk), act(width), act(width), act(hk),
                  const((1, width)), const(tril.shape), const(chunk_ind.shape), const((width, width))],
        out_specs=act(width),
        out_shape=jax.ShapeDtypeStruct((bsz, s, width), BF16),
        scratch_shapes=[pltpu.VMEM((hk, GLA_DV), F32)],
        compiler_params=pltpu.CompilerParams(
            dimension_semantics=("arbitrary", "arbitrary"), vmem_limit_bytes=VMEM_LIMIT),
        name="gla",
    )(gqk, gv, gza, la, gla_out_gain, jnp.asarray(tril, BF16), jnp.asarray(chunk_ind, BF16),
      jnp.asarray(ones_bd, BF16))


def _diff_lambda(lq1_ref, lk1_ref, lq2_ref, lk2_ref, lam_init):
    a = jnp.sum(lq1_ref[...] * lk1_ref[...], axis=-1, keepdims=True)
    b = jnp.sum(lq2_ref[...] * lk2_ref[...], axis=-1, keepdims=True)
    return jnp.exp(a) - jnp.exp(b) + lam_init


def _attn_spans(nq, tk):
    spans = []
    for qi in range(nq):
        if ATTN_SPAN_TILES == 1:
            tile = [(qi, tk * j, tk, None) for j in range(qi)] + [(qi, qi * tk, tk, 0)]
        else:
            tile = [(qi, 2 * tk * j, 2 * tk, None) for j in range(qi // 2)]
            if qi % 2 == 1:
                tile.append((qi, (qi - 1) * tk, 2 * tk, tk))
            else:
                tile.append((qi, qi * tk, tk, 0))
        for i, sp in enumerate(tile):
            spans.append(sp + (i == 0, i == len(tile) - 1))
    return spans


def _attn_bias_tables(tq, tk):
    def split(x, terms):
        parts = []
        for _ in range(terms):
            hi = x.astype(jnp.bfloat16)
            parts.append(hi)
            x = x - hi.astype(np.float32)
        return parts

    key = np.arange(2 * tk, dtype=np.float32)
    kaug = np.zeros((2 * tk, LANES), np.float32)
    kaug[:, 0] = kaug[:, 1] = key % 256
    kaug[:, 2] = kaug[:, 3] = key - key % 256
    kaug[:, 4:7] = 1.0
    slopes = (2.0 ** (-8.0 * np.arange(1, DIFF_HEADS + 1) / DIFF_HEADS) * LOG2E).astype(np.float32)
    qry = np.tile(np.arange(tq, dtype=np.float32), 2)
    qaug = np.zeros((DIFF_HEADS, 2 * tq, LANES), jnp.bfloat16)
    for hh in range(DIFF_HEADS):
        s_hi, s_lo = split(slopes[hh:hh + 1], 2)
        qaug[hh, :, 0] = qaug[hh, :, 2] = s_hi
        qaug[hh, :, 1] = qaug[hh, :, 3] = s_lo
        for lane, part in zip((4, 5, 6), split(-slopes[hh] * qry, 3)):
            qaug[hh, :, lane] = part
    slope_rows = np.broadcast_to(slopes[:, None, None], (DIFF_HEADS, 1, LANES))
    return jnp.asarray(kaug, BF16), jnp.asarray(qaug), jnp.asarray(slope_rows, F32)


def _attn_kernel(q_ref, k_ref, vt_ref, z_ref, gain_ref, kaug_ref, qaug_ref, slope_ref,
                 lq1_ref, lk1_ref, lq2_ref, lk2_ref,
                 o_ref, qs_ref, s_ref, mb_ref, acc_ref, *, lam_init):
    tq = ATTN_TQ
    tk = ATTN_TK
    hw = 2 * DIFF_DH
    nq = q_ref.shape[1] // tq
    slope = slope_ref[0][:, 0:1]
    n_buf = ATTN_AHEAD + 1

    first_half = lax.broadcasted_iota(jnp.int32, (1, hw), 1) < DIFF_DH
    for qi in range(nq):
        q = q_ref[0, qi * tq:(qi + 1) * tq, :]
        zq = jnp.zeros_like(q)
        qs_ref[qi, 0:tq, 0:hw] = jnp.where(first_half, q, zq)
        qs_ref[qi, tq:2 * tq, 0:hw] = jnp.where(first_half, zq, q)
        qs_ref[qi, :, hw:hw + LANES] = qaug_ref[0]

    def rel_pos(nk):
        key = lax.broadcasted_iota(jnp.int32, (nk, 2 * tq), 0)
        qry = lax.broadcasted_iota(jnp.int32, (nk, 2 * tq), 1) % tq
        return qry - key

    lam = _diff_lambda(lq1_ref, lk1_ref, lq2_ref, lk2_ref, lam_init)
    spans = _attn_spans(nq, tk)

    def span_offset(qi, start):
        return -slope * float(qi * tq - start)

    def masked_scores(t):
        qi, start, nk, diag_off, _, _ = spans[t]
        kblk = jnp.concatenate([k_ref[0, start:start + nk, :], kaug_ref[0:nk, :]], axis=1)
        s = lax.dot_general(kblk, qs_ref[qi], (((1,), (1,)), ((), ())), preferred_element_type=F32)
        if diag_off is not None:
            keep = rel_pos(tk) + (diag_off - (nk - tk)) >= 0
            tail = jnp.where(keep, s[nk - tk:nk], -jnp.inf)
            s = tail if nk == tk else jnp.concatenate([s[0:nk - tk], tail], axis=0)
        return s

    def values_and_ones(t):
        _, start, nk, _, _, _ = spans[t]
        return jnp.concatenate([vt_ref[0, :, start:start + nk], jnp.ones((SUM_ROWS, nk), BF16)], axis=0)

    def finish(qi, acc):
        rows = slice(qi * tq, (qi + 1) * tq)
        o_all = acc[0:DIFF_DV, :] / acc[DIFF_DV:DIFF_DV + 1, :]
        o = (o_all[:, :tq] - lam * o_all[:, tq:]).T
        y = o * lax.rsqrt(jnp.mean(o * o, axis=-1, keepdims=True) + EPS)
        y = y * gain_ref[0] * (1.0 - lam_init) * z_ref[0, rows, :].astype(F32)
        o_ref[0, rows, :] = y.astype(o_ref.dtype)

    k_all = k_ref[0]
    r_i = lax.broadcasted_iota(jnp.int32, (hw, LANES), 0)
    c_i = lax.broadcasted_iota(jnp.int32, (hw, LANES), 1)
    half_sel = jnp.where(((c_i == 0) & (r_i < DIFF_DH)) | ((c_i == 1) & (r_i >= DIFF_DH)), 1.0, 0.0).astype(BF16)
    k_norm2 = jnp.max(jnp.dot(k_all * k_all, half_sel, preferred_element_type=F32), axis=0, keepdims=True)
    k_max = jnp.sqrt(k_norm2)
    map_cols = lax.broadcasted_iota(jnp.int32, (1, 2 * tq), 1) < tq
    k_max_row = jnp.where(map_cols, k_max[:, 0:1], k_max[:, 1:2])
    ones_rows = jnp.ones((8, hw), BF16)
    bounds = []
    for qi in range(nq):
        q_sq = qs_ref[qi, :, 0:hw]
        q_norm2 = lax.dot_general(ones_rows, q_sq * q_sq, (((1,), (1,)), ((), ())),
                                  preferred_element_type=F32)[0:1]
        bounds.append(jnp.sqrt(q_norm2) * k_max_row * BOUND_SLACK)
    denom_min = None
    acc = None
    s_next = masked_scores(0)
    for t, (qi, start, _, _, first, last) in enumerate(spans):
        s = s_next
        if t + 1 < len(spans):
            s_next = masked_scores(t + 1)
        p = jnp.exp2(s - (bounds[qi] - span_offset(qi, start))).astype(BF16)
        pv = jnp.dot(values_and_ones(t), p, preferred_element_type=F32)
        acc = pv if first else acc + pv
        if last:
            denom = acc[DIFF_DV:DIFF_DV + 1, :]
            denom_min = denom if denom_min is None else jnp.minimum(denom_min, denom)
            finish(qi, acc)
    fast_ok = jnp.min(denom_min) >= FAST_PATH_MIN_DENOM

    def scores(t):
        qi, start, nk, _, _, _ = spans[t]
        s = masked_scores(t)
        s_ref[t % n_buf, 0:nk, :] = s
        mb_ref[t % n_buf] = jnp.max(s, axis=0, keepdims=True) + span_offset(qi, start)

    def update(t, m_old):
        qi, start, nk, _, first, last = spans[t]
        s = s_ref[t % n_buf, 0:nk, :]
        m_blk = mb_ref[t % n_buf]
        m_new = m_blk if first else jnp.maximum(m_old, m_blk)
        p = jnp.exp2(s - (m_new - span_offset(qi, start))).astype(BF16)
        pv = jnp.dot(values_and_ones(t), p, preferred_element_type=F32)
        if first:
            acc_ref[...] = pv
        else:
            acc_ref[...] = jnp.exp2(m_old - m_new) * acc_ref[...] + pv
        if last:
            finish(qi, acc_ref[...])
        return m_new

    @pl.when(jnp.logical_not(fast_ok))
    def _():
        for t in range(min(ATTN_AHEAD, len(spans))):
            scores(t)
        m = None
        for t in range(len(spans)):
            if t + ATTN_AHEAD < len(spans):
                scores(t + ATTN_AHEAD)
            m = update(t, m)


def _diff_attn(dq, dk, dvt, dz, gain, lq1, lk1, lq2, lk2, lam_init):
    bsz, s, width = dq.shape
    tq = ATTN_TQ
    assert ATTN_TQ == ATTN_TK
    hw = 2 * DIFF_DH
    rowspec = pl.BlockSpec((1, s, hw), lambda b, h: (b, 0, h))
    vtspec = pl.BlockSpec((1, DIFF_DV, s), lambda b, h: (b, h, 0))
    lspec = pl.BlockSpec((1, DIFF_DH), lambda b, h: (0, 0))
    kaug, qaug, slope_rows = _attn_bias_tables(tq, ATTN_TK)
    per_head = lambda rows: pl.BlockSpec((1, rows, LANES), lambda b, h: (h, 0, 0))
    return pl.pallas_call(
        functools.partial(_attn_kernel, lam_init=lam_init),
        grid=(bsz, DIFF_HEADS),
        in_specs=[rowspec, rowspec, vtspec, rowspec, per_head(1),
                  pl.BlockSpec(kaug.shape, lambda b, h: (0, 0)), per_head(2 * tq), per_head(1),
                  lspec, lspec, lspec, lspec],
        out_specs=rowspec,
        out_shape=jax.ShapeDtypeStruct((bsz, s, width), BF16),
        scratch_shapes=[pltpu.VMEM((s // tq, 2 * tq, hw + LANES), BF16),
                        pltpu.VMEM((ATTN_AHEAD + 1, 2 * ATTN_TK, 2 * tq), F32),
                        pltpu.VMEM((ATTN_AHEAD + 1, 1, 2 * tq), F32),
                        pltpu.VMEM((DIFF_DV + SUM_ROWS, 2 * tq), F32)],
        compiler_params=pltpu.CompilerParams(
            dimension_semantics=("arbitrary", "arbitrary"), vmem_limit_bytes=VMEM_LIMIT),
        name="diff_attn",
    )(dq, dk, dvt, dz, gain.reshape(DIFF_HEADS, 1, DIFF_DV), kaug, qaug, slope_rows, lq1, lk1, lq2, lk2)


def _out_proj_kernel(og_ref, od_ref, x_ref, gate_ref, w_ref, fg_ref, o_ref):
    half = og_ref.shape[2]
    mixw = (jnp.dot(og_ref[0], w_ref[0:half, :], preferred_element_type=F32)
            + jnp.dot(od_ref[0], w_ref[half:2 * half, :], preferred_element_type=F32))
    xn = x_ref[0] + gate_ref[0] * mixw
    y = xn * lax.rsqrt(jnp.mean(xn * xn, axis=-1, keepdims=True) + EPS)
    o_ref[0] = y * fg_ref[...]


def _out_proj(o_gla, o_diff, x, mod3, w_out_bf, final_gain):
    bsz, s, d = x.shape
    tm = ROWS_OUT_PROJ
    half = o_gla.shape[2]
    return pl.pallas_call(
        _out_proj_kernel,
        grid=(bsz, s // tm),
        in_specs=[
            pl.BlockSpec((1, tm, half), lambda b, t: (b, t, 0)),
            pl.BlockSpec((1, tm, half), lambda b, t: (b, t, 0)),
            pl.BlockSpec((1, tm, d), lambda b, t: (b, t, 0)),
            pl.BlockSpec((1, 1, d), lambda b, t: (b, 0, 2)),
            pl.BlockSpec((2 * half, d), lambda b, t: (0, 0)),
            pl.BlockSpec((1, d), lambda b, t: (0, 0)),
        ],
        out_specs=pl.BlockSpec((1, tm, d), lambda b, t: (b, t, 0)),
        out_shape=jax.ShapeDtypeStruct((bsz, s, d), F32),
        compiler_params=pltpu.CompilerParams(
            dimension_semantics=("arbitrary", "arbitrary"), vmem_limit_bytes=VMEM_LIMIT),
        name="out_proj",
    )(o_gla, o_diff, x, mod3, w_out_bf, final_gain)


def _w_in_layout():
    hk = GLA_HEADS * GLA_DK
    names = ("gq", "gk", "gv", "gz", "gr", "dq", "dk", "dv", "dz")
    sizes = (hk, hk, SECTION, SECTION, GLA_RANK, SECTION, SECTION, SECTION, SECTION)
    src = dict(zip(names, np.concatenate([[0], np.cumsum(sizes)[:-1]]).tolist()))
    width = dict(zip(names, sizes))
    scale = {"gq": GLA_DK ** -0.5, "dq": DIFF_DH ** -0.5 * LOG2E}
    pieces, dst = [], 0
    for name in ("gq", "gk", "gv", "gz", "dq", "dk", "dz", "dv", "gr"):
        pieces.append((src[name], dst, width[name], scale.get(name)))
        dst += width[name]
    return pieces, dst + RANK_PAD - GLA_RANK


def _w_prep_kernel(wt_ref, o_ref):
    pieces, _ = _w_in_layout()
    for src, dst, width, scale in pieces:
        piece = wt_ref[0, src:src + width, :]
        if scale is not None:
            piece = piece * scale
        if width < LANES:
            piece = jnp.concatenate([piece, jnp.zeros((LANES - width, piece.shape[1]), piece.dtype)], axis=0)
        o_ref[:, dst:dst + piece.shape[0]] = piece.T.astype(o_ref.dtype)


def _prep_w_in(w_in, layer):
    _, d, n_src = w_in.shape
    _, n_dst = _w_in_layout()
    rows = 256
    return pl.pallas_call(
        _w_prep_kernel,
        grid=(d // rows,),
        in_specs=[pl.BlockSpec((1, n_src, rows), lambda i: (layer, 0, i))],
        out_specs=pl.BlockSpec((rows, n_dst), lambda i: (i, 0)),
        out_shape=jax.ShapeDtypeStruct((d, n_dst), BF16),
        compiler_params=pltpu.CompilerParams(dimension_semantics=("arbitrary",), vmem_limit_bytes=VMEM_LIMIT),
        name="w_in_prep",
    )(jnp.swapaxes(w_in, 1, 2))


def kernel(x, c, w_ada, b_ada, norm_gain, w_in, w_gla_gate_up, b_gla_gate, gla_out_gain,
           lambda_q1, lambda_k1, lambda_q2, lambda_k2, diff_out_gain, w_out, final_gain):
    bsz, s, d = x.shape
    depth = w_in.shape[0]
    assert depth == 1, "out_proj applies the final rmsnorm, so exactly one layer is supported"
    for l in range(depth):
        mod = _adaln_mod(c, w_ada, b_ada, l)
        mod3 = mod.reshape(bsz, 1, 3 * d)
        w_in_r = _prep_w_in(w_in, l)
        wup_pad = jnp.pad(w_gla_gate_up[l], ((0, RANK_PAD - GLA_RANK), (0, 0))).astype(BF16)
        shift_w = _shift_proj(mod, w_in_r).reshape(bsz, 1, -1)
        gqk, gv, gz, dq, dk, dz, dvt, la = _in_proj(
            x, mod3, norm_gain[l].reshape(1, d), shift_w, w_in_r, wup_pad, b_gla_gate[l].reshape(1, -1))
        o_gla = _gla(gqk, gv, gz, la, gla_out_gain[l].reshape(1, -1))
        lam_init = float(0.8 - 0.6 * np.exp(-0.3 * l))
        o_diff = _diff_attn(dq, dk, dvt, dz, diff_out_gain[l],
                            lambda_q1[l].reshape(1, -1), lambda_k1[l].reshape(1, -1),
                            lambda_q2[l].reshape(1, -1), lambda_k2[l].reshape(1, -1), lam_init)
        x = _out_proj(o_gla, o_diff, x, mod3, w_out[l].astype(BF16), final_gain.reshape(1, d))
    return x
```

```python
import functools
import math

import jax
import jax.numpy as jnp
import numpy as np
from jax import lax
from jax.experimental import pallas as pl
from jax.experimental.pallas import tpu as pltpu

F32 = jnp.float32
BF16 = jnp.bfloat16

EPS = 1e-6
LOG2E = math.log2(math.e)
GLA_HEADS = 4
GLA_DK = 64
GLA_DV = 128
GLA_RANK = 16
GLA_GATE_NORM = 16.0
GLA_CHUNK = 64
DIFF_HEADS = 4
DIFF_DH = 64
DIFF_DV = 128
LANES = 128
RANK_PAD = LANES
SECTION = 512
SUM_ROWS = 16

ROWS_IN_PROJ = 512
ROWS_GLA = 512
GLA_CUMSUM_ROWS = 256
ATTN_TQ = 256
ATTN_TK = 256
ATTN_AHEAD = 2
ATTN_SPAN_TILES = 2
BOUND_SLACK = 1.01
FAST_PATH_MIN_DENOM = 2.0 ** -90
ROWS_OUT_PROJ = 1024
VMEM_LIMIT = 48 * 1024 * 1024


def _silu(v):
    return v / (1.0 + jnp.exp(-v))


def _log_sigmoid(v):
    return jnp.minimum(v, 0.0) - jnp.log(1.0 + jnp.exp(-jnp.abs(v)))


def _adaln_kernel(c_ref, w_ref, b_ref, o_ref):
    sc = _silu(c_ref[...]).astype(BF16)
    o_ref[...] = jnp.dot(sc, w_ref[0].astype(BF16), preferred_element_type=F32) + b_ref[0]


def _adaln_mod(c, w_ada, b_ada, layer):
    bsz, d = c.shape
    n = w_ada.shape[2]
    tn = 1024
    return pl.pallas_call(
        _adaln_kernel,
        grid=(n // tn,),
        in_specs=[
            pl.BlockSpec((bsz, d), lambda j: (0, 0)),
            pl.BlockSpec((1, d, tn), lambda j: (layer, 0, j)),
            pl.BlockSpec((1, 1, tn), lambda j: (layer, 0, j)),
        ],
        out_specs=pl.BlockSpec((bsz, tn), lambda j: (0, j)),
        out_shape=jax.ShapeDtypeStruct((bsz, n), F32),
        compiler_params=pltpu.CompilerParams(dimension_semantics=("arbitrary",)),
        name="adaln_mod",
    )(c, w_ada, b_ada.reshape(b_ada.shape[0], 1, n))


def _in_proj_kernel(x_ref, shift_ref, scale_ref, gain_ref, w_ref, wup_ref, bg_ref,
                    gqk_ref, gv_ref, gz_ref, dq_ref, dk_ref, dz_ref, dvt_ref, la_ref,
                    wvt_ref):
    n_sec = 7
    dv_sec = 6

    @pl.when((pl.program_id(0) == 0) & (pl.program_id(1) == 0))
    def _():
        wvt_ref[...] = w_ref[:, dv_sec * SECTION:(dv_sec + 1) * SECTION].T

    x = x_ref[0]
    g = gain_ref[...] * (1.0 + scale_ref[0])
    rstd = lax.rsqrt(jnp.mean(x * x, axis=-1, keepdims=True) + EPS)
    h = (x * rstd * g + shift_ref[0]).astype(BF16)

    outs = (gqk_ref, gv_ref, gz_ref, dq_ref, dk_ref, dz_ref)
    for j, o_ref in enumerate(outs):
        sec = jnp.dot(h, w_ref[:, j * SECTION:(j + 1) * SECTION], preferred_element_type=F32)
        if o_ref is gz_ref or o_ref is dz_ref:
            sec = _silu(sec)
        o_ref[0] = sec.astype(o_ref.dtype)

    dvt_ref[0] = lax.dot_general(wvt_ref[...], h, (((1,), (1,)), ((), ())),
                                 preferred_element_type=F32).astype(dvt_ref.dtype)
    gr = jnp.dot(h, w_ref[:, n_sec * SECTION:n_sec * SECTION + RANK_PAD],
                 preferred_element_type=F32)
    logit = jnp.dot(gr.astype(BF16), wup_ref[...], preferred_element_type=F32) + bg_ref[...]
    la_ref[0] = (_log_sigmoid(logit) * (1.0 / GLA_GATE_NORM)).astype(la_ref.dtype)


def _in_proj(x, mod3, norm_gain, w_in_r, wup_pad, b_gate):
    bsz, s, d = x.shape
    tm = ROWS_IN_PROJ
    ncol = w_in_r.shape[1]
    hk = GLA_HEADS * GLA_DK
    act = lambda width: pl.BlockSpec((1, tm, width), lambda b, t: (b, t, 0))
    sec = jax.ShapeDtypeStruct((bsz, s, SECTION), BF16)
    out_shapes = (
        sec,
        sec,
        sec,
        sec,
        sec,
        sec,
        jax.ShapeDtypeStruct((bsz, SECTION, s), BF16),
        jax.ShapeDtypeStruct((bsz, s, hk), BF16),
    )
    const = lambda shape: pl.BlockSpec(shape, lambda b, t: (0, 0))
    return pl.pallas_call(
        _in_proj_kernel,
        grid=(bsz, s // tm),
        in_specs=[
            act(d),
            pl.BlockSpec((1, 1, d), lambda b, t: (b, 0, 0)),
            pl.BlockSpec((1, 1, d), lambda b, t: (b, 0, 1)),
            const((1, d)), const((d, ncol)), const((RANK_PAD, hk)), const((1, hk)),
        ],
        out_specs=(act(SECTION), act(SECTION), act(SECTION), act(SECTION), act(SECTION), act(SECTION),
                   pl.BlockSpec((1, SECTION, tm), lambda b, t: (b, 0, t)), act(hk)),
        out_shape=out_shapes,
        scratch_shapes=[pltpu.VMEM((SECTION, d), BF16)],
        compiler_params=pltpu.CompilerParams(
            dimension_semantics=("arbitrary", "arbitrary"), vmem_limit_bytes=VMEM_LIMIT),
        name="in_proj",
    )(x, mod3, mod3, norm_gain, w_in_r, wup_pad, b_gate)


def _gla_out_kernel(qk_ref, v_ref, za_ref, la_ref, gain_ref, tril_ref, chunk_ind_ref, ones_bd_ref,
                    od_ref, x_ref, gate_ref, w_ref, fg_ref, o_ref, state_ref):
    tg = qk_ref.shape[1]
    half = od_ref.shape[2]
    mix_diff = jnp.dot(od_ref[0], w_ref[half:2 * half, :], preferred_element_type=F32)
    hk = GLA_HEADS * GLA_DK
    c_sz = GLA_CHUNK
    n_ch = tg // c_sz

    @pl.when(pl.program_id(1) == 0)
    def _():
        state_ref[...] = jnp.zeros_like(state_ref)

    la = la_ref[0]
    grp = tril_ref.shape[0]
    b_all = jnp.concatenate(
        [jnp.dot(tril_ref[...], la[g * grp:(g + 1) * grp], preferred_element_type=F32)
         for g in range(tg // grp)], axis=0)
    b_last = jnp.concatenate(
        [jnp.broadcast_to(b_all[(ch + 1) * c_sz - 1:(ch + 1) * c_sz], (c_sz, hk)) for ch in range(n_ch)], axis=0)
    q = qk_ref[0, :, 0:hk]
    k = qk_ref[0, :, hk:2 * hk]
    q_in = q * jnp.exp(b_all).astype(BF16)
    k_in = k * jnp.exp(-b_all).astype(BF16)
    k_st = k * jnp.exp(b_last - b_all).astype(BF16)

    lane_head = lax.broadcasted_iota(jnp.int32, (1, hk), 1) // GLA_DK
    ri = lax.broadcasted_iota(jnp.int32, (GLA_HEADS * c_sz, c_sz), 0) % c_sz
    ci = lax.broadcasted_iota(jnp.int32, (GLA_HEADS * c_sz, c_sz), 1)
    causal = ci <= ri
    dec_all = jnp.concatenate(
        [jnp.exp(lax.dot_general(la[g * grp:(g + 1) * grp], chunk_ind_ref[...], (((0,), (0,)), ((), ())),
                                 preferred_element_type=F32)) for g in range(tg // grp)], axis=1)

    chunk_rows = [slice(ch * c_sz, (ch + 1) * c_sz) for ch in range(n_ch)]
    head_rows = [slice(hh * c_sz, (hh + 1) * c_sz) for hh in range(GLA_HEADS)]
    head_cols = [slice(hh * GLA_DV, (hh + 1) * GLA_DV) for hh in range(GLA_HEADS)]
    vs = [v_ref[0, rows, :] for rows in chunk_rows]

    us = []
    for ch, rows in enumerate(chunk_rows):
        kst_t = k_st[rows].T
        us.append(jnp.concatenate(
            [jnp.dot(kst_t[hr], vs[ch][:, vc], preferred_element_type=F32)
             for hr, vc in zip(head_rows, head_cols)], axis=0))
    state = state_ref[...]
    states = []
    for ch in range(n_ch):
        states.append(state.astype(BF16))
        state = dec_all[:, ch * GLA_DV:(ch + 1) * GLA_DV] * state + us[ch]
    state_ref[...] = state
    boths = []
    for ch, rows in enumerate(chunk_rows):
        q_c = q_in[rows]
        qm = jnp.concatenate(
            [jnp.where(lane_head == hh, q_c, jnp.zeros_like(q_c)) for hh in range(GLA_HEADS)], axis=0)
        rhs = jnp.concatenate([states[ch], k_in[rows].T], axis=1)
        boths.append(jnp.dot(qm, rhs, preferred_element_type=F32))
    o_rows = []
    for ch in range(n_ch):
        inter = boths[ch][:, 0:GLA_DV]
        p = jnp.where(causal, boths[ch][:, GLA_DV:GLA_DV + c_sz], 0.0).astype(BF16)
        o_rows.append(jnp.concatenate(
            [jnp.dot(p[hr], vs[ch][:, vc], preferred_element_type=F32) + inter[hr]
             for hr, vc in zip(head_rows, head_cols)], axis=1))

    o = jnp.concatenate(o_rows, axis=0)
    ms = jnp.dot((o * o).astype(BF16), ones_bd_ref[...], preferred_element_type=F32) * (1.0 / GLA_DV)
    o_gla = (o * lax.rsqrt(ms + EPS) * gain_ref[...] * za_ref[0].astype(F32)).astype(BF16)

    mixw = jnp.dot(o_gla, w_ref[0:half, :], preferred_element_type=F32) + mix_diff
    xn = x_ref[0] + gate_ref[0] * mixw
    o_ref[0] = xn * lax.rsqrt(jnp.mean(xn * xn, axis=-1, keepdims=True) + EPS) * fg_ref[...]


def _gla_out(gqk, gv, gza, la, gla_out_gain, o_diff, x, mod3, w_out_bf, final_gain):
    bsz, s, d = x.shape
    tg = ROWS_GLA
    hk = GLA_HEADS * GLA_DK
    width = GLA_HEADS * GLA_DV
    grp = GLA_CUMSUM_ROWS
    idx = np.arange(grp)
    tril = ((idx[:, None] // GLA_CHUNK == idx[None, :] // GLA_CHUNK) & (idx[None, :] <= idx[:, None]))
    chunk_ind = idx[:, None] // GLA_CHUNK == np.arange(grp // GLA_CHUNK * GLA_DV)[None, :] // GLA_DV
    col = np.arange(width)
    ones_bd = col[:, None] // GLA_DV == col[None, :] // GLA_DV
    act = lambda w: pl.BlockSpec((1, tg, w), lambda b, t: (b, t, 0))
    const = lambda shape: pl.BlockSpec(shape, lambda b, t: (0, 0))
    return pl.pallas_call(
        _gla_out_kernel,
        grid=(bsz, s // tg),
        in_specs=[act(2 * hk), act(width), act(width), act(hk),
                  const((1, width)), const(tril.shape), const(chunk_ind.shape), const((width, width)),
                  act(o_diff.shape[2]), act(d),
                  pl.BlockSpec((1, 1, d), lambda b, t: (b, 0, 2)),
                  const(w_out_bf.shape), const((1, d))],
        out_specs=act(d),
        out_shape=jax.ShapeDtypeStruct((bsz, s, d), F32),
        scratch_shapes=[pltpu.VMEM((hk, GLA_DV), F32)],
        compiler_params=pltpu.CompilerParams(
            dimension_semantics=("arbitrary", "arbitrary"), vmem_limit_bytes=VMEM_LIMIT),
        name="gla_out_proj",
    )(gqk, gv, gza, la, gla_out_gain, jnp.asarray(tril, BF16), jnp.asarray(chunk_ind, BF16),
      jnp.asarray(ones_bd, BF16), o_diff, x, mod3, w_out_bf, final_gain)


def _diff_lambda(lq1_ref, lk1_ref, lq2_ref, lk2_ref, lam_init):
    a = jnp.sum(lq1_ref[...] * lk1_ref[...], axis=-1, keepdims=True)
    b = jnp.sum(lq2_ref[...] * lk2_ref[...], axis=-1, keepdims=True)
    return jnp.exp(a) - jnp.exp(b) + lam_init


def _attn_spans(nq, tk):
    spans = []
    for qi in range(nq):
        if ATTN_SPAN_TILES == 1:
            tile = [(qi, tk * j, tk, None) for j in range(qi)] + [(qi, qi * tk, tk, 0)]
        else:
            tile = [(qi, 2 * tk * j, 2 * tk, None) for j in range(qi // 2)]
            if qi % 2 == 1:
                tile.append((qi, (qi - 1) * tk, 2 * tk, tk))
            else:
                tile.append((qi, qi * tk, tk, 0))
        for i, sp in enumerate(tile):
            spans.append(sp + (i == 0, i == len(tile) - 1))
    return spans


def _attn_bias_tables(tq, tk):
    def split(x, terms):
        parts = []
        for _ in range(terms):
            hi = x.astype(jnp.bfloat16)
            parts.append(hi)
            x = x - hi.astype(np.float32)
        return parts

    key = np.arange(2 * tk, dtype=np.float32)
    kaug = np.zeros((2 * tk, LANES), np.float32)
    kaug[:, 0] = kaug[:, 1] = key % 256
    kaug[:, 2] = kaug[:, 3] = key - key % 256
    kaug[:, 4:7] = 1.0
    slopes = (2.0 ** (-8.0 * np.arange(1, DIFF_HEADS + 1) / DIFF_HEADS) * LOG2E).astype(np.float32)
    qry = np.tile(np.arange(tq, dtype=np.float32), 2)
    qaug = np.zeros((DIFF_HEADS, 2 * tq, LANES), jnp.bfloat16)
    for hh in range(DIFF_HEADS):
        s_hi, s_lo = split(slopes[hh:hh + 1], 2)
        qaug[hh, :, 0] = qaug[hh, :, 2] = s_hi
        qaug[hh, :, 1] = qaug[hh, :, 3] = s_lo
        for lane, part in zip((4, 5, 6), split(-slopes[hh] * qry, 3)):
            qaug[hh, :, lane] = part
    slope_rows = np.broadcast_to(slopes[:, None, None], (DIFF_HEADS, 1, LANES))
    return jnp.asarray(kaug, BF16), jnp.asarray(qaug), jnp.asarray(slope_rows, F32)


def _attn_kernel(q_ref, k_ref, vt_ref, z_ref, gain_ref, kaug_ref, qaug_ref, slope_ref,
                 lq1_ref, lk1_ref, lq2_ref, lk2_ref,
                 o_ref, qs_ref, s_ref, mb_ref, acc_ref, *, lam_init):
    tq = ATTN_TQ
    tk = ATTN_TK
    hw = 2 * DIFF_DH
    nq = q_ref.shape[1] // tq
    slope = slope_ref[0][:, 0:1]
    n_buf = ATTN_AHEAD + 1

    first_half = lax.broadcasted_iota(jnp.int32, (1, hw), 1) < DIFF_DH
    for qi in range(nq):
        q = q_ref[0, qi * tq:(qi + 1) * tq, :]
        zq = jnp.zeros_like(q)
        qs_ref[qi, 0:tq, 0:hw] = jnp.where(first_half, q, zq)
        qs_ref[qi, tq:2 * tq, 0:hw] = jnp.where(first_half, zq, q)
        qs_ref[qi, :, hw:hw + LANES] = qaug_ref[0]

    def rel_pos(nk):
        key = lax.broadcasted_iota(jnp.int32, (nk, 2 * tq), 0)
        qry = lax.broadcasted_iota(jnp.int32, (nk, 2 * tq), 1) % tq
        return qry - key

    lam = _diff_lambda(lq1_ref, lk1_ref, lq2_ref, lk2_ref, lam_init)
    spans = _attn_spans(nq, tk)

    def span_offset(qi, start):
        return -slope * float(qi * tq - start)

    def masked_scores(t):
        qi, start, nk, diag_off, _, _ = spans[t]
        kblk = jnp.concatenate([k_ref[0, start:start + nk, :], kaug_ref[0:nk, :]], axis=1)
        s = lax.dot_general(kblk, qs_ref[qi], (((1,), (1,)), ((), ())), preferred_element_type=F32)
        if diag_off is not None:
            keep = rel_pos(tk) + (diag_off - (nk - tk)) >= 0
            tail = jnp.where(keep, s[nk - tk:nk], -jnp.inf)
            s = tail if nk == tk else jnp.concatenate([s[0:nk - tk], tail], axis=0)
        return s

    def values_and_ones(t):
        _, start, nk, _, _, _ = spans[t]
        return jnp.concatenate([vt_ref[0, :, start:start + nk], jnp.ones((SUM_ROWS, nk), BF16)], axis=0)

    def finish(qi, acc):
        rows = slice(qi * tq, (qi + 1) * tq)
        o_all = acc[0:DIFF_DV, :] / acc[DIFF_DV:DIFF_DV + 1, :]
        o = (o_all[:, :tq] - lam * o_all[:, tq:]).T
        y = o * lax.rsqrt(jnp.mean(o * o, axis=-1, keepdims=True) + EPS)
        y = y * gain_ref[0] * (1.0 - lam_init) * z_ref[0, rows, :].astype(F32)
        o_ref[0, rows, :] = y.astype(o_ref.dtype)

    k_all = k_ref[0]
    r_i = lax.broadcasted_iota(jnp.int32, (hw, LANES), 0)
    c_i = lax.broadcasted_iota(jnp.int32, (hw, LANES), 1)
    half_sel = jnp.where(((c_i == 0) & (r_i < DIFF_DH)) | ((c_i == 1) & (r_i >= DIFF_DH)), 1.0, 0.0).astype(BF16)
    k_norm2 = jnp.max(jnp.dot(k_all * k_all, half_sel, preferred_element_type=F32), axis=0, keepdims=True)
    k_max = jnp.sqrt(k_norm2)
    map_cols = lax.broadcasted_iota(jnp.int32, (1, 2 * tq), 1) < tq
    k_max_row = jnp.where(map_cols, k_max[:, 0:1], k_max[:, 1:2])
    ones_rows = jnp.ones((8, hw), BF16)
    bounds = []
    for qi in range(nq):
        q_sq = qs_ref[qi, :, 0:hw]
        q_norm2 = lax.dot_general(ones_rows, q_sq * q_sq, (((1,), (1,)), ((), ())),
                                  preferred_element_type=F32)[0:1]
        bounds.append(jnp.sqrt(q_norm2) * k_max_row * BOUND_SLACK)
    denom_min = None
    acc = None
    s_next = masked_scores(0)
    for t, (qi, start, _, _, first, last) in enumerate(spans):
        s = s_next
        if t + 1 < len(spans):
            s_next = masked_scores(t + 1)
        p = jnp.exp2(s - (bounds[qi] - span_offset(qi, start))).astype(BF16)
        pv = jnp.dot(values_and_ones(t), p, preferred_element_type=F32)
        acc = pv if first else acc + pv
        if last:
            denom = acc[DIFF_DV:DIFF_DV + 1, :]
            denom_min = denom if denom_min is None else jnp.minimum(denom_min, denom)
            finish(qi, acc)
    fast_ok = jnp.min(denom_min) >= FAST_PATH_MIN_DENOM

    def scores(t):
        qi, start, nk, _, _, _ = spans[t]
        s = masked_scores(t)
        s_ref[t % n_buf, 0:nk, :] = s
        mb_ref[t % n_buf] = jnp.max(s, axis=0, keepdims=True) + span_offset(qi, start)

    def update(t, m_old):
        qi, start, nk, _, first, last = spans[t]
        s = s_ref[t % n_buf, 0:nk, :]
        m_blk = mb_ref[t % n_buf]
        m_new = m_blk if first else jnp.maximum(m_old, m_blk)
        p = jnp.exp2(s - (m_new - span_offset(qi, start))).astype(BF16)
        pv = jnp.dot(values_and_ones(t), p, preferred_element_type=F32)
        if first:
            acc_ref[...] = pv
        else:
            acc_ref[...] = jnp.exp2(m_old - m_new) * acc_ref[...] + pv
        if last:
            finish(qi, acc_ref[...])
        return m_new

    @pl.when(jnp.logical_not(fast_ok))
    def _():
        for t in range(min(ATTN_AHEAD, len(spans))):
            scores(t)
        m = None
        for t in range(len(spans)):
            if t + ATTN_AHEAD < len(spans):
                scores(t + ATTN_AHEAD)
            m = update(t, m)


def _diff_attn(dq, dk, dvt, dz, gain, lq1, lk1, lq2, lk2, lam_init):
    bsz, s, width = dq.shape
    tq = ATTN_TQ
    assert ATTN_TQ == ATTN_TK
    hw = 2 * DIFF_DH
    rowspec = pl.BlockSpec((1, s, hw), lambda b, h: (b, 0, h))
    vtspec = pl.BlockSpec((1, DIFF_DV, s), lambda b, h: (b, h, 0))
    lspec = pl.BlockSpec((1, DIFF_DH), lambda b, h: (0, 0))
    kaug, qaug, slope_rows = _attn_bias_tables(tq, ATTN_TK)
    per_head = lambda rows: pl.BlockSpec((1, rows, LANES), lambda b, h: (h, 0, 0))
    return pl.pallas_call(
        functools.partial(_attn_kernel, lam_init=lam_init),
        grid=(bsz, DIFF_HEADS),
        in_specs=[rowspec, rowspec, vtspec, rowspec, per_head(1),
                  pl.BlockSpec(kaug.shape, lambda b, h: (0, 0)), per_head(2 * tq), per_head(1),
                  lspec, lspec, lspec, lspec],
        out_specs=rowspec,
        out_shape=jax.ShapeDtypeStruct((bsz, s, width), BF16),
        scratch_shapes=[pltpu.VMEM((s // tq, 2 * tq, hw + LANES), BF16),
                        pltpu.VMEM((ATTN_AHEAD + 1, 2 * ATTN_TK, 2 * tq), F32),
                        pltpu.VMEM((ATTN_AHEAD + 1, 1, 2 * tq), F32),
                        pltpu.VMEM((DIFF_DV + SUM_ROWS, 2 * tq), F32)],
        compiler_params=pltpu.CompilerParams(
            dimension_semantics=("arbitrary", "arbitrary"), vmem_limit_bytes=VMEM_LIMIT),
        name="diff_attn",
    )(dq, dk, dvt, dz, gain.reshape(DIFF_HEADS, 1, DIFF_DV), kaug, qaug, slope_rows, lq1, lk1, lq2, lk2)


def _w_in_layout():
    hk = GLA_HEADS * GLA_DK
    names = ("gq", "gk", "gv", "gz", "gr", "dq", "dk", "dv", "dz")
    sizes = (hk, hk, SECTION, SECTION, GLA_RANK, SECTION, SECTION, SECTION, SECTION)
    src = dict(zip(names, np.concatenate([[0], np.cumsum(sizes)[:-1]]).tolist()))
    width = dict(zip(names, sizes))
    scale = {"gq": GLA_DK ** -0.5, "dq": DIFF_DH ** -0.5 * LOG2E}
    pieces, dst = [], 0
    for name in ("gq", "gk", "gv", "gz", "dq", "dk", "dz", "dv", "gr"):
        pieces.append((src[name], dst, width[name], scale.get(name)))
        dst += width[name]
    return pieces, dst + RANK_PAD - GLA_RANK


def _w_prep_kernel(wt_ref, o_ref):
    pieces, _ = _w_in_layout()
    for src, dst, width, scale in pieces:
        piece = wt_ref[0, src:src + width, :]
        if scale is not None:
            piece = piece * scale
        if width < LANES:
            piece = jnp.concatenate([piece, jnp.zeros((LANES - width, piece.shape[1]), piece.dtype)], axis=0)
        o_ref[:, dst:dst + piece.shape[0]] = piece.T.astype(o_ref.dtype)


def _prep_w_in(w_in, layer):
    _, d, n_src = w_in.shape
    _, n_dst = _w_in_layout()
    rows = 256
    return pl.pallas_call(
        _w_prep_kernel,
        grid=(d // rows,),
        in_specs=[pl.BlockSpec((1, n_src, rows), lambda i: (layer, 0, i))],
        out_specs=pl.BlockSpec((rows, n_dst), lambda i: (i, 0)),
        out_shape=jax.ShapeDtypeStruct((d, n_dst), BF16),
        compiler_params=pltpu.CompilerParams(dimension_semantics=("arbitrary",), vmem_limit_bytes=VMEM_LIMIT),
        name="w_in_prep",
    )(jnp.swapaxes(w_in, 1, 2))


def kernel(x, c, w_ada, b_ada, norm_gain, w_in, w_gla_gate_up, b_gla_gate, gla_out_gain,
           lambda_q1, lambda_k1, lambda_q2, lambda_k2, diff_out_gain, w_out, final_gain):
    bsz, s, d = x.shape
    depth = w_in.shape[0]
    assert depth == 1, "out_proj applies the final rmsnorm, so exactly one layer is supported"
    for l in range(depth):
        mod = _adaln_mod(c, w_ada, b_ada, l)
        mod3 = mod.reshape(bsz, 1, 3 * d)
        w_in_r = _prep_w_in(w_in, l)
        wup_pad = jnp.pad(w_gla_gate_up[l], ((0, RANK_PAD - GLA_RANK), (0, 0))).astype(BF16)
        gqk, gv, gz, dq, dk, dz, dvt, la = _in_proj(
            x, mod3, norm_gain[l].reshape(1, d), w_in_r, wup_pad, b_gla_gate[l].reshape(1, -1))
        lam_init = float(0.8 - 0.6 * np.exp(-0.3 * l))
        o_diff = _diff_attn(dq, dk, dvt, dz, diff_out_gain[l],
                            lambda_q1[l].reshape(1, -1), lambda_k1[l].reshape(1, -1),
                            lambda_q2[l].reshape(1, -1), lambda_k2[l].reshape(1, -1), lam_init)
        x = _gla_out(gqk, gv, gz, la, gla_out_gain[l].reshape(1, -1), o_diff, x, mod3,
                     w_out[l].astype(BF16), final_gain.reshape(1, d))
    return x
```

```python
import functools
import math

import jax
import jax.numpy as jnp
import numpy as np
from jax import lax
from jax.experimental import pallas as pl
from jax.experimental.pallas import tpu as pltpu

F32 = jnp.float32
BF16 = jnp.bfloat16

EPS = 1e-6
LOG2E = math.log2(math.e)
GLA_HEADS = 4
GLA_DK = 64
GLA_DV = 128
GLA_RANK = 16
GLA_GATE_NORM = 16.0
GLA_CHUNK = 64
DIFF_HEADS = 4
DIFF_DH = 64
DIFF_DV = 128
LANES = 128
RANK_PAD = LANES
SECTION = 512
SUM_ROWS = 16

ROWS_IN_PROJ = 512
ROWS_GLA = 512
GLA_CUMSUM_ROWS = 256
ATTN_TQ = 256
ATTN_TK = 256
ATTN_AHEAD = 2
ATTN_SPAN_TILES = 2
BOUND_SLACK = 1.01
FAST_PATH_MIN_DENOM = 2.0 ** -90
ROWS_OUT_PROJ = 1024
VMEM_LIMIT = 48 * 1024 * 1024


def _silu(v):
    return v / (1.0 + jnp.exp(-v))


def _log_sigmoid(v):
    return jnp.minimum(v, 0.0) - jnp.log(1.0 + jnp.exp(-jnp.abs(v)))


def _adaln_kernel(c_ref, w_ref, b_ref, o_ref):
    sc = _silu(c_ref[...]).astype(BF16)
    o_ref[...] = jnp.dot(sc, w_ref[0].astype(BF16), preferred_element_type=F32) + b_ref[0]


def _adaln_mod(c, w_ada, b_ada, layer):
    bsz, d = c.shape
    n = w_ada.shape[2]
    tn = 1024
    return pl.pallas_call(
        _adaln_kernel,
        grid=(n // tn,),
        in_specs=[
            pl.BlockSpec((bsz, d), lambda j: (0, 0)),
            pl.BlockSpec((1, d, tn), lambda j: (layer, 0, j)),
            pl.BlockSpec((1, 1, tn), lambda j: (layer, 0, j)),
        ],
        out_specs=pl.BlockSpec((bsz, tn), lambda j: (0, j)),
        out_shape=jax.ShapeDtypeStruct((bsz, n), F32),
        compiler_params=pltpu.CompilerParams(dimension_semantics=("arbitrary",)),
        name="adaln_mod",
    )(c, w_ada, b_ada.reshape(b_ada.shape[0], 1, n))


def _in_proj_kernel(x_ref, shift_ref, scale_ref, gain_ref, w_ref, wup_ref, bg_ref,
                    gqk_ref, gv_ref, gz_ref, dq_ref, dk_ref, dz_ref, dvt_ref, la_ref,
                    wvt_ref):
    n_sec = 7
    dv_sec = 6

    @pl.when((pl.program_id(0) == 0) & (pl.program_id(1) == 0))
    def _():
        wvt_ref[0:SECTION, :] = w_ref[:, dv_sec * SECTION:(dv_sec + 1) * SECTION].T
        wvt_ref[SECTION:SECTION + GLA_RANK, :] = (
            w_ref[:, n_sec * SECTION:n_sec * SECTION + RANK_PAD].T[0:GLA_RANK, :])

    x = x_ref[0]
    g = gain_ref[...] * (1.0 + scale_ref[0])
    rstd = lax.rsqrt(jnp.mean(x * x, axis=-1, keepdims=True) + EPS)
    h = (x * rstd * g + shift_ref[0]).astype(BF16)

    outs = (gqk_ref, gv_ref, gz_ref, dq_ref, dk_ref, dz_ref)
    for j, o_ref in enumerate(outs):
        sec = jnp.dot(h, w_ref[:, j * SECTION:(j + 1) * SECTION], preferred_element_type=F32)
        if o_ref is gz_ref or o_ref is dz_ref:
            sec = _silu(sec)
        o_ref[0] = sec.astype(o_ref.dtype)

    vt_gr = lax.dot_general(wvt_ref[...], h, (((1,), (1,)), ((), ())), preferred_element_type=F32)
    dvt_ref[0] = vt_gr[0:SECTION].astype(dvt_ref.dtype)
    tm = vt_gr.shape[1]
    gr_t = jnp.concatenate([vt_gr[SECTION:SECTION + GLA_RANK].astype(BF16),
                            jnp.zeros((RANK_PAD - GLA_RANK, tm), BF16)], axis=0)
    logit = lax.dot_general(gr_t, wup_ref[...], (((0,), (0,)), ((), ())),
                            preferred_element_type=F32) + bg_ref[...]
    la_ref[0] = (_log_sigmoid(logit) * (1.0 / GLA_GATE_NORM)).astype(la_ref.dtype)


def _in_proj(x, mod3, norm_gain, w_in_r, wup_pad, b_gate):
    bsz, s, d = x.shape
    tm = ROWS_IN_PROJ
    ncol = w_in_r.shape[1]
    hk = GLA_HEADS * GLA_DK
    act = lambda width: pl.BlockSpec((1, tm, width), lambda b, t: (b, t, 0))
    sec = jax.ShapeDtypeStruct((bsz, s, SECTION), BF16)
    out_shapes = (
        sec,
        sec,
        sec,
        sec,
        sec,
        sec,
        jax.ShapeDtypeStruct((bsz, SECTION, s), BF16),
        jax.ShapeDtypeStruct((bsz, s, hk), BF16),
    )
    const = lambda shape: pl.BlockSpec(shape, lambda b, t: (0, 0))
    return pl.pallas_call(
        _in_proj_kernel,
        grid=(bsz, s // tm),
        in_specs=[
            act(d),
            pl.BlockSpec((1, 1, d), lambda b, t: (b, 0, 0)),
            pl.BlockSpec((1, 1, d), lambda b, t: (b, 0, 1)),
            const((1, d)), const((d, ncol)), const((RANK_PAD, hk)), const((1, hk)),
        ],
        out_specs=(act(SECTION), act(SECTION), act(SECTION), act(SECTION), act(SECTION), act(SECTION),
                   pl.BlockSpec((1, SECTION, tm), lambda b, t: (b, 0, t)), act(hk)),
        out_shape=out_shapes,
        scratch_shapes=[pltpu.VMEM((SECTION + GLA_RANK, d), BF16)],
        compiler_params=pltpu.CompilerParams(
            dimension_semantics=("arbitrary", "arbitrary"), vmem_limit_bytes=VMEM_LIMIT),
        name="in_proj",
    )(x, mod3, mod3, norm_gain, w_in_r, wup_pad, b_gate)


def _gla_out_kernel(qk_ref, v_ref, za_ref, la_ref, gain_ref, tril_ref, chunk_ind_ref, ones_bd_ref,
                    od_ref, x_ref, gate_ref, w_ref, fg_ref, o_ref, state_ref):
    tg = qk_ref.shape[1]
    half = od_ref.shape[2]
    mix_diff = jnp.dot(od_ref[0], w_ref[half:2 * half, :], preferred_element_type=F32)
    hk = GLA_HEADS * GLA_DK
    c_sz = GLA_CHUNK
    n_ch = tg // c_sz

    @pl.when(pl.program_id(1) == 0)
    def _():
        state_ref[...] = jnp.zeros_like(state_ref)

    la = la_ref[0]
    grp = tril_ref.shape[0]
    b_all = jnp.concatenate(
        [jnp.dot(tril_ref[...], la[g * grp:(g + 1) * grp], preferred_element_type=F32)
         for g in range(tg // grp)], axis=0)
    b_last = jnp.concatenate(
        [jnp.broadcast_to(b_all[(ch + 1) * c_sz - 1:(ch + 1) * c_sz], (c_sz, hk)) for ch in range(n_ch)], axis=0)
    q = qk_ref[0, :, 0:hk]
    k = qk_ref[0, :, hk:2 * hk]
    q_in = q * jnp.exp(b_all).astype(BF16)
    k_in = k * jnp.exp(-b_all).astype(BF16)
    k_st = k * jnp.exp(b_last - b_all).astype(BF16)

    lane_head = lax.broadcasted_iota(jnp.int32, (1, hk), 1) // GLA_DK
    ri = lax.broadcasted_iota(jnp.int32, (GLA_HEADS * c_sz, c_sz), 0) % c_sz
    ci = lax.broadcasted_iota(jnp.int32, (GLA_HEADS * c_sz, c_sz), 1)
    causal = ci <= ri
    dec_all = jnp.concatenate(
        [jnp.exp(lax.dot_general(la[g * grp:(g + 1) * grp], chunk_ind_ref[...], (((0,), (0,)), ((), ())),
                                 preferred_element_type=F32)) for g in range(tg // grp)], axis=1)

    chunk_rows = [slice(ch * c_sz, (ch + 1) * c_sz) for ch in range(n_ch)]
    head_rows = [slice(hh * c_sz, (hh + 1) * c_sz) for hh in range(GLA_HEADS)]
    head_cols = [slice(hh * GLA_DV, (hh + 1) * GLA_DV) for hh in range(GLA_HEADS)]
    vs = [v_ref[0, rows, :] for rows in chunk_rows]

    us = []
    for ch, rows in enumerate(chunk_rows):
        kst_t = k_st[rows].T
        us.append(jnp.concatenate(
            [jnp.dot(kst_t[hr], vs[ch][:, vc], preferred_element_type=F32)
             for hr, vc in zip(head_rows, head_cols)], axis=0))
    state = state_ref[...]
    states = []
    for ch in range(n_ch):
        states.append(state.astype(BF16))
        state = dec_all[:, ch * GLA_DV:(ch + 1) * GLA_DV] * state + us[ch]
    state_ref[...] = state
    boths = []
    for ch, rows in enumerate(chunk_rows):
        q_c = q_in[rows]
        qm = jnp.concatenate(
            [jnp.where(lane_head == hh, q_c, jnp.zeros_like(q_c)) for hh in range(GLA_HEADS)], axis=0)
        rhs = jnp.concatenate([states[ch], k_in[rows].T], axis=1)
        boths.append(jnp.dot(qm, rhs, preferred_element_type=F32))
    o_rows = []
    for ch in range(n_ch):
        inter = boths[ch][:, 0:GLA_DV]
        p = jnp.where(causal, boths[ch][:, GLA_DV:GLA_DV + c_sz], 0.0).astype(BF16)
        o_rows.append(jnp.concatenate(
            [jnp.dot(p[hr], vs[ch][:, vc], preferred_element_type=F32) + inter[hr]
             for hr, vc in zip(head_rows, head_cols)], axis=1))

    o = jnp.concatenate(o_rows, axis=0)
    ms = jnp.dot((o * o).astype(BF16), ones_bd_ref[...], preferred_element_type=F32) * (1.0 / GLA_DV)
    o_gla = (o * lax.rsqrt(ms + EPS) * gain_ref[...] * za_ref[0].astype(F32)).astype(BF16)

    mixw = jnp.dot(o_gla, w_ref[0:half, :], preferred_element_type=F32) + mix_diff
    xn = x_ref[0] + gate_ref[0] * mixw
    o_ref[0] = xn * lax.rsqrt(jnp.mean(xn * xn, axis=-1, keepdims=True) + EPS) * fg_ref[...]


def _gla_out(gqk, gv, gza, la, gla_out_gain, o_diff, x, mod3, w_out_bf, final_gain):
    bsz, s, d = x.shape
    tg = ROWS_GLA
    hk = GLA_HEADS * GLA_DK
    width = GLA_HEADS * GLA_DV
    grp = GLA_CUMSUM_ROWS
    idx = np.arange(grp)
    tril = ((idx[:, None] // GLA_CHUNK == idx[None, :] // GLA_CHUNK) & (idx[None, :] <= idx[:, None]))
    chunk_ind = idx[:, None] // GLA_CHUNK == np.arange(grp // GLA_CHUNK * GLA_DV)[None, :] // GLA_DV
    col = np.arange(width)
    ones_bd = col[:, None] // GLA_DV == col[None, :] // GLA_DV
    act = lambda w: pl.BlockSpec((1, tg, w), lambda b, t: (b, t, 0))
    const = lambda shape: pl.BlockSpec(shape, lambda b, t: (0, 0))
    return pl.pallas_call(
        _gla_out_kernel,
        grid=(bsz, s // tg),
        in_specs=[act(2 * hk), act(width), act(width), act(hk),
                  const((1, width)), const(tril.shape), const(chunk_ind.shape), const((width, width)),
                  act(o_diff.shape[2]), act(d),
                  pl.BlockSpec((1, 1, d), lambda b, t: (b, 0, 2)),
                  const(w_out_bf.shape), const((1, d))],
        out_specs=act(d),
        out_shape=jax.ShapeDtypeStruct((bsz, s, d), F32),
        scratch_shapes=[pltpu.VMEM((hk, GLA_DV), F32)],
        compiler_params=pltpu.CompilerParams(
            dimension_semantics=("arbitrary", "arbitrary"), vmem_limit_bytes=VMEM_LIMIT),
        name="gla_out_proj",
    )(gqk, gv, gza, la, gla_out_gain, jnp.asarray(tril, BF16), jnp.asarray(chunk_ind, BF16),
      jnp.asarray(ones_bd, BF16), o_diff, x, mod3, w_out_bf, final_gain)


def _diff_lambda(lq1_ref, lk1_ref, lq2_ref, lk2_ref, lam_init):
    a = jnp.sum(lq1_ref[...] * lk1_ref[...], axis=-1, keepdims=True)
    b = jnp.sum(lq2_ref[...] * lk2_ref[...], axis=-1, keepdims=True)
    return jnp.exp(a) - jnp.exp(b) + lam_init


def _attn_spans(nq, tk):
    spans = []
    for qi in range(nq):
        if ATTN_SPAN_TILES == 1:
            tile = [(qi, tk * j, tk, None) for j in range(qi)] + [(qi, qi * tk, tk, 0)]
        else:
            tile = [(qi, 2 * tk * j, 2 * tk, None) for j in range(qi // 2)]
            if qi % 2 == 1:
                tile.append((qi, (qi - 1) * tk, 2 * tk, tk))
            else:
                tile.append((qi, qi * tk, tk, 0))
        for i, sp in enumerate(tile):
            spans.append(sp + (i == 0, i == len(tile) - 1))
    return spans


def _attn_bias_tables(tq, tk):
    def split(x, terms):
        parts = []
        for _ in range(terms):
            hi = x.astype(jnp.bfloat16)
            parts.append(hi)
            x = x - hi.astype(np.float32)
        return parts

    key = np.arange(2 * tk, dtype=np.float32)
    kaug = np.zeros((2 * tk, LANES), np.float32)
    kaug[:, 0] = kaug[:, 1] = key % 256
    kaug[:, 2] = kaug[:, 3] = key - key % 256
    kaug[:, 4:7] = 1.0
    slopes = (2.0 ** (-8.0 * np.arange(1, DIFF_HEADS + 1) / DIFF_HEADS) * LOG2E).astype(np.float32)
    qry = np.tile(np.arange(tq, dtype=np.float32), 2)
    qaug = np.zeros((DIFF_HEADS, 2 * tq, LANES), jnp.bfloat16)
    for hh in range(DIFF_HEADS):
        s_hi, s_lo = split(slopes[hh:hh + 1], 2)
        qaug[hh, :, 0] = qaug[hh, :, 2] = s_hi
        qaug[hh, :, 1] = qaug[hh, :, 3] = s_lo
        for lane, part in zip((4, 5, 6), split(-slopes[hh] * qry, 3)):
            qaug[hh, :, lane] = part
    slope_rows = np.broadcast_to(slopes[:, None, None], (DIFF_HEADS, 1, LANES))
    return jnp.asarray(kaug, BF16), jnp.asarray(qaug), jnp.asarray(slope_rows, F32)


def _attn_kernel(q_ref, k_ref, vt_ref, z_ref, gain_ref, kaug_ref, qaug_ref, slope_ref,
                 lq1_ref, lk1_ref, lq2_ref, lk2_ref,
                 o_ref, qs_ref, s_ref, mb_ref, acc_ref, *, lam_init):
    tq = ATTN_TQ
    tk = ATTN_TK
    hw = 2 * DIFF_DH
    nq = q_ref.shape[1] // tq
    slope = slope_ref[0][:, 0:1]
    n_buf = ATTN_AHEAD + 1

    first_half = lax.broadcasted_iota(jnp.int32, (1, hw), 1) < DIFF_DH
    for qi in range(nq):
        q = q_ref[0, qi * tq:(qi + 1) * tq, :]
        zq = jnp.zeros_like(q)
        qs_ref[qi, 0:tq, 0:hw] = jnp.where(first_half, q, zq)
        qs_ref[qi, tq:2 * tq, 0:hw] = jnp.where(first_half, zq, q)
        qs_ref[qi, :, hw:hw + LANES] = qaug_ref[0]

    def rel_pos(nk):
        key = lax.broadcasted_iota(jnp.int32, (nk, 2 * tq), 0)
        qry = lax.broadcasted_iota(jnp.int32, (nk, 2 * tq), 1) % tq
        return qry - key

    lam = _diff_lambda(lq1_ref, lk1_ref, lq2_ref, lk2_ref, lam_init)
    spans = _attn_spans(nq, tk)

    def span_offset(qi, start):
        return -slope * float(qi * tq - start)

    def masked_scores(t):
        qi, start, nk, diag_off, _, _ = spans[t]
        kblk = jnp.concatenate([k_ref[0, start:start + nk, :], kaug_ref[0:nk, :]], axis=1)
        s = lax.dot_general(kblk, qs_ref[qi], (((1,), (1,)), ((), ())), preferred_element_type=F32)
        if diag_off is not None:
            keep = rel_pos(tk) + (diag_off - (nk - tk)) >= 0
            tail = jnp.where(keep, s[nk - tk:nk], -jnp.inf)
            s = tail if nk == tk else jnp.concatenate([s[0:nk - tk], tail], axis=0)
        return s

    def values_and_ones(t):
        _, start, nk, _, _, _ = spans[t]
        return jnp.concatenate([vt_ref[0, :, start:start + nk], jnp.ones((SUM_ROWS, nk), BF16)], axis=0)

    def finish(qi, acc):
        rows = slice(qi * tq, (qi + 1) * tq)
        o_all = acc[0:DIFF_DV, :] / acc[DIFF_DV:DIFF_DV + 1, :]
        o = (o_all[:, :tq] - lam * o_all[:, tq:]).T
        y = o * lax.rsqrt(jnp.mean(o * o, axis=-1, keepdims=True) + EPS)
        y = y * gain_ref[0] * (1.0 - lam_init) * z_ref[0, rows, :].astype(F32)
        o_ref[0, rows, :] = y.astype(o_ref.dtype)

    k_all = k_ref[0]
    r_i = lax.broadcasted_iota(jnp.int32, (hw, LANES), 0)
    c_i = lax.broadcasted_iota(jnp.int32, (hw, LANES), 1)
    half_sel = jnp.where(((c_i == 0) & (r_i < DIFF_DH)) | ((c_i == 1) & (r_i >= DIFF_DH)), 1.0, 0.0).astype(BF16)
    k_norm2 = jnp.max(jnp.dot(k_all * k_all, half_sel, preferred_element_type=F32), axis=0, keepdims=True)
    k_max = jnp.sqrt(k_norm2)
    map_cols = lax.broadcasted_iota(jnp.int32, (1, 2 * tq), 1) < tq
    k_max_row = jnp.where(map_cols, k_max[:, 0:1], k_max[:, 1:2])
    ones_rows = jnp.ones((8, hw), BF16)
    bounds = []
    for qi in range(nq):
        q_sq = qs_ref[qi, :, 0:hw]
        q_norm2 = lax.dot_general(ones_rows, q_sq * q_sq, (((1,), (1,)), ((), ())),
                                  preferred_element_type=F32)[0:1]
        bounds.append(jnp.sqrt(q_norm2) * k_max_row * BOUND_SLACK)
    denom_min = None
    acc = None
    s_next = masked_scores(0)
    for t, (qi, start, _, _, first, last) in enumerate(spans):
        s = s_next
        if t + 1 < len(spans):
            s_next = masked_scores(t + 1)
        p = jnp.exp2(s - (bounds[qi] - span_offset(qi, start))).astype(BF16)
        pv = jnp.dot(values_and_ones(t), p, preferred_element_type=F32)
        acc = pv if first else acc + pv
        if last:
            denom = acc[DIFF_DV:DIFF_DV + 1, :]
            denom_min = denom if denom_min is None else jnp.minimum(denom_min, denom)
            finish(qi, acc)
    fast_ok = jnp.min(denom_min) >= FAST_PATH_MIN_DENOM

    def scores(t):
        qi, start, nk, _, _, _ = spans[t]
        s = masked_scores(t)
        s_ref[t % n_buf, 0:nk, :] = s
        mb_ref[t % n_buf] = jnp.max(s, axis=0, keepdims=True) + span_offset(qi, start)

    def update(t, m_old):
        qi, start, nk, _, first, last = spans[t]
        s = s_ref[t % n_buf, 0:nk, :]
        m_blk = mb_ref[t % n_buf]
        m_new = m_blk if first else jnp.maximum(m_old, m_blk)
        p = jnp.exp2(s - (m_new - span_offset(qi, start))).astype(BF16)
        pv = jnp.dot(values_and_ones(t), p, preferred_element_type=F32)
        if first:
            acc_ref[...] = pv
        else:
            acc_ref[...] = jnp.exp2(m_old - m_new) * acc_ref[...] + pv
        if last:
            finish(qi, acc_ref[...])
        return m_new

    @pl.when(jnp.logical_not(fast_ok))
    def _():
        for t in range(min(ATTN_AHEAD, len(spans))):
            scores(t)
        m = None
        for t in range(len(spans)):
            if t + ATTN_AHEAD < len(spans):
                scores(t + ATTN_AHEAD)
            m = update(t, m)


def _diff_attn(dq, dk, dvt, dz, gain, lq1, lk1, lq2, lk2, lam_init):
    bsz, s, width = dq.shape
    tq = ATTN_TQ
    assert ATTN_TQ == ATTN_TK
    hw = 2 * DIFF_DH
    rowspec = pl.BlockSpec((1, s, hw), lambda b, h: (b, 0, h))
    vtspec = pl.BlockSpec((1, DIFF_DV, s), lambda b, h: (b, h, 0))
    lspec = pl.BlockSpec((1, DIFF_DH), lambda b, h: (0, 0))
    kaug, qaug, slope_rows = _attn_bias_tables(tq, ATTN_TK)
    per_head = lambda rows: pl.BlockSpec((1, rows, LANES), lambda b, h: (h, 0, 0))
    return pl.pallas_call(
        functools.partial(_attn_kernel, lam_init=lam_init),
        grid=(bsz, DIFF_HEADS),
        in_specs=[rowspec, rowspec, vtspec, rowspec, per_head(1),
                  pl.BlockSpec(kaug.shape, lambda b, h: (0, 0)), per_head(2 * tq), per_head(1),
                  lspec, lspec, lspec, lspec],
        out_specs=rowspec,
        out_shape=jax.ShapeDtypeStruct((bsz, s, width), BF16),
        scratch_shapes=[pltpu.VMEM((s // tq, 2 * tq, hw + LANES), BF16),
                        pltpu.VMEM((ATTN_AHEAD + 1, 2 * ATTN_TK, 2 * tq), F32),
                        pltpu.VMEM((ATTN_AHEAD + 1, 1, 2 * tq), F32),
                        pltpu.VMEM((DIFF_DV + SUM_ROWS, 2 * tq), F32)],
        compiler_params=pltpu.CompilerParams(
            dimension_semantics=("arbitrary", "arbitrary"), vmem_limit_bytes=VMEM_LIMIT),
        name="diff_attn",
    )(dq, dk, dvt, dz, gain.reshape(DIFF_HEADS, 1, DIFF_DV), kaug, qaug, slope_rows, lq1, lk1, lq2, lk2)


def _w_in_layout():
    hk = GLA_HEADS * GLA_DK
    names = ("gq", "gk", "gv", "gz", "gr", "dq", "dk", "dv", "dz")
    sizes = (hk, hk, SECTION, SECTION, GLA_RANK, SECTION, SECTION, SECTION, SECTION)
    src = dict(zip(names, np.concatenate([[0], np.cumsum(sizes)[:-1]]).tolist()))
    width = dict(zip(names, sizes))
    scale = {"gq": GLA_DK ** -0.5, "dq": DIFF_DH ** -0.5 * LOG2E}
    pieces, dst = [], 0
    for name in ("gq", "gk", "gv", "gz", "dq", "dk", "dz", "dv", "gr"):
        pieces.append((src[name], dst, width[name], scale.get(name)))
        dst += width[name]
    return pieces, dst + RANK_PAD - GLA_RANK


def _w_prep_kernel(wt_ref, o_ref):
    pieces, _ = _w_in_layout()
    for src, dst, width, scale in pieces:
        piece = wt_ref[0, src:src + width, :]
        if scale is not None:
            piece = piece * scale
        if width < LANES:
            piece = jnp.concatenate([piece, jnp.zeros((LANES - width, piece.shape[1]), piece.dtype)], axis=0)
        o_ref[:, dst:dst + piece.shape[0]] = piece.T.astype(o_ref.dtype)


def _prep_w_in(w_in, layer):
    _, d, n_src = w_in.shape
    _, n_dst = _w_in_layout()
    rows = 256
    return pl.pallas_call(
        _w_prep_kernel,
        grid=(d // rows,),
        in_specs=[pl.BlockSpec((1, n_src, rows), lambda i: (layer, 0, i))],
        out_specs=pl.BlockSpec((rows, n_dst), lambda i: (i, 0)),
        out_shape=jax.ShapeDtypeStruct((d, n_dst), BF16),
        compiler_params=pltpu.CompilerParams(dimension_semantics=("arbitrary",), vmem_limit_bytes=VMEM_LIMIT),
        name="w_in_prep",
    )(jnp.swapaxes(w_in, 1, 2))


def kernel(x, c, w_ada, b_ada, norm_gain, w_in, w_gla_gate_up, b_gla_gate, gla_out_gain,
           lambda_q1, lambda_k1, lambda_q2, lambda_k2, diff_out_gain, w_out, final_gain):
    bsz, s, d = x.shape
    depth = w_in.shape[0]
    assert depth == 1, "out_proj applies the final rmsnorm, so exactly one layer is supported"
    for l in range(depth):
        mod = _adaln_mod(c, w_ada, b_ada, l)
        mod3 = mod.reshape(bsz, 1, 3 * d)
        w_in_r = _prep_w_in(w_in, l)
        wup_pad = jnp.pad(w_gla_gate_up[l], ((0, RANK_PAD - GLA_RANK), (0, 0))).astype(BF16)
        gqk, gv, gz, dq, dk, dz, dvt, la = _in_proj(
            x, mod3, norm_gain[l].reshape(1, d), w_in_r, wup_pad, b_gla_gate[l].reshape(1, -1))
        lam_init = float(0.8 - 0.6 * np.exp(-0.3 * l))
        o_diff = _diff_attn(dq, dk, dvt, dz, diff_out_gain[l],
                            lambda_q1[l].reshape(1, -1), lambda_k1[l].reshape(1, -1),
                            lambda_q2[l].reshape(1, -1), lambda_k2[l].reshape(1, -1), lam_init)
        x = _gla_out(gqk, gv, gz, la, gla_out_gain[l].reshape(1, -1), o_diff, x, mod3,
                     w_out[l].astype(BF16), final_gain.reshape(1, d))
    return x
```

```python
import functools
import math

import jax
import jax.numpy as jnp
import numpy as np
from jax import lax
from jax.experimental import pallas as pl
from jax.experimental.pallas import tpu as pltpu

F32 = jnp.float32
BF16 = jnp.bfloat16

EPS = 1e-6
LOG2E = math.log2(math.e)
GLA_HEADS = 4
GLA_DK = 64
GLA_DV = 128
GLA_RANK = 16
GLA_GATE_NORM = 16.0
GLA_CHUNK = 64
DIFF_HEADS = 4
DIFF_DH = 64
DIFF_DV = 128
LANES = 128
RANK_PAD = LANES
SECTION = 512
SUM_ROWS = 16

ROWS_IN_PROJ = 512
ROWS_GLA = 512
GLA_CUMSUM_ROWS = 256
ATTN_TQ = 256
ATTN_TK = 256
ATTN_AHEAD = 2
ATTN_SPAN_TILES = 2
BOUND_SLACK = 1.01
FAST_PATH_MIN_DENOM = 2.0 ** -90
ROWS_OUT_PROJ = 1024
VMEM_LIMIT = 48 * 1024 * 1024


def _silu(v):
    return v / (1.0 + jnp.exp(-v))


def _log_sigmoid(v):
    return jnp.minimum(v, 0.0) - jnp.log(1.0 + jnp.exp(-jnp.abs(v)))


def _adaln_kernel(c_ref, w_ref, b_ref, o_ref):
    sc = _silu(c_ref[...]).astype(BF16)
    o_ref[...] = jnp.dot(sc, w_ref[0].astype(BF16), preferred_element_type=F32) + b_ref[0]


def _adaln_mod(c, w_ada, b_ada, layer):
    bsz, d = c.shape
    n = w_ada.shape[2]
    tn = 1024
    return pl.pallas_call(
        _adaln_kernel,
        grid=(n // tn,),
        in_specs=[
            pl.BlockSpec((bsz, d), lambda j: (0, 0)),
            pl.BlockSpec((1, d, tn), lambda j: (layer, 0, j)),
            pl.BlockSpec((1, 1, tn), lambda j: (layer, 0, j)),
        ],
        out_specs=pl.BlockSpec((bsz, tn), lambda j: (0, j)),
        out_shape=jax.ShapeDtypeStruct((bsz, n), F32),
        compiler_params=pltpu.CompilerParams(dimension_semantics=("arbitrary",)),
        name="adaln_mod",
    )(c, w_ada, b_ada.reshape(b_ada.shape[0], 1, n))


def _in_proj_kernel(x_ref, shift_ref, scale_ref, gain_ref, w_ref, wup_ref, bg_ref,
                    gqk_ref, gv_ref, gz_ref, dq_ref, dk_ref, dz_ref, dvt_ref, la_ref,
                    wvt_ref):
    n_sec = 7
    dv_sec = 6

    @pl.when((pl.program_id(0) == 0) & (pl.program_id(1) == 0))
    def _():
        wvt_ref[0:SECTION, :] = w_ref[:, dv_sec * SECTION:(dv_sec + 1) * SECTION].T
        wvt_ref[SECTION:SECTION + GLA_RANK, :] = (
            w_ref[:, n_sec * SECTION:n_sec * SECTION + RANK_PAD].T[0:GLA_RANK, :])

    x = x_ref[0]
    g = gain_ref[...] * (1.0 + scale_ref[0])
    rstd = lax.rsqrt(jnp.mean(x * x, axis=-1, keepdims=True) + EPS)
    h = (x * rstd * g + shift_ref[0]).astype(BF16)

    outs = (gqk_ref, gv_ref, gz_ref, dq_ref, dk_ref, dz_ref)
    for j, o_ref in enumerate(outs):
        sec = jnp.dot(h, w_ref[:, j * SECTION:(j + 1) * SECTION], preferred_element_type=F32)
        if o_ref is gz_ref or o_ref is dz_ref:
            sec = _silu(sec)
        o_ref[0] = sec.astype(o_ref.dtype)

    vt_gr = lax.dot_general(wvt_ref[...], h, (((1,), (1,)), ((), ())), preferred_element_type=F32)
    dvt_ref[0] = vt_gr[0:SECTION].astype(dvt_ref.dtype)
    tm = vt_gr.shape[1]
    gr_t = jnp.concatenate([vt_gr[SECTION:SECTION + GLA_RANK].astype(BF16),
                            jnp.zeros((RANK_PAD - GLA_RANK, tm), BF16)], axis=0)
    logit = lax.dot_general(gr_t, wup_ref[...], (((0,), (0,)), ((), ())),
                            preferred_element_type=F32) + bg_ref[...]
    la_ref[0] = (_log_sigmoid(logit) * (1.0 / GLA_GATE_NORM)).astype(la_ref.dtype)


def _in_proj(x, mod3, norm_gain, w_in_r, wup_pad, b_gate):
    bsz, s, d = x.shape
    tm = ROWS_IN_PROJ
    ncol = w_in_r.shape[1]
    hk = GLA_HEADS * GLA_DK
    act = lambda width: pl.BlockSpec((1, tm, width), lambda b, t: (b, t, 0))
    sec = jax.ShapeDtypeStruct((bsz, s, SECTION), BF16)
    out_shapes = (
        sec,
        sec,
        sec,
        sec,
        sec,
        sec,
        jax.ShapeDtypeStruct((bsz, SECTION, s), BF16),
        jax.ShapeDtypeStruct((bsz, s, hk), BF16),
    )
    const = lambda shape: pl.BlockSpec(shape, lambda b, t: (0, 0))
    return pl.pallas_call(
        _in_proj_kernel,
        grid=(bsz, s // tm),
        in_specs=[
            act(d),
            pl.BlockSpec((1, 1, d), lambda b, t: (b, 0, 0)),
            pl.BlockSpec((1, 1, d), lambda b, t: (b, 0, 1)),
            const((1, d)), const((d, ncol)), const((RANK_PAD, hk)), const((1, hk)),
        ],
        out_specs=(act(SECTION), act(SECTION), act(SECTION), act(SECTION), act(SECTION), act(SECTION),
                   pl.BlockSpec((1, SECTION, tm), lambda b, t: (b, 0, t)), act(hk)),
        out_shape=out_shapes,
        scratch_shapes=[pltpu.VMEM((SECTION + GLA_RANK, d), BF16)],
        compiler_params=pltpu.CompilerParams(
            dimension_semantics=("arbitrary", "arbitrary"), vmem_limit_bytes=VMEM_LIMIT),
        name="in_proj",
    )(x, mod3, mod3, norm_gain, w_in_r, wup_pad, b_gate)


def _gla_out_kernel(qk_ref, v_ref, za_ref, la_ref, gain_ref, tril_ref, chunk_ind_ref, ones_bd_ref,
                    od_ref, x_ref, gate_ref, w_ref, fg_ref, o_ref, state_ref):
    tg = qk_ref.shape[1]
    half = od_ref.shape[2]
    hk = GLA_HEADS * GLA_DK
    c_sz = GLA_CHUNK
    n_ch = tg // c_sz

    @pl.when(pl.program_id(1) == 0)
    def _():
        state_ref[...] = jnp.zeros_like(state_ref)

    la = la_ref[0]
    grp = tril_ref.shape[0]
    b_all = jnp.concatenate(
        [jnp.dot(tril_ref[...], la[g * grp:(g + 1) * grp], preferred_element_type=F32)
         for g in range(tg // grp)], axis=0)
    b_last = jnp.concatenate(
        [jnp.broadcast_to(b_all[(ch + 1) * c_sz - 1:(ch + 1) * c_sz], (c_sz, hk)) for ch in range(n_ch)], axis=0)
    q = qk_ref[0, :, 0:hk]
    k = qk_ref[0, :, hk:2 * hk]
    q_in = q * jnp.exp(b_all).astype(BF16)
    k_in = k * jnp.exp(-b_all).astype(BF16)
    k_st = k * jnp.exp(b_last - b_all).astype(BF16)

    lane_head = lax.broadcasted_iota(jnp.int32, (1, hk), 1) // GLA_DK
    ri = lax.broadcasted_iota(jnp.int32, (GLA_HEADS * c_sz, c_sz), 0) % c_sz
    ci = lax.broadcasted_iota(jnp.int32, (GLA_HEADS * c_sz, c_sz), 1)
    causal = ci <= ri
    dec_all = jnp.concatenate(
        [jnp.exp(lax.dot_general(la[g * grp:(g + 1) * grp], chunk_ind_ref[...], (((0,), (0,)), ((), ())),
                                 preferred_element_type=F32)) for g in range(tg // grp)], axis=1)

    mix_diff = jnp.dot(od_ref[0], w_ref[half:2 * half, :], preferred_element_type=F32)

    chunk_rows = [slice(ch * c_sz, (ch + 1) * c_sz) for ch in range(n_ch)]
    head_rows = [slice(hh * c_sz, (hh + 1) * c_sz) for hh in range(GLA_HEADS)]
    head_cols = [slice(hh * GLA_DV, (hh + 1) * GLA_DV) for hh in range(GLA_HEADS)]
    vs = [v_ref[0, rows, :] for rows in chunk_rows]

    us = []
    for ch, rows in enumerate(chunk_rows):
        kst_t = k_st[rows].T
        us.append(jnp.concatenate(
            [jnp.dot(kst_t[hr], vs[ch][:, vc], preferred_element_type=F32)
             for hr, vc in zip(head_rows, head_cols)], axis=0))
    state = state_ref[...]
    states = []
    for ch in range(n_ch):
        states.append(state.astype(BF16))
        state = dec_all[:, ch * GLA_DV:(ch + 1) * GLA_DV] * state + us[ch]
    state_ref[...] = state
    boths = []
    for ch, rows in enumerate(chunk_rows):
        q_c = q_in[rows]
        qm = jnp.concatenate(
            [jnp.where(lane_head == hh, q_c, jnp.zeros_like(q_c)) for hh in range(GLA_HEADS)], axis=0)
        rhs = jnp.concatenate([states[ch], k_in[rows].T], axis=1)
        boths.append(jnp.dot(qm, rhs, preferred_element_type=F32))
    o_rows = []
    for ch in range(n_ch):
        inter = boths[ch][:, 0:GLA_DV]
        p = jnp.where(causal, boths[ch][:, GLA_DV:GLA_DV + c_sz], 0.0).astype(BF16)
        o_rows.append(jnp.concatenate(
            [jnp.dot(p[hr], vs[ch][:, vc], preferred_element_type=F32) + inter[hr]
             for hr, vc in zip(head_rows, head_cols)], axis=1))

    o = jnp.concatenate(o_rows, axis=0)
    ms = jnp.dot((o * o).astype(BF16), ones_bd_ref[...], preferred_element_type=F32) * (1.0 / GLA_DV)
    o_gla = (o * lax.rsqrt(ms + EPS) * gain_ref[...] * za_ref[0].astype(F32)).astype(BF16)

    mixw = jnp.dot(o_gla, w_ref[0:half, :], preferred_element_type=F32) + mix_diff
    xn = x_ref[0] + gate_ref[0] * mixw
    o_ref[0] = xn * lax.rsqrt(jnp.mean(xn * xn, axis=-1, keepdims=True) + EPS) * fg_ref[...]


def _gla_out(gqk, gv, gza, la, gla_out_gain, o_diff, x, mod3, w_out_bf, final_gain):
    bsz, s, d = x.shape
    tg = ROWS_GLA
    hk = GLA_HEADS * GLA_DK
    width = GLA_HEADS * GLA_DV
    grp = GLA_CUMSUM_ROWS
    idx = np.arange(grp)
    tril = ((idx[:, None] // GLA_CHUNK == idx[None, :] // GLA_CHUNK) & (idx[None, :] <= idx[:, None]))
    chunk_ind = idx[:, None] // GLA_CHUNK == np.arange(grp // GLA_CHUNK * GLA_DV)[None, :] // GLA_DV
    col = np.arange(width)
    ones_bd = col[:, None] // GLA_DV == col[None, :] // GLA_DV
    act = lambda w: pl.BlockSpec((1, tg, w), lambda b, t: (b, t, 0))
    const = lambda shape: pl.BlockSpec(shape, lambda b, t: (0, 0))
    return pl.pallas_call(
        _gla_out_kernel,
        grid=(bsz, s // tg),
        in_specs=[act(2 * hk), act(width), act(width), act(hk),
                  const((1, width)), const(tril.shape), const(chunk_ind.shape), const((width, width)),
                  act(o_diff.shape[2]), act(d),
                  pl.BlockSpec((1, 1, d), lambda b, t: (b, 0, 2)),
                  const(w_out_bf.shape), const((1, d))],
        out_specs=act(d),
        out_shape=jax.ShapeDtypeStruct((bsz, s, d), F32),
        scratch_shapes=[pltpu.VMEM((hk, GLA_DV), F32)],
        compiler_params=pltpu.CompilerParams(
            dimension_semantics=("arbitrary", "arbitrary"), vmem_limit_bytes=VMEM_LIMIT),
        name="gla_out_proj",
    )(gqk, gv, gza, la, gla_out_gain, jnp.asarray(tril, BF16), jnp.asarray(chunk_ind, BF16),
      jnp.asarray(ones_bd, BF16), o_diff, x, mod3, w_out_bf, final_gain)


def _diff_lambda(lq1_ref, lk1_ref, lq2_ref, lk2_ref, lam_init):
    a = jnp.sum(lq1_ref[...] * lk1_ref[...], axis=-1, keepdims=True)
    b = jnp.sum(lq2_ref[...] * lk2_ref[...], axis=-1, keepdims=True)
    return jnp.exp(a) - jnp.exp(b) + lam_init


def _attn_spans(nq, tk):
    spans = []
    for qi in range(nq):
        if ATTN_SPAN_TILES == 1:
            tile = [(qi, tk * j, tk, None) for j in range(qi)] + [(qi, qi * tk, tk, 0)]
        else:
            tile = [(qi, 2 * tk * j, 2 * tk, None) for j in range(qi // 2)]
            if qi % 2 == 1:
                tile.append((qi, (qi - 1) * tk, 2 * tk, tk))
            else:
                tile.append((qi, qi * tk, tk, 0))
        for i, sp in enumerate(tile):
            spans.append(sp + (i == 0, i == len(tile) - 1))
    return spans


def _attn_bias_tables(tq, tk):
    def split(x, terms):
        parts = []
        for _ in range(terms):
            hi = x.astype(jnp.bfloat16)
            parts.append(hi)
            x = x - hi.astype(np.float32)
        return parts

    key = np.arange(2 * tk, dtype=np.float32)
    kaug = np.zeros((2 * tk, LANES), np.float32)
    kaug[:, 0] = kaug[:, 1] = key % 256
    kaug[:, 2] = kaug[:, 3] = key - key % 256
    kaug[:, 4:7] = 1.0
    slopes = (2.0 ** (-8.0 * np.arange(1, DIFF_HEADS + 1) / DIFF_HEADS) * LOG2E).astype(np.float32)
    qry = np.tile(np.arange(tq, dtype=np.float32), 2)
    qaug = np.zeros((DIFF_HEADS, 2 * tq, LANES), jnp.bfloat16)
    for hh in range(DIFF_HEADS):
        s_hi, s_lo = split(slopes[hh:hh + 1], 2)
        qaug[hh, :, 0] = qaug[hh, :, 2] = s_hi
        qaug[hh, :, 1] = qaug[hh, :, 3] = s_lo
        for lane, part in zip((4, 5, 6), split(-slopes[hh] * qry, 3)):
            qaug[hh, :, lane] = part
    slope_rows = np.broadcast_to(slopes[:, None, None], (DIFF_HEADS, 1, LANES))
    return jnp.asarray(kaug, BF16), jnp.asarray(qaug), jnp.asarray(slope_rows, F32)


def _attn_kernel(q_ref, k_ref, vt_ref, z_ref, gain_ref, kaug_ref, qaug_ref, slope_ref,
                 lq1_ref, lk1_ref, lq2_ref, lk2_ref,
                 o_ref, qs_ref, s_ref, mb_ref, acc_ref, *, lam_init):
    tq = ATTN_TQ
    tk = ATTN_TK
    hw = 2 * DIFF_DH
    nq = q_ref.shape[1] // tq
    slope = slope_ref[0][:, 0:1]
    n_buf = ATTN_AHEAD + 1

    first_half = lax.broadcasted_iota(jnp.int32, (1, hw), 1) < DIFF_DH

    def rel_pos(nk):
        key = lax.broadcasted_iota(jnp.int32, (nk, 2 * tq), 0)
        qry = lax.broadcasted_iota(jnp.int32, (nk, 2 * tq), 1) % tq
        return qry - key

    lam = _diff_lambda(lq1_ref, lk1_ref, lq2_ref, lk2_ref, lam_init)
    spans = _attn_spans(nq, tk)

    def span_offset(qi, start):
        return -slope * float(qi * tq - start)

    for qi in range(nq):
        q = q_ref[0, qi * tq:(qi + 1) * tq, :]
        zq = jnp.zeros_like(q)
        qs_ref[qi, 0:tq, :] = jnp.where(first_half, q, zq)
        qs_ref[qi, tq:2 * tq, :] = jnp.where(first_half, zq, q)

    def masked_scores(t):
        qi, start, nk, diag_off, _, _ = spans[t]
        kblk = jnp.concatenate([k_ref[0, start:start + nk, :], kaug_ref[0:nk, :]], axis=1)
        qrhs = jnp.concatenate([qs_ref[qi], qaug_ref[0]], axis=1)
        s = lax.dot_general(kblk, qrhs, (((1,), (1,)), ((), ())), preferred_element_type=F32)
        if diag_off is not None:
            keep = rel_pos(tk) + (diag_off - (nk - tk)) >= 0
            tail = jnp.where(keep, s[nk - tk:nk], -jnp.inf)
            s = tail if nk == tk else jnp.concatenate([s[0:nk - tk], tail], axis=0)
        return s

    def values_and_ones(t):
        _, start, nk, _, _, _ = spans[t]
        return jnp.concatenate([vt_ref[0, :, start:start + nk], jnp.ones((SUM_ROWS, nk), BF16)], axis=0)

    def finish(qi, acc):
        rows = slice(qi * tq, (qi + 1) * tq)
        o_all = acc[0:DIFF_DV, :] / acc[DIFF_DV:DIFF_DV + 1, :]
        o = (o_all[:, :tq] - lam * o_all[:, tq:]).T
        y = o * lax.rsqrt(jnp.mean(o * o, axis=-1, keepdims=True) + EPS)
        y = y * gain_ref[0] * (1.0 - lam_init) * z_ref[0, rows, :].astype(F32)
        o_ref[0, rows, :] = y.astype(o_ref.dtype)

    r_i = lax.broadcasted_iota(jnp.int32, (hw, LANES), 0)
    c_i = lax.broadcasted_iota(jnp.int32, (hw, LANES), 1)
    half_sel = jnp.where(((c_i == 0) & (r_i < DIFF_DH)) | ((c_i == 1) & (r_i >= DIFF_DH)), 1.0, 0.0).astype(BF16)
    map_cols = lax.broadcasted_iota(jnp.int32, (1, 2 * tq), 1) < tq
    ones_rows = jnp.ones((8, hw), BF16)

    bounds = []
    k_norm2 = None
    for qi in range(nq):
        k_t = k_ref[0, qi * tk:(qi + 1) * tk, :]
        k_tile2 = jnp.max(jnp.dot(k_t * k_t, half_sel, preferred_element_type=F32), axis=0, keepdims=True)
        k_norm2 = k_tile2 if k_norm2 is None else jnp.maximum(k_norm2, k_tile2)
        k_max = jnp.sqrt(k_norm2)
        k_max_row = jnp.where(map_cols, k_max[:, 0:1], k_max[:, 1:2])
        q_maps = qs_ref[qi]
        q_norm2 = lax.dot_general(ones_rows, q_maps * q_maps, (((1,), (1,)), ((), ())),
                                  preferred_element_type=F32)[0:1]
        bounds.append(jnp.sqrt(q_norm2) * k_max_row * BOUND_SLACK)
    denom_min = None
    acc = None
    s_next = masked_scores(0)
    for t, (qi, start, _, _, first, last) in enumerate(spans):
        s = s_next
        if t + 1 < len(spans):
            s_next = masked_scores(t + 1)
        p = jnp.exp2(s - (bounds[qi] - span_offset(qi, start))).astype(BF16)
        pv = jnp.dot(values_and_ones(t), p, preferred_element_type=F32)
        acc = pv if first else acc + pv
        if last:
            denom = acc[DIFF_DV:DIFF_DV + 1, :]
            denom_min = denom if denom_min is None else jnp.minimum(denom_min, denom)
            finish(qi, acc)
    fast_ok = jnp.min(denom_min) >= FAST_PATH_MIN_DENOM

    def scores(t):
        qi, start, nk, _, _, _ = spans[t]
        s = masked_scores(t)
        s_ref[t % n_buf, 0:nk, :] = s
        mb_ref[t % n_buf] = jnp.max(s, axis=0, keepdims=True) + span_offset(qi, start)

    def update(t, m_old):
        qi, start, nk, _, first, last = spans[t]
        s = s_ref[t % n_buf, 0:nk, :]
        m_blk = mb_ref[t % n_buf]
        m_new = m_blk if first else jnp.maximum(m_old, m_blk)
        p = jnp.exp2(s - (m_new - span_offset(qi, start))).astype(BF16)
        pv = jnp.dot(values_and_ones(t), p, preferred_element_type=F32)
        if first:
            acc_ref[...] = pv
        else:
            acc_ref[...] = jnp.exp2(m_old - m_new) * acc_ref[...] + pv
        if last:
            finish(qi, acc_ref[...])
        return m_new

    @pl.when(jnp.logical_not(fast_ok))
    def _():
        for t in range(min(ATTN_AHEAD, len(spans))):
            scores(t)
        m = None
        for t in range(len(spans)):
            if t + ATTN_AHEAD < len(spans):
                scores(t + ATTN_AHEAD)
            m = update(t, m)


def _diff_attn(dq, dk, dvt, dz, gain, lq1, lk1, lq2, lk2, lam_init):
    bsz, s, width = dq.shape
    tq = ATTN_TQ
    assert ATTN_TQ == ATTN_TK
    hw = 2 * DIFF_DH
    rowspec = pl.BlockSpec((1, s, hw), lambda b, h: (b, 0, h))
    vtspec = pl.BlockSpec((1, DIFF_DV, s), lambda b, h: (b, h, 0))
    lspec = pl.BlockSpec((1, DIFF_DH), lambda b, h: (0, 0))
    kaug, qaug, slope_rows = _attn_bias_tables(tq, ATTN_TK)
    per_head = lambda rows: pl.BlockSpec((1, rows, LANES), lambda b, h: (h, 0, 0))
    return pl.pallas_call(
        functools.partial(_attn_kernel, lam_init=lam_init),
        grid=(bsz, DIFF_HEADS),
        in_specs=[rowspec, rowspec, vtspec, rowspec, per_head(1),
                  pl.BlockSpec(kaug.shape, lambda b, h: (0, 0)), per_head(2 * tq), per_head(1),
                  lspec, lspec, lspec, lspec],
        out_specs=rowspec,
        out_shape=jax.ShapeDtypeStruct((bsz, s, width), BF16),
        scratch_shapes=[pltpu.VMEM((s // tq, 2 * tq, hw), BF16),
                        pltpu.VMEM((ATTN_AHEAD + 1, 2 * ATTN_TK, 2 * tq), F32),
                        pltpu.VMEM((ATTN_AHEAD + 1, 1, 2 * tq), F32),
                        pltpu.VMEM((DIFF_DV + SUM_ROWS, 2 * tq), F32)],
        compiler_params=pltpu.CompilerParams(
            dimension_semantics=("arbitrary", "arbitrary"), vmem_limit_bytes=VMEM_LIMIT),
        name="diff_attn",
    )(dq, dk, dvt, dz, gain.reshape(DIFF_HEADS, 1, DIFF_DV), kaug, qaug, slope_rows, lq1, lk1, lq2, lk2)


def _w_in_layout():
    hk = GLA_HEADS * GLA_DK
    names = ("gq", "gk", "gv", "gz", "gr", "dq", "dk", "dv", "dz")
    sizes = (hk, hk, SECTION, SECTION, GLA_RANK, SECTION, SECTION, SECTION, SECTION)
    src = dict(zip(names, np.concatenate([[0], np.cumsum(sizes)[:-1]]).tolist()))
    width = dict(zip(names, sizes))
    scale = {"gq": GLA_DK ** -0.5, "dq": DIFF_DH ** -0.5 * LOG2E}
    pieces, dst = [], 0
    for name in ("gq", "gk", "gv", "gz", "dq", "dk", "dz", "dv", "gr"):
        pieces.append((src[name], dst, width[name], scale.get(name)))
        dst += width[name]
    return pieces, dst + RANK_PAD - GLA_RANK


def _w_prep_kernel(wt_ref, o_ref):
    pieces, _ = _w_in_layout()
    for src, dst, width, scale in pieces:
        piece = wt_ref[0, src:src + width, :]
        if scale is not None:
            piece = piece * scale
        if width < LANES:
            piece = jnp.concatenate([piece, jnp.zeros((LANES - width, piece.shape[1]), piece.dtype)], axis=0)
        o_ref[:, dst:dst + piece.shape[0]] = piece.T.astype(o_ref.dtype)


def _prep_w_in(w_in, layer):
    _, d, n_src = w_in.shape
    _, n_dst = _w_in_layout()
    rows = 256
    return pl.pallas_call(
        _w_prep_kernel,
        grid=(d // rows,),
        in_specs=[pl.BlockSpec((1, n_src, rows), lambda i: (layer, 0, i))],
        out_specs=pl.BlockSpec((rows, n_dst), lambda i: (i, 0)),
        out_shape=jax.ShapeDtypeStruct((d, n_dst), BF16),
        compiler_params=pltpu.CompilerParams(dimension_semantics=("arbitrary",), vmem_limit_bytes=VMEM_LIMIT),
        name="w_in_prep",
    )(jnp.swapaxes(w_in, 1, 2))


def kernel(x, c, w_ada, b_ada, norm_gain, w_in, w_gla_gate_up, b_gla_gate, gla_out_gain,
           lambda_q1, lambda_k1, lambda_q2, lambda_k2, diff_out_gain, w_out, final_gain):
    bsz, s, d = x.shape
    depth = w_in.shape[0]
    assert depth == 1, "out_proj applies the final rmsnorm, so exactly one layer is supported"
    for l in range(depth):
        mod = _adaln_mod(c, w_ada, b_ada, l)
        mod3 = mod.reshape(bsz, 1, 3 * d)
        w_in_r = _prep_w_in(w_in, l)
        wup_pad = jnp.pad(w_gla_gate_up[l], ((0, RANK_PAD - GLA_RANK), (0, 0))).astype(BF16)
        gqk, gv, gz, dq, dk, dz, dvt, la = _in_proj(
            x, mod3, norm_gain[l].reshape(1, d), w_in_r, wup_pad, b_gla_gate[l].reshape(1, -1))
        lam_init = float(0.8 - 0.6 * np.exp(-0.3 * l))
        o_diff = _diff_attn(dq, dk, dvt, dz, diff_out_gain[l],
                            lambda_q1[l].reshape(1, -1), lambda_k1[l].reshape(1, -1),
                            lambda_q2[l].reshape(1, -1), lambda_k2[l].reshape(1, -1), lam_init)
        x = _gla_out(gqk, gv, gz, la, gla_out_gain[l].reshape(1, -1), o_diff, x, mod3,
                     w_out[l].astype(BF16), final_gain.reshape(1, d))
    return x
```

```python
import functools
import math

import jax
import jax.numpy as jnp
import numpy as np
from jax import lax
from jax.experimental import pallas as pl
from jax.experimental.pallas import tpu as pltpu

F32 = jnp.float32
BF16 = jnp.bfloat16

EPS = 1e-6
LOG2E = math.log2(math.e)
GLA_HEADS = 4
GLA_DK = 64
GLA_DV = 128
GLA_RANK = 16
GLA_GATE_NORM = 16.0
GLA_CHUNK = 64
DIFF_HEADS = 4
DIFF_DH = 64
DIFF_DV = 128
LANES = 128
RANK_PAD = LANES
SECTION = 512
SUM_ROWS = 16

ROWS_IN_PROJ = 512
ROWS_GLA = 512
GLA_CUMSUM_ROWS = 256
ATTN_TQ = 256
ATTN_TK = 256
ATTN_AHEAD = 2
ATTN_SPAN_TILES = 2
BOUND_SLACK = 1.01
FAST_PATH_MIN_DENOM = 2.0 ** -90
ROWS_OUT_PROJ = 1024
VMEM_LIMIT = 48 * 1024 * 1024


def _silu(v):
    return v / (1.0 + jnp.exp(-v))


def _log_sigmoid(v):
    return jnp.minimum(v, 0.0) - jnp.log(1.0 + jnp.exp(-jnp.abs(v)))


def _adaln_kernel(c_ref, w_ref, b_ref, o_ref):
    sc = _silu(c_ref[...]).astype(BF16)
    o_ref[...] = jnp.dot(sc, w_ref[0].astype(BF16), preferred_element_type=F32) + b_ref[0]


def _adaln_mod(c, w_ada, b_ada, layer):
    bsz, d = c.shape
    n = w_ada.shape[2]
    tn = 1024
    return pl.pallas_call(
        _adaln_kernel,
        grid=(n // tn,),
        in_specs=[
            pl.BlockSpec((bsz, d), lambda j: (0, 0)),
            pl.BlockSpec((1, d, tn), lambda j: (layer, 0, j)),
            pl.BlockSpec((1, 1, tn), lambda j: (layer, 0, j)),
        ],
        out_specs=pl.BlockSpec((bsz, tn), lambda j: (0, j)),
        out_shape=jax.ShapeDtypeStruct((bsz, n), F32),
        compiler_params=pltpu.CompilerParams(dimension_semantics=("arbitrary",)),
        name="adaln_mod",
    )(c, w_ada, b_ada.reshape(b_ada.shape[0], 1, n))


def _in_proj_kernel(x_ref, shift_ref, scale_ref, gain_ref, w_ref, wup_ref, bg_ref,
                    gqk_ref, gv_ref, gz_ref, dq_ref, dk_ref, dz_ref, dvt_ref, la_ref,
                    wvt_ref):
    n_sec = 7
    dv_sec = 6

    @pl.when((pl.program_id(0) == 0) & (pl.program_id(1) == 0))
    def _():
        wvt_ref[0:SECTION, :] = w_ref[:, dv_sec * SECTION:(dv_sec + 1) * SECTION].T
        wvt_ref[SECTION:SECTION + GLA_RANK, :] = (
            w_ref[:, n_sec * SECTION:n_sec * SECTION + RANK_PAD].T[0:GLA_RANK, :])

    x = x_ref[0]
    g = gain_ref[...] * (1.0 + scale_ref[0])
    rstd = lax.rsqrt(jnp.mean(x * x, axis=-1, keepdims=True) + EPS)
    h = (x * rstd * g + shift_ref[0]).astype(BF16)

    outs = (gqk_ref, gv_ref, gz_ref, dq_ref, dk_ref, dz_ref)
    for j, o_ref in enumerate(outs):
        sec = jnp.dot(h, w_ref[:, j * SECTION:(j + 1) * SECTION], preferred_element_type=F32)
        if o_ref is gz_ref or o_ref is dz_ref:
            sec = _silu(sec)
        o_ref[0] = sec.astype(o_ref.dtype)

    vt_gr = lax.dot_general(wvt_ref[...], h, (((1,), (1,)), ((), ())), preferred_element_type=F32)
    dvt_ref[0] = vt_gr[0:SECTION].astype(dvt_ref.dtype)
    tm = vt_gr.shape[1]
    gr_t = jnp.concatenate([vt_gr[SECTION:SECTION + GLA_RANK].astype(BF16),
                            jnp.zeros((RANK_PAD - GLA_RANK, tm), BF16)], axis=0)
    logit = lax.dot_general(gr_t, wup_ref[...], (((0,), (0,)), ((), ())),
                            preferred_element_type=F32) + bg_ref[...]
    la_ref[0] = (_log_sigmoid(logit) * (1.0 / GLA_GATE_NORM)).astype(la_ref.dtype)


def _in_proj(x, mod3, norm_gain, w_in_r, wup_pad, b_gate):
    bsz, s, d = x.shape
    tm = ROWS_IN_PROJ
    ncol = w_in_r.shape[1]
    hk = GLA_HEADS * GLA_DK
    act = lambda width: pl.BlockSpec((1, tm, width), lambda b, t: (b, t, 0))
    sec = jax.ShapeDtypeStruct((bsz, s, SECTION), BF16)
    out_shapes = (
        sec,
        sec,
        sec,
        sec,
        sec,
        sec,
        jax.ShapeDtypeStruct((bsz, SECTION, s), BF16),
        jax.ShapeDtypeStruct((bsz, s, hk), BF16),
    )
    const = lambda shape: pl.BlockSpec(shape, lambda b, t: (0, 0))
    return pl.pallas_call(
        _in_proj_kernel,
        grid=(bsz, s // tm),
        in_specs=[
            act(d),
            pl.BlockSpec((1, 1, d), lambda b, t: (b, 0, 0)),
            pl.BlockSpec((1, 1, d), lambda b, t: (b, 0, 1)),
            const((1, d)), const((d, ncol)), const((RANK_PAD, hk)), const((1, hk)),
        ],
        out_specs=(act(SECTION), act(SECTION), act(SECTION), act(SECTION), act(SECTION), act(SECTION),
                   pl.BlockSpec((1, SECTION, tm), lambda b, t: (b, 0, t)), act(hk)),
        out_shape=out_shapes,
        scratch_shapes=[pltpu.VMEM((SECTION + GLA_RANK, d), BF16)],
        compiler_params=pltpu.CompilerParams(
            dimension_semantics=("arbitrary", "arbitrary"), vmem_limit_bytes=VMEM_LIMIT),
        name="in_proj",
    )(x, mod3, mod3, norm_gain, w_in_r, wup_pad, b_gate)


def _gla_out_kernel(qk_ref, v_ref, za_ref, la_ref, gain_ref, tril_ref, chunk_ind_ref, ones_bd_ref,
                    od_ref, x_ref, gate_ref, w_ref, fg_ref, o_ref, state_ref):
    tg = qk_ref.shape[1]
    half = od_ref.shape[2]
    hk = GLA_HEADS * GLA_DK
    c_sz = GLA_CHUNK
    n_ch = tg // c_sz

    @pl.when(pl.program_id(1) == 0)
    def _():
        state_ref[...] = jnp.zeros_like(state_ref)

    la = la_ref[0]
    grp = tril_ref.shape[0]
    b_all = jnp.concatenate(
        [jnp.dot(tril_ref[...], la[g * grp:(g + 1) * grp], preferred_element_type=F32)
         for g in range(tg // grp)], axis=0)
    b_last = jnp.concatenate(
        [jnp.broadcast_to(b_all[(ch + 1) * c_sz - 1:(ch + 1) * c_sz], (c_sz, hk)) for ch in range(n_ch)], axis=0)
    q = qk_ref[0, :, 0:hk]
    k = qk_ref[0, :, hk:2 * hk]
    q_in = q * jnp.exp(b_all).astype(BF16)
    k_in = k * jnp.exp(-b_all).astype(BF16)
    k_st = k * jnp.exp(b_last - b_all).astype(BF16)

    lane_head = lax.broadcasted_iota(jnp.int32, (1, hk), 1) // GLA_DK
    ri = lax.broadcasted_iota(jnp.int32, (GLA_HEADS * c_sz, c_sz), 0) % c_sz
    ci = lax.broadcasted_iota(jnp.int32, (GLA_HEADS * c_sz, c_sz), 1)
    causal = ci <= ri
    dec_all = jnp.concatenate(
        [jnp.exp(lax.dot_general(la[g * grp:(g + 1) * grp], chunk_ind_ref[...], (((0,), (0,)), ((), ())),
                                 preferred_element_type=F32)) for g in range(tg // grp)], axis=1)

    mix_diff = jnp.dot(od_ref[0], w_ref[half:2 * half, :], preferred_element_type=F32)

    chunk_rows = [slice(ch * c_sz, (ch + 1) * c_sz) for ch in range(n_ch)]
    head_rows = [slice(hh * c_sz, (hh + 1) * c_sz) for hh in range(GLA_HEADS)]
    head_cols = [slice(hh * GLA_DV, (hh + 1) * GLA_DV) for hh in range(GLA_HEADS)]
    vs = [v_ref[0, rows, :] for rows in chunk_rows]

    us = []
    for ch, rows in enumerate(chunk_rows):
        kst_t = k_st[rows].T
        us.append(jnp.concatenate(
            [jnp.dot(kst_t[hr], vs[ch][:, vc], preferred_element_type=F32)
             for hr, vc in zip(head_rows, head_cols)], axis=0))
    state = state_ref[...]
    states = []
    for ch in range(n_ch):
        states.append(state.astype(BF16))
        state = dec_all[:, ch * GLA_DV:(ch + 1) * GLA_DV] * state + us[ch]
    state_ref[...] = state
    boths = []
    for ch, rows in enumerate(chunk_rows):
        q_c = q_in[rows]
        qm = jnp.concatenate(
            [jnp.where(lane_head == hh, q_c, jnp.zeros_like(q_c)) for hh in range(GLA_HEADS)], axis=0)
        rhs = jnp.concatenate([states[ch], k_in[rows].T], axis=1)
        boths.append(jnp.dot(qm, rhs, preferred_element_type=F32))
    o_rows = []
    for ch in range(n_ch):
        inter = boths[ch][:, 0:GLA_DV]
        p = jnp.where(causal, boths[ch][:, GLA_DV:GLA_DV + c_sz], 0.0).astype(BF16)
        o_rows.append(jnp.concatenate(
            [jnp.dot(p[hr], vs[ch][:, vc], preferred_element_type=F32) + inter[hr]
             for hr, vc in zip(head_rows, head_cols)], axis=1))

    o = jnp.concatenate(o_rows, axis=0)
    ms = jnp.dot((o * o).astype(BF16), ones_bd_ref[...], preferred_element_type=F32) * (1.0 / GLA_DV)
    o_gla = (o * lax.rsqrt(ms + EPS) * gain_ref[...] * za_ref[0].astype(F32)).astype(BF16)

    mixw = jnp.dot(o_gla, w_ref[0:half, :], preferred_element_type=F32) + mix_diff
    xn = x_ref[0] + gate_ref[0] * mixw
    o_ref[0] = xn * lax.rsqrt(jnp.mean(xn * xn, axis=-1, keepdims=True) + EPS) * fg_ref[...]


def _gla_out(gqk, gv, gza, la, gla_out_gain, o_diff, x, mod3, w_out_bf, final_gain):
    bsz, s, d = x.shape
    tg = ROWS_GLA
    hk = GLA_HEADS * GLA_DK
    width = GLA_HEADS * GLA_DV
    grp = GLA_CUMSUM_ROWS
    idx = np.arange(grp)
    tril = ((idx[:, None] // GLA_CHUNK == idx[None, :] // GLA_CHUNK) & (idx[None, :] <= idx[:, None]))
    chunk_ind = idx[:, None] // GLA_CHUNK == np.arange(grp // GLA_CHUNK * GLA_DV)[None, :] // GLA_DV
    col = np.arange(width)
    ones_bd = col[:, None] // GLA_DV == col[None, :] // GLA_DV
    act = lambda w: pl.BlockSpec((1, tg, w), lambda b, t: (b, t, 0))
    const = lambda shape: pl.BlockSpec(shape, lambda b, t: (0, 0))
    return pl.pallas_call(
        _gla_out_kernel,
        grid=(bsz, s // tg),
        in_specs=[act(2 * hk), act(width), act(width), act(hk),
                  const((1, width)), const(tril.shape), const(chunk_ind.shape), const((width, width)),
                  act(o_diff.shape[2]), act(d),
                  pl.BlockSpec((1, 1, d), lambda b, t: (b, 0, 2)),
                  const(w_out_bf.shape), const((1, d))],
        out_specs=act(d),
        out_shape=jax.ShapeDtypeStruct((bsz, s, d), F32),
        scratch_shapes=[pltpu.VMEM((hk, GLA_DV), F32)],
        compiler_params=pltpu.CompilerParams(
            dimension_semantics=("arbitrary", "arbitrary"), vmem_limit_bytes=VMEM_LIMIT),
        name="gla_out_proj",
    )(gqk, gv, gza, la, gla_out_gain, jnp.asarray(tril, BF16), jnp.asarray(chunk_ind, BF16),
      jnp.asarray(ones_bd, BF16), o_diff, x, mod3, w_out_bf, final_gain)


def _diff_lambda(lq1_ref, lk1_ref, lq2_ref, lk2_ref, lam_init):
    a = jnp.sum(lq1_ref[...] * lk1_ref[...], axis=-1, keepdims=True)
    b = jnp.sum(lq2_ref[...] * lk2_ref[...], axis=-1, keepdims=True)
    return jnp.exp(a) - jnp.exp(b) + lam_init


def _attn_spans(nq, tk):
    spans = []
    for qi in range(nq):
        if ATTN_SPAN_TILES == 1:
            tile = [(qi, tk * j, tk, None) for j in range(qi)] + [(qi, qi * tk, tk, 0)]
        else:
            tile = [(qi, 2 * tk * j, 2 * tk, None) for j in range(qi // 2)]
            if qi % 2 == 1:
                tile.append((qi, (qi - 1) * tk, 2 * tk, tk))
            else:
                tile.append((qi, qi * tk, tk, 0))
        for i, sp in enumerate(tile):
            spans.append(sp + (i == 0, i == len(tile) - 1))
    return spans


def _attn_bias_tables(tq, tk):
    def split(x, terms):
        parts = []
        for _ in range(terms):
            hi = x.astype(jnp.bfloat16)
            parts.append(hi)
            x = x - hi.astype(np.float32)
        return parts

    key = np.arange(2 * tk, dtype=np.float32)
    kaug = np.zeros((2 * tk, LANES), np.float32)
    kaug[:, 0] = kaug[:, 1] = key % 256
    kaug[:, 2] = kaug[:, 3] = key - key % 256
    kaug[:, 4:7] = 1.0
    slopes = (2.0 ** (-8.0 * np.arange(1, DIFF_HEADS + 1) / DIFF_HEADS) * LOG2E).astype(np.float32)
    qry = np.tile(np.arange(tq, dtype=np.float32), 2)
    qaug = np.zeros((DIFF_HEADS, 2 * tq, LANES), jnp.bfloat16)
    for hh in range(DIFF_HEADS):
        s_hi, s_lo = split(slopes[hh:hh + 1], 2)
        qaug[hh, :, 0] = qaug[hh, :, 2] = s_hi
        qaug[hh, :, 1] = qaug[hh, :, 3] = s_lo
        for lane, part in zip((4, 5, 6), split(-slopes[hh] * qry, 3)):
            qaug[hh, :, lane] = part
    slope_rows = np.broadcast_to(slopes[:, None, None], (DIFF_HEADS, 1, LANES))
    return jnp.asarray(kaug, BF16), jnp.asarray(qaug), jnp.asarray(slope_rows, F32)


def _attn_kernel(q_ref, k_ref, vt_ref, z_ref, gain_ref, kaug_ref, qaug_ref, slope_ref,
                 lq1_ref, lk1_ref, lq2_ref, lk2_ref,
                 o_ref, qs_ref, s_ref, mb_ref, acc_ref, *, lam_init):
    tq = ATTN_TQ
    tk = ATTN_TK
    hw = 2 * DIFF_DH
    nq = q_ref.shape[1] // tq
    slope = slope_ref[0][:, 0:1]
    n_buf = ATTN_AHEAD + 1

    first_half = lax.broadcasted_iota(jnp.int32, (1, hw), 1) < DIFF_DH

    def rel_pos(nk):
        key = lax.broadcasted_iota(jnp.int32, (nk, 2 * tq), 0)
        qry = lax.broadcasted_iota(jnp.int32, (nk, 2 * tq), 1) % tq
        return qry - key

    lam = _diff_lambda(lq1_ref, lk1_ref, lq2_ref, lk2_ref, lam_init)
    spans = _attn_spans(nq, tk)

    def span_offset(qi, start):
        return -slope * float(qi * tq - start)

    for qi in range(nq):
        q = q_ref[0, qi * tq:(qi + 1) * tq, :]
        zq = jnp.zeros_like(q)
        qs_ref[qi, 0:tq, 0:hw] = jnp.where(first_half, q, zq)
        qs_ref[qi, tq:2 * tq, 0:hw] = jnp.where(first_half, zq, q)
        qs_ref[qi, :, hw:hw + LANES] = qaug_ref[0]

    def masked_scores(t):
        qi, start, nk, diag_off, _, _ = spans[t]
        kblk = jnp.concatenate([k_ref[0, start:start + nk, :], kaug_ref[0:nk, :]], axis=1)
        s = lax.dot_general(kblk, qs_ref[qi], (((1,), (1,)), ((), ())), preferred_element_type=F32)
        if diag_off is not None:
            keep = rel_pos(tk) + (diag_off - (nk - tk)) >= 0
            tail = jnp.where(keep, s[nk - tk:nk], -jnp.inf)
            s = tail if nk == tk else jnp.concatenate([s[0:nk - tk], tail], axis=0)
        return s

    def values_and_ones(t):
        _, start, nk, _, _, _ = spans[t]
        return jnp.concatenate([vt_ref[0, :, start:start + nk], jnp.ones((SUM_ROWS, nk), BF16)], axis=0)

    def finish(qi, acc):
        rows = slice(qi * tq, (qi + 1) * tq)
        o_all = acc[0:DIFF_DV, :] / acc[DIFF_DV:DIFF_DV + 1, :]
        o = (o_all[:, :tq] - lam * o_all[:, tq:]).T
        y = o * lax.rsqrt(jnp.mean(o * o, axis=-1, keepdims=True) + EPS)
        y = y * gain_ref[0] * (1.0 - lam_init) * z_ref[0, rows, :].astype(F32)
        o_ref[0, rows, :] = y.astype(o_ref.dtype)

    r_i = lax.broadcasted_iota(jnp.int32, (hw, LANES), 0)
    c_i = lax.broadcasted_iota(jnp.int32, (hw, LANES), 1)
    half_sel = jnp.where(((c_i == 0) & (r_i < DIFF_DH)) | ((c_i == 1) & (r_i >= DIFF_DH)), 1.0, 0.0).astype(BF16)
    map_cols = lax.broadcasted_iota(jnp.int32, (1, 2 * tq), 1) < tq
    ones_rows = jnp.ones((8, hw), BF16)

    k_all = k_ref[0]
    k_norm2 = jnp.max(jnp.dot(k_all * k_all, half_sel, preferred_element_type=F32), axis=0, keepdims=True)
    k_max = jnp.sqrt(k_norm2)
    k_max_row = jnp.where(map_cols, k_max[:, 0:1], k_max[:, 1:2])
    bounds = []
    for qi in range(nq):
        q_maps = qs_ref[qi, :, 0:hw]
        q_norm2 = lax.dot_general(ones_rows, q_maps * q_maps, (((1,), (1,)), ((), ())),
                                  preferred_element_type=F32)[0:1]
        bounds.append(jnp.sqrt(q_norm2) * k_max_row * BOUND_SLACK)
    denom_min = None
    acc = None
    s_next = masked_scores(0)
    for t, (qi, start, _, _, first, last) in enumerate(spans):
        s = s_next
        if t + 1 < len(spans):
            s_next = masked_scores(t + 1)
        p = jnp.exp2(s - (bounds[qi] - span_offset(qi, start))).astype(BF16)
        pv = jnp.dot(values_and_ones(t), p, preferred_element_type=F32)
        acc = pv if first else acc + pv
        if last:
            denom = acc[DIFF_DV:DIFF_DV + 1, :]
            denom_min = denom if denom_min is None else jnp.minimum(denom_min, denom)
            finish(qi, acc)
    fast_ok = jnp.min(denom_min) >= FAST_PATH_MIN_DENOM

    def scores(t):
        qi, start, nk, _, _, _ = spans[t]
        s = masked_scores(t)
        s_ref[t % n_buf, 0:nk, :] = s
        mb_ref[t % n_buf] = jnp.max(s, axis=0, keepdims=True) + span_offset(qi, start)

    def update(t, m_old):
        qi, start, nk, _, first, last = spans[t]
        s = s_ref[t % n_buf, 0:nk, :]
        m_blk = mb_ref[t % n_buf]
        m_new = m_blk if first else jnp.maximum(m_old, m_blk)
        p = jnp.exp2(s - (m_new - span_offset(qi, start))).astype(BF16)
        pv = jnp.dot(values_and_ones(t), p, preferred_element_type=F32)
        if first:
            acc_ref[...] = pv
        else:
            acc_ref[...] = jnp.exp2(m_old - m_new) * acc_ref[...] + pv
        if last:
            finish(qi, acc_ref[...])
        return m_new

    @pl.when(jnp.logical_not(fast_ok))
    def _():
        for t in range(min(ATTN_AHEAD, len(spans))):
            scores(t)
        m = None
        for t in range(len(spans)):
            if t + ATTN_AHEAD < len(spans):
                scores(t + ATTN_AHEAD)
            m = update(t, m)


def _diff_attn(dq, dk, dvt, dz, gain, lq1, lk1, lq2, lk2, lam_init):
    bsz, s, width = dq.shape
    tq = ATTN_TQ
    assert ATTN_TQ == ATTN_TK
    hw = 2 * DIFF_DH
    rowspec = pl.BlockSpec((1, s, hw), lambda b, h: (b, 0, h))
    vtspec = pl.BlockSpec((1, DIFF_DV, s), lambda b, h: (b, h, 0))
    lspec = pl.BlockSpec((1, DIFF_DH), lambda b, h: (0, 0))
    kaug, qaug, slope_rows = _attn_bias_tables(tq, ATTN_TK)
    per_head = lambda rows: pl.BlockSpec((1, rows, LANES), lambda b, h: (h, 0, 0))
    return pl.pallas_call(
        functools.partial(_attn_kernel, lam_init=lam_init),
        grid=(bsz, DIFF_HEADS),
        in_specs=[rowspec, rowspec, vtspec, rowspec, per_head(1),
                  pl.BlockSpec(kaug.shape, lambda b, h: (0, 0)), per_head(2 * tq), per_head(1),
                  lspec, lspec, lspec, lspec],
        out_specs=rowspec,
        out_shape=jax.ShapeDtypeStruct((bsz, s, width), BF16),
        scratch_shapes=[pltpu.VMEM((s // tq, 2 * tq, hw + LANES), BF16),
                        pltpu.VMEM((ATTN_AHEAD + 1, 2 * ATTN_TK, 2 * tq), F32),
                        pltpu.VMEM((ATTN_AHEAD + 1, 1, 2 * tq), F32),
                        pltpu.VMEM((DIFF_DV + SUM_ROWS, 2 * tq), F32)],
        compiler_params=pltpu.CompilerParams(
            dimension_semantics=("arbitrary", "arbitrary"), vmem_limit_bytes=VMEM_LIMIT),
        name="diff_attn",
    )(dq, dk, dvt, dz, gain.reshape(DIFF_HEADS, 1, DIFF_DV), kaug, qaug, slope_rows, lq1, lk1, lq2, lk2)


def _w_in_layout():
    hk = GLA_HEADS * GLA_DK
    names = ("gq", "gk", "gv", "gz", "gr", "dq", "dk", "dv", "dz")
    sizes = (hk, hk, SECTION, SECTION, GLA_RANK, SECTION, SECTION, SECTION, SECTION)
    src = dict(zip(names, np.concatenate([[0], np.cumsum(sizes)[:-1]]).tolist()))
    width = dict(zip(names, sizes))
    scale = {"gq": GLA_DK ** -0.5, "dq": DIFF_DH ** -0.5 * LOG2E}
    pieces, dst = [], 0
    for name in ("gq", "gk", "gv", "gz", "dq", "dk", "dz", "dv", "gr"):
        pieces.append((src[name], dst, width[name], scale.get(name)))
        dst += width[name]
    return pieces, dst + RANK_PAD - GLA_RANK


def _w_prep_kernel(wt_ref, o_ref):
    pieces, _ = _w_in_layout()
    for src, dst, width, scale in pieces:
        piece = wt_ref[0, src:src + width, :]
        if scale is not None:
            piece = piece * scale
        if width < LANES:
            piece = jnp.concatenate([piece, jnp.zeros((LANES - width, piece.shape[1]), piece.dtype)], axis=0)
        o_ref[:, dst:dst + piece.shape[0]] = piece.T.astype(o_ref.dtype)


def _prep_w_in(w_in, layer):
    _, d, n_src = w_in.shape
    _, n_dst = _w_in_layout()
    rows = 256
    return pl.pallas_call(
        _w_prep_kernel,
        grid=(d // rows,),
        in_specs=[pl.BlockSpec((1, n_src, rows), lambda i: (layer, 0, i))],
        out_specs=pl.BlockSpec((rows, n_dst), lambda i: (i, 0)),
        out_shape=jax.ShapeDtypeStruct((d, n_dst), BF16),
        compiler_params=pltpu.CompilerParams(dimension_semantics=("arbitrary",), vmem_limit_bytes=VMEM_LIMIT),
        name="w_in_prep",
    )(jnp.swapaxes(w_in, 1, 2))


def kernel(x, c, w_ada, b_ada, norm_gain, w_in, w_gla_gate_up, b_gla_gate, gla_out_gain,
           lambda_q1, lambda_k1, lambda_q2, lambda_k2, diff_out_gain, w_out, final_gain):
    bsz, s, d = x.shape
    depth = w_in.shape[0]
    assert depth == 1, "out_proj applies the final rmsnorm, so exactly one layer is supported"
    for l in range(depth):
        mod = _adaln_mod(c, w_ada, b_ada, l)
        mod3 = mod.reshape(bsz, 1, 3 * d)
        w_in_r = _prep_w_in(w_in, l)
        wup_pad = jnp.pad(w_gla_gate_up[l], ((0, RANK_PAD - GLA_RANK), (0, 0))).astype(BF16)
        gqk, gv, gz, dq, dk, dz, dvt, la = _in_proj(
            x, mod3, norm_gain[l].reshape(1, d), w_in_r, wup_pad, b_gla_gate[l].reshape(1, -1))
        lam_init = float(0.8 - 0.6 * np.exp(-0.3 * l))
        o_diff = _diff_attn(dq, dk, dvt, dz, diff_out_gain[l],
                            lambda_q1[l].reshape(1, -1), lambda_k1[l].reshape(1, -1),
                            lambda_q2[l].reshape(1, -1), lambda_k2[l].reshape(1, -1), lam_init)
        x = _gla_out(gqk, gv, gz, la, gla_out_gain[l].reshape(1, -1), o_diff, x, mod3,
                     w_out[l].astype(BF16), final_gain.reshape(1, d))
    return x
```

```python
import functools
import math

import jax
import jax.numpy as jnp
import numpy as np
from jax import lax
from jax.experimental import pallas as pl
from jax.experimental.pallas import tpu as pltpu

F32 = jnp.float32
BF16 = jnp.bfloat16

EPS = 1e-6
LOG2E = math.log2(math.e)
GLA_HEADS = 4
GLA_DK = 64
GLA_DV = 128
GLA_RANK = 16
GLA_GATE_NORM = 16.0
GLA_CHUNK = 64
DIFF_HEADS = 4
DIFF_DH = 64
DIFF_DV = 128
LANES = 128
RANK_PAD = LANES
SECTION = 512
SUM_ROWS = 16

ROWS_IN_PROJ = 1024
ROWS_GLA = 512
GLA_CUMSUM_ROWS = 256
ATTN_TQ = 256
ATTN_TK = 256
ATTN_AHEAD = 2
ATTN_SPAN_TILES = 2
BOUND_SLACK = 1.01
FAST_PATH_MIN_DENOM = 2.0 ** -90
ROWS_OUT_PROJ = 1024
VMEM_LIMIT = 48 * 1024 * 1024


def _silu(v):
    return v / (1.0 + jnp.exp(-v))


def _log_sigmoid(v):
    return jnp.minimum(v, 0.0) - jnp.log(1.0 + jnp.exp(-jnp.abs(v)))


def _adaln_kernel(c_ref, w_ref, b_ref, o_ref):
    sc = _silu(c_ref[...]).astype(BF16)
    o_ref[...] = jnp.dot(sc, w_ref[0].astype(BF16), preferred_element_type=F32) + b_ref[0]


def _adaln_mod(c, w_ada, b_ada, layer):
    bsz, d = c.shape
    n = w_ada.shape[2]
    tn = 1024
    return pl.pallas_call(
        _adaln_kernel,
        grid=(n // tn,),
        in_specs=[
            pl.BlockSpec((bsz, d), lambda j: (0, 0)),
            pl.BlockSpec((1, d, tn), lambda j: (layer, 0, j)),
            pl.BlockSpec((1, 1, tn), lambda j: (layer, 0, j)),
        ],
        out_specs=pl.BlockSpec((bsz, tn), lambda j: (0, j)),
        out_shape=jax.ShapeDtypeStruct((bsz, n), F32),
        compiler_params=pltpu.CompilerParams(dimension_semantics=("arbitrary",)),
        name="adaln_mod",
    )(c, w_ada, b_ada.reshape(b_ada.shape[0], 1, n))


def _in_proj_kernel(x_ref, shift_ref, scale_ref, gain_ref, w_ref, wup_ref, bg_ref,
                    gqk_ref, gv_ref, gz_ref, dq_ref, dk_ref, dz_ref, dvt_ref, la_ref,
                    wvt_ref):
    n_sec = 7
    dv_sec = 6

    @pl.when((pl.program_id(0) == 0) & (pl.program_id(1) == 0))
    def _():
        wvt_ref[0:SECTION, :] = w_ref[:, dv_sec * SECTION:(dv_sec + 1) * SECTION].T
        wvt_ref[SECTION:SECTION + GLA_RANK, :] = (
            w_ref[:, n_sec * SECTION:n_sec * SECTION + RANK_PAD].T[0:GLA_RANK, :])

    x = x_ref[0]
    g = gain_ref[...] * (1.0 + scale_ref[0])
    rstd = lax.rsqrt(jnp.mean(x * x, axis=-1, keepdims=True) + EPS)
    h = (x * rstd * g + shift_ref[0]).astype(BF16)

    outs = (gqk_ref, gv_ref, gz_ref, dq_ref, dk_ref, dz_ref)
    for j, o_ref in enumerate(outs):
        sec = jnp.dot(h, w_ref[:, j * SECTION:(j + 1) * SECTION], preferred_element_type=F32)
        if o_ref is gz_ref or o_ref is dz_ref:
            sec = _silu(sec)
        o_ref[0] = sec.astype(o_ref.dtype)

    vt_gr = lax.dot_general(wvt_ref[...], h, (((1,), (1,)), ((), ())), preferred_element_type=F32)
    dvt_ref[0] = vt_gr[0:SECTION].astype(dvt_ref.dtype)
    tm = vt_gr.shape[1]
    gr_t = jnp.concatenate([vt_gr[SECTION:SECTION + GLA_RANK].astype(BF16),
                            jnp.zeros((RANK_PAD - GLA_RANK, tm), BF16)], axis=0)
    logit = lax.dot_general(gr_t, wup_ref[...], (((0,), (0,)), ((), ())),
                            preferred_element_type=F32) + bg_ref[...]
    la_ref[0] = (_log_sigmoid(logit) * (1.0 / GLA_GATE_NORM)).astype(la_ref.dtype)


def _in_proj(x, mod3, norm_gain, w_in_r, wup_pad, b_gate):
    bsz, s, d = x.shape
    tm = ROWS_IN_PROJ
    ncol = w_in_r.shape[1]
    hk = GLA_HEADS * GLA_DK
    act = lambda width: pl.BlockSpec((1, tm, width), lambda b, t: (b, t, 0))
    sec = jax.ShapeDtypeStruct((bsz, s, SECTION), BF16)
    out_shapes = (
        sec,
        sec,
        sec,
        sec,
        sec,
        sec,
        jax.ShapeDtypeStruct((bsz, SECTION, s), BF16),
        jax.ShapeDtypeStruct((bsz, s, hk), BF16),
    )
    const = lambda shape: pl.BlockSpec(shape, lambda b, t: (0, 0))
    return pl.pallas_call(
        _in_proj_kernel,
        grid=(bsz, s // tm),
        in_specs=[
            act(d),
            pl.BlockSpec((1, 1, d), lambda b, t: (b, 0, 0)),
            pl.BlockSpec((1, 1, d), lambda b, t: (b, 0, 1)),
            const((1, d)), const((d, ncol)), const((RANK_PAD, hk)), const((1, hk)),
        ],
        out_specs=(act(SECTION), act(SECTION), act(SECTION), act(SECTION), act(SECTION), act(SECTION),
                   pl.BlockSpec((1, SECTION, tm), lambda b, t: (b, 0, t)), act(hk)),
        out_shape=out_shapes,
        scratch_shapes=[pltpu.VMEM((SECTION + GLA_RANK, d), BF16)],
        compiler_params=pltpu.CompilerParams(
            dimension_semantics=("arbitrary", "arbitrary"), vmem_limit_bytes=VMEM_LIMIT),
        name="in_proj",
    )(x, mod3, mod3, norm_gain, w_in_r, wup_pad, b_gate)


def _gla_out_kernel(qk_ref, v_ref, za_ref, la_ref, gain_ref, tril_ref, chunk_ind_ref, ones_bd_ref,
                    od_ref, x_ref, gate_ref, w_ref, fg_ref, o_ref, state_ref):
    tg = qk_ref.shape[1]
    half = od_ref.shape[2]
    hk = GLA_HEADS * GLA_DK
    c_sz = GLA_CHUNK
    n_ch = tg // c_sz

    @pl.when(pl.program_id(1) == 0)
    def _():
        state_ref[...] = jnp.zeros_like(state_ref)

    la = la_ref[0]
    grp = tril_ref.shape[0]
    b_all = jnp.concatenate(
        [jnp.dot(tril_ref[...], la[g * grp:(g + 1) * grp], preferred_element_type=F32)
         for g in range(tg // grp)], axis=0)
    b_last = jnp.concatenate(
        [jnp.broadcast_to(b_all[(ch + 1) * c_sz - 1:(ch + 1) * c_sz], (c_sz, hk)) for ch in range(n_ch)], axis=0)
    q = qk_ref[0, :, 0:hk]
    k = qk_ref[0, :, hk:2 * hk]
    q_in = q * jnp.exp(b_all).astype(BF16)
    k_in = k * jnp.exp(-b_all).astype(BF16)
    k_st = k * jnp.exp(b_last - b_all).astype(BF16)

    lane_head = lax.broadcasted_iota(jnp.int32, (1, hk), 1) // GLA_DK
    ri = lax.broadcasted_iota(jnp.int32, (GLA_HEADS * c_sz, c_sz), 0) % c_sz
    ci = lax.broadcasted_iota(jnp.int32, (GLA_HEADS * c_sz, c_sz), 1)
    causal = ci <= ri
    dec_all = jnp.concatenate(
        [jnp.exp(lax.dot_general(la[g * grp:(g + 1) * grp], chunk_ind_ref[...], (((0,), (0,)), ((), ())),
                                 preferred_element_type=F32)) for g in range(tg // grp)], axis=1)

    mix_diff = jnp.dot(od_ref[0], w_ref[half:2 * half, :], preferred_element_type=F32)

    chunk_rows = [slice(ch * c_sz, (ch + 1) * c_sz) for ch in range(n_ch)]
    head_rows = [slice(hh * c_sz, (hh + 1) * c_sz) for hh in range(GLA_HEADS)]
    head_cols = [slice(hh * GLA_DV, (hh + 1) * GLA_DV) for hh in range(GLA_HEADS)]
    vs = [v_ref[0, rows, :] for rows in chunk_rows]

    us = []
    for ch, rows in enumerate(chunk_rows):
        kst_t = k_st[rows].T
        us.append(jnp.concatenate(
            [jnp.dot(kst_t[hr], vs[ch][:, vc], preferred_element_type=F32)
             for hr, vc in zip(head_rows, head_cols)], axis=0))
    state = state_ref[...]
    states = []
    for ch in range(n_ch):
        states.append(state.astype(BF16))
        state = dec_all[:, ch * GLA_DV:(ch + 1) * GLA_DV] * state + us[ch]
    state_ref[...] = state
    boths = []
    for ch, rows in enumerate(chunk_rows):
        q_c = q_in[rows]
        qm = jnp.concatenate(
            [jnp.where(lane_head == hh, q_c, jnp.zeros_like(q_c)) for hh in range(GLA_HEADS)], axis=0)
        rhs = jnp.concatenate([states[ch], k_in[rows].T], axis=1)
        boths.append(jnp.dot(qm, rhs, preferred_element_type=F32))
    o_rows = []
    for ch in range(n_ch):
        inter = boths[ch][:, 0:GLA_DV]
        p = jnp.where(causal, boths[ch][:, GLA_DV:GLA_DV + c_sz], 0.0).astype(BF16)
        o_rows.append(jnp.concatenate(
            [jnp.dot(p[hr], vs[ch][:, vc], preferred_element_type=F32) + inter[hr]
             for hr, vc in zip(head_rows, head_cols)], axis=1))

    o = jnp.concatenate(o_rows, axis=0)
    ms = jnp.dot((o * o).astype(BF16), ones_bd_ref[...], preferred_element_type=F32) * (1.0 / GLA_DV)
    o_gla = (o * lax.rsqrt(ms + EPS) * gain_ref[...] * za_ref[0].astype(F32)).astype(BF16)

    mixw = jnp.dot(o_gla, w_ref[0:half, :], preferred_element_type=F32) + mix_diff
    xn = x_ref[0] + gate_ref[0] * mixw
    o_ref[0] = xn * lax.rsqrt(jnp.mean(xn * xn, axis=-1, keepdims=True) + EPS) * fg_ref[...]


def _gla_out(gqk, gv, gza, la, gla_out_gain, o_diff, x, mod3, w_out_bf, final_gain):
    bsz, s, d = x.shape
    tg = ROWS_GLA
    hk = GLA_HEADS * GLA_DK
    width = GLA_HEADS * GLA_DV
    grp = GLA_CUMSUM_ROWS
    idx = np.arange(grp)
    tril = ((idx[:, None] // GLA_CHUNK == idx[None, :] // GLA_CHUNK) & (idx[None, :] <= idx[:, None]))
    chunk_ind = idx[:, None] // GLA_CHUNK == np.arange(grp // GLA_CHUNK * GLA_DV)[None, :] // GLA_DV
    col = np.arange(width)
    ones_bd = col[:, None] // GLA_DV == col[None, :] // GLA_DV
    act = lambda w: pl.BlockSpec((1, tg, w), lambda b, t: (b, t, 0))
    const = lambda shape: pl.BlockSpec(shape, lambda b, t: (0, 0))
    return pl.pallas_call(
        _gla_out_kernel,
        grid=(bsz, s // tg),
        in_specs=[act(2 * hk), act(width), act(width), act(hk),
                  const((1, width)), const(tril.shape), const(chunk_ind.shape), const((width, width)),
                  act(o_diff.shape[2]), act(d),
                  pl.BlockSpec((1, 1, d), lambda b, t: (b, 0, 2)),
                  const(w_out_bf.shape), const((1, d))],
        out_specs=act(d),
        out_shape=jax.ShapeDtypeStruct((bsz, s, d), F32),
        scratch_shapes=[pltpu.VMEM((hk, GLA_DV), F32)],
        compiler_params=pltpu.CompilerParams(
            dimension_semantics=("arbitrary", "arbitrary"), vmem_limit_bytes=VMEM_LIMIT),
        name="gla_out_proj",
    )(gqk, gv, gza, la, gla_out_gain, jnp.asarray(tril, BF16), jnp.asarray(chunk_ind, BF16),
      jnp.asarray(ones_bd, BF16), o_diff, x, mod3, w_out_bf, final_gain)


def _diff_lambda(lq1_ref, lk1_ref, lq2_ref, lk2_ref, lam_init):
    a = jnp.sum(lq1_ref[...] * lk1_ref[...], axis=-1, keepdims=True)
    b = jnp.sum(lq2_ref[...] * lk2_ref[...], axis=-1, keepdims=True)
    return jnp.exp(a) - jnp.exp(b) + lam_init


def _attn_spans(nq, tk):
    spans = []
    for qi in range(nq):
        if ATTN_SPAN_TILES == 1:
            tile = [(qi, tk * j, tk, None) for j in range(qi)] + [(qi, qi * tk, tk, 0)]
        else:
            tile = [(qi, 2 * tk * j, 2 * tk, None) for j in range(qi // 2)]
            if qi % 2 == 1:
                tile.append((qi, (qi - 1) * tk, 2 * tk, tk))
            else:
                tile.append((qi, qi * tk, tk, 0))
        for i, sp in enumerate(tile):
            spans.append(sp + (i == 0, i == len(tile) - 1))
    return spans


def _attn_bias_tables(tq, tk):
    def split(x, terms):
        parts = []
        for _ in range(terms):
            hi = x.astype(jnp.bfloat16)
            parts.append(hi)
            x = x - hi.astype(np.float32)
        return parts

    key = np.arange(2 * tk, dtype=np.float32)
    kaug = np.zeros((2 * tk, LANES), np.float32)
    kaug[:, 0] = kaug[:, 1] = key % 256
    kaug[:, 2] = kaug[:, 3] = key - key % 256
    kaug[:, 4:7] = 1.0
    slopes = (2.0 ** (-8.0 * np.arange(1, DIFF_HEADS + 1) / DIFF_HEADS) * LOG2E).astype(np.float32)
    qry = np.tile(np.arange(tq, dtype=np.float32), 2)
    qaug = np.zeros((DIFF_HEADS, 2 * tq, LANES), jnp.bfloat16)
    for hh in range(DIFF_HEADS):
        s_hi, s_lo = split(slopes[hh:hh + 1], 2)
        qaug[hh, :, 0] = qaug[hh, :, 2] = s_hi
        qaug[hh, :, 1] = qaug[hh, :, 3] = s_lo
        for lane, part in zip((4, 5, 6), split(-slopes[hh] * qry, 3)):
            qaug[hh, :, lane] = part
    slope_rows = np.broadcast_to(slopes[:, None, None], (DIFF_HEADS, 1, LANES))
    return jnp.asarray(kaug, BF16), jnp.asarray(qaug), jnp.asarray(slope_rows, F32)


def _attn_kernel(q_ref, k_ref, vt_ref, z_ref, gain_ref, kaug_ref, qaug_ref, slope_ref,
                 lq1_ref, lk1_ref, lq2_ref, lk2_ref,
                 o_ref, qs_ref, s_ref, mb_ref, acc_ref, *, lam_init):
    tq = ATTN_TQ
    tk = ATTN_TK
    hw = 2 * DIFF_DH
    nq = q_ref.shape[1] // tq
    slope = slope_ref[0][:, 0:1]
    n_buf = ATTN_AHEAD + 1

    first_half = lax.broadcasted_iota(jnp.int32, (1, hw), 1) < DIFF_DH

    def rel_pos(nk):
        key = lax.broadcasted_iota(jnp.int32, (nk, 2 * tq), 0)
        qry = lax.broadcasted_iota(jnp.int32, (nk, 2 * tq), 1) % tq
        return qry - key

    lam = _diff_lambda(lq1_ref, lk1_ref, lq2_ref, lk2_ref, lam_init)
    spans = _attn_spans(nq, tk)

    def span_offset(qi, start):
        return -slope * float(qi * tq - start)

    for qi in range(nq):
        q = q_ref[0, qi * tq:(qi + 1) * tq, :]
        zq = jnp.zeros_like(q)
        qs_ref[qi, 0:tq, 0:hw] = jnp.where(first_half, q, zq)
        qs_ref[qi, tq:2 * tq, 0:hw] = jnp.where(first_half, zq, q)
        qs_ref[qi, :, hw:hw + LANES] = qaug_ref[0]

    def masked_scores(t):
        qi, start, nk, diag_off, _, _ = spans[t]
        kblk = jnp.concatenate([k_ref[0, start:start + nk, :], kaug_ref[0:nk, :]], axis=1)
        s = lax.dot_general(kblk, qs_ref[qi], (((1,), (1,)), ((), ())), preferred_element_type=F32)
        if diag_off is not None:
            keep = rel_pos(tk) + (diag_off - (nk - tk)) >= 0
            tail = jnp.where(keep, s[nk - tk:nk], -jnp.inf)
            s = tail if nk == tk else jnp.concatenate([s[0:nk - tk], tail], axis=0)
        return s

    def values_and_ones(t):
        _, start, nk, _, _, _ = spans[t]
        return jnp.concatenate([vt_ref[0, :, start:start + nk], jnp.ones((SUM_ROWS, nk), BF16)], axis=0)

    def finish(qi, acc):
        rows = slice(qi * tq, (qi + 1) * tq)
        o_all = acc[0:DIFF_DV, :] / acc[DIFF_DV:DIFF_DV + 1, :]
        o = (o_all[:, :tq] - lam * o_all[:, tq:]).T
        y = o * lax.rsqrt(jnp.mean(o * o, axis=-1, keepdims=True) + EPS)
        y = y * gain_ref[0] * (1.0 - lam_init) * z_ref[0, rows, :].astype(F32)
        o_ref[0, rows, :] = y.astype(o_ref.dtype)

    r_i = lax.broadcasted_iota(jnp.int32, (hw, LANES), 0)
    c_i = lax.broadcasted_iota(jnp.int32, (hw, LANES), 1)
    half_sel = jnp.where(((c_i == 0) & (r_i < DIFF_DH)) | ((c_i == 1) & (r_i >= DIFF_DH)), 1.0, 0.0).astype(BF16)
    map_cols = lax.broadcasted_iota(jnp.int32, (1, 2 * tq), 1) < tq
    ones_rows = jnp.ones((8, hw), BF16)

    k_all = k_ref[0]
    k_norm2 = jnp.max(jnp.dot(k_all * k_all, half_sel, preferred_element_type=F32), axis=0, keepdims=True)
    k_max = jnp.sqrt(k_norm2)
    k_max_row = jnp.where(map_cols, k_max[:, 0:1], k_max[:, 1:2])
    bounds = []
    for qi in range(nq):
        q_maps = qs_ref[qi, :, 0:hw]
        q_norm2 = lax.dot_general(ones_rows, q_maps * q_maps, (((1,), (1,)), ((), ())),
                                  preferred_element_type=F32)[0:1]
        bounds.append(jnp.sqrt(q_norm2) * k_max_row * BOUND_SLACK)
    denom_min = None
    acc = None
    s_next = masked_scores(0)
    for t, (qi, start, _, _, first, last) in enumerate(spans):
        s = s_next
        if t + 1 < len(spans):
            s_next = masked_scores(t + 1)
        p = jnp.exp2(s - (bounds[qi] - span_offset(qi, start))).astype(BF16)
        pv = jnp.dot(values_and_ones(t), p, preferred_element_type=F32)
        acc = pv if first else acc + pv
        if last:
            denom = acc[DIFF_DV:DIFF_DV + 1, :]
            denom_min = denom if denom_min is None else jnp.minimum(denom_min, denom)
            finish(qi, acc)
    fast_ok = jnp.min(denom_min) >= FAST_PATH_MIN_DENOM

    def scores(t):
        qi, start, nk, _, _, _ = spans[t]
        s = masked_scores(t)
        s_ref[t % n_buf, 0:nk, :] = s
        mb_ref[t % n_buf] = jnp.max(s, axis=0, keepdims=True) + span_offset(qi, start)

    def update(t, m_old):
        qi, start, nk, _, first, last = spans[t]
        s = s_ref[t % n_buf, 0:nk, :]
        m_blk = mb_ref[t % n_buf]
        m_new = m_blk if first else jnp.maximum(m_old, m_blk)
        p = jnp.exp2(s - (m_new - span_offset(qi, start))).astype(BF16)
        pv = jnp.dot(values_and_ones(t), p, preferred_element_type=F32)
        if first:
            acc_ref[...] = pv
        else:
            acc_ref[...] = jnp.exp2(m_old - m_new) * acc_ref[...] + pv
        if last:
            finish(qi, acc_ref[...])
        return m_new

    @pl.when(jnp.logical_not(fast_ok))
    def _():
        for t in range(min(ATTN_AHEAD, len(spans))):
            scores(t)
        m = None
        for t in range(len(spans)):
            if t + ATTN_AHEAD < len(spans):
                scores(t + ATTN_AHEAD)
            m = update(t, m)


def _diff_attn(dq, dk, dvt, dz, gain, lq1, lk1, lq2, lk2, lam_init):
    bsz, s, width = dq.shape
    tq = ATTN_TQ
    assert ATTN_TQ == ATTN_TK
    hw = 2 * DIFF_DH
    rowspec = pl.BlockSpec((1, s, hw), lambda b, h: (b, 0, h))
    vtspec = pl.BlockSpec((1, DIFF_DV, s), lambda b, h: (b, h, 0))
    lspec = pl.BlockSpec((1, DIFF_DH), lambda b, h: (0, 0))
    kaug, qaug, slope_rows = _attn_bias_tables(tq, ATTN_TK)
    per_head = lambda rows: pl.BlockSpec((1, rows, LANES), lambda b, h: (h, 0, 0))
    return pl.pallas_call(
        functools.partial(_attn_kernel, lam_init=lam_init),
        grid=(bsz, DIFF_HEADS),
        in_specs=[rowspec, rowspec, vtspec, rowspec, per_head(1),
                  pl.BlockSpec(kaug.shape, lambda b, h: (0, 0)), per_head(2 * tq), per_head(1),
                  lspec, lspec, lspec, lspec],
        out_specs=rowspec,
        out_shape=jax.ShapeDtypeStruct((bsz, s, width), BF16),
        scratch_shapes=[pltpu.VMEM((s // tq, 2 * tq, hw + LANES), BF16),
                        pltpu.VMEM((ATTN_AHEAD + 1, 2 * ATTN_TK, 2 * tq), F32),
                        pltpu.VMEM((ATTN_AHEAD + 1, 1, 2 * tq), F32),
                        pltpu.VMEM((DIFF_DV + SUM_ROWS, 2 * tq), F32)],
        compiler_params=pltpu.CompilerParams(
            dimension_semantics=("arbitrary", "arbitrary"), vmem_limit_bytes=VMEM_LIMIT),
        name="diff_attn",
    )(dq, dk, dvt, dz, gain.reshape(DIFF_HEADS, 1, DIFF_DV), kaug, qaug, slope_rows, lq1, lk1, lq2, lk2)


def _w_in_layout():
    hk = GLA_HEADS * GLA_DK
    names = ("gq", "gk", "gv", "gz", "gr", "dq", "dk", "dv", "dz")
    sizes = (hk, hk, SECTION, SECTION, GLA_RANK, SECTION, SECTION, SECTION, SECTION)
    src = dict(zip(names, np.concatenate([[0], np.cumsum(sizes)[:-1]]).tolist()))
    width = dict(zip(names, sizes))
    scale = {"gq": GLA_DK ** -0.5, "dq": DIFF_DH ** -0.5 * LOG2E}
    pieces, dst = [], 0
    for name in ("gq", "gk", "gv", "gz", "dq", "dk", "dz", "dv", "gr"):
        pieces.append((src[name], dst, width[name], scale.get(name)))
        dst += width[name]
    return pieces, dst + RANK_PAD - GLA_RANK


def _w_prep_kernel(wt_ref, o_ref):
    pieces, _ = _w_in_layout()
    for src, dst, width, scale in pieces:
        piece = wt_ref[0, src:src + width, :]
        if scale is not None:
            piece = piece * scale
        if width < LANES:
            piece = jnp.concatenate([piece, jnp.zeros((LANES - width, piece.shape[1]), piece.dtype)], axis=0)
        o_ref[:, dst:dst + piece.shape[0]] = piece.T.astype(o_ref.dtype)


def _prep_w_in(w_in, layer):
    _, d, n_src = w_in.shape
    _, n_dst = _w_in_layout()
    rows = 256
    return pl.pallas_call(
        _w_prep_kernel,
        grid=(d // rows,),
        in_specs=[pl.BlockSpec((1, n_src, rows), lambda i: (layer, 0, i))],
        out_specs=pl.BlockSpec((rows, n_dst), lambda i: (i, 0)),
        out_shape=jax.ShapeDtypeStruct((d, n_dst), BF16),
        compiler_params=pltpu.CompilerParams(dimension_semantics=("arbitrary",), vmem_limit_bytes=VMEM_LIMIT),
        name="w_in_prep",
    )(jnp.swapaxes(w_in, 1, 2))


def kernel(x, c, w_ada, b_ada, norm_gain, w_in, w_gla_gate_up, b_gla_gate, gla_out_gain,
           lambda_q1, lambda_k1, lambda_q2, lambda_k2, diff_out_gain, w_out, final_gain):
    bsz, s, d = x.shape
    depth = w_in.shape[0]
    assert depth == 1, "out_proj applies the final rmsnorm, so exactly one layer is supported"
    for l in range(depth):
        mod = _adaln_mod(c, w_ada, b_ada, l)
        mod3 = mod.reshape(bsz, 1, 3 * d)
        w_in_r = _prep_w_in(w_in, l)
        wup_pad = jnp.pad(w_gla_gate_up[l], ((0, RANK_PAD - GLA_RANK), (0, 0))).astype(BF16)
        gqk, gv, gz, dq, dk, dz, dvt, la = _in_proj(
            x, mod3, norm_gain[l].reshape(1, d), w_in_r, wup_pad, b_gla_gate[l].reshape(1, -1))
        lam_init = float(0.8 - 0.6 * np.exp(-0.3 * l))
        o_diff = _diff_attn(dq, dk, dvt, dz, diff_out_gain[l],
                            lambda_q1[l].reshape(1, -1), lambda_k1[l].reshape(1, -1),
                            lambda_q2[l].reshape(1, -1), lambda_k2[l].reshape(1, -1), lam_init)
        x = _gla_out(gqk, gv, gz, la, gla_out_gain[l].reshape(1, -1), o_diff, x, mod3,
                     w_out[l].astype(BF16), final_gain.reshape(1, d))
    return x
```

```python
import functools
import math

import jax
import jax.numpy as jnp
import numpy as np
from jax import lax
from jax.experimental import pallas as pl
from jax.experimental.pallas import tpu as pltpu

F32 = jnp.float32
BF16 = jnp.bfloat16

EPS = 1e-6
LOG2E = math.log2(math.e)
GLA_HEADS = 4
GLA_DK = 64
GLA_DV = 128
GLA_RANK = 16
GLA_GATE_NORM = 16.0
GLA_CHUNK = 64
DIFF_HEADS = 4
DIFF_DH = 64
DIFF_DV = 128
LANES = 128
RANK_PAD = LANES
SECTION = 512
SUM_ROWS = 16

ROWS_IN_PROJ = 1024
ROWS_GLA = 1024
GLA_CUMSUM_ROWS = 256
ATTN_TQ = 256
ATTN_TK = 256
ATTN_AHEAD = 2
ATTN_SPAN_TILES = 2
BOUND_SLACK = 1.01
FAST_PATH_MIN_DENOM = 2.0 ** -90
ROWS_OUT_PROJ = 1024
VMEM_LIMIT = 48 * 1024 * 1024


def _silu(v):
    return v / (1.0 + jnp.exp(-v))


def _log_sigmoid(v):
    return jnp.minimum(v, 0.0) - jnp.log(1.0 + jnp.exp(-jnp.abs(v)))


def _adaln_kernel(c_ref, w_ref, b_ref, o_ref):
    sc = _silu(c_ref[...]).astype(BF16)
    o_ref[...] = jnp.dot(sc, w_ref[0].astype(BF16), preferred_element_type=F32) + b_ref[0]


def _adaln_mod(c, w_ada, b_ada, layer):
    bsz, d = c.shape
    n = w_ada.shape[2]
    tn = 1024
    return pl.pallas_call(
        _adaln_kernel,
        grid=(n // tn,),
        in_specs=[
            pl.BlockSpec((bsz, d), lambda j: (0, 0)),
            pl.BlockSpec((1, d, tn), lambda j: (layer, 0, j)),
            pl.BlockSpec((1, 1, tn), lambda j: (layer, 0, j)),
        ],
        out_specs=pl.BlockSpec((bsz, tn), lambda j: (0, j)),
        out_shape=jax.ShapeDtypeStruct((bsz, n), F32),
        compiler_params=pltpu.CompilerParams(dimension_semantics=("arbitrary",)),
        name="adaln_mod",
    )(c, w_ada, b_ada.reshape(b_ada.shape[0], 1, n))


def _in_proj_kernel(x_ref, shift_ref, scale_ref, gain_ref, w_ref, wup_ref, bg_ref,
                    gqk_ref, gv_ref, gz_ref, dq_ref, dk_ref, dz_ref, dvt_ref, la_ref,
                    wvt_ref):
    n_sec = 7
    dv_sec = 6

    @pl.when((pl.program_id(0) == 0) & (pl.program_id(1) == 0))
    def _():
        wvt_ref[0:SECTION, :] = w_ref[:, dv_sec * SECTION:(dv_sec + 1) * SECTION].T
        wvt_ref[SECTION:SECTION + GLA_RANK, :] = (
            w_ref[:, n_sec * SECTION:n_sec * SECTION + RANK_PAD].T[0:GLA_RANK, :])

    x = x_ref[0]
    g = gain_ref[...] * (1.0 + scale_ref[0])
    rstd = lax.rsqrt(jnp.mean(x * x, axis=-1, keepdims=True) + EPS)
    h = (x * rstd * g + shift_ref[0]).astype(BF16)

    outs = (gqk_ref, gv_ref, gz_ref, dq_ref, dk_ref, dz_ref)
    for j, o_ref in enumerate(outs):
        sec = jnp.dot(h, w_ref[:, j * SECTION:(j + 1) * SECTION], preferred_element_type=F32)
        if o_ref is gz_ref or o_ref is dz_ref:
            sec = _silu(sec)
        o_ref[0] = sec.astype(o_ref.dtype)

    vt_gr = lax.dot_general(wvt_ref[...], h, (((1,), (1,)), ((), ())), preferred_element_type=F32)
    dvt_ref[0] = vt_gr[0:SECTION].astype(dvt_ref.dtype)
    tm = vt_gr.shape[1]
    gr_t = jnp.concatenate([vt_gr[SECTION:SECTION + GLA_RANK].astype(BF16),
                            jnp.zeros((RANK_PAD - GLA_RANK, tm), BF16)], axis=0)
    logit = lax.dot_general(gr_t, wup_ref[...], (((0,), (0,)), ((), ())),
                            preferred_element_type=F32) + bg_ref[...]
    la_ref[0] = (_log_sigmoid(logit) * (1.0 / GLA_GATE_NORM)).astype(la_ref.dtype)


def _in_proj(x, mod3, norm_gain, w_in_r, wup_pad, b_gate):
    bsz, s, d = x.shape
    tm = ROWS_IN_PROJ
    ncol = w_in_r.shape[1]
    hk = GLA_HEADS * GLA_DK
    act = lambda width: pl.BlockSpec((1, tm, width), lambda b, t: (b, t, 0))
    sec = jax.ShapeDtypeStruct((bsz, s, SECTION), BF16)
    out_shapes = (
        sec,
        sec,
        sec,
        sec,
        sec,
        sec,
        jax.ShapeDtypeStruct((bsz, SECTION, s), BF16),
        jax.ShapeDtypeStruct((bsz, s, hk), BF16),
    )
    const = lambda shape: pl.BlockSpec(shape, lambda b, t: (0, 0))
    return pl.pallas_call(
        _in_proj_kernel,
        grid=(bsz, s // tm),
        in_specs=[
            act(d),
            pl.BlockSpec((1, 1, d), lambda b, t: (b, 0, 0)),
            pl.BlockSpec((1, 1, d), lambda b, t: (b, 0, 1)),
            const((1, d)), const((d, ncol)), const((RANK_PAD, hk)), const((1, hk)),
        ],
        out_specs=(act(SECTION), act(SECTION), act(SECTION), act(SECTION), act(SECTION), act(SECTION),
                   pl.BlockSpec((1, SECTION, tm), lambda b, t: (b, 0, t)), act(hk)),
        out_shape=out_shapes,
        scratch_shapes=[pltpu.VMEM((SECTION + GLA_RANK, d), BF16)],
        compiler_params=pltpu.CompilerParams(
            dimension_semantics=("arbitrary", "arbitrary"), vmem_limit_bytes=VMEM_LIMIT),
        name="in_proj",
    )(x, mod3, mod3, norm_gain, w_in_r, wup_pad, b_gate)


def _gla_out_kernel(qk_ref, v_ref, za_ref, la_ref, gain_ref, tril_ref, chunk_ind_ref, ones_bd_ref,
                    od_ref, x_ref, gate_ref, w_ref, fg_ref, o_ref, state_ref):
    tg = qk_ref.shape[1]
    half = od_ref.shape[2]
    hk = GLA_HEADS * GLA_DK
    c_sz = GLA_CHUNK
    n_ch = tg // c_sz

    @pl.when(pl.program_id(1) == 0)
    def _():
        state_ref[...] = jnp.zeros_like(state_ref)

    la = la_ref[0]
    grp = tril_ref.shape[0]
    b_all = jnp.concatenate(
        [jnp.dot(tril_ref[...], la[g * grp:(g + 1) * grp], preferred_element_type=F32)
         for g in range(tg // grp)], axis=0)
    b_last = jnp.concatenate(
        [jnp.broadcast_to(b_all[(ch + 1) * c_sz - 1:(ch + 1) * c_sz], (c_sz, hk)) for ch in range(n_ch)], axis=0)
    q = qk_ref[0, :, 0:hk]
    k = qk_ref[0, :, hk:2 * hk]
    q_in = q * jnp.exp(b_all).astype(BF16)
    k_in = k * jnp.exp(-b_all).astype(BF16)
    k_st = k * jnp.exp(b_last - b_all).astype(BF16)

    lane_head = lax.broadcasted_iota(jnp.int32, (1, hk), 1) // GLA_DK
    ri = lax.broadcasted_iota(jnp.int32, (GLA_HEADS * c_sz, c_sz), 0) % c_sz
    ci = lax.broadcasted_iota(jnp.int32, (GLA_HEADS * c_sz, c_sz), 1)
    causal = ci <= ri
    dec_all = jnp.concatenate(
        [jnp.exp(lax.dot_general(la[g * grp:(g + 1) * grp], chunk_ind_ref[...], (((0,), (0,)), ((), ())),
                                 preferred_element_type=F32)) for g in range(tg // grp)], axis=1)

    mix_diff = jnp.dot(od_ref[0], w_ref[half:2 * half, :], preferred_element_type=F32)

    chunk_rows = [slice(ch * c_sz, (ch + 1) * c_sz) for ch in range(n_ch)]
    head_rows = [slice(hh * c_sz, (hh + 1) * c_sz) for hh in range(GLA_HEADS)]
    head_cols = [slice(hh * GLA_DV, (hh + 1) * GLA_DV) for hh in range(GLA_HEADS)]
    vs = [v_ref[0, rows, :] for rows in chunk_rows]

    us = []
    for ch, rows in enumerate(chunk_rows):
        kst_t = k_st[rows].T
        us.append(jnp.concatenate(
            [jnp.dot(kst_t[hr], vs[ch][:, vc], preferred_element_type=F32)
             for hr, vc in zip(head_rows, head_cols)], axis=0))
    state = state_ref[...]
    states = []
    for ch in range(n_ch):
        states.append(state.astype(BF16))
        state = dec_all[:, ch * GLA_DV:(ch + 1) * GLA_DV] * state + us[ch]
    state_ref[...] = state
    boths = []
    for ch, rows in enumerate(chunk_rows):
        q_c = q_in[rows]
        qm = jnp.concatenate(
            [jnp.where(lane_head == hh, q_c, jnp.zeros_like(q_c)) for hh in range(GLA_HEADS)], axis=0)
        rhs = jnp.concatenate([states[ch], k_in[rows].T], axis=1)
        boths.append(jnp.dot(qm, rhs, preferred_element_type=F32))
    o_rows = []
    for ch in range(n_ch):
        inter = boths[ch][:, 0:GLA_DV]
        p = jnp.where(causal, boths[ch][:, GLA_DV:GLA_DV + c_sz], 0.0).astype(BF16)
        o_rows.append(jnp.concatenate(
            [jnp.dot(p[hr], vs[ch][:, vc], preferred_element_type=F32) + inter[hr]
             for hr, vc in zip(head_rows, head_cols)], axis=1))

    o = jnp.concatenate(o_rows, axis=0)
    ms = jnp.dot((o * o).astype(BF16), ones_bd_ref[...], preferred_element_type=F32) * (1.0 / GLA_DV)
    o_gla = (o * lax.rsqrt(ms + EPS) * gain_ref[...] * za_ref[0].astype(F32)).astype(BF16)

    mixw = jnp.dot(o_gla, w_ref[0:half, :], preferred_element_type=F32) + mix_diff
    xn = x_ref[0] + gate_ref[0] * mixw
    o_ref[0] = xn * lax.rsqrt(jnp.mean(xn * xn, axis=-1, keepdims=True) + EPS) * fg_ref[...]


def _gla_out(gqk, gv, gza, la, gla_out_gain, o_diff, x, mod3, w_out_bf, final_gain):
    bsz, s, d = x.shape
    tg = ROWS_GLA
    hk = GLA_HEADS * GLA_DK
    width = GLA_HEADS * GLA_DV
    grp = GLA_CUMSUM_ROWS
    idx = np.arange(grp)
    tril = ((idx[:, None] // GLA_CHUNK == idx[None, :] // GLA_CHUNK) & (idx[None, :] <= idx[:, None]))
    chunk_ind = idx[:, None] // GLA_CHUNK == np.arange(grp // GLA_CHUNK * GLA_DV)[None, :] // GLA_DV
    col = np.arange(width)
    ones_bd = col[:, None] // GLA_DV == col[None, :] // GLA_DV
    act = lambda w: pl.BlockSpec((1, tg, w), lambda b, t: (b, t, 0))
    const = lambda shape: pl.BlockSpec(shape, lambda b, t: (0, 0))
    return pl.pallas_call(
        _gla_out_kernel,
        grid=(bsz, s // tg),
        in_specs=[act(2 * hk), act(width), act(width), act(hk),
                  const((1, width)), const(tril.shape), const(chunk_ind.shape), const((width, width)),
                  act(o_diff.shape[2]), act(d),
                  pl.BlockSpec((1, 1, d), lambda b, t: (b, 0, 2)),
                  const(w_out_bf.shape), const((1, d))],
        out_specs=act(d),
        out_shape=jax.ShapeDtypeStruct((bsz, s, d), F32),
        scratch_shapes=[pltpu.VMEM((hk, GLA_DV), F32)],
        compiler_params=pltpu.CompilerParams(
            dimension_semantics=("arbitrary", "arbitrary"), vmem_limit_bytes=VMEM_LIMIT),
        name="gla_out_proj",
    )(gqk, gv, gza, la, gla_out_gain, jnp.asarray(tril, BF16), jnp.asarray(chunk_ind, BF16),
      jnp.asarray(ones_bd, BF16), o_diff, x, mod3, w_out_bf, final_gain)


def _diff_lambda(lq1_ref, lk1_ref, lq2_ref, lk2_ref, lam_init):
    a = jnp.sum(lq1_ref[...] * lk1_ref[...], axis=-1, keepdims=True)
    b = jnp.sum(lq2_ref[...] * lk2_ref[...], axis=-1, keepdims=True)
    return jnp.exp(a) - jnp.exp(b) + lam_init


def _attn_spans(nq, tk):
    spans = []
    for qi in range(nq):
        if ATTN_SPAN_TILES == 1:
            tile = [(qi, tk * j, tk, None) for j in range(qi)] + [(qi, qi * tk, tk, 0)]
        else:
            tile = [(qi, 2 * tk * j, 2 * tk, None) for j in range(qi // 2)]
            if qi % 2 == 1:
                tile.append((qi, (qi - 1) * tk, 2 * tk, tk))
            else:
                tile.append((qi, qi * tk, tk, 0))
        for i, sp in enumerate(tile):
            spans.append(sp + (i == 0, i == len(tile) - 1))
    return spans


def _attn_bias_tables(tq, tk):
    def split(x, terms):
        parts = []
        for _ in range(terms):
            hi = x.astype(jnp.bfloat16)
            parts.append(hi)
            x = x - hi.astype(np.float32)
        return parts

    key = np.arange(2 * tk, dtype=np.float32)
    kaug = np.zeros((2 * tk, LANES), np.float32)
    kaug[:, 0] = kaug[:, 1] = key % 256
    kaug[:, 2] = kaug[:, 3] = key - key % 256
    kaug[:, 4:7] = 1.0
    slopes = (2.0 ** (-8.0 * np.arange(1, DIFF_HEADS + 1) / DIFF_HEADS) * LOG2E).astype(np.float32)
    qry = np.tile(np.arange(tq, dtype=np.float32), 2)
    qaug = np.zeros((DIFF_HEADS, 2 * tq, LANES), jnp.bfloat16)
    for hh in range(DIFF_HEADS):
        s_hi, s_lo = split(slopes[hh:hh + 1], 2)
        qaug[hh, :, 0] = qaug[hh, :, 2] = s_hi
        qaug[hh, :, 1] = qaug[hh, :, 3] = s_lo
        for lane, part in zip((4, 5, 6), split(-slopes[hh] * qry, 3)):
            qaug[hh, :, lane] = part
    slope_rows = np.broadcast_to(slopes[:, None, None], (DIFF_HEADS, 1, LANES))
    return jnp.asarray(kaug, BF16), jnp.asarray(qaug), jnp.asarray(slope_rows, F32)


def _attn_kernel(q_ref, k_ref, vt_ref, z_ref, gain_ref, kaug_ref, qaug_ref, slope_ref,
                 lq1_ref, lk1_ref, lq2_ref, lk2_ref,
                 o_ref, qs_ref, s_ref, mb_ref, acc_ref, *, lam_init):
    tq = ATTN_TQ
    tk = ATTN_TK
    hw = 2 * DIFF_DH
    nq = q_ref.shape[1] // tq
    slope = slope_ref[0][:, 0:1]
    n_buf = ATTN_AHEAD + 1

    first_half = lax.broadcasted_iota(jnp.int32, (1, hw), 1) < DIFF_DH

    def rel_pos(nk):
        key = lax.broadcasted_iota(jnp.int32, (nk, 2 * tq), 0)
        qry = lax.broadcasted_iota(jnp.int32, (nk, 2 * tq), 1) % tq
        return qry - key

    lam = _diff_lambda(lq1_ref, lk1_ref, lq2_ref, lk2_ref, lam_init)
    spans = _attn_spans(nq, tk)

    def span_offset(qi, start):
        return -slope * float(qi * tq - start)

    for qi in range(nq):
        q = q_ref[0, qi * tq:(qi + 1) * tq, :]
        zq = jnp.zeros_like(q)
        qs_ref[qi, 0:tq, 0:hw] = jnp.where(first_half, q, zq)
        qs_ref[qi, tq:2 * tq, 0:hw] = jnp.where(first_half, zq, q)
        qs_ref[qi, :, hw:hw + LANES] = qaug_ref[0]

    def masked_scores(t):
        qi, start, nk, diag_off, _, _ = spans[t]
        kblk = jnp.concatenate([k_ref[0, start:start + nk, :], kaug_ref[0:nk, :]], axis=1)
        s = lax.dot_general(kblk, qs_ref[qi], (((1,), (1,)), ((), ())), preferred_element_type=F32)
        if diag_off is not None:
            keep = rel_pos(tk) + (diag_off - (nk - tk)) >= 0
            tail = jnp.where(keep, s[nk - tk:nk], -jnp.inf)
            s = tail if nk == tk else jnp.concatenate([s[0:nk - tk], tail], axis=0)
        return s

    def values_and_ones(t):
        _, start, nk, _, _, _ = spans[t]
        return jnp.concatenate([vt_ref[0, :, start:start + nk], jnp.ones((SUM_ROWS, nk), BF16)], axis=0)

    def finish(qi, acc):
        rows = slice(qi * tq, (qi + 1) * tq)
        o_all = acc[0:DIFF_DV, :] / acc[DIFF_DV:DIFF_DV + 1, :]
        o = (o_all[:, :tq] - lam * o_all[:, tq:]).T
        y = o * lax.rsqrt(jnp.mean(o * o, axis=-1, keepdims=True) + EPS)
        y = y * gain_ref[0] * (1.0 - lam_init) * z_ref[0, rows, :].astype(F32)
        o_ref[0, rows, :] = y.astype(o_ref.dtype)

    r_i = lax.broadcasted_iota(jnp.int32, (hw, LANES), 0)
    c_i = lax.broadcasted_iota(jnp.int32, (hw, LANES), 1)
    half_sel = jnp.where(((c_i == 0) & (r_i < DIFF_DH)) | ((c_i == 1) & (r_i >= DIFF_DH)), 1.0, 0.0).astype(BF16)
    map_cols = lax.broadcasted_iota(jnp.int32, (1, 2 * tq), 1) < tq
    ones_rows = jnp.ones((8, hw), BF16)

    k_all = k_ref[0]
    k_norm2 = jnp.max(jnp.dot(k_all * k_all, half_sel, preferred_element_type=F32), axis=0, keepdims=True)
    k_max = jnp.sqrt(k_norm2)
    k_max_row = jnp.where(map_cols, k_max[:, 0:1], k_max[:, 1:2])
    bounds = []
    for qi in range(nq):
        q_maps = qs_ref[qi, :, 0:hw]
        q_norm2 = lax.dot_general(ones_rows, q_maps * q_maps, (((1,), (1,)), ((), ())),
                                  preferred_element_type=F32)[0:1]
        bounds.append(jnp.sqrt(q_norm2) * k_max_row * BOUND_SLACK)
    denom_min = None
    acc = None
    s_next = masked_scores(0)
    for t, (qi, start, _, _, first, last) in enumerate(spans):
        s = s_next
        if t + 1 < len(spans):
            s_next = masked_scores(t + 1)
        p = jnp.exp2(s - (bounds[qi] - span_offset(qi, start))).astype(BF16)
        pv = jnp.dot(values_and_ones(t), p, preferred_element_type=F32)
        acc = pv if first else acc + pv
        if last:
            denom = acc[DIFF_DV:DIFF_DV + 1, :]
            denom_min = denom if denom_min is None else jnp.minimum(denom_min, denom)
            finish(qi, acc)
    fast_ok = jnp.min(denom_min) >= FAST_PATH_MIN_DENOM

    def scores(t):
        qi, start, nk, _, _, _ = spans[t]
        s = masked_scores(t)
        s_ref[t % n_buf, 0:nk, :] = s
        mb_ref[t % n_buf] = jnp.max(s, axis=0, keepdims=True) + span_offset(qi, start)

    def update(t, m_old):
        qi, start, nk, _, first, last = spans[t]
        s = s_ref[t % n_buf, 0:nk, :]
        m_blk = mb_ref[t % n_buf]
        m_new = m_blk if first else jnp.maximum(m_old, m_blk)
        p = jnp.exp2(s - (m_new - span_offset(qi, start))).astype(BF16)
        pv = jnp.dot(values_and_ones(t), p, preferred_element_type=F32)
        if first:
            acc_ref[...] = pv
        else:
            acc_ref[...] = jnp.exp2(m_old - m_new) * acc_ref[...] + pv
        if last:
            finish(qi, acc_ref[...])
        return m_new

    @pl.when(jnp.logical_not(fast_ok))
    def _():
        for t in range(min(ATTN_AHEAD, len(spans))):
            scores(t)
        m = None
        for t in range(len(spans)):
            if t + ATTN_AHEAD < len(spans):
                scores(t + ATTN_AHEAD)
            m = update(t, m)


def _diff_attn(dq, dk, dvt, dz, gain, lq1, lk1, lq2, lk2, lam_init):
    bsz, s, width = dq.shape
    tq = ATTN_TQ
    assert ATTN_TQ == ATTN_TK
    hw = 2 * DIFF_DH
    rowspec = pl.BlockSpec((1, s, hw), lambda b, h: (b, 0, h))
    vtspec = pl.BlockSpec((1, DIFF_DV, s), lambda b, h: (b, h, 0))
    lspec = pl.BlockSpec((1, DIFF_DH), lambda b, h: (0, 0))
    kaug, qaug, slope_rows = _attn_bias_tables(tq, ATTN_TK)
    per_head = lambda rows: pl.BlockSpec((1, rows, LANES), lambda b, h: (h, 0, 0))
    return pl.pallas_call(
        functools.partial(_attn_kernel, lam_init=lam_init),
        grid=(bsz, DIFF_HEADS),
        in_specs=[rowspec, rowspec, vtspec, rowspec, per_head(1),
                  pl.BlockSpec(kaug.shape, lambda b, h: (0, 0)), per_head(2 * tq), per_head(1),
                  lspec, lspec, lspec, lspec],
        out_specs=rowspec,
        out_shape=jax.ShapeDtypeStruct((bsz, s, width), BF16),
        scratch_shapes=[pltpu.VMEM((s // tq, 2 * tq, hw + LANES), BF16),
                        pltpu.VMEM((ATTN_AHEAD + 1, 2 * ATTN_TK, 2 * tq), F32),
                        pltpu.VMEM((ATTN_AHEAD + 1, 1, 2 * tq), F32),
                        pltpu.VMEM((DIFF_DV + SUM_ROWS, 2 * tq), F32)],
        compiler_params=pltpu.CompilerParams(
            dimension_semantics=("arbitrary", "arbitrary"), vmem_limit_bytes=VMEM_LIMIT),
        name="diff_attn",
    )(dq, dk, dvt, dz, gain.reshape(DIFF_HEADS, 1, DIFF_DV), kaug, qaug, slope_rows, lq1, lk1, lq2, lk2)


def _w_in_layout():
    hk = GLA_HEADS * GLA_DK
    names = ("gq", "gk", "gv", "gz", "gr", "dq", "dk", "dv", "dz")
    sizes = (hk, hk, SECTION, SECTION, GLA_RANK, SECTION, SECTION, SECTION, SECTION)
    src = dict(zip(names, np.concatenate([[0], np.cumsum(sizes)[:-1]]).tolist()))
    width = dict(zip(names, sizes))
    scale = {"gq": GLA_DK ** -0.5, "dq": DIFF_DH ** -0.5 * LOG2E}
    pieces, dst = [], 0
    for name in ("gq", "gk", "gv", "gz", "dq", "dk", "dz", "dv", "gr"):
        pieces.append((src[name], dst, width[name], scale.get(name)))
        dst += width[name]
    return pieces, dst + RANK_PAD - GLA_RANK


def _w_prep_kernel(wt_ref, o_ref):
    pieces, _ = _w_in_layout()
    for src, dst, width, scale in pieces:
        piece = wt_ref[0, src:src + width, :]
        if scale is not None:
            piece = piece * scale
        if width < LANES:
            piece = jnp.concatenate([piece, jnp.zeros((LANES - width, piece.shape[1]), piece.dtype)], axis=0)
        o_ref[:, dst:dst + piece.shape[0]] = piece.T.astype(o_ref.dtype)


def _prep_w_in(w_in, layer):
    _, d, n_src = w_in.shape
    _, n_dst = _w_in_layout()
    rows = 256
    return pl.pallas_call(
        _w_prep_kernel,
        grid=(d // rows,),
        in_specs=[pl.BlockSpec((1, n_src, rows), lambda i: (layer, 0, i))],
        out_specs=pl.BlockSpec((rows, n_dst), lambda i: (i, 0)),
        out_shape=jax.ShapeDtypeStruct((d, n_dst), BF16),
        compiler_params=pltpu.CompilerParams(dimension_semantics=("arbitrary",), vmem_limit_bytes=VMEM_LIMIT),
        name="w_in_prep",
    )(jnp.swapaxes(w_in, 1, 2))


def kernel(x, c, w_ada, b_ada, norm_gain, w_in, w_gla_gate_up, b_gla_gate, gla_out_gain,
           lambda_q1, lambda_k1, lambda_q2, lambda_k2, diff_out_gain, w_out, final_gain):
    bsz, s, d = x.shape
    depth = w_in.shape[0]
    assert depth == 1, "out_proj applies the final rmsnorm, so exactly one layer is supported"
    for l in range(depth):
        mod = _adaln_mod(c, w_ada, b_ada, l)
        mod3 = mod.reshape(bsz, 1, 3 * d)
        w_in_r = _prep_w_in(w_in, l)
        wup_pad = jnp.pad(w_gla_gate_up[l], ((0, RANK_PAD - GLA_RANK), (0, 0))).astype(BF16)
        gqk, gv, gz, dq, dk, dz, dvt, la = _in_proj(
            x, mod3, norm_gain[l].reshape(1, d), w_in_r, wup_pad, b_gla_gate[l].reshape(1, -1))
        lam_init = float(0.8 - 0.6 * np.exp(-0.3 * l))
        o_diff = _diff_attn(dq, dk, dvt, dz, diff_out_gain[l],
                            lambda_q1[l].reshape(1, -1), lambda_k1[l].reshape(1, -1),
                            lambda_q2[l].reshape(1, -1), lambda_k2[l].reshape(1, -1), lam_init)
        x = _gla_out(gqk, gv, gz, la, gla_out_gain[l].reshape(1, -1), o_diff, x, mod3,
                     w_out[l].astype(BF16), final_gain.reshape(1, d))
    return x
```

```python
import functools
import math

import jax
import jax.numpy as jnp
import numpy as np
from jax import lax
from jax.experimental import pallas as pl
from jax.experimental.pallas import tpu as pltpu

F32 = jnp.float32
BF16 = jnp.bfloat16

EPS = 1e-6
LOG2E = math.log2(math.e)
GLA_HEADS = 4
GLA_DK = 64
GLA_DV = 128
GLA_RANK = 16
GLA_GATE_NORM = 16.0
GLA_CHUNK = 64
DIFF_HEADS = 4
DIFF_DH = 64
DIFF_DV = 128
LANES = 128
RANK_PAD = LANES
SECTION = 512
SUM_ROWS = 16

ROWS_IN_PROJ = 1024
ROWS_GLA = 1024
GLA_CUMSUM_ROWS = 256
ATTN_TQ = 256
ATTN_TK = 256
ATTN_AHEAD = 2
BF16_EXACT_INT = 256
BOUND_SLACK = 1.01
FAST_PATH_MIN_DENOM = 2.0 ** -90
VMEM_LIMIT = 48 * 1024 * 1024


def _silu(v):
    return v / (1.0 + jnp.exp(-v))


def _log_sigmoid(v):
    return jnp.minimum(v, 0.0) - jnp.log(1.0 + jnp.exp(-jnp.abs(v)))


def _adaln_kernel(c_ref, w_ref, b_ref, o_ref):
    sc = _silu(c_ref[...]).astype(BF16)
    o_ref[...] = jnp.dot(sc, w_ref[0].astype(BF16), preferred_element_type=F32) + b_ref[0]


def _adaln_mod(c, w_ada, b_ada, layer):
    bsz, d = c.shape
    n = w_ada.shape[2]
    tn = 1024
    return pl.pallas_call(
        _adaln_kernel,
        grid=(n // tn,),
        in_specs=[
            pl.BlockSpec((bsz, d), lambda j: (0, 0)),
            pl.BlockSpec((1, d, tn), lambda j: (layer, 0, j)),
            pl.BlockSpec((1, 1, tn), lambda j: (layer, 0, j)),
        ],
        out_specs=pl.BlockSpec((bsz, tn), lambda j: (0, j)),
        out_shape=jax.ShapeDtypeStruct((bsz, n), F32),
        compiler_params=pltpu.CompilerParams(dimension_semantics=("arbitrary",)),
        name="adaln_mod",
    )(c, w_ada, b_ada.reshape(b_ada.shape[0], 1, n))


def _in_proj_kernel(x_ref, shift_ref, scale_ref, gain_ref, w_ref, wup_ref, bg_ref,
                    gqk_ref, gv_ref, gz_ref, dq_ref, dk_ref, dz_ref, dvt_ref, la_ref,
                    wvt_ref):
    n_sec = 7
    dv_sec = 6

    @pl.when((pl.program_id(0) == 0) & (pl.program_id(1) == 0))
    def _():
        wvt_ref[0:SECTION, :] = w_ref[:, dv_sec * SECTION:(dv_sec + 1) * SECTION].T
        wvt_ref[SECTION:SECTION + GLA_RANK, :] = (
            w_ref[:, n_sec * SECTION:n_sec * SECTION + RANK_PAD].T[0:GLA_RANK, :])

    x = x_ref[0]
    g = gain_ref[...] * (1.0 + scale_ref[0])
    rstd = lax.rsqrt(jnp.mean(x * x, axis=-1, keepdims=True) + EPS)
    h = (x * rstd * g + shift_ref[0]).astype(BF16)

    outs = (gqk_ref, gv_ref, gz_ref, dq_ref, dk_ref, dz_ref)
    for j, o_ref in enumerate(outs):
        sec = jnp.dot(h, w_ref[:, j * SECTION:(j + 1) * SECTION], preferred_element_type=F32)
        if o_ref is gz_ref or o_ref is dz_ref:
            sec = _silu(sec)
        o_ref[0] = sec.astype(o_ref.dtype)

    vt_gr = lax.dot_general(wvt_ref[...], h, (((1,), (1,)), ((), ())), preferred_element_type=F32)
    dvt_ref[0] = vt_gr[0:SECTION].astype(dvt_ref.dtype)
    tm = vt_gr.shape[1]
    gr_t = jnp.concatenate([vt_gr[SECTION:SECTION + GLA_RANK].astype(BF16),
                            jnp.zeros((RANK_PAD - GLA_RANK, tm), BF16)], axis=0)
    logit = lax.dot_general(gr_t, wup_ref[...], (((0,), (0,)), ((), ())),
                            preferred_element_type=F32) + bg_ref[...]
    la_ref[0] = (_log_sigmoid(logit) * (1.0 / GLA_GATE_NORM)).astype(la_ref.dtype)


def _in_proj(x, mod3, norm_gain, w_in_r, wup_pad, b_gate):
    bsz, s, d = x.shape
    tm = ROWS_IN_PROJ
    ncol = w_in_r.shape[1]
    hk = GLA_HEADS * GLA_DK
    act = lambda width: pl.BlockSpec((1, tm, width), lambda b, t: (b, t, 0))
    sec = jax.ShapeDtypeStruct((bsz, s, SECTION), BF16)
    out_shapes = (
        sec,
        sec,
        sec,
        sec,
        sec,
        sec,
        jax.ShapeDtypeStruct((bsz, SECTION, s), BF16),
        jax.ShapeDtypeStruct((bsz, s, hk), BF16),
    )
    const = lambda shape: pl.BlockSpec(shape, lambda b, t: (0, 0))
    return pl.pallas_call(
        _in_proj_kernel,
        grid=(bsz, s // tm),
        in_specs=[
            act(d),
            pl.BlockSpec((1, 1, d), lambda b, t: (b, 0, 0)),
            pl.BlockSpec((1, 1, d), lambda b, t: (b, 0, 1)),
            const((1, d)), const((d, ncol)), const((RANK_PAD, hk)), const((1, hk)),
        ],
        out_specs=(act(SECTION), act(SECTION), act(SECTION), act(SECTION), act(SECTION), act(SECTION),
                   pl.BlockSpec((1, SECTION, tm), lambda b, t: (b, 0, t)), act(hk)),
        out_shape=out_shapes,
        scratch_shapes=[pltpu.VMEM((SECTION + GLA_RANK, d), BF16)],
        compiler_params=pltpu.CompilerParams(
            dimension_semantics=("arbitrary", "arbitrary"), vmem_limit_bytes=VMEM_LIMIT),
        name="in_proj",
    )(x, mod3, mod3, norm_gain, w_in_r, wup_pad, b_gate)


def _gla_out_kernel(qk_ref, v_ref, za_ref, la_ref, gain_ref, tril_ref, chunk_ind_ref, ones_bd_ref,
                    od_ref, x_ref, gate_ref, w_ref, fg_ref, o_ref, state_ref):
    tg = qk_ref.shape[1]
    half = od_ref.shape[2]
    hk = GLA_HEADS * GLA_DK
    c_sz = GLA_CHUNK
    n_ch = tg // c_sz

    @pl.when(pl.program_id(1) == 0)
    def _():
        state_ref[...] = jnp.zeros_like(state_ref)

    la = la_ref[0]
    grp = tril_ref.shape[0]
    b_all = jnp.concatenate(
        [jnp.dot(tril_ref[...], la[g * grp:(g + 1) * grp], preferred_element_type=F32)
         for g in range(tg // grp)], axis=0)
    b_last = jnp.concatenate(
        [jnp.broadcast_to(b_all[(ch + 1) * c_sz - 1:(ch + 1) * c_sz], (c_sz, hk)) for ch in range(n_ch)], axis=0)
    q = qk_ref[0, :, 0:hk]
    k = qk_ref[0, :, hk:2 * hk]
    q_in = q * jnp.exp(b_all).astype(BF16)
    k_in = k * jnp.exp(-b_all).astype(BF16)
    k_st = k * jnp.exp(b_last - b_all).astype(BF16)

    lane_head = lax.broadcasted_iota(jnp.int32, (1, hk), 1) // GLA_DK
    ri = lax.broadcasted_iota(jnp.int32, (GLA_HEADS * c_sz, c_sz), 0) % c_sz
    ci = lax.broadcasted_iota(jnp.int32, (GLA_HEADS * c_sz, c_sz), 1)
    causal = ci <= ri
    dec_all = jnp.concatenate(
        [jnp.exp(lax.dot_general(la[g * grp:(g + 1) * grp], chunk_ind_ref[...], (((0,), (0,)), ((), ())),
                                 preferred_element_type=F32)) for g in range(tg // grp)], axis=1)

    mix_diff = jnp.dot(od_ref[0], w_ref[half:2 * half, :], preferred_element_type=F32)

    chunk_rows = [slice(ch * c_sz, (ch + 1) * c_sz) for ch in range(n_ch)]
    head_rows = [slice(hh * c_sz, (hh + 1) * c_sz) for hh in range(GLA_HEADS)]
    head_cols = [slice(hh * GLA_DV, (hh + 1) * GLA_DV) for hh in range(GLA_HEADS)]
    vs = [v_ref[0, rows, :] for rows in chunk_rows]

    us = []
    for ch, rows in enumerate(chunk_rows):
        kst_t = k_st[rows].T
        us.append(jnp.concatenate(
            [jnp.dot(kst_t[hr], vs[ch][:, vc], preferred_element_type=F32)
             for hr, vc in zip(head_rows, head_cols)], axis=0))
    state = state_ref[...]
    states = []
    for ch in range(n_ch):
        states.append(state.astype(BF16))
        state = dec_all[:, ch * GLA_DV:(ch + 1) * GLA_DV] * state + us[ch]
    state_ref[...] = state
    boths = []
    for ch, rows in enumerate(chunk_rows):
        q_c = q_in[rows]
        qm = jnp.concatenate(
            [jnp.where(lane_head == hh, q_c, jnp.zeros_like(q_c)) for hh in range(GLA_HEADS)], axis=0)
        rhs = jnp.concatenate([states[ch], k_in[rows].T], axis=1)
        boths.append(jnp.dot(qm, rhs, preferred_element_type=F32))
    o_rows = []
    for ch in range(n_ch):
        inter = boths[ch][:, 0:GLA_DV]
        p = jnp.where(causal, boths[ch][:, GLA_DV:GLA_DV + c_sz], 0.0).astype(BF16)
        o_rows.append(jnp.concatenate(
            [jnp.dot(p[hr], vs[ch][:, vc], preferred_element_type=F32) + inter[hr]
             for hr, vc in zip(head_rows, head_cols)], axis=1))

    o = jnp.concatenate(o_rows, axis=0)
    ms = jnp.dot((o * o).astype(BF16), ones_bd_ref[...], preferred_element_type=F32) * (1.0 / GLA_DV)
    o_gla = (o * lax.rsqrt(ms + EPS) * gain_ref[...] * za_ref[0].astype(F32)).astype(BF16)

    mixw = jnp.dot(o_gla, w_ref[0:half, :], preferred_element_type=F32) + mix_diff
    xn = x_ref[0] + gate_ref[0] * mixw
    o_ref[0] = xn * lax.rsqrt(jnp.mean(xn * xn, axis=-1, keepdims=True) + EPS) * fg_ref[...]


def _gla_out(gqk, gv, gza, la, gla_out_gain, o_diff, x, mod3, w_out_bf, final_gain):
    bsz, s, d = x.shape
    tg = ROWS_GLA
    hk = GLA_HEADS * GLA_DK
    width = GLA_HEADS * GLA_DV
    grp = GLA_CUMSUM_ROWS
    idx = np.arange(grp)
    tril = ((idx[:, None] // GLA_CHUNK == idx[None, :] // GLA_CHUNK) & (idx[None, :] <= idx[:, None]))
    chunk_ind = idx[:, None] // GLA_CHUNK == np.arange(grp // GLA_CHUNK * GLA_DV)[None, :] // GLA_DV
    col = np.arange(width)
    ones_bd = col[:, None] // GLA_DV == col[None, :] // GLA_DV
    act = lambda w: pl.BlockSpec((1, tg, w), lambda b, t: (b, t, 0))
    const = lambda shape: pl.BlockSpec(shape, lambda b, t: (0, 0))
    return pl.pallas_call(
        _gla_out_kernel,
        grid=(bsz, s // tg),
        in_specs=[act(2 * hk), act(width), act(width), act(hk),
                  const((1, width)), const(tril.shape), const(chunk_ind.shape), const((width, width)),
                  act(o_diff.shape[2]), act(d),
                  pl.BlockSpec((1, 1, d), lambda b, t: (b, 0, 2)),
                  const(w_out_bf.shape), const((1, d))],
        out_specs=act(d),
        out_shape=jax.ShapeDtypeStruct((bsz, s, d), F32),
        scratch_shapes=[pltpu.VMEM((hk, GLA_DV), F32)],
        compiler_params=pltpu.CompilerParams(
            dimension_semantics=("arbitrary", "arbitrary"), vmem_limit_bytes=VMEM_LIMIT),
        name="gla_out_proj",
    )(gqk, gv, gza, la, gla_out_gain, jnp.asarray(tril, BF16), jnp.asarray(chunk_ind, BF16),
      jnp.asarray(ones_bd, BF16), o_diff, x, mod3, w_out_bf, final_gain)


def _diff_lambda(lq1_ref, lk1_ref, lq2_ref, lk2_ref, lam_init):
    a = jnp.sum(lq1_ref[...] * lk1_ref[...], axis=-1, keepdims=True)
    b = jnp.sum(lq2_ref[...] * lk2_ref[...], axis=-1, keepdims=True)
    return jnp.exp(a) - jnp.exp(b) + lam_init


def _attn_spans(nq, tk):
    spans = []
    for qi in range(nq):
        tile = [(qi, 2 * tk * j, 2 * tk, None) for j in range(qi // 2)]
        if qi % 2 == 1:
            tile.append((qi, (qi - 1) * tk, 2 * tk, tk))
        else:
            tile.append((qi, qi * tk, tk, 0))
        for i, sp in enumerate(tile):
            spans.append(sp + (i == 0, i == len(tile) - 1))
    return spans


def _attn_bias_tables(tq, tk):
    def split(x, terms):
        parts = []
        for _ in range(terms):
            hi = x.astype(jnp.bfloat16)
            parts.append(hi)
            x = x - hi.astype(np.float32)
        return parts

    key = np.arange(2 * tk, dtype=np.float32)
    kaug = np.zeros((2 * tk, LANES), np.float32)
    kaug[:, 0] = kaug[:, 1] = key % BF16_EXACT_INT
    kaug[:, 2] = kaug[:, 3] = key - key % BF16_EXACT_INT
    kaug[:, 4:7] = 1.0
    slopes = (2.0 ** (-8.0 * np.arange(1, DIFF_HEADS + 1) / DIFF_HEADS) * LOG2E).astype(np.float32)
    qry = np.tile(np.arange(tq, dtype=np.float32), 2)
    qaug = np.zeros((DIFF_HEADS, 2 * tq, LANES), jnp.bfloat16)
    for hh in range(DIFF_HEADS):
        s_hi, s_lo = split(slopes[hh:hh + 1], 2)
        qaug[hh, :, 0] = qaug[hh, :, 2] = s_hi
        qaug[hh, :, 1] = qaug[hh, :, 3] = s_lo
        for lane, part in zip((4, 5, 6), split(-slopes[hh] * qry, 3)):
            qaug[hh, :, lane] = part
    slope_rows = np.broadcast_to(slopes[:, None, None], (DIFF_HEADS, 1, LANES))
    return jnp.asarray(kaug, BF16), jnp.asarray(qaug), jnp.asarray(slope_rows, F32)


def _attn_kernel(q_ref, k_ref, vt_ref, z_ref, gain_ref, kaug_ref, qaug_ref, slope_ref,
                 lq1_ref, lk1_ref, lq2_ref, lk2_ref,
                 o_ref, qs_ref, s_ref, mb_ref, acc_ref, *, lam_init):
    tq = ATTN_TQ
    tk = ATTN_TK
    hw = 2 * DIFF_DH
    nq = q_ref.shape[1] // tq
    slope = slope_ref[0][:, 0:1]
    n_buf = ATTN_AHEAD + 1

    first_half = lax.broadcasted_iota(jnp.int32, (1, hw), 1) < DIFF_DH

    def rel_pos(nk):
        key = lax.broadcasted_iota(jnp.int32, (nk, 2 * tq), 0)
        qry = lax.broadcasted_iota(jnp.int32, (nk, 2 * tq), 1) % tq
        return qry - key

    lam = _diff_lambda(lq1_ref, lk1_ref, lq2_ref, lk2_ref, lam_init)
    spans = _attn_spans(nq, tk)

    def span_offset(qi, start):
        return -slope * float(qi * tq - start)

    for qi in range(nq):
        q = q_ref[0, qi * tq:(qi + 1) * tq, :]
        zq = jnp.zeros_like(q)
        qs_ref[qi, 0:tq, 0:hw] = jnp.where(first_half, q, zq)
        qs_ref[qi, tq:2 * tq, 0:hw] = jnp.where(first_half, zq, q)
        qs_ref[qi, :, hw:hw + LANES] = qaug_ref[0]

    def masked_scores(t):
        qi, start, nk, diag_off, _, _ = spans[t]
        kblk = jnp.concatenate([k_ref[0, start:start + nk, :], kaug_ref[0:nk, :]], axis=1)
        s = lax.dot_general(kblk, qs_ref[qi], (((1,), (1,)), ((), ())), preferred_element_type=F32)
        if diag_off is not None:
            keep = rel_pos(tk) + (diag_off - (nk - tk)) >= 0
            tail = jnp.where(keep, s[nk - tk:nk], -jnp.inf)
            s = tail if nk == tk else jnp.concatenate([s[0:nk - tk], tail], axis=0)
        return s

    def values_and_ones(t):
        _, start, nk, _, _, _ = spans[t]
        return jnp.concatenate([vt_ref[0, :, start:start + nk], jnp.ones((SUM_ROWS, nk), BF16)], axis=0)

    def finish(qi, acc):
        rows = slice(qi * tq, (qi + 1) * tq)
        o_all = acc[0:DIFF_DV, :] / acc[DIFF_DV:DIFF_DV + 1, :]
        o = (o_all[:, :tq] - lam * o_all[:, tq:]).T
        y = o * lax.rsqrt(jnp.mean(o * o, axis=-1, keepdims=True) + EPS)
        y = y * gain_ref[0] * (1.0 - lam_init) * z_ref[0, rows, :].astype(F32)
        o_ref[0, rows, :] = y.astype(o_ref.dtype)

    map_cols = lax.broadcasted_iota(jnp.int32, (1, 2 * tq), 1) < tq
    ones_rows = jnp.ones((8, hw), BF16)

    r_i = lax.broadcasted_iota(jnp.int32, (hw, LANES), 0)
    c_i = lax.broadcasted_iota(jnp.int32, (hw, LANES), 1)
    half_sel = jnp.where(r_i // DIFF_DH == c_i, 1.0, 0.0).astype(BF16)
    k_all = k_ref[0]
    k_norm2 = jnp.max(jnp.dot(k_all * k_all, half_sel, preferred_element_type=F32), axis=0, keepdims=True)
    k_max = jnp.sqrt(k_norm2)
    k_max_row = jnp.where(map_cols, k_max[:, 0:1], k_max[:, 1:2])
    bounds = []
    for qi in range(nq):
        q_maps = qs_ref[qi, :, 0:hw]
        q_norm2 = lax.dot_general(ones_rows, q_maps * q_maps, (((1,), (1,)), ((), ())),
                                  preferred_element_type=F32)[0:1]
        bounds.append(jnp.sqrt(q_norm2) * k_max_row * BOUND_SLACK)
    denom_min = None
    acc = None
    s_next = masked_scores(0)
    for t, (qi, start, _, _, first, last) in enumerate(spans):
        s = s_next
        if t + 1 < len(spans):
            s_next = masked_scores(t + 1)
        p = jnp.exp2(s - (bounds[qi] - span_offset(qi, start))).astype(BF16)
        pv = jnp.dot(values_and_ones(t), p, preferred_element_type=F32)
        acc = pv if first else acc + pv
        if last:
            denom = acc[DIFF_DV:DIFF_DV + 1, :]
            denom_min = denom if denom_min is None else jnp.minimum(denom_min, denom)
            finish(qi, acc)
    fast_ok = jnp.min(denom_min) >= FAST_PATH_MIN_DENOM

    def scores(t):
        qi, start, nk, _, _, _ = spans[t]
        s = masked_scores(t)
        s_ref[t % n_buf, 0:nk, :] = s
        mb_ref[t % n_buf] = jnp.max(s, axis=0, keepdims=True) + span_offset(qi, start)

    def update(t, m_old):
        qi, start, nk, _, first, last = spans[t]
        s = s_ref[t % n_buf, 0:nk, :]
        m_blk = mb_ref[t % n_buf]
        m_new = m_blk if first else jnp.maximum(m_old, m_blk)
        p = jnp.exp2(s - (m_new - span_offset(qi, start))).astype(BF16)
        pv = jnp.dot(values_and_ones(t), p, preferred_element_type=F32)
        if first:
            acc_ref[...] = pv
        else:
            acc_ref[...] = jnp.exp2(m_old - m_new) * acc_ref[...] + pv
        if last:
            finish(qi, acc_ref[...])
        return m_new

    @pl.when(jnp.logical_not(fast_ok))
    def _():
        for t in range(min(ATTN_AHEAD, len(spans))):
            scores(t)
        m = None
        for t in range(len(spans)):
            if t + ATTN_AHEAD < len(spans):
                scores(t + ATTN_AHEAD)
            m = update(t, m)


def _diff_attn(dq, dk, dvt, dz, gain, lq1, lk1, lq2, lk2, lam_init):
    bsz, s, width = dq.shape
    tq = ATTN_TQ
    assert ATTN_TQ == ATTN_TK
    hw = 2 * DIFF_DH
    rowspec = pl.BlockSpec((1, s, hw), lambda b, h: (b, 0, h))
    vtspec = pl.BlockSpec((1, DIFF_DV, s), lambda b, h: (b, h, 0))
    lspec = pl.BlockSpec((1, DIFF_DH), lambda b, h: (0, 0))
    kaug, qaug, slope_rows = _attn_bias_tables(tq, ATTN_TK)
    per_head = lambda rows: pl.BlockSpec((1, rows, LANES), lambda b, h: (h, 0, 0))
    return pl.pallas_call(
        functools.partial(_attn_kernel, lam_init=lam_init),
        grid=(bsz, DIFF_HEADS),
        in_specs=[rowspec, rowspec, vtspec, rowspec, per_head(1),
                  pl.BlockSpec(kaug.shape, lambda b, h: (0, 0)), per_head(2 * tq), per_head(1),
                  lspec, lspec, lspec, lspec],
        out_specs=rowspec,
        out_shape=jax.ShapeDtypeStruct((bsz, s, width), BF16),
        scratch_shapes=[pltpu.VMEM((s // tq, 2 * tq, hw + LANES), BF16),
                        pltpu.VMEM((ATTN_AHEAD + 1, 2 * ATTN_TK, 2 * tq), F32),
                        pltpu.VMEM((ATTN_AHEAD + 1, 1, 2 * tq), F32),
                        pltpu.VMEM((DIFF_DV + SUM_ROWS, 2 * tq), F32)],
        compiler_params=pltpu.CompilerParams(
            dimension_semantics=("arbitrary", "arbitrary"), vmem_limit_bytes=VMEM_LIMIT),
        name="diff_attn",
    )(dq, dk, dvt, dz, gain.reshape(DIFF_HEADS, 1, DIFF_DV), kaug, qaug, slope_rows, lq1, lk1, lq2, lk2)


def _w_in_layout():
    hk = GLA_HEADS * GLA_DK
    names = ("gq", "gk", "gv", "gz", "gr", "dq", "dk", "dv", "dz")
    sizes = (hk, hk, SECTION, SECTION, GLA_RANK, SECTION, SECTION, SECTION, SECTION)
    src = dict(zip(names, np.concatenate([[0], np.cumsum(sizes)[:-1]]).tolist()))
    width = dict(zip(names, sizes))
    scale = {"gq": GLA_DK ** -0.5, "dq": DIFF_DH ** -0.5 * LOG2E}
    pieces, dst = [], 0
    for name in ("gq", "gk", "gv", "gz", "dq", "dk", "dz", "dv", "gr"):
        pieces.append((src[name], dst, width[name], scale.get(name)))
        dst += width[name]
    return pieces, dst + RANK_PAD - GLA_RANK


def _w_prep_kernel(wt_ref, o_ref):
    pieces, _ = _w_in_layout()
    for src, dst, width, scale in pieces:
        piece = wt_ref[0, src:src + width, :]
        if scale is not None:
            piece = piece * scale
        if width < LANES:
            piece = jnp.concatenate([piece, jnp.zeros((LANES - width, piece.shape[1]), piece.dtype)], axis=0)
        o_ref[:, dst:dst + piece.shape[0]] = piece.T.astype(o_ref.dtype)


def _prep_w_in(w_in, layer):
    _, d, n_src = w_in.shape
    _, n_dst = _w_in_layout()
    rows = 256
    return pl.pallas_call(
        _w_prep_kernel,
        grid=(d // rows,),
        in_specs=[pl.BlockSpec((1, n_src, rows), lambda i: (layer, 0, i))],
        out_specs=pl.BlockSpec((rows, n_dst), lambda i: (i, 0)),
        out_shape=jax.ShapeDtypeStruct((d, n_dst), BF16),
        compiler_params=pltpu.CompilerParams(dimension_semantics=("arbitrary",), vmem_limit_bytes=VMEM_LIMIT),
        name="w_in_prep",
    )(jnp.swapaxes(w_in, 1, 2))


def kernel(x, c, w_ada, b_ada, norm_gain, w_in, w_gla_gate_up, b_gla_gate, gla_out_gain,
           lambda_q1, lambda_k1, lambda_q2, lambda_k2, diff_out_gain, w_out, final_gain):
    bsz, s, d = x.shape
    depth = w_in.shape[0]
    assert depth == 1, "gla_out_proj applies the final rmsnorm, so exactly one layer is supported"
    for l in range(depth):
        mod = _adaln_mod(c, w_ada, b_ada, l)
        mod3 = mod.reshape(bsz, 1, 3 * d)
        w_in_r = _prep_w_in(w_in, l)
        wup_pad = jnp.pad(w_gla_gate_up[l], ((0, RANK_PAD - GLA_RANK), (0, 0))).astype(BF16)
        gqk, gv, gz, dq, dk, dz, dvt, la = _in_proj(
            x, mod3, norm_gain[l].reshape(1, d), w_in_r, wup_pad, b_gla_gate[l].reshape(1, -1))
        lam_init = float(0.8 - 0.6 * np.exp(-0.3 * l))
        o_diff = _diff_attn(dq, dk, dvt, dz, diff_out_gain[l],
                            lambda_q1[l].reshape(1, -1), lambda_k1[l].reshape(1, -1),
                            lambda_q2[l].reshape(1, -1), lambda_k2[l].reshape(1, -1), lam_init)
        x = _gla_out(gqk, gv, gz, la, gla_out_gain[l].reshape(1, -1), o_diff, x, mod3,
                     w_out[l].astype(BF16), final_gain.reshape(1, d))
    return x
```

```python
import functools
import math

import jax
import jax.numpy as jnp
import numpy as np
from jax import lax
from jax.experimental import pallas as pl
from jax.experimental.pallas import tpu as pltpu

F32 = jnp.float32
BF16 = jnp.bfloat16

EPS = 1e-6
LOG2E = math.log2(math.e)
GLA_HEADS = 4
GLA_DK = 64
GLA_DV = 128
GLA_RANK = 16
GLA_GATE_NORM = 16.0
GLA_CHUNK = 64
DIFF_HEADS = 4
DIFF_DH = 64
DIFF_DV = 128
LANES = 128
RANK_PAD = LANES
SECTION = 512
SUM_ROWS = 16

ROWS_IN_PROJ = 1024
ROWS_GLA = 1024
GLA_CUMSUM_ROWS = 256
ATTN_TQ = 256
ATTN_TK = 256
ATTN_AHEAD = 2
BF16_EXACT_INT = 256
BOUND_SLACK = 1.01
FAST_PATH_MIN_DENOM = 2.0 ** -90
VMEM_LIMIT = 48 * 1024 * 1024


def _silu(v):
    return v / (1.0 + jnp.exp(-v))


def _log_sigmoid(v):
    return jnp.minimum(v, 0.0) - jnp.log(1.0 + jnp.exp(-jnp.abs(v)))


def _adaln_kernel(c_ref, w_ref, b_ref, o_ref):
    sc = _silu(c_ref[...]).astype(BF16)
    o_ref[...] = jnp.dot(sc, w_ref[0].astype(BF16), preferred_element_type=F32) + b_ref[0]


def _adaln_mod(c, w_ada, b_ada, layer):
    bsz, d = c.shape
    n = w_ada.shape[2]
    tn = 1024
    return pl.pallas_call(
        _adaln_kernel,
        grid=(n // tn,),
        in_specs=[
            pl.BlockSpec((bsz, d), lambda j: (0, 0)),
            pl.BlockSpec((1, d, tn), lambda j: (layer, 0, j)),
            pl.BlockSpec((1, 1, tn), lambda j: (layer, 0, j)),
        ],
        out_specs=pl.BlockSpec((bsz, tn), lambda j: (0, j)),
        out_shape=jax.ShapeDtypeStruct((bsz, n), F32),
        compiler_params=pltpu.CompilerParams(dimension_semantics=("arbitrary",)),
        name="adaln_mod",
    )(c, w_ada, b_ada.reshape(b_ada.shape[0], 1, n))


def _in_proj_kernel(x_ref, shift_ref, scale_ref, gain_ref, w_ref, wup_ref, bg_ref,
                    gqk_ref, gv_ref, gz_ref, dq_ref, dk_ref, dz_ref, dvt_ref, la_ref,
                    wvt_ref):
    n_sec = 7
    dv_sec = 6

    @pl.when((pl.program_id(0) == 0) & (pl.program_id(1) == 0))
    def _():
        wvt_ref[0:SECTION, :] = w_ref[:, dv_sec * SECTION:(dv_sec + 1) * SECTION].T
        wvt_ref[SECTION:SECTION + GLA_RANK, :] = (
            w_ref[:, n_sec * SECTION:n_sec * SECTION + RANK_PAD].T[0:GLA_RANK, :])

    x = x_ref[0]
    g = gain_ref[...] * (1.0 + scale_ref[0])
    rstd = lax.rsqrt(jnp.mean(x * x, axis=-1, keepdims=True) + EPS)
    h = (x * rstd * g + shift_ref[0]).astype(BF16)

    outs = (gqk_ref, gv_ref, gz_ref, dq_ref, dk_ref, dz_ref)
    for j, o_ref in enumerate(outs):
        sec = jnp.dot(h, w_ref[:, j * SECTION:(j + 1) * SECTION], preferred_element_type=F32)
        if o_ref is gz_ref or o_ref is dz_ref:
            sec = _silu(sec)
        o_ref[0] = sec.astype(o_ref.dtype)

    vt_gr = lax.dot_general(wvt_ref[...], h, (((1,), (1,)), ((), ())), preferred_element_type=F32)
    dvt_ref[0] = vt_gr[0:SECTION].astype(dvt_ref.dtype)
    tm = vt_gr.shape[1]
    gr_t = jnp.concatenate([vt_gr[SECTION:SECTION + GLA_RANK].astype(BF16),
                            jnp.zeros((RANK_PAD - GLA_RANK, tm), BF16)], axis=0)
    logit = lax.dot_general(gr_t, wup_ref[...], (((0,), (0,)), ((), ())),
                            preferred_element_type=F32) + bg_ref[...]
    la_ref[0] = (_log_sigmoid(logit) * (1.0 / GLA_GATE_NORM)).astype(la_ref.dtype)


def _in_proj(x, mod3, norm_gain, w_in_r, wup_pad, b_gate):
    bsz, s, d = x.shape
    tm = ROWS_IN_PROJ
    ncol = w_in_r.shape[1]
    hk = GLA_HEADS * GLA_DK
    act = lambda width: pl.BlockSpec((1, tm, width), lambda b, t: (b, t, 0))
    sec = jax.ShapeDtypeStruct((bsz, s, SECTION), BF16)
    out_shapes = (
        sec,
        sec,
        sec,
        sec,
        sec,
        sec,
        jax.ShapeDtypeStruct((bsz, SECTION, s), BF16),
        jax.ShapeDtypeStruct((bsz, s, hk), BF16),
    )
    const = lambda shape: pl.BlockSpec(shape, lambda b, t: (0, 0))
    return pl.pallas_call(
        _in_proj_kernel,
        grid=(bsz, s // tm),
        in_specs=[
            act(d),
            pl.BlockSpec((1, 1, d), lambda b, t: (b, 0, 0)),
            pl.BlockSpec((1, 1, d), lambda b, t: (b, 0, 1)),
            const((1, d)), const((d, ncol)), const((RANK_PAD, hk)), const((1, hk)),
        ],
        out_specs=(act(SECTION), act(SECTION), act(SECTION), act(SECTION), act(SECTION), act(SECTION),
                   pl.BlockSpec((1, SECTION, tm), lambda b, t: (b, 0, t)), act(hk)),
        out_shape=out_shapes,
        scratch_shapes=[pltpu.VMEM((SECTION + GLA_RANK, d), BF16)],
        compiler_params=pltpu.CompilerParams(
            dimension_semantics=("arbitrary", "arbitrary"), vmem_limit_bytes=VMEM_LIMIT),
        name="in_proj",
    )(x, mod3, mod3, norm_gain, w_in_r, wup_pad, b_gate)


def _gla_out_kernel(qk_ref, v_ref, za_ref, la_ref, gain_ref, tril_ref, chunk_ind_ref, ones_bd_ref,
                    od_ref, x_ref, gate_ref, w_ref, fg_ref, o_ref, state_ref):
    tg = qk_ref.shape[1]
    half = od_ref.shape[2]
    hk = GLA_HEADS * GLA_DK
    c_sz = GLA_CHUNK
    n_ch = tg // c_sz

    @pl.when(pl.program_id(1) == 0)
    def _():
        state_ref[...] = jnp.zeros_like(state_ref)

    la = la_ref[0]
    grp = tril_ref.shape[0]
    b_all = jnp.concatenate(
        [jnp.dot(tril_ref[...], la[g * grp:(g + 1) * grp], preferred_element_type=F32)
         for g in range(tg // grp)], axis=0)
    b_last = jnp.concatenate(
        [jnp.broadcast_to(b_all[(ch + 1) * c_sz - 1:(ch + 1) * c_sz], (c_sz, hk)) for ch in range(n_ch)], axis=0)
    q = qk_ref[0, :, 0:hk]
    k = qk_ref[0, :, hk:2 * hk]
    q_in = q * jnp.exp(b_all).astype(BF16)
    k_in = k * jnp.exp(-b_all).astype(BF16)
    k_st = k * jnp.exp(b_last - b_all).astype(BF16)

    lane_head = lax.broadcasted_iota(jnp.int32, (1, hk), 1) // GLA_DK
    ri = lax.broadcasted_iota(jnp.int32, (GLA_HEADS * c_sz, c_sz), 0) % c_sz
    ci = lax.broadcasted_iota(jnp.int32, (GLA_HEADS * c_sz, c_sz), 1)
    causal = ci <= ri
    dec_all = jnp.concatenate(
        [jnp.exp(lax.dot_general(la[g * grp:(g + 1) * grp], chunk_ind_ref[...], (((0,), (0,)), ((), ())),
                                 preferred_element_type=F32)) for g in range(tg // grp)], axis=1)

    mix_diff = jnp.dot(od_ref[0], w_ref[half:2 * half, :], preferred_element_type=F32)

    chunk_rows = [slice(ch * c_sz, (ch + 1) * c_sz) for ch in range(n_ch)]
    head_rows = [slice(hh * c_sz, (hh + 1) * c_sz) for hh in range(GLA_HEADS)]
    head_cols = [slice(hh * GLA_DV, (hh + 1) * GLA_DV) for hh in range(GLA_HEADS)]
    vs = [v_ref[0, rows, :] for rows in chunk_rows]

    us = []
    for ch, rows in enumerate(chunk_rows):
        kst_t = k_st[rows].T
        us.append(jnp.concatenate(
            [jnp.dot(kst_t[hr], vs[ch][:, vc], preferred_element_type=F32)
             for hr, vc in zip(head_rows, head_cols)], axis=0))
    state = state_ref[...]
    states = []
    for ch in range(n_ch):
        states.append(state.astype(BF16))
        state = dec_all[:, ch * GLA_DV:(ch + 1) * GLA_DV] * state + us[ch]
    state_ref[...] = state
    boths = []
    for ch, rows in enumerate(chunk_rows):
        q_c = q_in[rows]
        qm = jnp.concatenate(
            [jnp.where(lane_head == hh, q_c, jnp.zeros_like(q_c)) for hh in range(GLA_HEADS)], axis=0)
        rhs = jnp.concatenate([states[ch], k_in[rows].T], axis=1)
        boths.append(jnp.dot(qm, rhs, preferred_element_type=F32))
    o_rows = []
    for ch in range(n_ch):
        inter = boths[ch][:, 0:GLA_DV]
        p = jnp.where(causal, boths[ch][:, GLA_DV:GLA_DV + c_sz], 0.0).astype(BF16)
        o_rows.append(jnp.concatenate(
            [jnp.dot(p[hr], vs[ch][:, vc], preferred_element_type=F32) + inter[hr]
             for hr, vc in zip(head_rows, head_cols)], axis=1))

    o = jnp.concatenate(o_rows, axis=0)
    ms = jnp.dot((o * o).astype(BF16), ones_bd_ref[...], preferred_element_type=F32) * (1.0 / GLA_DV)
    o_gla = (o * lax.rsqrt(ms + EPS) * gain_ref[...] * za_ref[0].astype(F32)).astype(BF16)

    mixw = jnp.dot(o_gla, w_ref[0:half, :], preferred_element_type=F32) + mix_diff
    xn = x_ref[0] + gate_ref[0] * mixw
    o_ref[0] = xn * lax.rsqrt(jnp.mean(xn * xn, axis=-1, keepdims=True) + EPS) * fg_ref[...]


def _gla_out(gqk, gv, gza, la, gla_out_gain, o_diff, x, mod3, w_out_bf, final_gain):
    bsz, s, d = x.shape
    tg = ROWS_GLA
    hk = GLA_HEADS * GLA_DK
    width = GLA_HEADS * GLA_DV
    grp = GLA_CUMSUM_ROWS
    idx = np.arange(grp)
    tril = ((idx[:, None] // GLA_CHUNK == idx[None, :] // GLA_CHUNK) & (idx[None, :] <= idx[:, None]))
    chunk_ind = idx[:, None] // GLA_CHUNK == np.arange(grp // GLA_CHUNK * GLA_DV)[None, :] // GLA_DV
    col = np.arange(width)
    ones_bd = col[:, None] // GLA_DV == col[None, :] // GLA_DV
    act = lambda w: pl.BlockSpec((1, tg, w), lambda b, t: (b, t, 0))
    const = lambda shape: pl.BlockSpec(shape, lambda b, t: (0, 0))
    return pl.pallas_call(
        _gla_out_kernel,
        grid=(bsz, s // tg),
        in_specs=[act(2 * hk), act(width), act(width), act(hk),
                  const((1, width)), const(tril.shape), const(chunk_ind.shape), const((width, width)),
                  act(o_diff.shape[2]), act(d),
                  pl.BlockSpec((1, 1, d), lambda b, t: (b, 0, 2)),
                  const(w_out_bf.shape), const((1, d))],
        out_specs=act(d),
        out_shape=jax.ShapeDtypeStruct((bsz, s, d), F32),
        scratch_shapes=[pltpu.VMEM((hk, GLA_DV), F32)],
        compiler_params=pltpu.CompilerParams(
            dimension_semantics=("arbitrary", "arbitrary"), vmem_limit_bytes=VMEM_LIMIT),
        name="gla_out_proj",
    )(gqk, gv, gza, la, gla_out_gain, jnp.asarray(tril, BF16), jnp.asarray(chunk_ind, BF16),
      jnp.asarray(ones_bd, BF16), o_diff, x, mod3, w_out_bf, final_gain)


def _diff_lambda(lq1_ref, lk1_ref, lq2_ref, lk2_ref, lam_init):
    a = jnp.sum(lq1_ref[...] * lk1_ref[...], axis=-1, keepdims=True)
    b = jnp.sum(lq2_ref[...] * lk2_ref[...], axis=-1, keepdims=True)
    return jnp.exp(a) - jnp.exp(b) + lam_init


def _attn_spans(nq, tk):
    spans = []
    for qi in range(nq):
        tile = [(qi, 2 * tk * j, 2 * tk, None) for j in range(qi // 2)]
        if qi % 2 == 1:
            tile.append((qi, (qi - 1) * tk, 2 * tk, tk))
        else:
            tile.append((qi, qi * tk, tk, 0))
        for i, sp in enumerate(tile):
            spans.append(sp + (i == 0, i == len(tile) - 1))
    return spans


def _attn_bias_tables(tq, tk):
    def split(x, terms):
        parts = []
        for _ in range(terms):
            hi = x.astype(jnp.bfloat16)
            parts.append(hi)
            x = x - hi.astype(np.float32)
        return parts

    key = np.arange(2 * tk, dtype=np.float32)
    kaug = np.zeros((2 * tk, LANES), np.float32)
    kaug[:, 0] = kaug[:, 1] = key % BF16_EXACT_INT
    kaug[:, 2] = kaug[:, 3] = key - key % BF16_EXACT_INT
    kaug[:, 4:7] = 1.0
    slopes = (2.0 ** (-8.0 * np.arange(1, DIFF_HEADS + 1) / DIFF_HEADS) * LOG2E).astype(np.float32)
    qry = np.arange(tq, dtype=np.float32)
    qaug = np.zeros((DIFF_HEADS, tq, LANES), jnp.bfloat16)
    for hh in range(DIFF_HEADS):
        s_hi, s_lo = split(slopes[hh:hh + 1], 2)
        qaug[hh, :, 0] = qaug[hh, :, 2] = s_hi
        qaug[hh, :, 1] = qaug[hh, :, 3] = s_lo
        for lane, part in zip((4, 5, 6), split(-slopes[hh] * qry, 3)):
            qaug[hh, :, lane] = part
    slope_rows = np.broadcast_to(slopes[:, None, None], (DIFF_HEADS, 1, LANES))
    return jnp.asarray(kaug, BF16), jnp.asarray(qaug), jnp.asarray(slope_rows, F32)


def _attn_kernel(q_ref, k_ref, vt_ref, z_ref, gain_ref, kaug_ref, qaug_ref, slope_ref,
                 lq1_ref, lk1_ref, lq2_ref, lk2_ref,
                 o_ref, km_ref, s_ref, mb_ref, acc_ref, *, lam_init):
    tq = ATTN_TQ
    tk = ATTN_TK
    hw = 2 * DIFF_DH
    nq = q_ref.shape[1] // tq
    slope = slope_ref[0][:, 0:1]
    n_buf = ATTN_AHEAD + 1

    first_half = lax.broadcasted_iota(jnp.int32, (1, hw), 1) < DIFF_DH

    def rel_pos(nk):
        key = lax.broadcasted_iota(jnp.int32, (nk, 2 * tq), 0)
        qry = lax.broadcasted_iota(jnp.int32, (nk, 2 * tq), 1) % tq
        return qry - key

    lam = _diff_lambda(lq1_ref, lk1_ref, lq2_ref, lk2_ref, lam_init)
    spans = _attn_spans(nq, tk)

    def span_offset(qi, start):
        return -slope * float(qi * tq - start)

    k_all = k_ref[0]
    zk = jnp.zeros_like(k_all)
    km_ref[0] = jnp.where(first_half, k_all, zk)
    km_ref[1] = jnp.where(first_half, zk, k_all)

    def masked_scores(t):
        qi, start, nk, diag_off, _, _ = spans[t]
        qrhs = jnp.concatenate([q_ref[0, qi * tq:(qi + 1) * tq, :], qaug_ref[0]], axis=1)
        s_maps = []
        for m in range(2):
            kblk = jnp.concatenate([km_ref[m, start:start + nk, :], kaug_ref[0:nk, :]], axis=1)
            s_maps.append(lax.dot_general(kblk, qrhs, (((1,), (1,)), ((), ())), preferred_element_type=F32))
        s = jnp.concatenate(s_maps, axis=1)
        if diag_off is not None:
            keep = rel_pos(tk) + (diag_off - (nk - tk)) >= 0
            tail = jnp.where(keep, s[nk - tk:nk], -jnp.inf)
            s = tail if nk == tk else jnp.concatenate([s[0:nk - tk], tail], axis=0)
        return s

    def values_and_ones(t):
        _, start, nk, _, _, _ = spans[t]
        return jnp.concatenate([vt_ref[0, :, start:start + nk], jnp.ones((SUM_ROWS, nk), BF16)], axis=0)

    def finish(qi, acc):
        rows = slice(qi * tq, (qi + 1) * tq)
        o_all = acc[0:DIFF_DV, :] / acc[DIFF_DV:DIFF_DV + 1, :]
        o = (o_all[:, :tq] - lam * o_all[:, tq:]).T
        y = o * lax.rsqrt(jnp.mean(o * o, axis=-1, keepdims=True) + EPS)
        y = y * gain_ref[0] * (1.0 - lam_init) * z_ref[0, rows, :].astype(F32)
        o_ref[0, rows, :] = y.astype(o_ref.dtype)

    map_cols = lax.broadcasted_iota(jnp.int32, (1, 2 * tq), 1) < tq

    r_i = lax.broadcasted_iota(jnp.int32, (hw, LANES), 0)
    c_i = lax.broadcasted_iota(jnp.int32, (hw, LANES), 1)
    half_sel = jnp.where(r_i // DIFF_DH == c_i, 1.0, 0.0).astype(BF16)
    k_norm2 = jnp.max(jnp.dot(k_all * k_all, half_sel, preferred_element_type=F32), axis=0, keepdims=True)
    k_max = jnp.sqrt(k_norm2)
    k_max_row = jnp.where(map_cols, k_max[:, 0:1], k_max[:, 1:2])
    half_rows = (lax.broadcasted_iota(jnp.int32, (8, hw), 1) // DIFF_DH
                 == lax.broadcasted_iota(jnp.int32, (8, hw), 0)).astype(BF16)
    bounds = []
    for qi in range(nq):
        q = q_ref[0, qi * tq:(qi + 1) * tq, :]
        q_half2 = lax.dot_general(half_rows, q * q, (((1,), (1,)), ((), ())), preferred_element_type=F32)
        q_norm2 = jnp.concatenate([q_half2[0:1], q_half2[1:2]], axis=1)
        bounds.append(jnp.sqrt(q_norm2) * k_max_row * BOUND_SLACK)
    denom_min = None
    acc = None
    s_next = masked_scores(0)
    for t, (qi, start, _, _, first, last) in enumerate(spans):
        s = s_next
        if t + 1 < len(spans):
            s_next = masked_scores(t + 1)
        p = jnp.exp2(s - (bounds[qi] - span_offset(qi, start))).astype(BF16)
        pv = jnp.dot(values_and_ones(t), p, preferred_element_type=F32)
        acc = pv if first else acc + pv
        if last:
            denom = acc[DIFF_DV:DIFF_DV + 1, :]
            denom_min = denom if denom_min is None else jnp.minimum(denom_min, denom)
            finish(qi, acc)
    fast_ok = jnp.min(denom_min) >= FAST_PATH_MIN_DENOM

    def scores(t):
        qi, start, nk, _, _, _ = spans[t]
        s = masked_scores(t)
        s_ref[t % n_buf, 0:nk, :] = s
        mb_ref[t % n_buf] = jnp.max(s, axis=0, keepdims=True) + span_offset(qi, start)

    def update(t, m_old):
        qi, start, nk, _, first, last = spans[t]
        s = s_ref[t % n_buf, 0:nk, :]
        m_blk = mb_ref[t % n_buf]
        m_new = m_blk if first else jnp.maximum(m_old, m_blk)
        p = jnp.exp2(s - (m_new - span_offset(qi, start))).astype(BF16)
        pv = jnp.dot(values_and_ones(t), p, preferred_element_type=F32)
        if first:
            acc_ref[...] = pv
        else:
            acc_ref[...] = jnp.exp2(m_old - m_new) * acc_ref[...] + pv
        if last:
            finish(qi, acc_ref[...])
        return m_new

    @pl.when(jnp.logical_not(fast_ok))
    def _():
        for t in range(min(ATTN_AHEAD, len(spans))):
            scores(t)
        m = None
        for t in range(len(spans)):
            if t + ATTN_AHEAD < len(spans):
                scores(t + ATTN_AHEAD)
            m = update(t, m)


def _diff_attn(dq, dk, dvt, dz, gain, lq1, lk1, lq2, lk2, lam_init):
    bsz, s, width = dq.shape
    tq = ATTN_TQ
    assert ATTN_TQ == ATTN_TK
    hw = 2 * DIFF_DH
    rowspec = pl.BlockSpec((1, s, hw), lambda b, h: (b, 0, h))
    vtspec = pl.BlockSpec((1, DIFF_DV, s), lambda b, h: (b, h, 0))
    lspec = pl.BlockSpec((1, DIFF_DH), lambda b, h: (0, 0))
    kaug, qaug, slope_rows = _attn_bias_tables(tq, ATTN_TK)
    per_head = lambda rows: pl.BlockSpec((1, rows, LANES), lambda b, h: (h, 0, 0))
    return pl.pallas_call(
        functools.partial(_attn_kernel, lam_init=lam_init),
        grid=(bsz, DIFF_HEADS),
        in_specs=[rowspec, rowspec, vtspec, rowspec, per_head(1),
                  pl.BlockSpec(kaug.shape, lambda b, h: (0, 0)), per_head(tq), per_head(1),
                  lspec, lspec, lspec, lspec],
        out_specs=rowspec,
        out_shape=jax.ShapeDtypeStruct((bsz, s, width), BF16),
        scratch_shapes=[pltpu.VMEM((2, s, hw), BF16),
                        pltpu.VMEM((ATTN_AHEAD + 1, 2 * ATTN_TK, 2 * tq), F32),
                        pltpu.VMEM((ATTN_AHEAD + 1, 1, 2 * tq), F32),
                        pltpu.VMEM((DIFF_DV + SUM_ROWS, 2 * tq), F32)],
        compiler_params=pltpu.CompilerParams(
            dimension_semantics=("arbitrary", "arbitrary"), vmem_limit_bytes=VMEM_LIMIT),
        name="diff_attn",
    )(dq, dk, dvt, dz, gain.reshape(DIFF_HEADS, 1, DIFF_DV), kaug, qaug, slope_rows, lq1, lk1, lq2, lk2)


def _w_in_layout():
    hk = GLA_HEADS * GLA_DK
    names = ("gq", "gk", "gv", "gz", "gr", "dq", "dk", "dv", "dz")
    sizes = (hk, hk, SECTION, SECTION, GLA_RANK, SECTION, SECTION, SECTION, SECTION)
    src = dict(zip(names, np.concatenate([[0], np.cumsum(sizes)[:-1]]).tolist()))
    width = dict(zip(names, sizes))
    scale = {"gq": GLA_DK ** -0.5, "dq": DIFF_DH ** -0.5 * LOG2E}
    pieces, dst = [], 0
    for name in ("gq", "gk", "gv", "gz", "dq", "dk", "dz", "dv", "gr"):
        pieces.append((src[name], dst, width[name], scale.get(name)))
        dst += width[name]
    return pieces, dst + RANK_PAD - GLA_RANK


def _w_prep_kernel(wt_ref, o_ref):
    pieces, _ = _w_in_layout()
    for src, dst, width, scale in pieces:
        piece = wt_ref[0, src:src + width, :]
        if scale is not None:
            piece = piece * scale
        if width < LANES:
            piece = jnp.concatenate([piece, jnp.zeros((LANES - width, piece.shape[1]), piece.dtype)], axis=0)
        o_ref[:, dst:dst + piece.shape[0]] = piece.T.astype(o_ref.dtype)


def _prep_w_in(w_in, layer):
    _, d, n_src = w_in.shape
    _, n_dst = _w_in_layout()
    rows = 256
    return pl.pallas_call(
        _w_prep_kernel,
        grid=(d // rows,),
        in_specs=[pl.BlockSpec((1, n_src, rows), lambda i: (layer, 0, i))],
        out_specs=pl.BlockSpec((rows, n_dst), lambda i: (i, 0)),
        out_shape=jax.ShapeDtypeStruct((d, n_dst), BF16),
        compiler_params=pltpu.CompilerParams(dimension_semantics=("arbitrary",), vmem_limit_bytes=VMEM_LIMIT),
        name="w_in_prep",
    )(jnp.swapaxes(w_in, 1, 2))


def kernel(x, c, w_ada, b_ada, norm_gain, w_in, w_gla_gate_up, b_gla_gate, gla_out_gain,
           lambda_q1, lambda_k1, lambda_q2, lambda_k2, diff_out_gain, w_out, final_gain):
    bsz, s, d = x.shape
    depth = w_in.shape[0]
    assert depth == 1, "gla_out_proj applies the final rmsnorm, so exactly one layer is supported"
    for l in range(depth):
        mod = _adaln_mod(c, w_ada, b_ada, l)
        mod3 = mod.reshape(bsz, 1, 3 * d)
        w_in_r = _prep_w_in(w_in, l)
        wup_pad = jnp.pad(w_gla_gate_up[l], ((0, RANK_PAD - GLA_RANK), (0, 0))).astype(BF16)
        gqk, gv, gz, dq, dk, dz, dvt, la = _in_proj(
            x, mod3, norm_gain[l].reshape(1, d), w_in_r, wup_pad, b_gla_gate[l].reshape(1, -1))
        lam_init = float(0.8 - 0.6 * np.exp(-0.3 * l))
        o_diff = _diff_attn(dq, dk, dvt, dz, diff_out_gain[l],
                            lambda_q1[l].reshape(1, -1), lambda_k1[l].reshape(1, -1),
                            lambda_q2[l].reshape(1, -1), lambda_k2[l].reshape(1, -1), lam_init)
        x = _gla_out(gqk, gv, gz, la, gla_out_gain[l].reshape(1, -1), o_diff, x, mod3,
                     w_out[l].astype(BF16), final_gain.reshape(1, d))
    return x
```

```python
import functools
import math

import jax
import jax.numpy as jnp
import numpy as np
from jax import lax
from jax.experimental import pallas as pl
from jax.experimental.pallas import tpu as pltpu

F32 = jnp.float32
BF16 = jnp.bfloat16

EPS = 1e-6
LOG2E = math.log2(math.e)
GLA_HEADS = 4
GLA_DK = 64
GLA_DV = 128
GLA_RANK = 16
GLA_GATE_NORM = 16.0
GLA_CHUNK = 64
DIFF_HEADS = 4
DIFF_DH = 64
DIFF_DV = 128
LANES = 128
RANK_PAD = LANES
SECTION = 512
SUM_ROWS = 16

ROWS_IN_PROJ = 1024
ROWS_GLA = 1024
GLA_CUMSUM_ROWS = 256
ATTN_TQ = 256
ATTN_TK = 256
ATTN_HEADS_PER_STEP = 2
ATTN_AHEAD = 2
BF16_EXACT_INT = 256
BOUND_SLACK = 1.01
FAST_PATH_MIN_DENOM = 2.0 ** -90
VMEM_LIMIT = 48 * 1024 * 1024


def _silu(v):
    return v / (1.0 + jnp.exp(-v))


def _log_sigmoid(v):
    return jnp.minimum(v, 0.0) - jnp.log(1.0 + jnp.exp(-jnp.abs(v)))


def _adaln_kernel(c_ref, w_ref, b_ref, o_ref):
    sc = _silu(c_ref[...]).astype(BF16)
    o_ref[...] = jnp.dot(sc, w_ref[0].astype(BF16), preferred_element_type=F32) + b_ref[0]


def _adaln_mod(c, w_ada, b_ada, layer):
    bsz, d = c.shape
    n = w_ada.shape[2]
    tn = 1024
    return pl.pallas_call(
        _adaln_kernel,
        grid=(n // tn,),
        in_specs=[
            pl.BlockSpec((bsz, d), lambda j: (0, 0)),
            pl.BlockSpec((1, d, tn), lambda j: (layer, 0, j)),
            pl.BlockSpec((1, 1, tn), lambda j: (layer, 0, j)),
        ],
        out_specs=pl.BlockSpec((bsz, tn), lambda j: (0, j)),
        out_shape=jax.ShapeDtypeStruct((bsz, n), F32),
        compiler_params=pltpu.CompilerParams(dimension_semantics=("arbitrary",)),
        name="adaln_mod",
    )(c, w_ada, b_ada.reshape(b_ada.shape[0], 1, n))


def _in_proj_kernel(x_ref, shift_ref, scale_ref, gain_ref, w_ref, wup_ref, bg_ref,
                    gqk_ref, gv_ref, gz_ref, dq_ref, dk_ref, dz_ref, dvt_ref, la_ref,
                    wvt_ref):
    n_sec = 7
    dv_sec = 6

    @pl.when((pl.program_id(0) == 0) & (pl.program_id(1) == 0))
    def _():
        wvt_ref[0:SECTION, :] = w_ref[:, dv_sec * SECTION:(dv_sec + 1) * SECTION].T
        wvt_ref[SECTION:SECTION + GLA_RANK, :] = (
            w_ref[:, n_sec * SECTION:n_sec * SECTION + RANK_PAD].T[0:GLA_RANK, :])

    x = x_ref[0]
    g = gain_ref[...] * (1.0 + scale_ref[0])
    rstd = lax.rsqrt(jnp.mean(x * x, axis=-1, keepdims=True) + EPS)
    h = (x * rstd * g + shift_ref[0]).astype(BF16)

    outs = (gqk_ref, gv_ref, gz_ref, dq_ref, dk_ref, dz_ref)
    for j, o_ref in enumerate(outs):
        sec = jnp.dot(h, w_ref[:, j * SECTION:(j + 1) * SECTION], preferred_element_type=F32)
        if o_ref is gz_ref or o_ref is dz_ref:
            sec = _silu(sec)
        o_ref[0] = sec.astype(o_ref.dtype)

    vt_gr = lax.dot_general(wvt_ref[...], h, (((1,), (1,)), ((), ())), preferred_element_type=F32)
    dvt_ref[0] = vt_gr[0:SECTION].astype(dvt_ref.dtype)
    tm = vt_gr.shape[1]
    gr_t = jnp.concatenate([vt_gr[SECTION:SECTION + GLA_RANK].astype(BF16),
                            jnp.zeros((RANK_PAD - GLA_RANK, tm), BF16)], axis=0)
    logit = lax.dot_general(gr_t, wup_ref[...], (((0,), (0,)), ((), ())),
                            preferred_element_type=F32) + bg_ref[...]
    la_ref[0] = (_log_sigmoid(logit) * (1.0 / GLA_GATE_NORM)).astype(la_ref.dtype)


def _in_proj(x, mod3, norm_gain, w_in_r, wup_pad, b_gate):
    bsz, s, d = x.shape
    tm = ROWS_IN_PROJ
    ncol = w_in_r.shape[1]
    hk = GLA_HEADS * GLA_DK
    act = lambda width: pl.BlockSpec((1, tm, width), lambda b, t: (b, t, 0))
    sec = jax.ShapeDtypeStruct((bsz, s, SECTION), BF16)
    out_shapes = (
        sec,
        sec,
        sec,
        sec,
        sec,
        sec,
        jax.ShapeDtypeStruct((bsz, SECTION, s), BF16),
        jax.ShapeDtypeStruct((bsz, s, hk), BF16),
    )
    const = lambda shape: pl.BlockSpec(shape, lambda b, t: (0, 0))
    return pl.pallas_call(
        _in_proj_kernel,
        grid=(bsz, s // tm),
        in_specs=[
            act(d),
            pl.BlockSpec((1, 1, d), lambda b, t: (b, 0, 0)),
            pl.BlockSpec((1, 1, d), lambda b, t: (b, 0, 1)),
            const((1, d)), const((d, ncol)), const((RANK_PAD, hk)), const((1, hk)),
        ],
        out_specs=(act(SECTION), act(SECTION), act(SECTION), act(SECTION), act(SECTION), act(SECTION),
                   pl.BlockSpec((1, SECTION, tm), lambda b, t: (b, 0, t)), act(hk)),
        out_shape=out_shapes,
        scratch_shapes=[pltpu.VMEM((SECTION + GLA_RANK, d), BF16)],
        compiler_params=pltpu.CompilerParams(
            dimension_semantics=("arbitrary", "arbitrary"), vmem_limit_bytes=VMEM_LIMIT),
        name="in_proj",
    )(x, mod3, mod3, norm_gain, w_in_r, wup_pad, b_gate)


def _gla_out_kernel(qk_ref, v_ref, za_ref, la_ref, gain_ref, tril_ref, chunk_ind_ref, ones_bd_ref,
                    od_ref, x_ref, gate_ref, w_ref, fg_ref, o_ref, state_ref):
    tg = qk_ref.shape[1]
    half = od_ref.shape[2]
    hk = GLA_HEADS * GLA_DK
    c_sz = GLA_CHUNK
    n_ch = tg // c_sz

    @pl.when(pl.program_id(1) == 0)
    def _():
        state_ref[...] = jnp.zeros_like(state_ref)

    la = la_ref[0]
    grp = tril_ref.shape[0]
    b_all = jnp.concatenate(
        [jnp.dot(tril_ref[...], la[g * grp:(g + 1) * grp], preferred_element_type=F32)
         for g in range(tg // grp)], axis=0)
    b_last = jnp.concatenate(
        [jnp.broadcast_to(b_all[(ch + 1) * c_sz - 1:(ch + 1) * c_sz], (c_sz, hk)) for ch in range(n_ch)], axis=0)
    q = qk_ref[0, :, 0:hk]
    k = qk_ref[0, :, hk:2 * hk]
    q_in = q * jnp.exp(b_all).astype(BF16)
    k_in = k * jnp.exp(-b_all).astype(BF16)
    k_st = k * jnp.exp(b_last - b_all).astype(BF16)

    lane_head = lax.broadcasted_iota(jnp.int32, (1, hk), 1) // GLA_DK
    ri = lax.broadcasted_iota(jnp.int32, (GLA_HEADS * c_sz, c_sz), 0) % c_sz
    ci = lax.broadcasted_iota(jnp.int32, (GLA_HEADS * c_sz, c_sz), 1)
    causal = ci <= ri
    dec_all = jnp.concatenate(
        [jnp.exp(lax.dot_general(la[g * grp:(g + 1) * grp], chunk_ind_ref[...], (((0,), (0,)), ((), ())),
                                 preferred_element_type=F32)) for g in range(tg // grp)], axis=1)

    mix_diff = jnp.dot(od_ref[0], w_ref[half:2 * half, :], preferred_element_type=F32)

    chunk_rows = [slice(ch * c_sz, (ch + 1) * c_sz) for ch in range(n_ch)]
    head_rows = [slice(hh * c_sz, (hh + 1) * c_sz) for hh in range(GLA_HEADS)]
    head_cols = [slice(hh * GLA_DV, (hh + 1) * GLA_DV) for hh in range(GLA_HEADS)]
    vs = [v_ref[0, rows, :] for rows in chunk_rows]

    us = []
    for ch, rows in enumerate(chunk_rows):
        kst_t = k_st[rows].T
        us.append(jnp.concatenate(
            [jnp.dot(kst_t[hr], vs[ch][:, vc], preferred_element_type=F32)
             for hr, vc in zip(head_rows, head_cols)], axis=0))
    state = state_ref[...]
    states = []
    for ch in range(n_ch):
        states.append(state.astype(BF16))
        state = dec_all[:, ch * GLA_DV:(ch + 1) * GLA_DV] * state + us[ch]
    state_ref[...] = state
    boths = []
    for ch, rows in enumerate(chunk_rows):
        q_c = q_in[rows]
        qm = jnp.concatenate(
            [jnp.where(lane_head == hh, q_c, jnp.zeros_like(q_c)) for hh in range(GLA_HEADS)], axis=0)
        rhs = jnp.concatenate([states[ch], k_in[rows].T], axis=1)
        boths.append(jnp.dot(qm, rhs, preferred_element_type=F32))
    o_rows = []
    for ch in range(n_ch):
        inter = boths[ch][:, 0:GLA_DV]
        p = jnp.where(causal, boths[ch][:, GLA_DV:GLA_DV + c_sz], 0.0).astype(BF16)
        o_rows.append(jnp.concatenate(
            [jnp.dot(p[hr], vs[ch][:, vc], preferred_element_type=F32) + inter[hr]
             for hr, vc in zip(head_rows, head_cols)], axis=1))

    o = jnp.concatenate(o_rows, axis=0)
    ms = jnp.dot((o * o).astype(BF16), ones_bd_ref[...], preferred_element_type=F32) * (1.0 / GLA_DV)
    o_gla = (o * lax.rsqrt(ms + EPS) * gain_ref[...] * za_ref[0].astype(F32)).astype(BF16)

    mixw = jnp.dot(o_gla, w_ref[0:half, :], preferred_element_type=F32) + mix_diff
    xn = x_ref[0] + gate_ref[0] * mixw
    o_ref[0] = xn * lax.rsqrt(jnp.mean(xn * xn, axis=-1, keepdims=True) + EPS) * fg_ref[...]


def _gla_out(gqk, gv, gza, la, gla_out_gain, o_diff, x, mod3, w_out_bf, final_gain):
    bsz, s, d = x.shape
    tg = ROWS_GLA
    hk = GLA_HEADS * GLA_DK
    width = GLA_HEADS * GLA_DV
    grp = GLA_CUMSUM_ROWS
    idx = np.arange(grp)
    tril = ((idx[:, None] // GLA_CHUNK == idx[None, :] // GLA_CHUNK) & (idx[None, :] <= idx[:, None]))
    chunk_ind = idx[:, None] // GLA_CHUNK == np.arange(grp // GLA_CHUNK * GLA_DV)[None, :] // GLA_DV
    col = np.arange(width)
    ones_bd = col[:, None] // GLA_DV == col[None, :] // GLA_DV
    act = lambda w: pl.BlockSpec((1, tg, w), lambda b, t: (b, t, 0))
    const = lambda shape: pl.BlockSpec(shape, lambda b, t: (0, 0))
    return pl.pallas_call(
        _gla_out_kernel,
        grid=(bsz, s // tg),
        in_specs=[act(2 * hk), act(width), act(width), act(hk),
                  const((1, width)), const(tril.shape), const(chunk_ind.shape), const((width, width)),
                  act(o_diff.shape[2]), act(d),
                  pl.BlockSpec((1, 1, d), lambda b, t: (b, 0, 2)),
                  const(w_out_bf.shape), const((1, d))],
        out_specs=act(d),
        out_shape=jax.ShapeDtypeStruct((bsz, s, d), F32),
        scratch_shapes=[pltpu.VMEM((hk, GLA_DV), F32)],
        compiler_params=pltpu.CompilerParams(
            dimension_semantics=("arbitrary", "arbitrary"), vmem_limit_bytes=VMEM_LIMIT),
        name="gla_out_proj",
    )(gqk, gv, gza, la, gla_out_gain, jnp.asarray(tril, BF16), jnp.asarray(chunk_ind, BF16),
      jnp.asarray(ones_bd, BF16), o_diff, x, mod3, w_out_bf, final_gain)


def _diff_lambda(lq1_ref, lk1_ref, lq2_ref, lk2_ref, lam_init):
    a = jnp.sum(lq1_ref[...] * lk1_ref[...], axis=-1, keepdims=True)
    b = jnp.sum(lq2_ref[...] * lk2_ref[...], axis=-1, keepdims=True)
    return jnp.exp(a) - jnp.exp(b) + lam_init


def _attn_spans(nq, tk):
    spans = []
    for qi in range(nq):
        tile = [(qi, 2 * tk * j, 2 * tk, None) for j in range(qi // 2)]
        if qi % 2 == 1:
            tile.append((qi, (qi - 1) * tk, 2 * tk, tk))
        else:
            tile.append((qi, qi * tk, tk, 0))
        for i, sp in enumerate(tile):
            spans.append(sp + (i == 0, i == len(tile) - 1))
    return spans


def _attn_bias_tables(tq, tk):
    def split(x, terms):
        parts = []
        for _ in range(terms):
            hi = x.astype(jnp.bfloat16)
            parts.append(hi)
            x = x - hi.astype(np.float32)
        return parts

    key = np.arange(2 * tk, dtype=np.float32)
    kaug = np.zeros((2 * tk, LANES), np.float32)
    kaug[:, 0] = kaug[:, 1] = key % BF16_EXACT_INT
    kaug[:, 2] = kaug[:, 3] = key - key % BF16_EXACT_INT
    kaug[:, 4:7] = 1.0
    slopes = (2.0 ** (-8.0 * np.arange(1, DIFF_HEADS + 1) / DIFF_HEADS) * LOG2E).astype(np.float32)
    qry = np.arange(tq, dtype=np.float32)
    qaug = np.zeros((DIFF_HEADS, tq, LANES), jnp.bfloat16)
    for hh in range(DIFF_HEADS):
        s_hi, s_lo = split(slopes[hh:hh + 1], 2)
        qaug[hh, :, 0] = qaug[hh, :, 2] = s_hi
        qaug[hh, :, 1] = qaug[hh, :, 3] = s_lo
        for lane, part in zip((4, 5, 6), split(-slopes[hh] * qry, 3)):
            qaug[hh, :, lane] = part
    slope_rows = np.broadcast_to(slopes[:, None, None], (DIFF_HEADS, 1, LANES))
    return jnp.asarray(kaug, BF16), jnp.asarray(qaug), jnp.asarray(slope_rows, F32)


def _attn_kernel(q_ref, k_ref, vt_ref, z_ref, gain_ref, kaug_ref, qaug_ref, slope_ref,
                 lq1_ref, lk1_ref, lq2_ref, lk2_ref,
                 o_ref, km_ref, s_ref, mb_ref, acc_ref, *, lam_init):
    tq = ATTN_TQ
    tk = ATTN_TK
    hw = 2 * DIFF_DH
    nq = q_ref.shape[1] // tq
    n_heads = q_ref.shape[2] // hw
    n_buf = ATTN_AHEAD + 1

    first_half = lax.broadcasted_iota(jnp.int32, (1, hw), 1) < DIFF_DH
    map_cols = lax.broadcasted_iota(jnp.int32, (1, 2 * tq), 1) < tq
    r_i = lax.broadcasted_iota(jnp.int32, (hw, LANES), 0)
    c_i = lax.broadcasted_iota(jnp.int32, (hw, LANES), 1)
    half_sel = jnp.where(r_i // DIFF_DH == c_i, 1.0, 0.0).astype(BF16)
    half_rows = (lax.broadcasted_iota(jnp.int32, (8, hw), 1) // DIFF_DH
                 == lax.broadcasted_iota(jnp.int32, (8, hw), 0)).astype(BF16)

    def rel_pos(nk):
        key = lax.broadcasted_iota(jnp.int32, (nk, 2 * tq), 0)
        qry = lax.broadcasted_iota(jnp.int32, (nk, 2 * tq), 1) % tq
        return qry - key

    lam = _diff_lambda(lq1_ref, lk1_ref, lq2_ref, lk2_ref, lam_init)
    spans = _attn_spans(nq, tk)

    class Head:
        def __init__(self, j):
            self.j = j
            self.lanes = slice(j * hw, (j + 1) * hw)
            self.slope = slope_ref[j][:, 0:1]
            k_all = k_ref[0, :, self.lanes]
            zk = jnp.zeros_like(k_all)
            km_ref[j, 0] = jnp.where(first_half, k_all, zk)
            km_ref[j, 1] = jnp.where(first_half, zk, k_all)
            k_norm2 = jnp.max(jnp.dot(k_all * k_all, half_sel, preferred_element_type=F32), axis=0, keepdims=True)
            k_max = jnp.sqrt(k_norm2)
            k_max_row = jnp.where(map_cols, k_max[:, 0:1], k_max[:, 1:2])
            self.bounds = []
            for qi in range(nq):
                q = q_ref[0, qi * tq:(qi + 1) * tq, self.lanes]
                q_half2 = lax.dot_general(half_rows, q * q, (((1,), (1,)), ((), ())), preferred_element_type=F32)
                q_norm2 = jnp.concatenate([q_half2[0:1], q_half2[1:2]], axis=1)
                self.bounds.append(jnp.sqrt(q_norm2) * k_max_row * BOUND_SLACK)

        def span_offset(self, qi, start):
            return -self.slope * float(qi * tq - start)

        def masked_scores(self, t):
            qi, start, nk, diag_off, _, _ = spans[t]
            qrhs = jnp.concatenate([q_ref[0, qi * tq:(qi + 1) * tq, self.lanes], qaug_ref[self.j]], axis=1)
            s_maps = []
            for m in range(2):
                kblk = jnp.concatenate([km_ref[self.j, m, start:start + nk, :], kaug_ref[0:nk, :]], axis=1)
                s_maps.append(lax.dot_general(kblk, qrhs, (((1,), (1,)), ((), ())), preferred_element_type=F32))
            s = jnp.concatenate(s_maps, axis=1)
            if diag_off is not None:
                keep = rel_pos(tk) + (diag_off - (nk - tk)) >= 0
                tail = jnp.where(keep, s[nk - tk:nk], -jnp.inf)
                s = tail if nk == tk else jnp.concatenate([s[0:nk - tk], tail], axis=0)
            return s

        def values_and_ones(self, t):
            _, start, nk, _, _, _ = spans[t]
            vt = vt_ref[0, self.j * DIFF_DV:(self.j + 1) * DIFF_DV, start:start + nk]
            return jnp.concatenate([vt, jnp.ones((SUM_ROWS, nk), BF16)], axis=0)

        def finish(self, qi, acc):
            rows = slice(qi * tq, (qi + 1) * tq)
            o_all = acc[0:DIFF_DV, :] / acc[DIFF_DV:DIFF_DV + 1, :]
            o = (o_all[:, :tq] - lam * o_all[:, tq:]).T
            y = o * lax.rsqrt(jnp.mean(o * o, axis=-1, keepdims=True) + EPS)
            y = y * gain_ref[self.j] * (1.0 - lam_init) * z_ref[0, rows, self.lanes].astype(F32)
            o_ref[0, rows, self.lanes] = y.astype(o_ref.dtype)

        def exact_path(self):
            def scores(t):
                qi, start, nk, _, _, _ = spans[t]
                s = self.masked_scores(t)
                s_ref[t % n_buf, 0:nk, :] = s
                mb_ref[t % n_buf] = jnp.max(s, axis=0, keepdims=True) + self.span_offset(qi, start)

            def update(t, m_old):
                qi, start, nk, _, first, last = spans[t]
                s = s_ref[t % n_buf, 0:nk, :]
                m_blk = mb_ref[t % n_buf]
                m_new = m_blk if first else jnp.maximum(m_old, m_blk)
                p = jnp.exp2(s - (m_new - self.span_offset(qi, start))).astype(BF16)
                pv = jnp.dot(self.values_and_ones(t), p, preferred_element_type=F32)
                if first:
                    acc_ref[...] = pv
                else:
                    acc_ref[...] = jnp.exp2(m_old - m_new) * acc_ref[...] + pv
                if last:
                    self.finish(qi, acc_ref[...])
                return m_new

            for t in range(min(ATTN_AHEAD, len(spans))):
                scores(t)
            m = None
            for t in range(len(spans)):
                if t + ATTN_AHEAD < len(spans):
                    scores(t + ATTN_AHEAD)
                m = update(t, m)

    heads = [Head(j) for j in range(n_heads)]

    acc = [None] * n_heads
    denom_min = [None] * n_heads
    s_next = [h.masked_scores(0) for h in heads]
    for t, (qi, start, _, _, first, last) in enumerate(spans):
        for j, h in enumerate(heads):
            s = s_next[j]
            if t + 1 < len(spans):
                s_next[j] = h.masked_scores(t + 1)
            p = jnp.exp2(s - (h.bounds[qi] - h.span_offset(qi, start))).astype(BF16)
            pv = jnp.dot(h.values_and_ones(t), p, preferred_element_type=F32)
            acc[j] = pv if first else acc[j] + pv
            if last:
                denom = acc[j][DIFF_DV:DIFF_DV + 1, :]
                denom_min[j] = denom if denom_min[j] is None else jnp.minimum(denom_min[j], denom)
                h.finish(qi, acc[j])

    for j, h in enumerate(heads):
        fast_ok = jnp.min(denom_min[j]) >= FAST_PATH_MIN_DENOM
        pl.when(jnp.logical_not(fast_ok))(h.exact_path)


def _diff_attn(dq, dk, dvt, dz, gain, lq1, lk1, lq2, lk2, lam_init):
    bsz, s, width = dq.shape
    tq = ATTN_TQ
    assert ATTN_TQ == ATTN_TK
    hw = 2 * DIFF_DH
    nh = ATTN_HEADS_PER_STEP
    rowspec = pl.BlockSpec((1, s, nh * hw), lambda b, h: (b, 0, h))
    vtspec = pl.BlockSpec((1, nh * DIFF_DV, s), lambda b, h: (b, h, 0))
    lspec = pl.BlockSpec((1, DIFF_DH), lambda b, h: (0, 0))
    kaug, qaug, slope_rows = _attn_bias_tables(tq, ATTN_TK)
    per_head = lambda rows: pl.BlockSpec((nh, rows, LANES), lambda b, h: (h, 0, 0))
    return pl.pallas_call(
        functools.partial(_attn_kernel, lam_init=lam_init),
        grid=(bsz, DIFF_HEADS // nh),
        in_specs=[rowspec, rowspec, vtspec, rowspec, per_head(1),
                  pl.BlockSpec(kaug.shape, lambda b, h: (0, 0)), per_head(tq), per_head(1),
                  lspec, lspec, lspec, lspec],
        out_specs=rowspec,
        out_shape=jax.ShapeDtypeStruct((bsz, s, width), BF16),
        scratch_shapes=[pltpu.VMEM((nh, 2, s, hw), BF16),
                        pltpu.VMEM((ATTN_AHEAD + 1, 2 * ATTN_TK, 2 * tq), F32),
                        pltpu.VMEM((ATTN_AHEAD + 1, 1, 2 * tq), F32),
                        pltpu.VMEM((DIFF_DV + SUM_ROWS, 2 * tq), F32)],
        compiler_params=pltpu.CompilerParams(
            dimension_semantics=("arbitrary", "arbitrary"), vmem_limit_bytes=VMEM_LIMIT),
        name="diff_attn",
    )(dq, dk, dvt, dz, gain.reshape(DIFF_HEADS, 1, DIFF_DV), kaug, qaug, slope_rows, lq1, lk1, lq2, lk2)


def _w_in_layout():
    hk = GLA_HEADS * GLA_DK
    names = ("gq", "gk", "gv", "gz", "gr", "dq", "dk", "dv", "dz")
    sizes = (hk, hk, SECTION, SECTION, GLA_RANK, SECTION, SECTION, SECTION, SECTION)
    src = dict(zip(names, np.concatenate([[0], np.cumsum(sizes)[:-1]]).tolist()))
    width = dict(zip(names, sizes))
    scale = {"gq": GLA_DK ** -0.5, "dq": DIFF_DH ** -0.5 * LOG2E}
    pieces, dst = [], 0
    for name in ("gq", "gk", "gv", "gz", "dq", "dk", "dz", "dv", "gr"):
        pieces.append((src[name], dst, width[name], scale.get(name)))
        dst += width[name]
    return pieces, dst + RANK_PAD - GLA_RANK


def _w_prep_kernel(wt_ref, o_ref):
    pieces, _ = _w_in_layout()
    for src, dst, width, scale in pieces:
        piece = wt_ref[0, src:src + width, :]
        if scale is not None:
            piece = piece * scale
        if width < LANES:
            piece = jnp.concatenate([piece, jnp.zeros((LANES - width, piece.shape[1]), piece.dtype)], axis=0)
        o_ref[:, dst:dst + piece.shape[0]] = piece.T.astype(o_ref.dtype)


def _prep_w_in(w_in, layer):
    _, d, n_src = w_in.shape
    _, n_dst = _w_in_layout()
    rows = 256
    return pl.pallas_call(
        _w_prep_kernel,
        grid=(d // rows,),
        in_specs=[pl.BlockSpec((1, n_src, rows), lambda i: (layer, 0, i))],
        out_specs=pl.BlockSpec((rows, n_dst), lambda i: (i, 0)),
        out_shape=jax.ShapeDtypeStruct((d, n_dst), BF16),
        compiler_params=pltpu.CompilerParams(dimension_semantics=("arbitrary",), vmem_limit_bytes=VMEM_LIMIT),
        name="w_in_prep",
    )(jnp.swapaxes(w_in, 1, 2))


def kernel(x, c, w_ada, b_ada, norm_gain, w_in, w_gla_gate_up, b_gla_gate, gla_out_gain,
           lambda_q1, lambda_k1, lambda_q2, lambda_k2, diff_out_gain, w_out, final_gain):
    bsz, s, d = x.shape
    depth = w_in.shape[0]
    assert depth == 1, "gla_out_proj applies the final rmsnorm, so exactly one layer is supported"
    for l in range(depth):
        mod = _adaln_mod(c, w_ada, b_ada, l)
        mod3 = mod.reshape(bsz, 1, 3 * d)
        w_in_r = _prep_w_in(w_in, l)
        wup_pad = jnp.pad(w_gla_gate_up[l], ((0, RANK_PAD - GLA_RANK), (0, 0))).astype(BF16)
        gqk, gv, gz, dq, dk, dz, dvt, la = _in_proj(
            x, mod3, norm_gain[l].reshape(1, d), w_in_r, wup_pad, b_gla_gate[l].reshape(1, -1))
        lam_init = float(0.8 - 0.6 * np.exp(-0.3 * l))
        o_diff = _diff_attn(dq, dk, dvt, dz, diff_out_gain[l],
                            lambda_q1[l].reshape(1, -1), lambda_k1[l].reshape(1, -1),
                            lambda_q2[l].reshape(1, -1), lambda_k2[l].reshape(1, -1), lam_init)
        x = _gla_out(gqk, gv, gz, la, gla_out_gain[l].reshape(1, -1), o_diff, x, mod3,
                     w_out[l].astype(BF16), final_gain.reshape(1, d))
    return x
```

```python
import functools
import math

import jax
import jax.numpy as jnp
import numpy as np
from jax import lax
from jax.experimental import pallas as pl
from jax.experimental.pallas import tpu as pltpu

F32 = jnp.float32
BF16 = jnp.bfloat16

EPS = 1e-6
LOG2E = math.log2(math.e)
GLA_HEADS = 4
GLA_DK = 64
GLA_DV = 128
GLA_RANK = 16
GLA_GATE_NORM = 16.0
GLA_CHUNK = 64
DIFF_HEADS = 4
DIFF_DH = 64
DIFF_DV = 128
LANES = 128
RANK_PAD = LANES
SECTION = 512
SUM_ROWS = 16

ROWS_IN_PROJ = 1024
ROWS_GLA = 1024
GLA_CUMSUM_ROWS = 256
ATTN_TQ = 256
ATTN_TK = 256
ATTN_HEADS_PER_STEP = 2
ATTN_AHEAD = 2
BF16_EXACT_INT = 256
BOUND_SLACK = 1.01
FAST_PATH_MIN_DENOM = 2.0 ** -90
VMEM_LIMIT = 48 * 1024 * 1024


def _silu(v):
    return v / (1.0 + jnp.exp(-v))


def _log_sigmoid(v):
    return jnp.minimum(v, 0.0) - jnp.log(1.0 + jnp.exp(-jnp.abs(v)))


def _adaln_kernel(c_ref, w_ref, b_ref, o_ref):
    sc = _silu(c_ref[...]).astype(BF16)
    o_ref[...] = jnp.dot(sc, w_ref[0].astype(BF16), preferred_element_type=F32) + b_ref[0]


def _adaln_mod(c, w_ada, b_ada, layer):
    bsz, d = c.shape
    n = w_ada.shape[2]
    tn = 1024
    return pl.pallas_call(
        _adaln_kernel,
        grid=(n // tn,),
        in_specs=[
            pl.BlockSpec((bsz, d), lambda j: (0, 0)),
            pl.BlockSpec((1, d, tn), lambda j: (layer, 0, j)),
            pl.BlockSpec((1, 1, tn), lambda j: (layer, 0, j)),
        ],
        out_specs=pl.BlockSpec((bsz, tn), lambda j: (0, j)),
        out_shape=jax.ShapeDtypeStruct((bsz, n), F32),
        compiler_params=pltpu.CompilerParams(dimension_semantics=("arbitrary",)),
        name="adaln_mod",
    )(c, w_ada, b_ada.reshape(b_ada.shape[0], 1, n))


def _in_proj_kernel(x_ref, shift_ref, scale_ref, gain_ref, w_ref, wup_ref, bg_ref,
                    gqk_ref, gv_ref, gz_ref, dq_ref, dk_ref, dz_ref, dvt_ref, la_ref,
                    wvt_ref):
    n_sec = 7
    dv_sec = 6

    @pl.when((pl.program_id(0) == 0) & (pl.program_id(1) == 0))
    def _():
        wvt_ref[0:SECTION, :] = w_ref[:, dv_sec * SECTION:(dv_sec + 1) * SECTION].T
        wvt_ref[SECTION:SECTION + GLA_RANK, :] = (
            w_ref[:, n_sec * SECTION:n_sec * SECTION + RANK_PAD].T[0:GLA_RANK, :])

    g = gain_ref[...] * (1.0 + scale_ref[0])
    tm = x_ref.shape[1]
    half = tm // 2

    def modulated(rows):
        x = x_ref[0, rows, :]
        rstd = lax.rsqrt(jnp.mean(x * x, axis=-1, keepdims=True) + EPS)
        return (x * rstd * g + shift_ref[0]).astype(BF16)

    h_halves = [modulated(slice(0, half)), modulated(slice(half, tm))]
    for r, h_half in enumerate(h_halves):
        gqk_ref[0, r * half:(r + 1) * half, :] = jnp.dot(
            h_half, w_ref[:, 0:SECTION], preferred_element_type=F32).astype(gqk_ref.dtype)
    h = jnp.concatenate(h_halves, axis=0)

    outs = (gqk_ref, gv_ref, gz_ref, dq_ref, dk_ref, dz_ref)

    def section(j):
        sec = jnp.dot(h, w_ref[:, j * SECTION:(j + 1) * SECTION], preferred_element_type=F32)
        if outs[j] is gz_ref or outs[j] is dz_ref:
            sec = _silu(sec)
        outs[j][0] = sec.astype(outs[j].dtype)

    section(2)
    section(5)

    vt_gr = lax.dot_general(wvt_ref[...], h, (((1,), (1,)), ((), ())), preferred_element_type=F32)
    dvt_ref[0] = vt_gr[0:SECTION].astype(dvt_ref.dtype)
    tm = vt_gr.shape[1]
    gr_t = jnp.concatenate([vt_gr[SECTION:SECTION + GLA_RANK].astype(BF16),
                            jnp.zeros((RANK_PAD - GLA_RANK, tm), BF16)], axis=0)
    logit = lax.dot_general(gr_t, wup_ref[...], (((0,), (0,)), ((), ())),
                            preferred_element_type=F32) + bg_ref[...]
    la_ref[0] = (_log_sigmoid(logit) * (1.0 / GLA_GATE_NORM)).astype(la_ref.dtype)

    for j in (1, 3, 4):
        section(j)


def _in_proj(x, mod3, norm_gain, w_in_r, wup_pad, b_gate):
    bsz, s, d = x.shape
    tm = ROWS_IN_PROJ
    ncol = w_in_r.shape[1]
    hk = GLA_HEADS * GLA_DK
    act = lambda width: pl.BlockSpec((1, tm, width), lambda b, t: (b, t, 0))
    sec = jax.ShapeDtypeStruct((bsz, s, SECTION), BF16)
    out_shapes = (
        sec,
        sec,
        sec,
        sec,
        sec,
        sec,
        jax.ShapeDtypeStruct((bsz, SECTION, s), BF16),
        jax.ShapeDtypeStruct((bsz, s, hk), BF16),
    )
    const = lambda shape: pl.BlockSpec(shape, lambda b, t: (0, 0))
    return pl.pallas_call(
        _in_proj_kernel,
        grid=(bsz, s // tm),
        in_specs=[
            act(d),
            pl.BlockSpec((1, 1, d), lambda b, t: (b, 0, 0)),
            pl.BlockSpec((1, 1, d), lambda b, t: (b, 0, 1)),
            const((1, d)), const((d, ncol)), const((RANK_PAD, hk)), const((1, hk)),
        ],
        out_specs=(act(SECTION), act(SECTION), act(SECTION), act(SECTION), act(SECTION), act(SECTION),
                   pl.BlockSpec((1, SECTION, tm), lambda b, t: (b, 0, t)), act(hk)),
        out_shape=out_shapes,
        scratch_shapes=[pltpu.VMEM((SECTION + GLA_RANK, d), BF16)],
        compiler_params=pltpu.CompilerParams(
            dimension_semantics=("arbitrary", "arbitrary"), vmem_limit_bytes=VMEM_LIMIT),
        name="in_proj",
    )(x, mod3, mod3, norm_gain, w_in_r, wup_pad, b_gate)


def _gla_out_kernel(qk_ref, v_ref, za_ref, la_ref, gain_ref, tril_ref, chunk_ind_ref, ones_bd_ref,
                    od_ref, x_ref, gate_ref, w_ref, fg_ref, o_ref, state_ref):
    tg = qk_ref.shape[1]
    half = od_ref.shape[2]
    hk = GLA_HEADS * GLA_DK
    c_sz = GLA_CHUNK
    n_ch = tg // c_sz

    @pl.when(pl.program_id(1) == 0)
    def _():
        state_ref[...] = jnp.zeros_like(state_ref)

    la = la_ref[0]
    grp = tril_ref.shape[0]
    b_all = jnp.concatenate(
        [jnp.dot(tril_ref[...], la[g * grp:(g + 1) * grp], preferred_element_type=F32)
         for g in range(tg // grp)], axis=0)
    b_last = jnp.concatenate(
        [jnp.broadcast_to(b_all[(ch + 1) * c_sz - 1:(ch + 1) * c_sz], (c_sz, hk)) for ch in range(n_ch)], axis=0)
    q = qk_ref[0, :, 0:hk]
    k = qk_ref[0, :, hk:2 * hk]
    q_in = q * jnp.exp(b_all).astype(BF16)
    k_in = k * jnp.exp(-b_all).astype(BF16)
    k_st = k * jnp.exp(b_last - b_all).astype(BF16)

    lane_head = lax.broadcasted_iota(jnp.int32, (1, hk), 1) // GLA_DK
    ri = lax.broadcasted_iota(jnp.int32, (GLA_HEADS * c_sz, c_sz), 0) % c_sz
    ci = lax.broadcasted_iota(jnp.int32, (GLA_HEADS * c_sz, c_sz), 1)
    causal = ci <= ri
    dec_all = jnp.concatenate(
        [jnp.exp(lax.dot_general(la[g * grp:(g + 1) * grp], chunk_ind_ref[...], (((0,), (0,)), ((), ())),
                                 preferred_element_type=F32)) for g in range(tg // grp)], axis=1)

    mix_diff = jnp.dot(od_ref[0], w_ref[half:2 * half, :], preferred_element_type=F32)

    chunk_rows = [slice(ch * c_sz, (ch + 1) * c_sz) for ch in range(n_ch)]
    head_rows = [slice(hh * c_sz, (hh + 1) * c_sz) for hh in range(GLA_HEADS)]
    head_cols = [slice(hh * GLA_DV, (hh + 1) * GLA_DV) for hh in range(GLA_HEADS)]
    vs = [v_ref[0, rows, :] for rows in chunk_rows]

    us = []
    for ch, rows in enumerate(chunk_rows):
        kst_t = k_st[rows].T
        us.append(jnp.concatenate(
            [jnp.dot(kst_t[hr], vs[ch][:, vc], preferred_element_type=F32)
             for hr, vc in zip(head_rows, head_cols)], axis=0))
    state = state_ref[...]
    states = []
    for ch in range(n_ch):
        states.append(state.astype(BF16))
        state = dec_all[:, ch * GLA_DV:(ch + 1) * GLA_DV] * state + us[ch]
    state_ref[...] = state
    boths = []
    for ch, rows in enumerate(chunk_rows):
        q_c = q_in[rows]
        qm = jnp.concatenate(
            [jnp.where(lane_head == hh, q_c, jnp.zeros_like(q_c)) for hh in range(GLA_HEADS)], axis=0)
        rhs = jnp.concatenate([states[ch], k_in[rows].T], axis=1)
        boths.append(jnp.dot(qm, rhs, preferred_element_type=F32))
    o_rows = []
    for ch in range(n_ch):
        inter = boths[ch][:, 0:GLA_DV]
        p = jnp.where(causal, boths[ch][:, GLA_DV:GLA_DV + c_sz], 0.0).astype(BF16)
        o_rows.append(jnp.concatenate(
            [jnp.dot(p[hr], vs[ch][:, vc], preferred_element_type=F32) + inter[hr]
             for hr, vc in zip(head_rows, head_cols)], axis=1))

    o = jnp.concatenate(o_rows, axis=0)
    ms = jnp.dot((o * o).astype(BF16), ones_bd_ref[...], preferred_element_type=F32) * (1.0 / GLA_DV)
    o_gla = (o * lax.rsqrt(ms + EPS) * gain_ref[...] * za_ref[0].astype(F32)).astype(BF16)

    mixw = jnp.dot(o_gla, w_ref[0:half, :], preferred_element_type=F32) + mix_diff
    xn = x_ref[0] + gate_ref[0] * mixw
    o_ref[0] = xn * lax.rsqrt(jnp.mean(xn * xn, axis=-1, keepdims=True) + EPS) * fg_ref[...]


def _gla_out(gqk, gv, gza, la, gla_out_gain, o_diff, x, mod3, w_out_bf, final_gain):
    bsz, s, d = x.shape
    tg = ROWS_GLA
    hk = GLA_HEADS * GLA_DK
    width = GLA_HEADS * GLA_DV
    grp = GLA_CUMSUM_ROWS
    idx = np.arange(grp)
    tril = ((idx[:, None] // GLA_CHUNK == idx[None, :] // GLA_CHUNK) & (idx[None, :] <= idx[:, None]))
    chunk_ind = idx[:, None] // GLA_CHUNK == np.arange(grp // GLA_CHUNK * GLA_DV)[None, :] // GLA_DV
    col = np.arange(width)
    ones_bd = col[:, None] // GLA_DV == col[None, :] // GLA_DV
    act = lambda w: pl.BlockSpec((1, tg, w), lambda b, t: (b, t, 0))
    const = lambda shape: pl.BlockSpec(shape, lambda b, t: (0, 0))
    return pl.pallas_call(
        _gla_out_kernel,
        grid=(bsz, s // tg),
        in_specs=[act(2 * hk), act(width), act(width), act(hk),
                  const((1, width)), const(tril.shape), const(chunk_ind.shape), const((width, width)),
                  act(o_diff.shape[2]), act(d),
                  pl.BlockSpec((1, 1, d), lambda b, t: (b, 0, 2)),
                  const(w_out_bf.shape), const((1, d))],
        out_specs=act(d),
        out_shape=jax.ShapeDtypeStruct((bsz, s, d), F32),
        scratch_shapes=[pltpu.VMEM((hk, GLA_DV), F32)],
        compiler_params=pltpu.CompilerParams(
            dimension_semantics=("arbitrary", "arbitrary"), vmem_limit_bytes=VMEM_LIMIT),
        name="gla_out_proj",
    )(gqk, gv, gza, la, gla_out_gain, jnp.asarray(tril, BF16), jnp.asarray(chunk_ind, BF16),
      jnp.asarray(ones_bd, BF16), o_diff, x, mod3, w_out_bf, final_gain)


def _diff_lambda(lq1_ref, lk1_ref, lq2_ref, lk2_ref, lam_init):
    a = jnp.sum(lq1_ref[...] * lk1_ref[...], axis=-1, keepdims=True)
    b = jnp.sum(lq2_ref[...] * lk2_ref[...], axis=-1, keepdims=True)
    return jnp.exp(a) - jnp.exp(b) + lam_init


def _attn_spans(nq, tk):
    spans = []
    for qi in range(nq):
        tile = [(qi, 2 * tk * j, 2 * tk, None) for j in range(qi // 2)]
        if qi % 2 == 1:
            tile.append((qi, (qi - 1) * tk, 2 * tk, tk))
        else:
            tile.append((qi, qi * tk, tk, 0))
        for i, sp in enumerate(tile):
            spans.append(sp + (i == 0, i == len(tile) - 1))
    return spans


def _attn_bias_tables(tq, tk):
    def split(x, terms):
        parts = []
        for _ in range(terms):
            hi = x.astype(jnp.bfloat16)
            parts.append(hi)
            x = x - hi.astype(np.float32)
        return parts

    key = np.arange(2 * tk, dtype=np.float32)
    kaug = np.zeros((2 * tk, LANES), np.float32)
    kaug[:, 0] = kaug[:, 1] = key % BF16_EXACT_INT
    kaug[:, 2] = kaug[:, 3] = key - key % BF16_EXACT_INT
    kaug[:, 4:7] = 1.0
    slopes = (2.0 ** (-8.0 * np.arange(1, DIFF_HEADS + 1) / DIFF_HEADS) * LOG2E).astype(np.float32)
    qry = np.arange(tq, dtype=np.float32)
    qaug = np.zeros((DIFF_HEADS, tq, LANES), jnp.bfloat16)
    for hh in range(DIFF_HEADS):
        s_hi, s_lo = split(slopes[hh:hh + 1], 2)
        qaug[hh, :, 0] = qaug[hh, :, 2] = s_hi
        qaug[hh, :, 1] = qaug[hh, :, 3] = s_lo
        for lane, part in zip((4, 5, 6), split(-slopes[hh] * qry, 3)):
            qaug[hh, :, lane] = part
    slope_rows = np.broadcast_to(slopes[:, None, None], (DIFF_HEADS, 1, LANES))
    return jnp.asarray(kaug, BF16), jnp.asarray(qaug), jnp.asarray(slope_rows, F32)


def _attn_kernel(q_ref, k_ref, vt_ref, z_ref, gain_ref, kaug_ref, qaug_ref, slope_ref,
                 lq1_ref, lk1_ref, lq2_ref, lk2_ref,
                 o_ref, km_ref, s_ref, mb_ref, acc_ref, *, lam_init):
    tq = ATTN_TQ
    tk = ATTN_TK
    hw = 2 * DIFF_DH
    nq = q_ref.shape[1] // tq
    n_heads = q_ref.shape[2] // hw
    n_buf = ATTN_AHEAD + 1

    first_half = lax.broadcasted_iota(jnp.int32, (1, hw), 1) < DIFF_DH
    map_cols = lax.broadcasted_iota(jnp.int32, (1, 2 * tq), 1) < tq
    r_i = lax.broadcasted_iota(jnp.int32, (hw, LANES), 0)
    c_i = lax.broadcasted_iota(jnp.int32, (hw, LANES), 1)
    half_sel = jnp.where(r_i // DIFF_DH == c_i, 1.0, 0.0).astype(BF16)
    half_rows = (lax.broadcasted_iota(jnp.int32, (8, hw), 1) // DIFF_DH
                 == lax.broadcasted_iota(jnp.int32, (8, hw), 0)).astype(BF16)

    def rel_pos(nk):
        key = lax.broadcasted_iota(jnp.int32, (nk, 2 * tq), 0)
        qry = lax.broadcasted_iota(jnp.int32, (nk, 2 * tq), 1) % tq
        return qry - key

    lam = _diff_lambda(lq1_ref, lk1_ref, lq2_ref, lk2_ref, lam_init)
    spans = _attn_spans(nq, tk)

    class Head:
        def __init__(self, j):
            self.j = j
            self.lanes = slice(j * hw, (j + 1) * hw)
            self.slope = slope_ref[j][:, 0:1]
            k_all = k_ref[0, :, self.lanes]
            zk = jnp.zeros_like(k_all)
            km_ref[j, 0] = jnp.where(first_half, k_all, zk)
            km_ref[j, 1] = jnp.where(first_half, zk, k_all)
            k_norm2 = jnp.max(jnp.dot(k_all * k_all, half_sel, preferred_element_type=F32), axis=0, keepdims=True)
            k_max = jnp.sqrt(k_norm2)
            k_max_row = jnp.where(map_cols, k_max[:, 0:1], k_max[:, 1:2])
            self.bounds = []
            for qi in range(nq):
                q = q_ref[0, qi * tq:(qi + 1) * tq, self.lanes]
                q_half2 = lax.dot_general(half_rows, q * q, (((1,), (1,)), ((), ())), preferred_element_type=F32)
                q_norm2 = jnp.concatenate([q_half2[0:1], q_half2[1:2]], axis=1)
                self.bounds.append(jnp.sqrt(q_norm2) * k_max_row * BOUND_SLACK)

        def span_offset(self, qi, start):
            return -self.slope * float(qi * tq - start)

        def masked_scores(self, t):
            qi, start, nk, diag_off, _, _ = spans[t]
            qrhs = jnp.concatenate([q_ref[0, qi * tq:(qi + 1) * tq, self.lanes], qaug_ref[self.j]], axis=1)
            s_maps = []
            for m in range(2):
                kblk = jnp.concatenate([km_ref[self.j, m, start:start + nk, :], kaug_ref[0:nk, :]], axis=1)
                s_maps.append(lax.dot_general(kblk, qrhs, (((1,), (1,)), ((), ())), preferred_element_type=F32))
            s = jnp.concatenate(s_maps, axis=1)
            if diag_off is not None:
                keep = rel_pos(tk) + (diag_off - (nk - tk)) >= 0
                tail = jnp.where(keep, s[nk - tk:nk], -jnp.inf)
                s = tail if nk == tk else jnp.concatenate([s[0:nk - tk], tail], axis=0)
            return s

        def values_and_ones(self, t):
            _, start, nk, _, _, _ = spans[t]
            vt = vt_ref[0, self.j * DIFF_DV:(self.j + 1) * DIFF_DV, start:start + nk]
            return jnp.concatenate([vt, jnp.ones((SUM_ROWS, nk), BF16)], axis=0)

        def finish(self, qi, acc):
            rows = slice(qi * tq, (qi + 1) * tq)
            o_all = acc[0:DIFF_DV, :] / acc[DIFF_DV:DIFF_DV + 1, :]
            o = (o_all[:, :tq] - lam * o_all[:, tq:]).T
            y = o * lax.rsqrt(jnp.mean(o * o, axis=-1, keepdims=True) + EPS)
            y = y * gain_ref[self.j] * (1.0 - lam_init) * z_ref[0, rows, self.lanes].astype(F32)
            o_ref[0, rows, self.lanes] = y.astype(o_ref.dtype)

        def exact_path(self):
            def scores(t):
                qi, start, nk, _, _, _ = spans[t]
                s = self.masked_scores(t)
                s_ref[t % n_buf, 0:nk, :] = s
                mb_ref[t % n_buf] = jnp.max(s, axis=0, keepdims=True) + self.span_offset(qi, start)

            def update(t, m_old):
                qi, start, nk, _, first, last = spans[t]
                s = s_ref[t % n_buf, 0:nk, :]
                m_blk = mb_ref[t % n_buf]
                m_new = m_blk if first else jnp.maximum(m_old, m_blk)
                p = jnp.exp2(s - (m_new - self.span_offset(qi, start))).astype(BF16)
                pv = jnp.dot(self.values_and_ones(t), p, preferred_element_type=F32)
                if first:
                    acc_ref[...] = pv
                else:
                    acc_ref[...] = jnp.exp2(m_old - m_new) * acc_ref[...] + pv
                if last:
                    self.finish(qi, acc_ref[...])
                return m_new

            for t in range(min(ATTN_AHEAD, len(spans))):
                scores(t)
            m = None
            for t in range(len(spans)):
                if t + ATTN_AHEAD < len(spans):
                    scores(t + ATTN_AHEAD)
                m = update(t, m)

    heads = [Head(j) for j in range(n_heads)]

    acc = [None] * n_heads
    denom_min = [None] * n_heads
    s_next = [h.masked_scores(0) for h in heads]
    for t, (qi, start, _, _, first, last) in enumerate(spans):
        for j, h in enumerate(heads):
            s = s_next[j]
            if t + 1 < len(spans):
                s_next[j] = h.masked_scores(t + 1)
            p = jnp.exp2(s - (h.bounds[qi] - h.span_offset(qi, start))).astype(BF16)
            pv = jnp.dot(h.values_and_ones(t), p, preferred_element_type=F32)
            acc[j] = pv if first else acc[j] + pv
            if last:
                denom = acc[j][DIFF_DV:DIFF_DV + 1, :]
                denom_min[j] = denom if denom_min[j] is None else jnp.minimum(denom_min[j], denom)
                h.finish(qi, acc[j])

    for j, h in enumerate(heads):
        fast_ok = jnp.min(denom_min[j]) >= FAST_PATH_MIN_DENOM
        pl.when(jnp.logical_not(fast_ok))(h.exact_path)


def _diff_attn(dq, dk, dvt, dz, gain, lq1, lk1, lq2, lk2, lam_init):
    bsz, s, width = dq.shape
    tq = ATTN_TQ
    assert ATTN_TQ == ATTN_TK
    hw = 2 * DIFF_DH
    nh = ATTN_HEADS_PER_STEP
    rowspec = pl.BlockSpec((1, s, nh * hw), lambda b, h: (b, 0, h))
    vtspec = pl.BlockSpec((1, nh * DIFF_DV, s), lambda b, h: (b, h, 0))
    lspec = pl.BlockSpec((1, DIFF_DH), lambda b, h: (0, 0))
    kaug, qaug, slope_rows = _attn_bias_tables(tq, ATTN_TK)
    per_head = lambda rows: pl.BlockSpec((nh, rows, LANES), lambda b, h: (h, 0, 0))
    return pl.pallas_call(
        functools.partial(_attn_kernel, lam_init=lam_init),
        grid=(bsz, DIFF_HEADS // nh),
        in_specs=[rowspec, rowspec, vtspec, rowspec, per_head(1),
                  pl.BlockSpec(kaug.shape, lambda b, h: (0, 0)), per_head(tq), per_head(1),
                  lspec, lspec, lspec, lspec],
        out_specs=rowspec,
        out_shape=jax.ShapeDtypeStruct((bsz, s, width), BF16),
        scratch_shapes=[pltpu.VMEM((nh, 2, s, hw), BF16),
                        pltpu.VMEM((ATTN_AHEAD + 1, 2 * ATTN_TK, 2 * tq), F32),
                        pltpu.VMEM((ATTN_AHEAD + 1, 1, 2 * tq), F32),
                        pltpu.VMEM((DIFF_DV + SUM_ROWS, 2 * tq), F32)],
        compiler_params=pltpu.CompilerParams(
            dimension_semantics=("arbitrary", "arbitrary"), vmem_limit_bytes=VMEM_LIMIT),
        name="diff_attn",
    )(dq, dk, dvt, dz, gain.reshape(DIFF_HEADS, 1, DIFF_DV), kaug, qaug, slope_rows, lq1, lk1, lq2, lk2)


def _w_in_layout():
    hk = GLA_HEADS * GLA_DK
    names = ("gq", "gk", "gv", "gz", "gr", "dq", "dk", "dv", "dz")
    sizes = (hk, hk, SECTION, SECTION, GLA_RANK, SECTION, SECTION, SECTION, SECTION)
    src = dict(zip(names, np.concatenate([[0], np.cumsum(sizes)[:-1]]).tolist()))
    width = dict(zip(names, sizes))
    scale = {"gq": GLA_DK ** -0.5, "dq": DIFF_DH ** -0.5 * LOG2E}
    pieces, dst = [], 0
    for name in ("gq", "gk", "gv", "gz", "dq", "dk", "dz", "dv", "gr"):
        pieces.append((src[name], dst, width[name], scale.get(name)))
        dst += width[name]
    return pieces, dst + RANK_PAD - GLA_RANK


def _w_prep_kernel(wt_ref, o_ref):
    pieces, _ = _w_in_layout()
    for src, dst, width, scale in pieces:
        piece = wt_ref[0, src:src + width, :]
        if scale is not None:
            piece = piece * scale
        if width < LANES:
            piece = jnp.concatenate([piece, jnp.zeros((LANES - width, piece.shape[1]), piece.dtype)], axis=0)
        o_ref[:, dst:dst + piece.shape[0]] = piece.T.astype(o_ref.dtype)


def _prep_w_in(w_in, layer):
    _, d, n_src = w_in.shape
    _, n_dst = _w_in_layout()
    rows = 256
    return pl.pallas_call(
        _w_prep_kernel,
        grid=(d // rows,),
        in_specs=[pl.BlockSpec((1, n_src, rows), lambda i: (layer, 0, i))],
        out_specs=pl.BlockSpec((rows, n_dst), lambda i: (i, 0)),
        out_shape=jax.ShapeDtypeStruct((d, n_dst), BF16),
        compiler_params=pltpu.CompilerParams(dimension_semantics=("arbitrary",), vmem_limit_bytes=VMEM_LIMIT),
        name="w_in_prep",
    )(jnp.swapaxes(w_in, 1, 2))


def kernel(x, c, w_ada, b_ada, norm_gain, w_in, w_gla_gate_up, b_gla_gate, gla_out_gain,
           lambda_q1, lambda_k1, lambda_q2, lambda_k2, diff_out_gain, w_out, final_gain):
    bsz, s, d = x.shape
    depth = w_in.shape[0]
    assert depth == 1, "gla_out_proj applies the final rmsnorm, so exactly one layer is supported"
    for l in range(depth):
        mod = _adaln_mod(c, w_ada, b_ada, l)
        mod3 = mod.reshape(bsz, 1, 3 * d)
        w_in_r = _prep_w_in(w_in, l)
        wup_pad = jnp.pad(w_gla_gate_up[l], ((0, RANK_PAD - GLA_RANK), (0, 0))).astype(BF16)
        gqk, gv, gz, dq, dk, dz, dvt, la = _in_proj(
            x, mod3, norm_gain[l].reshape(1, d), w_in_r, wup_pad, b_gla_gate[l].reshape(1, -1))
        lam_init = float(0.8 - 0.6 * np.exp(-0.3 * l))
        o_diff = _diff_attn(dq, dk, dvt, dz, diff_out_gain[l],
                            lambda_q1[l].reshape(1, -1), lambda_k1[l].reshape(1, -1),
                            lambda_q2[l].reshape(1, -1), lambda_k2[l].reshape(1, -1), lam_init)
        x = _gla_out(gqk, gv, gz, la, gla_out_gain[l].reshape(1, -1), o_diff, x, mod3,
                     w_out[l].astype(BF16), final_gain.reshape(1, d))
    return x
```

```python
import functools
import math

import jax
import jax.numpy as jnp
import numpy as np
from jax import lax
from jax.experimental import pallas as pl
from jax.experimental.pallas import tpu as pltpu

F32 = jnp.float32
BF16 = jnp.bfloat16

EPS = 1e-6
LOG2E = math.log2(math.e)
GLA_HEADS = 4
GLA_DK = 64
GLA_DV = 128
GLA_RANK = 16
GLA_GATE_NORM = 16.0
GLA_CHUNK = 64
DIFF_HEADS = 4
DIFF_DH = 64
DIFF_DV = 128
LANES = 128
RANK_PAD = LANES
SECTION = 512
SUM_ROWS = 16

ROWS_IN_PROJ = 1024
IN_PROJ_LEAD_PARTS = 4
ROWS_GLA = 1024
GLA_CUMSUM_ROWS = 256
GLA_OUT_PARTS = 2
ATTN_TQ = 256
ATTN_TK = 256
ATTN_HEADS_PER_STEP = 2
ATTN_AHEAD = 2
BF16_EXACT_INT = 256
BOUND_SLACK = 1.01
FAST_PATH_MIN_DENOM = 2.0 ** -90
VMEM_LIMIT = 48 * 1024 * 1024


def _silu(v):
    return v / (1.0 + jnp.exp(-v))


def _log_sigmoid(v):
    return jnp.minimum(v, 0.0) - jnp.log(1.0 + jnp.exp(-jnp.abs(v)))


def _adaln_kernel(c_ref, w_ref, b_ref, o_ref):
    sc = _silu(c_ref[...]).astype(BF16)
    o_ref[...] = jnp.dot(sc, w_ref[0].astype(BF16), preferred_element_type=F32) + b_ref[0]


def _adaln_mod(c, w_ada, b_ada, layer):
    bsz, d = c.shape
    n = w_ada.shape[2]
    tn = 512
    return pl.pallas_call(
        _adaln_kernel,
        grid=(n // tn,),
        in_specs=[
            pl.BlockSpec((bsz, d), lambda j: (0, 0)),
            pl.BlockSpec((1, d, tn), lambda j: (layer, 0, j)),
            pl.BlockSpec((1, 1, tn), lambda j: (layer, 0, j)),
        ],
        out_specs=pl.BlockSpec((bsz, tn), lambda j: (0, j)),
        out_shape=jax.ShapeDtypeStruct((bsz, n), F32),
        compiler_params=pltpu.CompilerParams(dimension_semantics=("arbitrary",)),
        name="adaln_mod",
    )(c, w_ada, b_ada.reshape(b_ada.shape[0], 1, n))


def _in_proj_kernel(x_ref, shift_ref, scale_ref, gain_ref, w_ref, wup_ref, bg_ref,
                    gqk_ref, gv_ref, gz_ref, dq_ref, dk_ref, dz_ref, dvt_ref, la_ref,
                    wvt_ref):
    n_sec = 7
    dv_sec = 6

    @pl.when((pl.program_id(0) == 0) & (pl.program_id(1) == 0))
    def _():
        wvt_ref[0:SECTION, :] = w_ref[:, dv_sec * SECTION:(dv_sec + 1) * SECTION].T
        wvt_ref[SECTION:SECTION + GLA_RANK, :] = (
            w_ref[:, n_sec * SECTION:n_sec * SECTION + RANK_PAD].T[0:GLA_RANK, :])

    g = gain_ref[...] * (1.0 + scale_ref[0])
    tm = x_ref.shape[1]
    part = tm // IN_PROJ_LEAD_PARTS

    def modulated(rows):
        x = x_ref[0, rows, :]
        rstd = lax.rsqrt(jnp.mean(x * x, axis=-1, keepdims=True) + EPS)
        return (x * rstd * g + shift_ref[0]).astype(BF16)

    h_parts = []
    for r in range(IN_PROJ_LEAD_PARTS):
        rows = slice(r * part, (r + 1) * part)
        h_parts.append(modulated(rows))
        gqk_ref[0, rows, :] = jnp.dot(
            h_parts[r], w_ref[:, 0:SECTION], preferred_element_type=F32).astype(gqk_ref.dtype)
    h = jnp.concatenate(h_parts, axis=0)

    outs = (gqk_ref, gv_ref, gz_ref, dq_ref, dk_ref, dz_ref)

    def section(j):
        sec = jnp.dot(h, w_ref[:, j * SECTION:(j + 1) * SECTION], preferred_element_type=F32)
        if outs[j] is gz_ref or outs[j] is dz_ref:
            sec = _silu(sec)
        outs[j][0] = sec.astype(outs[j].dtype)

    section(2)
    section(5)

    vt_gr = lax.dot_general(wvt_ref[...], h, (((1,), (1,)), ((), ())), preferred_element_type=F32)
    dvt_ref[0] = vt_gr[0:SECTION].astype(dvt_ref.dtype)
    tm = vt_gr.shape[1]
    gr_t = jnp.concatenate([vt_gr[SECTION:SECTION + GLA_RANK].astype(BF16),
                            jnp.zeros((RANK_PAD - GLA_RANK, tm), BF16)], axis=0)
    logit = lax.dot_general(gr_t, wup_ref[...], (((0,), (0,)), ((), ())),
                            preferred_element_type=F32) + bg_ref[...]
    la_ref[0] = (_log_sigmoid(logit) * (1.0 / GLA_GATE_NORM)).astype(la_ref.dtype)

    for j in (1, 3, 4):
        section(j)


def _in_proj(x, mod3, norm_gain, w_in_r, wup_pad, b_gate):
    bsz, s, d = x.shape
    tm = ROWS_IN_PROJ
    ncol = w_in_r.shape[1]
    hk = GLA_HEADS * GLA_DK
    act = lambda width: pl.BlockSpec((1, tm, width), lambda b, t: (b, t, 0))
    sec = jax.ShapeDtypeStruct((bsz, s, SECTION), BF16)
    out_shapes = (
        sec,
        sec,
        sec,
        sec,
        sec,
        sec,
        jax.ShapeDtypeStruct((bsz, SECTION, s), BF16),
        jax.ShapeDtypeStruct((bsz, s, hk), BF16),
    )
    const = lambda shape: pl.BlockSpec(shape, lambda b, t: (0, 0))
    return pl.pallas_call(
        _in_proj_kernel,
        grid=(bsz, s // tm),
        in_specs=[
            act(d),
            pl.BlockSpec((1, 1, d), lambda b, t: (b, 0, 0)),
            pl.BlockSpec((1, 1, d), lambda b, t: (b, 0, 1)),
            const((1, d)), const((d, ncol)), const((RANK_PAD, hk)), const((1, hk)),
        ],
        out_specs=(act(SECTION), act(SECTION), act(SECTION), act(SECTION), act(SECTION), act(SECTION),
                   pl.BlockSpec((1, SECTION, tm), lambda b, t: (b, 0, t)), act(hk)),
        out_shape=out_shapes,
        scratch_shapes=[pltpu.VMEM((SECTION + GLA_RANK, d), BF16)],
        compiler_params=pltpu.CompilerParams(
            dimension_semantics=("arbitrary", "arbitrary"), vmem_limit_bytes=VMEM_LIMIT),
        name="in_proj",
    )(x, mod3, mod3, norm_gain, w_in_r, wup_pad, b_gate)


def _gla_out_kernel(qk_ref, v_ref, za_ref, la_ref, gain_ref, tril_ref, chunk_ind_ref, ones_bd_ref,
                    od_ref, x_ref, gate_ref, w_ref, fg_ref, o_ref, state_ref):
    tg = qk_ref.shape[1]
    half = od_ref.shape[2]
    hk = GLA_HEADS * GLA_DK
    c_sz = GLA_CHUNK
    n_ch = tg // c_sz

    @pl.when(pl.program_id(1) == 0)
    def _():
        state_ref[...] = jnp.zeros_like(state_ref)

    la = la_ref[0]
    grp = tril_ref.shape[0]
    b_all = jnp.concatenate(
        [jnp.dot(tril_ref[...], la[g * grp:(g + 1) * grp], preferred_element_type=F32)
         for g in range(tg // grp)], axis=0)
    b_last = jnp.concatenate(
        [jnp.broadcast_to(b_all[(ch + 1) * c_sz - 1:(ch + 1) * c_sz], (c_sz, hk)) for ch in range(n_ch)], axis=0)
    q = qk_ref[0, :, 0:hk]
    k = qk_ref[0, :, hk:2 * hk]
    q_in = q * jnp.exp(b_all).astype(BF16)
    k_in = k * jnp.exp(-b_all).astype(BF16)
    k_st = k * jnp.exp(b_last - b_all).astype(BF16)

    lane_head = lax.broadcasted_iota(jnp.int32, (1, hk), 1) // GLA_DK
    ri = lax.broadcasted_iota(jnp.int32, (GLA_HEADS * c_sz, c_sz), 0) % c_sz
    ci = lax.broadcasted_iota(jnp.int32, (GLA_HEADS * c_sz, c_sz), 1)
    causal = ci <= ri
    dec_all = jnp.concatenate(
        [jnp.exp(lax.dot_general(la[g * grp:(g + 1) * grp], chunk_ind_ref[...], (((0,), (0,)), ((), ())),
                                 preferred_element_type=F32)) for g in range(tg // grp)], axis=1)

    mix_diff = jnp.dot(od_ref[0], w_ref[half:2 * half, :], preferred_element_type=F32)

    chunk_rows = [slice(ch * c_sz, (ch + 1) * c_sz) for ch in range(n_ch)]
    head_rows = [slice(hh * c_sz, (hh + 1) * c_sz) for hh in range(GLA_HEADS)]
    head_cols = [slice(hh * GLA_DV, (hh + 1) * GLA_DV) for hh in range(GLA_HEADS)]
    vs = [v_ref[0, rows, :] for rows in chunk_rows]

    us = []
    for ch, rows in enumerate(chunk_rows):
        kst_t = k_st[rows].T
        us.append(jnp.concatenate(
            [jnp.dot(kst_t[hr], vs[ch][:, vc], preferred_element_type=F32)
             for hr, vc in zip(head_rows, head_cols)], axis=0))
    state = state_ref[...]
    states = []
    for ch in range(n_ch):
        states.append(state.astype(BF16))
        state = dec_all[:, ch * GLA_DV:(ch + 1) * GLA_DV] * state + us[ch]
    state_ref[...] = state
    boths = []
    for ch, rows in enumerate(chunk_rows):
        q_c = q_in[rows]
        qm = jnp.concatenate(
            [jnp.where(lane_head == hh, q_c, jnp.zeros_like(q_c)) for hh in range(GLA_HEADS)], axis=0)
        rhs = jnp.concatenate([states[ch], k_in[rows].T], axis=1)
        boths.append(jnp.dot(qm, rhs, preferred_element_type=F32))
    o_rows = []
    for ch in range(n_ch):
        inter = boths[ch][:, 0:GLA_DV]
        p = jnp.where(causal, boths[ch][:, GLA_DV:GLA_DV + c_sz], 0.0).astype(BF16)
        o_rows.append(jnp.concatenate(
            [jnp.dot(p[hr], vs[ch][:, vc], preferred_element_type=F32) + inter[hr]
             for hr, vc in zip(head_rows, head_cols)], axis=1))

    o = jnp.concatenate(o_rows, axis=0)
    ms = jnp.dot((o * o).astype(BF16), ones_bd_ref[...], preferred_element_type=F32) * (1.0 / GLA_DV)
    o_gla = (o * lax.rsqrt(ms + EPS) * gain_ref[...] * za_ref[0].astype(F32)).astype(BF16)

    part = tg // GLA_OUT_PARTS
    for r in range(GLA_OUT_PARTS):
        rows = slice(r * part, (r + 1) * part)
        mixw = jnp.dot(o_gla[rows], w_ref[0:half, :], preferred_element_type=F32) + mix_diff[rows]
        xn = x_ref[0, rows, :] + gate_ref[0] * mixw
        o_ref[0, rows, :] = xn * lax.rsqrt(jnp.mean(xn * xn, axis=-1, keepdims=True) + EPS) * fg_ref[...]


def _gla_out(gqk, gv, gza, la, gla_out_gain, o_diff, x, mod3, w_out_bf, final_gain):
    bsz, s, d = x.shape
    tg = ROWS_GLA
    hk = GLA_HEADS * GLA_DK
    width = GLA_HEADS * GLA_DV
    grp = GLA_CUMSUM_ROWS
    idx = np.arange(grp)
    tril = ((idx[:, None] // GLA_CHUNK == idx[None, :] // GLA_CHUNK) & (idx[None, :] <= idx[:, None]))
    chunk_ind = idx[:, None] // GLA_CHUNK == np.arange(grp // GLA_CHUNK * GLA_DV)[None, :] // GLA_DV
    col = np.arange(width)
    ones_bd = col[:, None] // GLA_DV == col[None, :] // GLA_DV
    act = lambda w: pl.BlockSpec((1, tg, w), lambda b, t: (b, t, 0))
    const = lambda shape: pl.BlockSpec(shape, lambda b, t: (0, 0))
    return pl.pallas_call(
        _gla_out_kernel,
        grid=(bsz, s // tg),
        in_specs=[act(2 * hk), act(width), act(width), act(hk),
                  const((1, width)), const(tril.shape), const(chunk_ind.shape), const((width, width)),
                  act(o_diff.shape[2]), act(d),
                  pl.BlockSpec((1, 1, d), lambda b, t: (b, 0, 2)),
                  const(w_out_bf.shape), const((1, d))],
        out_specs=act(d),
        out_shape=jax.ShapeDtypeStruct((bsz, s, d), F32),
        scratch_shapes=[pltpu.VMEM((hk, GLA_DV), F32)],
        compiler_params=pltpu.CompilerParams(
            dimension_semantics=("arbitrary", "arbitrary"), vmem_limit_bytes=VMEM_LIMIT),
        name="gla_out_proj",
    )(gqk, gv, gza, la, gla_out_gain, jnp.asarray(tril, BF16), jnp.asarray(chunk_ind, BF16),
      jnp.asarray(ones_bd, BF16), o_diff, x, mod3, w_out_bf, final_gain)


def _diff_lambda(lq1_ref, lk1_ref, lq2_ref, lk2_ref, lam_init):
    a = jnp.sum(lq1_ref[...] * lk1_ref[...], axis=-1, keepdims=True)
    b = jnp.sum(lq2_ref[...] * lk2_ref[...], axis=-1, keepdims=True)
    return jnp.exp(a) - jnp.exp(b) + lam_init


def _attn_spans(nq, tk):
    spans = []
    for qi in range(nq):
        tile = [(qi, 2 * tk * j, 2 * tk, None) for j in range(qi // 2)]
        if qi % 2 == 1:
            tile.append((qi, (qi - 1) * tk, 2 * tk, tk))
        else:
            tile.append((qi, qi * tk, tk, 0))
        for i, sp in enumerate(tile):
            spans.append(sp + (i == 0, i == len(tile) - 1))
    return spans


def _attn_bias_tables(tq, tk):
    def split(x, terms):
        parts = []
        for _ in range(terms):
            hi = x.astype(jnp.bfloat16)
            parts.append(hi)
            x = x - hi.astype(np.float32)
        return parts

    key = np.arange(2 * tk, dtype=np.float32)
    kaug = np.zeros((2 * tk, LANES), np.float32)
    kaug[:, 0] = kaug[:, 1] = key % BF16_EXACT_INT
    kaug[:, 2] = kaug[:, 3] = key - key % BF16_EXACT_INT
    kaug[:, 4:7] = 1.0
    slopes = (2.0 ** (-8.0 * np.arange(1, DIFF_HEADS + 1) / DIFF_HEADS) * LOG2E).astype(np.float32)
    qry = np.arange(tq, dtype=np.float32)
    qaug = np.zeros((DIFF_HEADS, tq, LANES), jnp.bfloat16)
    for hh in range(DIFF_HEADS):
        s_hi, s_lo = split(slopes[hh:hh + 1], 2)
        qaug[hh, :, 0] = qaug[hh, :, 2] = s_hi
        qaug[hh, :, 1] = qaug[hh, :, 3] = s_lo
        for lane, part in zip((4, 5, 6), split(-slopes[hh] * qry, 3)):
            qaug[hh, :, lane] = part
    slope_rows = np.broadcast_to(slopes[:, None, None], (DIFF_HEADS, 1, LANES))
    return jnp.asarray(kaug, BF16), jnp.asarray(qaug), jnp.asarray(slope_rows, F32)


def _attn_kernel(q_ref, k_ref, vt_ref, z_ref, gain_ref, kaug_ref, qaug_ref, slope_ref,
                 lq1_ref, lk1_ref, lq2_ref, lk2_ref,
                 o_ref, km_ref, s_ref, mb_ref, acc_ref, *, lam_init):
    tq = ATTN_TQ
    tk = ATTN_TK
    hw = 2 * DIFF_DH
    nq = q_ref.shape[1] // tq
    n_heads = q_ref.shape[2] // hw
    n_buf = ATTN_AHEAD + 1

    first_half = lax.broadcasted_iota(jnp.int32, (1, hw), 1) < DIFF_DH
    map_cols = lax.broadcasted_iota(jnp.int32, (1, 2 * tq), 1) < tq
    r_i = lax.broadcasted_iota(jnp.int32, (hw, LANES), 0)
    c_i = lax.broadcasted_iota(jnp.int32, (hw, LANES), 1)
    half_sel = jnp.where(r_i // DIFF_DH == c_i, 1.0, 0.0).astype(BF16)
    half_rows = (lax.broadcasted_iota(jnp.int32, (8, hw), 1) // DIFF_DH
                 == lax.broadcasted_iota(jnp.int32, (8, hw), 0)).astype(BF16)

    def rel_pos(nk):
        key = lax.broadcasted_iota(jnp.int32, (nk, 2 * tq), 0)
        qry = lax.broadcasted_iota(jnp.int32, (nk, 2 * tq), 1) % tq
        return qry - key

    lam = _diff_lambda(lq1_ref, lk1_ref, lq2_ref, lk2_ref, lam_init)
    spans = _attn_spans(nq, tk)

    class Head:
        def __init__(self, j):
            self.j = j
            self.lanes = slice(j * hw, (j + 1) * hw)
            self.slope = slope_ref[j][:, 0:1]
            k_all = k_ref[0, :, self.lanes]
            zk = jnp.zeros_like(k_all)
            km_ref[j, 0] = jnp.where(first_half, k_all, zk)
            km_ref[j, 1] = jnp.where(first_half, zk, k_all)
            k_norm2 = jnp.max(jnp.dot(k_all * k_all, half_sel, preferred_element_type=F32), axis=0, keepdims=True)
            k_max = jnp.sqrt(k_norm2)
            k_max_row = jnp.where(map_cols, k_max[:, 0:1], k_max[:, 1:2])
            self.bounds = []
            for qi in range(nq):
                q = q_ref[0, qi * tq:(qi + 1) * tq, self.lanes]
                q_half2 = lax.dot_general(half_rows, q * q, (((1,), (1,)), ((), ())), preferred_element_type=F32)
                q_norm2 = jnp.concatenate([q_half2[0:1], q_half2[1:2]], axis=1)
                self.bounds.append(jnp.sqrt(q_norm2) * k_max_row * BOUND_SLACK)

        def span_offset(self, qi, start):
            return -self.slope * float(qi * tq - start)

        def masked_scores(self, t):
            qi, start, nk, diag_off, _, _ = spans[t]
            qrhs = jnp.concatenate([q_ref[0, qi * tq:(qi + 1) * tq, self.lanes], qaug_ref[self.j]], axis=1)
            s_maps = []
            for m in range(2):
                kblk = jnp.concatenate([km_ref[self.j, m, start:start + nk, :], kaug_ref[0:nk, :]], axis=1)
                s_maps.append(lax.dot_general(kblk, qrhs, (((1,), (1,)), ((), ())), preferred_element_type=F32))
            s = jnp.concatenate(s_maps, axis=1)
            if diag_off is not None:
                keep = rel_pos(tk) + (diag_off - (nk - tk)) >= 0
                tail = jnp.where(keep, s[nk - tk:nk], -jnp.inf)
                s = tail if nk == tk else jnp.concatenate([s[0:nk - tk], tail], axis=0)
            return s

        def values_and_ones(self, t):
            _, start, nk, _, _, _ = spans[t]
            vt = vt_ref[0, self.j * DIFF_DV:(self.j + 1) * DIFF_DV, start:start + nk]
            return jnp.concatenate([vt, jnp.ones((SUM_ROWS, nk), BF16)], axis=0)

        def finish(self, qi, acc):
            rows = slice(qi * tq, (qi + 1) * tq)
            o_all = acc[0:DIFF_DV, :] / acc[DIFF_DV:DIFF_DV + 1, :]
            o = (o_all[:, :tq] - lam * o_all[:, tq:]).T
            y = o * lax.rsqrt(jnp.mean(o * o, axis=-1, keepdims=True) + EPS)
            y = y * gain_ref[self.j] * (1.0 - lam_init) * z_ref[0, rows, self.lanes].astype(F32)
            o_ref[0, rows, self.lanes] = y.astype(o_ref.dtype)

        def exact_path(self):
            def scores(t):
                qi, start, nk, _, _, _ = spans[t]
                s = self.masked_scores(t)
                s_ref[t % n_buf, 0:nk, :] = s
                mb_ref[t % n_buf] = jnp.max(s, axis=0, keepdims=True) + self.span_offset(qi, start)

            def update(t, m_old):
                qi, start, nk, _, first, last = spans[t]
                s = s_ref[t % n_buf, 0:nk, :]
                m_blk = mb_ref[t % n_buf]
                m_new = m_blk if first else jnp.maximum(m_old, m_blk)
                p = jnp.exp2(s - (m_new - self.span_offset(qi, start))).astype(BF16)
                pv = jnp.dot(self.values_and_ones(t), p, preferred_element_type=F32)
                if first:
                    acc_ref[...] = pv
                else:
                    acc_ref[...] = jnp.exp2(m_old - m_new) * acc_ref[...] + pv
                if last:
                    self.finish(qi, acc_ref[...])
                return m_new

            for t in range(min(ATTN_AHEAD, len(spans))):
                scores(t)
            m = None
            for t in range(len(spans)):
                if t + ATTN_AHEAD < len(spans):
                    scores(t + ATTN_AHEAD)
                m = update(t, m)

    heads = [Head(j) for j in range(n_heads)]

    acc = [None] * n_heads
    denom_min = [None] * n_heads
    s_next = [h.masked_scores(0) for h in heads]
    for t, (qi, start, _, _, first, last) in enumerate(spans):
        for j, h in enumerate(heads):
            s = s_next[j]
            if t + 1 < len(spans):
                s_next[j] = h.masked_scores(t + 1)
            p = jnp.exp2(s - (h.bounds[qi] - h.span_offset(qi, start))).astype(BF16)
            pv = jnp.dot(h.values_and_ones(t), p, preferred_element_type=F32)
            acc[j] = pv if first else acc[j] + pv
            if last:
                denom = acc[j][DIFF_DV:DIFF_DV + 1, :]
                denom_min[j] = denom if denom_min[j] is None else jnp.minimum(denom_min[j], denom)
                h.finish(qi, acc[j])

    for j, h in enumerate(heads):
        fast_ok = jnp.min(denom_min[j]) >= FAST_PATH_MIN_DENOM
        pl.when(jnp.logical_not(fast_ok))(h.exact_path)


def _diff_attn(dq, dk, dvt, dz, gain, lq1, lk1, lq2, lk2, lam_init):
    bsz, s, width = dq.shape
    tq = ATTN_TQ
    assert ATTN_TQ == ATTN_TK
    hw = 2 * DIFF_DH
    nh = ATTN_HEADS_PER_STEP
    rowspec = pl.BlockSpec((1, s, nh * hw), lambda b, h: (b, 0, h))
    vtspec = pl.BlockSpec((1, nh * DIFF_DV, s), lambda b, h: (b, h, 0))
    lspec = pl.BlockSpec((1, DIFF_DH), lambda b, h: (0, 0))
    kaug, qaug, slope_rows = _attn_bias_tables(tq, ATTN_TK)
    per_head = lambda rows: pl.BlockSpec((nh, rows, LANES), lambda b, h: (h, 0, 0))
    return pl.pallas_call(
        functools.partial(_attn_kernel, lam_init=lam_init),
        grid=(bsz, DIFF_HEADS // nh),
        in_specs=[rowspec, rowspec, vtspec, rowspec, per_head(1),
                  pl.BlockSpec(kaug.shape, lambda b, h: (0, 0)), per_head(tq), per_head(1),
                  lspec, lspec, lspec, lspec],
        out_specs=rowspec,
        out_shape=jax.ShapeDtypeStruct((bsz, s, width), BF16),
        scratch_shapes=[pltpu.VMEM((nh, 2, s, hw), BF16),
                        pltpu.VMEM((ATTN_AHEAD + 1, 2 * ATTN_TK, 2 * tq), F32),
                        pltpu.VMEM((ATTN_AHEAD + 1, 1, 2 * tq), F32),
                        pltpu.VMEM((DIFF_DV + SUM_ROWS, 2 * tq), F32)],
        compiler_params=pltpu.CompilerParams(
            dimension_semantics=("arbitrary", "arbitrary"), vmem_limit_bytes=VMEM_LIMIT),
        name="diff_attn",
    )(dq, dk, dvt, dz, gain.reshape(DIFF_HEADS, 1, DIFF_DV), kaug, qaug, slope_rows, lq1, lk1, lq2, lk2)


def _w_in_layout():
    hk = GLA_HEADS * GLA_DK
    names = ("gq", "gk", "gv", "gz", "gr", "dq", "dk", "dv", "dz")
    sizes = (hk, hk, SECTION, SECTION, GLA_RANK, SECTION, SECTION, SECTION, SECTION)
    src = dict(zip(names, np.concatenate([[0], np.cumsum(sizes)[:-1]]).tolist()))
    width = dict(zip(names, sizes))
    scale = {"gq": GLA_DK ** -0.5, "dq": DIFF_DH ** -0.5 * LOG2E}
    pieces, dst = [], 0
    for name in ("gq", "gk", "gv", "gz", "dq", "dk", "dz", "dv", "gr"):
        pieces.append((src[name], dst, width[name], scale.get(name)))
        dst += width[name]
    return pieces, dst + RANK_PAD - GLA_RANK


def _w_prep_kernel(wt_ref, o_ref):
    pieces, _ = _w_in_layout()
    for src, dst, width, scale in pieces:
        piece = wt_ref[0, src:src + width, :]
        if scale is not None:
            piece = piece * scale
        if width < LANES:
            piece = jnp.concatenate([piece, jnp.zeros((LANES - width, piece.shape[1]), piece.dtype)], axis=0)
        o_ref[:, dst:dst + piece.shape[0]] = piece.T.astype(o_ref.dtype)


def _prep_w_in(w_in, layer):
    _, d, n_src = w_in.shape
    _, n_dst = _w_in_layout()
    rows = 128
    return pl.pallas_call(
        _w_prep_kernel,
        grid=(d // rows,),
        in_specs=[pl.BlockSpec((1, n_src, rows), lambda i: (layer, 0, i))],
        out_specs=pl.BlockSpec((rows, n_dst), lambda i: (i, 0)),
        out_shape=jax.ShapeDtypeStruct((d, n_dst), BF16),
        compiler_params=pltpu.CompilerParams(dimension_semantics=("arbitrary",), vmem_limit_bytes=VMEM_LIMIT),
        name="w_in_prep",
    )(jnp.swapaxes(w_in, 1, 2))


def kernel(x, c, w_ada, b_ada, norm_gain, w_in, w_gla_gate_up, b_gla_gate, gla_out_gain,
           lambda_q1, lambda_k1, lambda_q2, lambda_k2, diff_out_gain, w_out, final_gain):
    bsz, s, d = x.shape
    depth = w_in.shape[0]
    assert depth == 1, "gla_out_proj applies the final rmsnorm, so exactly one layer is supported"
    for l in range(depth):
        mod = _adaln_mod(c, w_ada, b_ada, l)
        mod3 = mod.reshape(bsz, 1, 3 * d)
        w_in_r = _prep_w_in(w_in, l)
        wup_pad = jnp.pad(w_gla_gate_up[l], ((0, RANK_PAD - GLA_RANK), (0, 0))).astype(BF16)
        gqk, gv, gz, dq, dk, dz, dvt, la = _in_proj(
            x, mod3, norm_gain[l].reshape(1, d), w_in_r, wup_pad, b_gla_gate[l].reshape(1, -1))
        lam_init = float(0.8 - 0.6 * np.exp(-0.3 * l))
        o_diff = _diff_attn(dq, dk, dvt, dz, diff_out_gain[l],
                            lambda_q1[l].reshape(1, -1), lambda_k1[l].reshape(1, -1),
                            lambda_q2[l].reshape(1, -1), lambda_k2[l].reshape(1, -1), lam_init)
        x = _gla_out(gqk, gv, gz, la, gla_out_gain[l].reshape(1, -1), o_diff, x, mod3,
                     w_out[l].astype(BF16), final_gain.reshape(1, d))
    return x
```

```python
import functools
import math

import jax
import jax.numpy as jnp
import numpy as np
from jax import lax
from jax.experimental import pallas as pl
from jax.experimental.pallas import tpu as pltpu

F32 = jnp.float32
BF16 = jnp.bfloat16

EPS = 1e-6
LOG2E = math.log2(math.e)
GLA_HEADS = 4
GLA_DK = 64
GLA_DV = 128
GLA_RANK = 16
GLA_GATE_NORM = 16.0
GLA_CHUNK = 64
DIFF_HEADS = 4
DIFF_DH = 64
DIFF_DV = 128
LANES = 128
RANK_PAD = LANES
SECTION = 512
SUM_ROWS = 16

ROWS_IN_PROJ = 1024
ROWS_GLA = 1024
GLA_CUMSUM_ROWS = 256
ATTN_TQ = 256
ATTN_TK = 256
ATTN_HEADS_PER_STEP = 2
ATTN_AHEAD = 2
BF16_EXACT_INT = 256
BOUND_SLACK = 1.01
FAST_PATH_MIN_DENOM = 2.0 ** -90
VMEM_LIMIT = 48 * 1024 * 1024


def _silu(v):
    return v / (1.0 + jnp.exp(-v))


def _log_sigmoid(v):
    return jnp.minimum(v, 0.0) - jnp.log(1.0 + jnp.exp(-jnp.abs(v)))


def _adaln_kernel(c_ref, w_ref, b_ref, o_ref):
    sc = _silu(c_ref[...]).astype(BF16)
    o_ref[...] = jnp.dot(sc, w_ref[0].astype(BF16), preferred_element_type=F32) + b_ref[0]


def _adaln_mod(c, w_ada, b_ada, layer):
    bsz, d = c.shape
    n = w_ada.shape[2]
    tn = 1024
    return pl.pallas_call(
        _adaln_kernel,
        grid=(n // tn,),
        in_specs=[
            pl.BlockSpec((bsz, d), lambda j: (0, 0)),
            pl.BlockSpec((1, d, tn), lambda j: (layer, 0, j)),
            pl.BlockSpec((1, 1, tn), lambda j: (layer, 0, j)),
        ],
        out_specs=pl.BlockSpec((bsz, tn), lambda j: (0, j)),
        out_shape=jax.ShapeDtypeStruct((bsz, n), F32),
        compiler_params=pltpu.CompilerParams(dimension_semantics=("arbitrary",)),
        name="adaln_mod",
    )(c, w_ada, b_ada.reshape(b_ada.shape[0], 1, n))


def _in_proj_kernel(x_ref, shift_ref, scale_ref, gain_ref, w_ref, wup_ref, bg_ref, ggain_ref, dgain_ref,
                    gqk_ref, gv_ref, gz_ref, dq_ref, dk_ref, dz_ref, dvt_ref, la_ref,
                    wvt_ref):
    n_sec = 7
    dv_sec = 6

    @pl.when((pl.program_id(0) == 0) & (pl.program_id(1) == 0))
    def _():
        wvt_ref[0:SECTION, :] = w_ref[:, dv_sec * SECTION:(dv_sec + 1) * SECTION].T
        wvt_ref[SECTION:SECTION + GLA_RANK, :] = (
            w_ref[:, n_sec * SECTION:n_sec * SECTION + RANK_PAD].T[0:GLA_RANK, :])

    g = gain_ref[...] * (1.0 + scale_ref[0])
    tm = x_ref.shape[1]
    half = tm // 2

    def modulated(rows):
        x = x_ref[0, rows, :]
        rstd = lax.rsqrt(jnp.mean(x * x, axis=-1, keepdims=True) + EPS)
        return (x * rstd * g + shift_ref[0]).astype(BF16)

    h_halves = [modulated(slice(0, half)), modulated(slice(half, tm))]
    for r, h_half in enumerate(h_halves):
        gqk_ref[0, r * half:(r + 1) * half, :] = jnp.dot(
            h_half, w_ref[:, 0:SECTION], preferred_element_type=F32).astype(gqk_ref.dtype)
    h = jnp.concatenate(h_halves, axis=0)

    outs = (gqk_ref, gv_ref, gz_ref, dq_ref, dk_ref, dz_ref)

    def section(j):
        sec = jnp.dot(h, w_ref[:, j * SECTION:(j + 1) * SECTION], preferred_element_type=F32)
        if outs[j] is gz_ref:
            sec = _silu(sec) * ggain_ref[...]
        elif outs[j] is dz_ref:
            sec = _silu(sec) * dgain_ref[...]
        outs[j][0] = sec.astype(outs[j].dtype)

    section(2)
    section(5)

    vt_gr = lax.dot_general(wvt_ref[...], h, (((1,), (1,)), ((), ())), preferred_element_type=F32)
    dvt_ref[0] = vt_gr[0:SECTION].astype(dvt_ref.dtype)
    tm = vt_gr.shape[1]
    gr_t = jnp.concatenate([vt_gr[SECTION:SECTION + GLA_RANK].astype(BF16),
                            jnp.zeros((RANK_PAD - GLA_RANK, tm), BF16)], axis=0)
    logit = lax.dot_general(gr_t, wup_ref[...], (((0,), (0,)), ((), ())),
                            preferred_element_type=F32) + bg_ref[...]
    la_ref[0] = (_log_sigmoid(logit) * (1.0 / GLA_GATE_NORM)).astype(la_ref.dtype)

    for j in (1, 3, 4):
        section(j)


def _in_proj(x, mod3, norm_gain, w_in_r, wup_pad, b_gate, gla_gate_gain, diff_gate_gain):
    bsz, s, d = x.shape
    tm = ROWS_IN_PROJ
    ncol = w_in_r.shape[1]
    hk = GLA_HEADS * GLA_DK
    act = lambda width: pl.BlockSpec((1, tm, width), lambda b, t: (b, t, 0))
    sec = jax.ShapeDtypeStruct((bsz, s, SECTION), BF16)
    out_shapes = (
        sec,
        sec,
        sec,
        sec,
        sec,
        sec,
        jax.ShapeDtypeStruct((bsz, SECTION, s), BF16),
        jax.ShapeDtypeStruct((bsz, s, hk), BF16),
    )
    const = lambda shape: pl.BlockSpec(shape, lambda b, t: (0, 0))
    return pl.pallas_call(
        _in_proj_kernel,
        grid=(bsz, s // tm),
        in_specs=[
            act(d),
            pl.BlockSpec((1, 1, d), lambda b, t: (b, 0, 0)),
            pl.BlockSpec((1, 1, d), lambda b, t: (b, 0, 1)),
            const((1, d)), const((d, ncol)), const((RANK_PAD, hk)), const((1, hk)),
            const((1, SECTION)), const((1, SECTION)),
        ],
        out_specs=(act(SECTION), act(SECTION), act(SECTION), act(SECTION), act(SECTION), act(SECTION),
                   pl.BlockSpec((1, SECTION, tm), lambda b, t: (b, 0, t)), act(hk)),
        out_shape=out_shapes,
        scratch_shapes=[pltpu.VMEM((SECTION + GLA_RANK, d), BF16)],
        compiler_params=pltpu.CompilerParams(
            dimension_semantics=("arbitrary", "arbitrary"), vmem_limit_bytes=VMEM_LIMIT),
        name="in_proj",
    )(x, mod3, mod3, norm_gain, w_in_r, wup_pad, b_gate, gla_gate_gain, diff_gate_gain)


def _gla_out_kernel(qk_ref, v_ref, za_ref, la_ref, tril_ref, chunk_ind_ref, mean_bd_ref,
                    od_ref, x_ref, gate_ref, w_ref, fg_ref, o_ref, state_ref):
    tg = qk_ref.shape[1]
    half = od_ref.shape[2]
    hk = GLA_HEADS * GLA_DK
    c_sz = GLA_CHUNK
    n_ch = tg // c_sz

    @pl.when(pl.program_id(1) == 0)
    def _():
        state_ref[...] = jnp.zeros_like(state_ref)

    la = la_ref[0]
    grp = tril_ref.shape[0]
    b_all = jnp.concatenate(
        [jnp.dot(tril_ref[...], la[g * grp:(g + 1) * grp], preferred_element_type=F32)
         for g in range(tg // grp)], axis=0)
    b_last = jnp.concatenate(
        [jnp.broadcast_to(b_all[(ch + 1) * c_sz - 1:(ch + 1) * c_sz], (c_sz, hk)) for ch in range(n_ch)], axis=0)
    q = qk_ref[0, :, 0:hk]
    k = qk_ref[0, :, hk:2 * hk]
    q_in = q * jnp.exp(b_all).astype(BF16)
    k_in = k * jnp.exp(-b_all).astype(BF16)
    k_st = k * jnp.exp(b_last - b_all).astype(BF16)

    lane_head = lax.broadcasted_iota(jnp.int32, (1, hk), 1) // GLA_DK
    ri = lax.broadcasted_iota(jnp.int32, (GLA_HEADS * c_sz, c_sz), 0) % c_sz
    ci = lax.broadcasted_iota(jnp.int32, (GLA_HEADS * c_sz, c_sz), 1)
    causal = ci <= ri
    dec_all = jnp.concatenate(
        [jnp.exp(lax.dot_general(la[g * grp:(g + 1) * grp], chunk_ind_ref[...], (((0,), (0,)), ((), ())),
                                 preferred_element_type=F32)) for g in range(tg // grp)], axis=1)

    mix_diff = jnp.dot(od_ref[0], w_ref[half:2 * half, :], preferred_element_type=F32)

    chunk_rows = [slice(ch * c_sz, (ch + 1) * c_sz) for ch in range(n_ch)]
    head_rows = [slice(hh * c_sz, (hh + 1) * c_sz) for hh in range(GLA_HEADS)]
    head_cols = [slice(hh * GLA_DV, (hh + 1) * GLA_DV) for hh in range(GLA_HEADS)]
    vs = [v_ref[0, rows, :] for rows in chunk_rows]

    us = []
    for ch, rows in enumerate(chunk_rows):
        kst_t = k_st[rows].T
        us.append(jnp.concatenate(
            [jnp.dot(kst_t[hr], vs[ch][:, vc], preferred_element_type=F32)
             for hr, vc in zip(head_rows, head_cols)], axis=0))
    state = state_ref[...]
    states = []
    for ch in range(n_ch):
        states.append(state.astype(BF16))
        state = dec_all[:, ch * GLA_DV:(ch + 1) * GLA_DV] * state + us[ch]
    state_ref[...] = state
    boths = []
    for ch, rows in enumerate(chunk_rows):
        q_c = q_in[rows]
        qm = jnp.concatenate(
            [jnp.where(lane_head == hh, q_c, jnp.zeros_like(q_c)) for hh in range(GLA_HEADS)], axis=0)
        rhs = jnp.concatenate([states[ch], k_in[rows].T], axis=1)
        boths.append(jnp.dot(qm, rhs, preferred_element_type=F32))
    o_rows = []
    for ch in range(n_ch):
        inter = boths[ch][:, 0:GLA_DV]
        p = jnp.where(causal, boths[ch][:, GLA_DV:GLA_DV + c_sz], 0.0).astype(BF16)
        o_rows.append(jnp.concatenate(
            [jnp.dot(p[hr], vs[ch][:, vc], preferred_element_type=F32) + inter[hr]
             for hr, vc in zip(head_rows, head_cols)], axis=1))

    o = jnp.concatenate(o_rows, axis=0)
    ms = jnp.dot((o * o).astype(BF16), mean_bd_ref[...], preferred_element_type=F32)
    o_gla = (o * lax.rsqrt(ms + EPS) * za_ref[0].astype(F32)).astype(BF16)

    mixw = jnp.dot(o_gla, w_ref[0:half, :], preferred_element_type=F32) + mix_diff
    xn = x_ref[0] + gate_ref[0] * mixw
    o_ref[0] = xn * lax.rsqrt(jnp.mean(xn * xn, axis=-1, keepdims=True) + EPS) * fg_ref[...]


def _gla_out(gqk, gv, gza, la, o_diff, x, mod3, w_out_bf, final_gain):
    bsz, s, d = x.shape
    tg = ROWS_GLA
    hk = GLA_HEADS * GLA_DK
    width = GLA_HEADS * GLA_DV
    grp = GLA_CUMSUM_ROWS
    idx = np.arange(grp)
    tril = ((idx[:, None] // GLA_CHUNK == idx[None, :] // GLA_CHUNK) & (idx[None, :] <= idx[:, None]))
    chunk_ind = idx[:, None] // GLA_CHUNK == np.arange(grp // GLA_CHUNK * GLA_DV)[None, :] // GLA_DV
    col = np.arange(width)
    mean_bd = (col[:, None] // GLA_DV == col[None, :] // GLA_DV) / GLA_DV
    act = lambda w: pl.BlockSpec((1, tg, w), lambda b, t: (b, t, 0))
    const = lambda shape: pl.BlockSpec(shape, lambda b, t: (0, 0))
    return pl.pallas_call(
        _gla_out_kernel,
        grid=(bsz, s // tg),
        in_specs=[act(2 * hk), act(width), act(width), act(hk),
                  const(tril.shape), const(chunk_ind.shape), const((width, width)),
                  act(o_diff.shape[2]), act(d),
                  pl.BlockSpec((1, 1, d), lambda b, t: (b, 0, 2)),
                  const(w_out_bf.shape), const((1, d))],
        out_specs=act(d),
        out_shape=jax.ShapeDtypeStruct((bsz, s, d), F32),
        scratch_shapes=[pltpu.VMEM((hk, GLA_DV), F32)],
        compiler_params=pltpu.CompilerParams(
            dimension_semantics=("arbitrary", "arbitrary"), vmem_limit_bytes=VMEM_LIMIT),
        name="gla_out_proj",
    )(gqk, gv, gza, la, jnp.asarray(tril, BF16), jnp.asarray(chunk_ind, BF16),
      jnp.asarray(mean_bd, BF16), o_diff, x, mod3, w_out_bf, final_gain)


def _diff_lambda(lq1_ref, lk1_ref, lq2_ref, lk2_ref, lam_init):
    a = jnp.sum(lq1_ref[...] * lk1_ref[...], axis=-1, keepdims=True)
    b = jnp.sum(lq2_ref[...] * lk2_ref[...], axis=-1, keepdims=True)
    return jnp.exp(a) - jnp.exp(b) + lam_init


def _attn_spans(nq, tk):
    spans = []
    for qi in range(nq):
        tile = [(qi, 2 * tk * j, 2 * tk, None) for j in range(qi // 2)]
        if qi % 2 == 1:
            tile.append((qi, (qi - 1) * tk, 2 * tk, tk))
        else:
            tile.append((qi, qi * tk, tk, 0))
        for i, sp in enumerate(tile):
            spans.append(sp + (i == 0, i == len(tile) - 1))
    return spans


def _attn_bias_tables(tq, tk):
    def split(x, terms):
        parts = []
        for _ in range(terms):
            hi = x.astype(jnp.bfloat16)
            parts.append(hi)
            x = x - hi.astype(np.float32)
        return parts

    key = np.arange(2 * tk, dtype=np.float32)
    kaug = np.zeros((2 * tk, LANES), np.float32)
    kaug[:, 0] = kaug[:, 1] = key % BF16_EXACT_INT
    kaug[:, 2] = kaug[:, 3] = key - key % BF16_EXACT_INT
    kaug[:, 4:7] = 1.0
    slopes = (2.0 ** (-8.0 * np.arange(1, DIFF_HEADS + 1) / DIFF_HEADS) * LOG2E).astype(np.float32)
    qry = np.arange(tq, dtype=np.float32)
    qaug = np.zeros((DIFF_HEADS, tq, LANES), jnp.bfloat16)
    for hh in range(DIFF_HEADS):
        s_hi, s_lo = split(slopes[hh:hh + 1], 2)
        qaug[hh, :, 0] = qaug[hh, :, 2] = s_hi
        qaug[hh, :, 1] = qaug[hh, :, 3] = s_lo
        for lane, part in zip((4, 5, 6), split(-slopes[hh] * qry, 3)):
            qaug[hh, :, lane] = part
    slope_rows = np.broadcast_to(slopes[:, None, None], (DIFF_HEADS, 1, LANES))
    return jnp.asarray(kaug, BF16), jnp.asarray(qaug), jnp.asarray(slope_rows, F32)


def _attn_kernel(q_ref, k_ref, vt_ref, z_ref, kaug_ref, qaug_ref, slope_ref,
                 lq1_ref, lk1_ref, lq2_ref, lk2_ref,
                 o_ref, km_ref, s_ref, mb_ref, acc_ref, *, lam_init):
    tq = ATTN_TQ
    tk = ATTN_TK
    hw = 2 * DIFF_DH
    nq = q_ref.shape[1] // tq
    n_heads = q_ref.shape[2] // hw
    n_buf = ATTN_AHEAD + 1

    first_half = lax.broadcasted_iota(jnp.int32, (1, hw), 1) < DIFF_DH
    map_cols = lax.broadcasted_iota(jnp.int32, (1, 2 * tq), 1) < tq
    r_i = lax.broadcasted_iota(jnp.int32, (hw, LANES), 0)
    c_i = lax.broadcasted_iota(jnp.int32, (hw, LANES), 1)
    half_sel = jnp.where(r_i // DIFF_DH == c_i, 1.0, 0.0).astype(BF16)
    half_rows = (lax.broadcasted_iota(jnp.int32, (8, hw), 1) // DIFF_DH
                 == lax.broadcasted_iota(jnp.int32, (8, hw), 0)).astype(BF16)

    def rel_pos(nk):
        key = lax.broadcasted_iota(jnp.int32, (nk, 2 * tq), 0)
        qry = lax.broadcasted_iota(jnp.int32, (nk, 2 * tq), 1) % tq
        return qry - key

    lam = _diff_lambda(lq1_ref, lk1_ref, lq2_ref, lk2_ref, lam_init)
    spans = _attn_spans(nq, tk)

    class Head:
        def __init__(self, j):
            self.j = j
            self.lanes = slice(j * hw, (j + 1) * hw)
            self.slope = slope_ref[j][:, 0:1]
            k_all = k_ref[0, :, self.lanes]
            zk = jnp.zeros_like(k_all)
            km_ref[j, 0] = jnp.where(first_half, k_all, zk)
            km_ref[j, 1] = jnp.where(first_half, zk, k_all)
            k_norm2 = jnp.max(jnp.dot(k_all * k_all, half_sel, preferred_element_type=F32), axis=0, keepdims=True)
            k_max = jnp.sqrt(k_norm2)
            k_max_row = jnp.where(map_cols, k_max[:, 0:1], k_max[:, 1:2])
            self.bounds = []
            for qi in range(nq):
                q = q_ref[0, qi * tq:(qi + 1) * tq, self.lanes]
                q_half2 = lax.dot_general(half_rows, q * q, (((1,), (1,)), ((), ())), preferred_element_type=F32)
                q_norm2 = jnp.concatenate([q_half2[0:1], q_half2[1:2]], axis=1)
                self.bounds.append(jnp.sqrt(q_norm2) * k_max_row * BOUND_SLACK)

        def span_offset(self, qi, start):
            return -self.slope * float(qi * tq - start)

        def masked_scores(self, t):
            qi, start, nk, diag_off, _, _ = spans[t]
            qrhs = jnp.concatenate([q_ref[0, qi * tq:(qi + 1) * tq, self.lanes], qaug_ref[self.j]], axis=1)
            s_maps = []
            for m in range(2):
                kblk = jnp.concatenate([km_ref[self.j, m, start:start + nk, :], kaug_ref[0:nk, :]], axis=1)
                s_maps.append(lax.dot_general(kblk, qrhs, (((1,), (1,)), ((), ())), preferred_element_type=F32))
            s = jnp.concatenate(s_maps, axis=1)
            if diag_off is not None:
                keep = rel_pos(tk) + (diag_off - (nk - tk)) >= 0
                tail = jnp.where(keep, s[nk - tk:nk], -jnp.inf)
                s = tail if nk == tk else jnp.concatenate([s[0:nk - tk], tail], axis=0)
            return s

        def values_and_ones(self, t):
            _, start, nk, _, _, _ = spans[t]
            vt = vt_ref[0, self.j * DIFF_DV:(self.j + 1) * DIFF_DV, start:start + nk]
            return jnp.concatenate([vt, jnp.ones((SUM_ROWS, nk), BF16)], axis=0)

        def finish(self, qi, acc):
            rows = slice(qi * tq, (qi + 1) * tq)
            o_all = acc[0:DIFF_DV, :] / acc[DIFF_DV:DIFF_DV + 1, :]
            o = (o_all[:, :tq] - lam * o_all[:, tq:]).T
            y = o * lax.rsqrt(jnp.mean(o * o, axis=-1, keepdims=True) + EPS)
            y = y * z_ref[0, rows, self.lanes].astype(F32)
            o_ref[0, rows, self.lanes] = y.astype(o_ref.dtype)

        def exact_path(self):
            def scores(t):
                qi, start, nk, _, _, _ = spans[t]
                s = self.masked_scores(t)
                s_ref[t % n_buf, 0:nk, :] = s
                mb_ref[t % n_buf] = jnp.max(s, axis=0, keepdims=True) + self.span_offset(qi, start)

            def update(t, m_old):
                qi, start, nk, _, first, last = spans[t]
                s = s_ref[t % n_buf, 0:nk, :]
                m_blk = mb_ref[t % n_buf]
                m_new = m_blk if first else jnp.maximum(m_old, m_blk)
                p = jnp.exp2(s - (m_new - self.span_offset(qi, start))).astype(BF16)
                pv = jnp.dot(self.values_and_ones(t), p, preferred_element_type=F32)
                if first:
                    acc_ref[...] = pv
                else:
                    acc_ref[...] = jnp.exp2(m_old - m_new) * acc_ref[...] + pv
                if last:
                    self.finish(qi, acc_ref[...])
                return m_new

            for t in range(min(ATTN_AHEAD, len(spans))):
                scores(t)
            m = None
            for t in range(len(spans)):
                if t + ATTN_AHEAD < len(spans):
                    scores(t + ATTN_AHEAD)
                m = update(t, m)

    heads = [Head(j) for j in range(n_heads)]

    acc = [None] * n_heads
    denom_min = [None] * n_heads
    s_next = [h.masked_scores(0) for h in heads]
    for t, (qi, start, _, _, first, last) in enumerate(spans):
        for j, h in enumerate(heads):
            s = s_next[j]
            if t + 1 < len(spans):
                s_next[j] = h.masked_scores(t + 1)
            p = jnp.exp2(s - (h.bounds[qi] - h.span_offset(qi, start))).astype(BF16)
            pv = jnp.dot(h.values_and_ones(t), p, preferred_element_type=F32)
            acc[j] = pv if first else acc[j] + pv
            if last:
                denom = acc[j][DIFF_DV:DIFF_DV + 1, :]
                denom_min[j] = denom if denom_min[j] is None else jnp.minimum(denom_min[j], denom)
                h.finish(qi, acc[j])

    for j, h in enumerate(heads):
        fast_ok = jnp.min(denom_min[j]) >= FAST_PATH_MIN_DENOM
        pl.when(jnp.logical_not(fast_ok))(h.exact_path)


def _diff_attn(dq, dk, dvt, dz, lq1, lk1, lq2, lk2, lam_init):
    bsz, s, width = dq.shape
    tq = ATTN_TQ
    assert ATTN_TQ == ATTN_TK
    hw = 2 * DIFF_DH
    nh = ATTN_HEADS_PER_STEP
    rowspec = pl.BlockSpec((1, s, nh * hw), lambda b, h: (b, 0, h))
    vtspec = pl.BlockSpec((1, nh * DIFF_DV, s), lambda b, h: (b, h, 0))
    lspec = pl.BlockSpec((1, DIFF_DH), lambda b, h: (0, 0))
    kaug, qaug, slope_rows = _attn_bias_tables(tq, ATTN_TK)
    per_head = lambda rows: pl.BlockSpec((nh, rows, LANES), lambda b, h: (h, 0, 0))
    return pl.pallas_call(
        functools.partial(_attn_kernel, lam_init=lam_init),
        grid=(bsz, DIFF_HEADS // nh),
        in_specs=[rowspec, rowspec, vtspec, rowspec,
                  pl.BlockSpec(kaug.shape, lambda b, h: (0, 0)), per_head(tq), per_head(1),
                  lspec, lspec, lspec, lspec],
        out_specs=rowspec,
        out_shape=jax.ShapeDtypeStruct((bsz, s, width), BF16),
        scratch_shapes=[pltpu.VMEM((nh, 2, s, hw), BF16),
                        pltpu.VMEM((ATTN_AHEAD + 1, 2 * ATTN_TK, 2 * tq), F32),
                        pltpu.VMEM((ATTN_AHEAD + 1, 1, 2 * tq), F32),
                        pltpu.VMEM((DIFF_DV + SUM_ROWS, 2 * tq), F32)],
        compiler_params=pltpu.CompilerParams(
            dimension_semantics=("arbitrary", "arbitrary"), vmem_limit_bytes=VMEM_LIMIT),
        name="diff_attn",
    )(dq, dk, dvt, dz, kaug, qaug, slope_rows, lq1, lk1, lq2, lk2)


def _w_in_layout():
    hk = GLA_HEADS * GLA_DK
    names = ("gq", "gk", "gv", "gz", "gr", "dq", "dk", "dv", "dz")
    sizes = (hk, hk, SECTION, SECTION, GLA_RANK, SECTION, SECTION, SECTION, SECTION)
    src = dict(zip(names, np.concatenate([[0], np.cumsum(sizes)[:-1]]).tolist()))
    width = dict(zip(names, sizes))
    scale = {"gq": GLA_DK ** -0.5, "dq": DIFF_DH ** -0.5 * LOG2E}
    pieces, dst = [], 0
    for name in ("gq", "gk", "gv", "gz", "dq", "dk", "dz", "dv", "gr"):
        pieces.append((src[name], dst, width[name], scale.get(name)))
        dst += width[name]
    return pieces, dst + RANK_PAD - GLA_RANK


def _w_prep_kernel(wt_ref, o_ref):
    pieces, _ = _w_in_layout()
    for src, dst, width, scale in pieces:
        piece = wt_ref[0, src:src + width, :]
        if scale is not None:
            piece = piece * scale
        if width < LANES:
            piece = jnp.concatenate([piece, jnp.zeros((LANES - width, piece.shape[1]), piece.dtype)], axis=0)
        o_ref[:, dst:dst + piece.shape[0]] = piece.T.astype(o_ref.dtype)


def _prep_w_in(w_in, layer):
    _, d, n_src = w_in.shape
    _, n_dst = _w_in_layout()
    rows = 256
    return pl.pallas_call(
        _w_prep_kernel,
        grid=(d // rows,),
        in_specs=[pl.BlockSpec((1, n_src, rows), lambda i: (layer, 0, i))],
        out_specs=pl.BlockSpec((rows, n_dst), lambda i: (i, 0)),
        out_shape=jax.ShapeDtypeStruct((d, n_dst), BF16),
        compiler_params=pltpu.CompilerParams(dimension_semantics=("arbitrary",), vmem_limit_bytes=VMEM_LIMIT),
        name="w_in_prep",
    )(jnp.swapaxes(w_in, 1, 2))


def kernel(x, c, w_ada, b_ada, norm_gain, w_in, w_gla_gate_up, b_gla_gate, gla_out_gain,
           lambda_q1, lambda_k1, lambda_q2, lambda_k2, diff_out_gain, w_out, final_gain):
    bsz, s, d = x.shape
    depth = w_in.shape[0]
    assert depth == 1, "gla_out_proj applies the final rmsnorm, so exactly one layer is supported"
    for l in range(depth):
        mod = _adaln_mod(c, w_ada, b_ada, l)
        mod3 = mod.reshape(bsz, 1, 3 * d)
        w_in_r = _prep_w_in(w_in, l)
        wup_pad = jnp.pad(w_gla_gate_up[l], ((0, RANK_PAD - GLA_RANK), (0, 0))).astype(BF16)
        lam_init = float(0.8 - 0.6 * np.exp(-0.3 * l))
        gqk, gv, gz, dq, dk, dz, dvt, la = _in_proj(
            x, mod3, norm_gain[l].reshape(1, d), w_in_r, wup_pad, b_gla_gate[l].reshape(1, -1),
            gla_out_gain[l].reshape(1, -1), (diff_out_gain[l] * (1.0 - lam_init)).reshape(1, -1))
        o_diff = _diff_attn(dq, dk, dvt, dz,
                            lambda_q1[l].reshape(1, -1), lambda_k1[l].reshape(1, -1),
                            lambda_q2[l].reshape(1, -1), lambda_k2[l].reshape(1, -1), lam_init)
        x = _gla_out(gqk, gv, gz, la, o_diff, x, mod3, w_out[l].astype(BF16), final_gain.reshape(1, d))
    return x
```

```python
import functools
import math

import jax
import jax.numpy as jnp
import numpy as np
from jax import lax
from jax.experimental import pallas as pl
from jax.experimental.pallas import tpu as pltpu

F32 = jnp.float32
BF16 = jnp.bfloat16

EPS = 1e-6
LOG2E = math.log2(math.e)
GLA_HEADS = 4
GLA_DK = 64
GLA_DV = 128
GLA_RANK = 16
GLA_GATE_NORM = 16.0
GLA_CHUNK = 64
DIFF_HEADS = 4
DIFF_DH = 64
DIFF_DV = 128
LANES = 128
RANK_PAD = LANES
SECTION = 512
SUM_ROWS = 16

ROWS_IN_PROJ = 1024
ROWS_GLA = 1024
GLA_CUMSUM_ROWS = 256
ATTN_TQ = 256
ATTN_TK = 256
ATTN_HEADS_PER_STEP = 2
ATTN_SPAN_TILES = 4
ATTN_AHEAD = 2
BF16_EXACT_INT = 256
BOUND_SLACK = 1.01
FAST_PATH_MIN_DENOM = 2.0 ** -90
VMEM_LIMIT = 48 * 1024 * 1024


def _silu(v):
    return v / (1.0 + jnp.exp(-v))


def _log_sigmoid(v):
    return jnp.minimum(v, 0.0) - jnp.log(1.0 + jnp.exp(-jnp.abs(v)))


def _adaln_kernel(c_ref, w_ref, b_ref, o_ref):
    sc = _silu(c_ref[...]).astype(BF16)
    o_ref[...] = jnp.dot(sc, w_ref[0].astype(BF16), preferred_element_type=F32) + b_ref[0]


def _adaln_mod(c, w_ada, b_ada, layer):
    bsz, d = c.shape
    n = w_ada.shape[2]
    tn = 1024
    return pl.pallas_call(
        _adaln_kernel,
        grid=(n // tn,),
        in_specs=[
            pl.BlockSpec((bsz, d), lambda j: (0, 0)),
            pl.BlockSpec((1, d, tn), lambda j: (layer, 0, j)),
            pl.BlockSpec((1, 1, tn), lambda j: (layer, 0, j)),
        ],
        out_specs=pl.BlockSpec((bsz, tn), lambda j: (0, j)),
        out_shape=jax.ShapeDtypeStruct((bsz, n), F32),
        compiler_params=pltpu.CompilerParams(dimension_semantics=("arbitrary",)),
        name="adaln_mod",
    )(c, w_ada, b_ada.reshape(b_ada.shape[0], 1, n))


def _in_proj_kernel(x_ref, shift_ref, scale_ref, gain_ref, w_ref, wup_ref, bg_ref, ggain_ref, dgain_ref,
                    gqk_ref, gv_ref, gz_ref, dq_ref, dk_ref, dz_ref, dvt_ref, la_ref,
                    wvt_ref):
    n_sec = 7
    dv_sec = 6

    @pl.when((pl.program_id(0) == 0) & (pl.program_id(1) == 0))
    def _():
        wvt_ref[0:SECTION, :] = w_ref[:, dv_sec * SECTION:(dv_sec + 1) * SECTION].T
        wvt_ref[SECTION:SECTION + GLA_RANK, :] = (
            w_ref[:, n_sec * SECTION:n_sec * SECTION + RANK_PAD].T[0:GLA_RANK, :])

    g = gain_ref[...] * (1.0 + scale_ref[0])
    tm = x_ref.shape[1]
    half = tm // 2

    def modulated(rows):
        x = x_ref[0, rows, :]
        rstd = lax.rsqrt(jnp.mean(x * x, axis=-1, keepdims=True) + EPS)
        return (x * rstd * g + shift_ref[0]).astype(BF16)

    h_halves = [modulated(slice(0, half)), modulated(slice(half, tm))]
    for r, h_half in enumerate(h_halves):
        gqk_ref[0, r * half:(r + 1) * half, :] = jnp.dot(
            h_half, w_ref[:, 0:SECTION], preferred_element_type=F32).astype(gqk_ref.dtype)
    h = jnp.concatenate(h_halves, axis=0)

    outs = (gqk_ref, gv_ref, gz_ref, dq_ref, dk_ref, dz_ref)

    def section(j):
        sec = jnp.dot(h, w_ref[:, j * SECTION:(j + 1) * SECTION], preferred_element_type=F32)
        if outs[j] is gz_ref:
            sec = _silu(sec) * ggain_ref[...]
        elif outs[j] is dz_ref:
            sec = _silu(sec) * dgain_ref[...]
        outs[j][0] = sec.astype(outs[j].dtype)

    section(2)
    section(5)

    vt_gr = lax.dot_general(wvt_ref[...], h, (((1,), (1,)), ((), ())), preferred_element_type=F32)
    dvt_ref[0] = vt_gr[0:SECTION].astype(dvt_ref.dtype)
    tm = vt_gr.shape[1]
    gr_t = jnp.concatenate([vt_gr[SECTION:SECTION + GLA_RANK].astype(BF16),
                            jnp.zeros((RANK_PAD - GLA_RANK, tm), BF16)], axis=0)
    logit = lax.dot_general(gr_t, wup_ref[...], (((0,), (0,)), ((), ())),
                            preferred_element_type=F32) + bg_ref[...]
    la_ref[0] = (_log_sigmoid(logit) * (1.0 / GLA_GATE_NORM)).astype(la_ref.dtype)

    for j in (1, 3, 4):
        section(j)


def _in_proj(x, mod3, norm_gain, w_in_r, wup_pad, b_gate, gla_gate_gain, diff_gate_gain):
    bsz, s, d = x.shape
    tm = ROWS_IN_PROJ
    ncol = w_in_r.shape[1]
    hk = GLA_HEADS * GLA_DK
    act = lambda width: pl.BlockSpec((1, tm, width), lambda b, t: (b, t, 0))
    sec = jax.ShapeDtypeStruct((bsz, s, SECTION), BF16)
    out_shapes = (
        sec,
        sec,
        sec,
        sec,
        sec,
        sec,
        jax.ShapeDtypeStruct((bsz, SECTION, s), BF16),
        jax.ShapeDtypeStruct((bsz, s, hk), BF16),
    )
    const = lambda shape: pl.BlockSpec(shape, lambda b, t: (0, 0))
    return pl.pallas_call(
        _in_proj_kernel,
        grid=(bsz, s // tm),
        in_specs=[
            act(d),
            pl.BlockSpec((1, 1, d), lambda b, t: (b, 0, 0)),
            pl.BlockSpec((1, 1, d), lambda b, t: (b, 0, 1)),
            const((1, d)), const((d, ncol)), const((RANK_PAD, hk)), const((1, hk)),
            const((1, SECTION)), const((1, SECTION)),
        ],
        out_specs=(act(SECTION), act(SECTION), act(SECTION), act(SECTION), act(SECTION), act(SECTION),
                   pl.BlockSpec((1, SECTION, tm), lambda b, t: (b, 0, t)), act(hk)),
        out_shape=out_shapes,
        scratch_shapes=[pltpu.VMEM((SECTION + GLA_RANK, d), BF16)],
        compiler_params=pltpu.CompilerParams(
            dimension_semantics=("arbitrary", "arbitrary"), vmem_limit_bytes=VMEM_LIMIT),
        name="in_proj",
    )(x, mod3, mod3, norm_gain, w_in_r, wup_pad, b_gate, gla_gate_gain, diff_gate_gain)


def _gla_out_kernel(qk_ref, v_ref, za_ref, la_ref, tril_ref, chunk_ind_ref, mean_bd_ref,
                    od_ref, x_ref, gate_ref, w_ref, fg_ref, o_ref, state_ref):
    tg = qk_ref.shape[1]
    half = od_ref.shape[2]
    hk = GLA_HEADS * GLA_DK
    c_sz = GLA_CHUNK
    n_ch = tg // c_sz

    @pl.when(pl.program_id(1) == 0)
    def _():
        state_ref[...] = jnp.zeros_like(state_ref)

    la = la_ref[0]
    grp = tril_ref.shape[0]
    b_all = jnp.concatenate(
        [jnp.dot(tril_ref[...], la[g * grp:(g + 1) * grp], preferred_element_type=F32)
         for g in range(tg // grp)], axis=0)
    b_last = jnp.concatenate(
        [jnp.broadcast_to(b_all[(ch + 1) * c_sz - 1:(ch + 1) * c_sz], (c_sz, hk)) for ch in range(n_ch)], axis=0)
    q = qk_ref[0, :, 0:hk]
    k = qk_ref[0, :, hk:2 * hk]
    q_in = q * jnp.exp(b_all).astype(BF16)
    k_in = k * jnp.exp(-b_all).astype(BF16)
    k_st = k * jnp.exp(b_last - b_all).astype(BF16)

    lane_head = lax.broadcasted_iota(jnp.int32, (1, hk), 1) // GLA_DK
    ri = lax.broadcasted_iota(jnp.int32, (GLA_HEADS * c_sz, c_sz), 0) % c_sz
    ci = lax.broadcasted_iota(jnp.int32, (GLA_HEADS * c_sz, c_sz), 1)
    causal = ci <= ri
    dec_all = jnp.concatenate(
        [jnp.exp(lax.dot_general(la[g * grp:(g + 1) * grp], chunk_ind_ref[...], (((0,), (0,)), ((), ())),
                                 preferred_element_type=F32)) for g in range(tg // grp)], axis=1)

    mix_diff = jnp.dot(od_ref[0], w_ref[half:2 * half, :], preferred_element_type=F32)

    chunk_rows = [slice(ch * c_sz, (ch + 1) * c_sz) for ch in range(n_ch)]
    head_rows = [slice(hh * c_sz, (hh + 1) * c_sz) for hh in range(GLA_HEADS)]
    head_cols = [slice(hh * GLA_DV, (hh + 1) * GLA_DV) for hh in range(GLA_HEADS)]
    vs = [v_ref[0, rows, :] for rows in chunk_rows]

    us = []
    for ch, rows in enumerate(chunk_rows):
        kst_t = k_st[rows].T
        us.append(jnp.concatenate(
            [jnp.dot(kst_t[hr], vs[ch][:, vc], preferred_element_type=F32)
             for hr, vc in zip(head_rows, head_cols)], axis=0))
    state = state_ref[...]
    states = []
    for ch in range(n_ch):
        states.append(state.astype(BF16))
        state = dec_all[:, ch * GLA_DV:(ch + 1) * GLA_DV] * state + us[ch]
    state_ref[...] = state
    boths = []
    for ch, rows in enumerate(chunk_rows):
        q_c = q_in[rows]
        qm = jnp.concatenate(
            [jnp.where(lane_head == hh, q_c, jnp.zeros_like(q_c)) for hh in range(GLA_HEADS)], axis=0)
        rhs = jnp.concatenate([states[ch], k_in[rows].T], axis=1)
        boths.append(jnp.dot(qm, rhs, preferred_element_type=F32))
    o_rows = []
    for ch in range(n_ch):
        inter = boths[ch][:, 0:GLA_DV]
        p = jnp.where(causal, boths[ch][:, GLA_DV:GLA_DV + c_sz], 0.0).astype(BF16)
        o_rows.append(jnp.concatenate(
            [jnp.dot(p[hr], vs[ch][:, vc], preferred_element_type=F32) + inter[hr]
             for hr, vc in zip(head_rows, head_cols)], axis=1))

    o = jnp.concatenate(o_rows, axis=0)
    ms = jnp.dot((o * o).astype(BF16), mean_bd_ref[...], preferred_element_type=F32)
    o_gla = (o * lax.rsqrt(ms + EPS) * za_ref[0].astype(F32)).astype(BF16)

    mixw = jnp.dot(o_gla, w_ref[0:half, :], preferred_element_type=F32) + mix_diff
    xn = x_ref[0] + gate_ref[0] * mixw
    o_ref[0] = xn * lax.rsqrt(jnp.mean(xn * xn, axis=-1, keepdims=True) + EPS) * fg_ref[...]


def _gla_out(gqk, gv, gza, la, o_diff, x, mod3, w_out_bf, final_gain):
    bsz, s, d = x.shape
    tg = ROWS_GLA
    hk = GLA_HEADS * GLA_DK
    width = GLA_HEADS * GLA_DV
    grp = GLA_CUMSUM_ROWS
    idx = np.arange(grp)
    tril = ((idx[:, None] // GLA_CHUNK == idx[None, :] // GLA_CHUNK) & (idx[None, :] <= idx[:, None]))
    chunk_ind = idx[:, None] // GLA_CHUNK == np.arange(grp // GLA_CHUNK * GLA_DV)[None, :] // GLA_DV
    col = np.arange(width)
    mean_bd = (col[:, None] // GLA_DV == col[None, :] // GLA_DV) / GLA_DV
    act = lambda w: pl.BlockSpec((1, tg, w), lambda b, t: (b, t, 0))
    const = lambda shape: pl.BlockSpec(shape, lambda b, t: (0, 0))
    return pl.pallas_call(
        _gla_out_kernel,
        grid=(bsz, s // tg),
        in_specs=[act(2 * hk), act(width), act(width), act(hk),
                  const(tril.shape), const(chunk_ind.shape), const((width, width)),
                  act(o_diff.shape[2]), act(d),
                  pl.BlockSpec((1, 1, d), lambda b, t: (b, 0, 2)),
                  const(w_out_bf.shape), const((1, d))],
        out_specs=act(d),
        out_shape=jax.ShapeDtypeStruct((bsz, s, d), F32),
        scratch_shapes=[pltpu.VMEM((hk, GLA_DV), F32)],
        compiler_params=pltpu.CompilerParams(
            dimension_semantics=("arbitrary", "arbitrary"), vmem_limit_bytes=VMEM_LIMIT),
        name="gla_out_proj",
    )(gqk, gv, gza, la, jnp.asarray(tril, BF16), jnp.asarray(chunk_ind, BF16),
      jnp.asarray(mean_bd, BF16), o_diff, x, mod3, w_out_bf, final_gain)


def _diff_lambda(lq1_ref, lk1_ref, lq2_ref, lk2_ref, lam_init):
    a = jnp.sum(lq1_ref[...] * lk1_ref[...], axis=-1, keepdims=True)
    b = jnp.sum(lq2_ref[...] * lk2_ref[...], axis=-1, keepdims=True)
    return jnp.exp(a) - jnp.exp(b) + lam_init


def _attn_spans(nq, tk):
    spans = []
    for qi in range(nq):
        m = ATTN_SPAN_TILES
        tile = [(qi, m * tk * j, m * tk, None) for j in range(qi // m)]
        rest = qi % m
        tile.append((qi, (qi - rest) * tk, (rest + 1) * tk, rest * tk))
        for i, sp in enumerate(tile):
            spans.append(sp + (i == 0, i == len(tile) - 1))
    return spans


def _attn_bias_tables(tq, tk):
    def split(x, terms):
        parts = []
        for _ in range(terms):
            hi = x.astype(jnp.bfloat16)
            parts.append(hi)
            x = x - hi.astype(np.float32)
        return parts

    key = np.arange(ATTN_SPAN_TILES * tk, dtype=np.float32)
    kaug = np.zeros((ATTN_SPAN_TILES * tk, LANES), np.float32)
    kaug[:, 0] = kaug[:, 1] = key % BF16_EXACT_INT
    kaug[:, 2] = kaug[:, 3] = key - key % BF16_EXACT_INT
    kaug[:, 4:7] = 1.0
    slopes = (2.0 ** (-8.0 * np.arange(1, DIFF_HEADS + 1) / DIFF_HEADS) * LOG2E).astype(np.float32)
    qry = np.arange(tq, dtype=np.float32)
    qaug = np.zeros((DIFF_HEADS, tq, LANES), jnp.bfloat16)
    for hh in range(DIFF_HEADS):
        s_hi, s_lo = split(slopes[hh:hh + 1], 2)
        qaug[hh, :, 0] = qaug[hh, :, 2] = s_hi
        qaug[hh, :, 1] = qaug[hh, :, 3] = s_lo
        for lane, part in zip((4, 5, 6), split(-slopes[hh] * qry, 3)):
            qaug[hh, :, lane] = part
    slope_rows = np.broadcast_to(slopes[:, None, None], (DIFF_HEADS, 1, LANES))
    return jnp.asarray(kaug, BF16), jnp.asarray(qaug), jnp.asarray(slope_rows, F32)


def _attn_kernel(q_ref, k_ref, vt_ref, z_ref, kaug_ref, qaug_ref, slope_ref,
                 lq1_ref, lk1_ref, lq2_ref, lk2_ref,
                 o_ref, km_ref, s_ref, mb_ref, acc_ref, *, lam_init):
    tq = ATTN_TQ
    tk = ATTN_TK
    hw = 2 * DIFF_DH
    nq = q_ref.shape[1] // tq
    n_heads = q_ref.shape[2] // hw
    n_buf = ATTN_AHEAD + 1

    first_half = lax.broadcasted_iota(jnp.int32, (1, hw), 1) < DIFF_DH
    map_cols = lax.broadcasted_iota(jnp.int32, (1, 2 * tq), 1) < tq
    r_i = lax.broadcasted_iota(jnp.int32, (hw, LANES), 0)
    c_i = lax.broadcasted_iota(jnp.int32, (hw, LANES), 1)
    half_sel = jnp.where(r_i // DIFF_DH == c_i, 1.0, 0.0).astype(BF16)
    half_rows = (lax.broadcasted_iota(jnp.int32, (8, hw), 1) // DIFF_DH
                 == lax.broadcasted_iota(jnp.int32, (8, hw), 0)).astype(BF16)

    def rel_pos(nk):
        key = lax.broadcasted_iota(jnp.int32, (nk, 2 * tq), 0)
        qry = lax.broadcasted_iota(jnp.int32, (nk, 2 * tq), 1) % tq
        return qry - key

    lam = _diff_lambda(lq1_ref, lk1_ref, lq2_ref, lk2_ref, lam_init)
    spans = _attn_spans(nq, tk)

    class Head:
        def __init__(self, j):
            self.j = j
            self.lanes = slice(j * hw, (j + 1) * hw)
            self.slope = slope_ref[j][:, 0:1]
            k_all = k_ref[0, :, self.lanes]
            zk = jnp.zeros_like(k_all)
            km_ref[j, 0] = jnp.where(first_half, k_all, zk)
            km_ref[j, 1] = jnp.where(first_half, zk, k_all)
            k_norm2 = jnp.max(jnp.dot(k_all * k_all, half_sel, preferred_element_type=F32), axis=0, keepdims=True)
            k_max = jnp.sqrt(k_norm2)
            k_max_row = jnp.where(map_cols, k_max[:, 0:1], k_max[:, 1:2])
            self.bounds = []
            for qi in range(nq):
                q = q_ref[0, qi * tq:(qi + 1) * tq, self.lanes]
                q_half2 = lax.dot_general(half_rows, q * q, (((1,), (1,)), ((), ())), preferred_element_type=F32)
                q_norm2 = jnp.concatenate([q_half2[0:1], q_half2[1:2]], axis=1)
                self.bounds.append(jnp.sqrt(q_norm2) * k_max_row * BOUND_SLACK)

        def span_offset(self, qi, start):
            return -self.slope * float(qi * tq - start)

        def masked_scores(self, t):
            qi, start, nk, diag_off, _, _ = spans[t]
            qrhs = jnp.concatenate([q_ref[0, qi * tq:(qi + 1) * tq, self.lanes], qaug_ref[self.j]], axis=1)
            s_maps = []
            for m in range(2):
                kblk = jnp.concatenate([km_ref[self.j, m, start:start + nk, :], kaug_ref[0:nk, :]], axis=1)
                s_maps.append(lax.dot_general(kblk, qrhs, (((1,), (1,)), ((), ())), preferred_element_type=F32))
            s = jnp.concatenate(s_maps, axis=1)
            if diag_off is not None:
                keep = rel_pos(tk) + (diag_off - (nk - tk)) >= 0
                tail = jnp.where(keep, s[nk - tk:nk], -jnp.inf)
                s = tail if nk == tk else jnp.concatenate([s[0:nk - tk], tail], axis=0)
            return s

        def values_and_ones(self, t):
            _, start, nk, _, _, _ = spans[t]
            vt = vt_ref[0, self.j * DIFF_DV:(self.j + 1) * DIFF_DV, start:start + nk]
            return jnp.concatenate([vt, jnp.ones((SUM_ROWS, nk), BF16)], axis=0)

        def finish(self, qi, acc):
            rows = slice(qi * tq, (qi + 1) * tq)
            o_all = acc[0:DIFF_DV, :] / acc[DIFF_DV:DIFF_DV + 1, :]
            o = (o_all[:, :tq] - lam * o_all[:, tq:]).T
            y = o * lax.rsqrt(jnp.mean(o * o, axis=-1, keepdims=True) + EPS)
            y = y * z_ref[0, rows, self.lanes].astype(F32)
            o_ref[0, rows, self.lanes] = y.astype(o_ref.dtype)

        def exact_path(self):
            def scores(t):
                qi, start, nk, _, _, _ = spans[t]
                s = self.masked_scores(t)
                s_ref[t % n_buf, 0:nk, :] = s
                mb_ref[t % n_buf] = jnp.max(s, axis=0, keepdims=True) + self.span_offset(qi, start)

            def update(t, m_old):
                qi, start, nk, _, first, last = spans[t]
                s = s_ref[t % n_buf, 0:nk, :]
                m_blk = mb_ref[t % n_buf]
                m_new = m_blk if first else jnp.maximum(m_old, m_blk)
                p = jnp.exp2(s - (m_new - self.span_offset(qi, start))).astype(BF16)
                pv = jnp.dot(self.values_and_ones(t), p, preferred_element_type=F32)
                if first:
                    acc_ref[...] = pv
                else:
                    acc_ref[...] = jnp.exp2(m_old - m_new) * acc_ref[...] + pv
                if last:
                    self.finish(qi, acc_ref[...])
                return m_new

            for t in range(min(ATTN_AHEAD, len(spans))):
                scores(t)
            m = None
            for t in range(len(spans)):
                if t + ATTN_AHEAD < len(spans):
                    scores(t + ATTN_AHEAD)
                m = update(t, m)

    heads = [Head(j) for j in range(n_heads)]

    acc = [None] * n_heads
    denom_min = [None] * n_heads
    s_next = [h.masked_scores(0) for h in heads]
    for t, (qi, start, _, _, first, last) in enumerate(spans):
        for j, h in enumerate(heads):
            s = s_next[j]
            if t + 1 < len(spans):
                s_next[j] = h.masked_scores(t + 1)
            p = jnp.exp2(s - (h.bounds[qi] - h.span_offset(qi, start))).astype(BF16)
            pv = jnp.dot(h.values_and_ones(t), p, preferred_element_type=F32)
            acc[j] = pv if first else acc[j] + pv
            if last:
                denom = acc[j][DIFF_DV:DIFF_DV + 1, :]
                denom_min[j] = denom if denom_min[j] is None else jnp.minimum(denom_min[j], denom)
                h.finish(qi, acc[j])

    for j, h in enumerate(heads):
        fast_ok = jnp.min(denom_min[j]) >= FAST_PATH_MIN_DENOM
        pl.when(jnp.logical_not(fast_ok))(h.exact_path)


def _diff_attn(dq, dk, dvt, dz, lq1, lk1, lq2, lk2, lam_init):
    bsz, s, width = dq.shape
    tq = ATTN_TQ
    assert ATTN_TQ == ATTN_TK
    hw = 2 * DIFF_DH
    nh = ATTN_HEADS_PER_STEP
    rowspec = pl.BlockSpec((1, s, nh * hw), lambda b, h: (b, 0, h))
    vtspec = pl.BlockSpec((1, nh * DIFF_DV, s), lambda b, h: (b, h, 0))
    lspec = pl.BlockSpec((1, DIFF_DH), lambda b, h: (0, 0))
    kaug, qaug, slope_rows = _attn_bias_tables(tq, ATTN_TK)
    per_head = lambda rows: pl.BlockSpec((nh, rows, LANES), lambda b, h: (h, 0, 0))
    return pl.pallas_call(
        functools.partial(_attn_kernel, lam_init=lam_init),
        grid=(bsz, DIFF_HEADS // nh),
        in_specs=[rowspec, rowspec, vtspec, rowspec,
                  pl.BlockSpec(kaug.shape, lambda b, h: (0, 0)), per_head(tq), per_head(1),
                  lspec, lspec, lspec, lspec],
        out_specs=rowspec,
        out_shape=jax.ShapeDtypeStruct((bsz, s, width), BF16),
        scratch_shapes=[pltpu.VMEM((nh, 2, s, hw), BF16),
                        pltpu.VMEM((ATTN_AHEAD + 1, ATTN_SPAN_TILES * ATTN_TK, 2 * tq), F32),
                        pltpu.VMEM((ATTN_AHEAD + 1, 1, 2 * tq), F32),
                        pltpu.VMEM((DIFF_DV + SUM_ROWS, 2 * tq), F32)],
        compiler_params=pltpu.CompilerParams(
            dimension_semantics=("arbitrary", "arbitrary"), vmem_limit_bytes=VMEM_LIMIT),
        name="diff_attn",
    )(dq, dk, dvt, dz, kaug, qaug, slope_rows, lq1, lk1, lq2, lk2)


def _w_in_layout():
    hk = GLA_HEADS * GLA_DK
    names = ("gq", "gk", "gv", "gz", "gr", "dq", "dk", "dv", "dz")
    sizes = (hk, hk, SECTION, SECTION, GLA_RANK, SECTION, SECTION, SECTION, SECTION)
    src = dict(zip(names, np.concatenate([[0], np.cumsum(sizes)[:-1]]).tolist()))
    width = dict(zip(names, sizes))
    scale = {"gq": GLA_DK ** -0.5, "dq": DIFF_DH ** -0.5 * LOG2E}
    pieces, dst = [], 0
    for name in ("gq", "gk", "gv", "gz", "dq", "dk", "dz", "dv", "gr"):
        pieces.append((src[name], dst, width[name], scale.get(name)))
        dst += width[name]
    return pieces, dst + RANK_PAD - GLA_RANK


def _w_prep_kernel(wt_ref, o_ref):
    pieces, _ = _w_in_layout()
    for src, dst, width, scale in pieces:
        piece = wt_ref[0, src:src + width, :]
        if scale is not None:
            piece = piece * scale
        if width < LANES:
            piece = jnp.concatenate([piece, jnp.zeros((LANES - width, piece.shape[1]), piece.dtype)], axis=0)
        o_ref[:, dst:dst + piece.shape[0]] = piece.T.astype(o_ref.dtype)


def _prep_w_in(w_in, layer):
    _, d, n_src = w_in.shape
    _, n_dst = _w_in_layout()
    rows = 256
    return pl.pallas_call(
        _w_prep_kernel,
        grid=(d // rows,),
        in_specs=[pl.BlockSpec((1, n_src, rows), lambda i: (layer, 0, i))],
        out_specs=pl.BlockSpec((rows, n_dst), lambda i: (i, 0)),
        out_shape=jax.ShapeDtypeStruct((d, n_dst), BF16),
        compiler_params=pltpu.CompilerParams(dimension_semantics=("arbitrary",), vmem_limit_bytes=VMEM_LIMIT),
        name="w_in_prep",
    )(jnp.swapaxes(w_in, 1, 2))


def kernel(x, c, w_ada, b_ada, norm_gain, w_in, w_gla_gate_up, b_gla_gate, gla_out_gain,
           lambda_q1, lambda_k1, lambda_q2, lambda_k2, diff_out_gain, w_out, final_gain):
    bsz, s, d = x.shape
    depth = w_in.shape[0]
    assert depth == 1, "gla_out_proj applies the final rmsnorm, so exactly one layer is supported"
    for l in range(depth):
        mod = _adaln_mod(c, w_ada, b_ada, l)
        mod3 = mod.reshape(bsz, 1, 3 * d)
        w_in_r = _prep_w_in(w_in, l)
        wup_pad = jnp.pad(w_gla_gate_up[l], ((0, RANK_PAD - GLA_RANK), (0, 0))).astype(BF16)
        lam_init = float(0.8 - 0.6 * np.exp(-0.3 * l))
        gqk, gv, gz, dq, dk, dz, dvt, la = _in_proj(
            x, mod3, norm_gain[l].reshape(1, d), w_in_r, wup_pad, b_gla_gate[l].reshape(1, -1),
            gla_out_gain[l].reshape(1, -1), (diff_out_gain[l] * (1.0 - lam_init)).reshape(1, -1))
        o_diff = _diff_attn(dq, dk, dvt, dz,
                            lambda_q1[l].reshape(1, -1), lambda_k1[l].reshape(1, -1),
                            lambda_q2[l].reshape(1, -1), lambda_k2[l].reshape(1, -1), lam_init)
        x = _gla_out(gqk, gv, gz, la, o_diff, x, mod3, w_out[l].astype(BF16), final_gain.reshape(1, d))
    return x
```

```python
import functools
import math

import jax
import jax.numpy as jnp
import numpy as np
from jax import lax
from jax.experimental import pallas as pl
from jax.experimental.pallas import tpu as pltpu

F32 = jnp.float32
BF16 = jnp.bfloat16

EPS = 1e-6
LOG2E = math.log2(math.e)
GLA_HEADS = 4
GLA_DK = 64
GLA_DV = 128
GLA_RANK = 16
GLA_GATE_NORM = 16.0
GLA_CHUNK = 64
DIFF_HEADS = 4
DIFF_DH = 64
DIFF_DV = 128
LANES = 128
RANK_PAD = LANES
SECTION = 512
SUM_ROWS = 16

ROWS_IN_PROJ = 1024
ROWS_GLA = 1024
GLA_CUMSUM_ROWS = 256
ATTN_TQ = 256
ATTN_TK = 256
ATTN_HEADS_PER_STEP = 2
ATTN_SPAN_TILES = 8
ATTN_AHEAD = 2
BF16_EXACT_INT = 256
BOUND_SLACK = 1.01
FAST_PATH_MIN_DENOM = 2.0 ** -90
VMEM_LIMIT = 48 * 1024 * 1024


def _silu(v):
    return v / (1.0 + jnp.exp(-v))


def _log_sigmoid(v):
    return jnp.minimum(v, 0.0) - jnp.log(1.0 + jnp.exp(-jnp.abs(v)))


def _adaln_kernel(c_ref, w_ref, b_ref, o_ref):
    sc = _silu(c_ref[...]).astype(BF16)
    o_ref[...] = jnp.dot(sc, w_ref[0].astype(BF16), preferred_element_type=F32) + b_ref[0]


def _adaln_mod(c, w_ada, b_ada, layer):
    bsz, d = c.shape
    n = w_ada.shape[2]
    tn = 1024
    return pl.pallas_call(
        _adaln_kernel,
        grid=(n // tn,),
        in_specs=[
            pl.BlockSpec((bsz, d), lambda j: (0, 0)),
            pl.BlockSpec((1, d, tn), lambda j: (layer, 0, j)),
            pl.BlockSpec((1, 1, tn), lambda j: (layer, 0, j)),
        ],
        out_specs=pl.BlockSpec((bsz, tn), lambda j: (0, j)),
        out_shape=jax.ShapeDtypeStruct((bsz, n), F32),
        compiler_params=pltpu.CompilerParams(dimension_semantics=("arbitrary",)),
        name="adaln_mod",
    )(c, w_ada, b_ada.reshape(b_ada.shape[0], 1, n))


def _in_proj_kernel(x_ref, shift_ref, scale_ref, gain_ref, w_ref, wup_ref, bg_ref, ggain_ref, dgain_ref,
                    gqk_ref, gv_ref, gz_ref, dq_ref, dk_ref, dz_ref, dvt_ref, la_ref,
                    wvt_ref):
    n_sec = 7
    dv_sec = 6

    @pl.when((pl.program_id(0) == 0) & (pl.program_id(1) == 0))
    def _():
        wvt_ref[0:SECTION, :] = w_ref[:, dv_sec * SECTION:(dv_sec + 1) * SECTION].T
        wvt_ref[SECTION:SECTION + GLA_RANK, :] = (
            w_ref[:, n_sec * SECTION:n_sec * SECTION + RANK_PAD].T[0:GLA_RANK, :])

    g = gain_ref[...] * (1.0 + scale_ref[0])
    tm = x_ref.shape[1]
    half = tm // 2

    def modulated(rows):
        x = x_ref[0, rows, :]
        rstd = lax.rsqrt(jnp.mean(x * x, axis=-1, keepdims=True) + EPS)
        return (x * rstd * g + shift_ref[0]).astype(BF16)

    h_halves = [modulated(slice(0, half)), modulated(slice(half, tm))]
    for r, h_half in enumerate(h_halves):
        gqk_ref[0, r * half:(r + 1) * half, :] = jnp.dot(
            h_half, w_ref[:, 0:SECTION], preferred_element_type=F32).astype(gqk_ref.dtype)
    h = jnp.concatenate(h_halves, axis=0)

    outs = (gqk_ref, gv_ref, gz_ref, dq_ref, dk_ref, dz_ref)

    def section(j):
        sec = jnp.dot(h, w_ref[:, j * SECTION:(j + 1) * SECTION], preferred_element_type=F32)
        if outs[j] is gz_ref:
            sec = _silu(sec) * ggain_ref[...]
        elif outs[j] is dz_ref:
            sec = _silu(sec) * dgain_ref[...]
        outs[j][0] = sec.astype(outs[j].dtype)

    section(2)
    section(5)

    vt_gr = lax.dot_general(wvt_ref[...], h, (((1,), (1,)), ((), ())), preferred_element_type=F32)
    dvt_ref[0] = vt_gr[0:SECTION].astype(dvt_ref.dtype)
    tm = vt_gr.shape[1]
    gr_t = jnp.concatenate([vt_gr[SECTION:SECTION + GLA_RANK].astype(BF16),
                            jnp.zeros((RANK_PAD - GLA_RANK, tm), BF16)], axis=0)
    logit = lax.dot_general(gr_t, wup_ref[...], (((0,), (0,)), ((), ())),
                            preferred_element_type=F32) + bg_ref[...]
    la_ref[0] = (_log_sigmoid(logit) * (1.0 / GLA_GATE_NORM)).astype(la_ref.dtype)

    for j in (1, 3, 4):
        section(j)


def _in_proj(x, mod3, norm_gain, w_in_r, wup_pad, b_gate, gla_gate_gain, diff_gate_gain):
    bsz, s, d = x.shape
    tm = ROWS_IN_PROJ
    ncol = w_in_r.shape[1]
    hk = GLA_HEADS * GLA_DK
    act = lambda width: pl.BlockSpec((1, tm, width), lambda b, t: (b, t, 0))
    sec = jax.ShapeDtypeStruct((bsz, s, SECTION), BF16)
    out_shapes = (
        sec,
        sec,
        sec,
        sec,
        sec,
        sec,
        jax.ShapeDtypeStruct((bsz, SECTION, s), BF16),
        jax.ShapeDtypeStruct((bsz, s, hk), BF16),
    )
    const = lambda shape: pl.BlockSpec(shape, lambda b, t: (0, 0))
    return pl.pallas_call(
        _in_proj_kernel,
        grid=(bsz, s // tm),
        in_specs=[
            act(d),
            pl.BlockSpec((1, 1, d), lambda b, t: (b, 0, 0)),
            pl.BlockSpec((1, 1, d), lambda b, t: (b, 0, 1)),
            const((1, d)), const((d, ncol)), const((RANK_PAD, hk)), const((1, hk)),
            const((1, SECTION)), const((1, SECTION)),
        ],
        out_specs=(act(SECTION), act(SECTION), act(SECTION), act(SECTION), act(SECTION), act(SECTION),
                   pl.BlockSpec((1, SECTION, tm), lambda b, t: (b, 0, t)), act(hk)),
        out_shape=out_shapes,
        scratch_shapes=[pltpu.VMEM((SECTION + GLA_RANK, d), BF16)],
        compiler_params=pltpu.CompilerParams(
            dimension_semantics=("arbitrary", "arbitrary"), vmem_limit_bytes=VMEM_LIMIT),
        name="in_proj",
    )(x, mod3, mod3, norm_gain, w_in_r, wup_pad, b_gate, gla_gate_gain, diff_gate_gain)


def _gla_out_kernel(qk_ref, v_ref, za_ref, la_ref, tril_ref, chunk_ind_ref, mean_bd_ref,
                    od_ref, x_ref, gate_ref, w_ref, fg_ref, o_ref, state_ref):
    tg = qk_ref.shape[1]
    half = od_ref.shape[2]
    hk = GLA_HEADS * GLA_DK
    c_sz = GLA_CHUNK
    n_ch = tg // c_sz

    @pl.when(pl.program_id(1) == 0)
    def _():
        state_ref[...] = jnp.zeros_like(state_ref)

    la = la_ref[0]
    grp = tril_ref.shape[0]
    b_all = jnp.concatenate(
        [jnp.dot(tril_ref[...], la[g * grp:(g + 1) * grp], preferred_element_type=F32)
         for g in range(tg // grp)], axis=0)
    b_last = jnp.concatenate(
        [jnp.broadcast_to(b_all[(ch + 1) * c_sz - 1:(ch + 1) * c_sz], (c_sz, hk)) for ch in range(n_ch)], axis=0)
    q = qk_ref[0, :, 0:hk]
    k = qk_ref[0, :, hk:2 * hk]
    q_in = q * jnp.exp(b_all).astype(BF16)
    k_in = k * jnp.exp(-b_all).astype(BF16)
    k_st = k * jnp.exp(b_last - b_all).astype(BF16)

    lane_head = lax.broadcasted_iota(jnp.int32, (1, hk), 1) // GLA_DK
    ri = lax.broadcasted_iota(jnp.int32, (GLA_HEADS * c_sz, c_sz), 0) % c_sz
    ci = lax.broadcasted_iota(jnp.int32, (GLA_HEADS * c_sz, c_sz), 1)
    causal = ci <= ri
    dec_all = jnp.concatenate(
        [jnp.exp(lax.dot_general(la[g * grp:(g + 1) * grp], chunk_ind_ref[...], (((0,), (0,)), ((), ())),
                                 preferred_element_type=F32)) for g in range(tg // grp)], axis=1)

    mix_diff = jnp.dot(od_ref[0], w_ref[half:2 * half, :], preferred_element_type=F32)

    chunk_rows = [slice(ch * c_sz, (ch + 1) * c_sz) for ch in range(n_ch)]
    head_rows = [slice(hh * c_sz, (hh + 1) * c_sz) for hh in range(GLA_HEADS)]
    head_cols = [slice(hh * GLA_DV, (hh + 1) * GLA_DV) for hh in range(GLA_HEADS)]
    vs = [v_ref[0, rows, :] for rows in chunk_rows]

    us = []
    for ch, rows in enumerate(chunk_rows):
        kst_t = k_st[rows].T
        us.append(jnp.concatenate(
            [jnp.dot(kst_t[hr], vs[ch][:, vc], preferred_element_type=F32)
             for hr, vc in zip(head_rows, head_cols)], axis=0))
    state = state_ref[...]
    states = []
    for ch in range(n_ch):
        states.append(state.astype(BF16))
        state = dec_all[:, ch * GLA_DV:(ch + 1) * GLA_DV] * state + us[ch]
    state_ref[...] = state
    boths = []
    for ch, rows in enumerate(chunk_rows):
        q_c = q_in[rows]
        qm = jnp.concatenate(
            [jnp.where(lane_head == hh, q_c, jnp.zeros_like(q_c)) for hh in range(GLA_HEADS)], axis=0)
        rhs = jnp.concatenate([states[ch], k_in[rows].T], axis=1)
        boths.append(jnp.dot(qm, rhs, preferred_element_type=F32))
    o_rows = []
    for ch in range(n_ch):
        inter = boths[ch][:, 0:GLA_DV]
        p = jnp.where(causal, boths[ch][:, GLA_DV:GLA_DV + c_sz], 0.0).astype(BF16)
        o_rows.append(jnp.concatenate(
            [jnp.dot(p[hr], vs[ch][:, vc], preferred_element_type=F32) + inter[hr]
             for hr, vc in zip(head_rows, head_cols)], axis=1))

    o = jnp.concatenate(o_rows, axis=0)
    ms = jnp.dot((o * o).astype(BF16), mean_bd_ref[...], preferred_element_type=F32)
    o_gla = (o * lax.rsqrt(ms + EPS) * za_ref[0].astype(F32)).astype(BF16)

    mixw = jnp.dot(o_gla, w_ref[0:half, :], preferred_element_type=F32) + mix_diff
    xn = x_ref[0] + gate_ref[0] * mixw
    o_ref[0] = xn * lax.rsqrt(jnp.mean(xn * xn, axis=-1, keepdims=True) + EPS) * fg_ref[...]


def _gla_out(gqk, gv, gza, la, o_diff, x, mod3, w_out_bf, final_gain):
    bsz, s, d = x.shape
    tg = ROWS_GLA
    hk = GLA_HEADS * GLA_DK
    width = GLA_HEADS * GLA_DV
    grp = GLA_CUMSUM_ROWS
    idx = np.arange(grp)
    tril = ((idx[:, None] // GLA_CHUNK == idx[None, :] // GLA_CHUNK) & (idx[None, :] <= idx[:, None]))
    chunk_ind = idx[:, None] // GLA_CHUNK == np.arange(grp // GLA_CHUNK * GLA_DV)[None, :] // GLA_DV
    col = np.arange(width)
    mean_bd = (col[:, None] // GLA_DV == col[None, :] // GLA_DV) / GLA_DV
    act = lambda w: pl.BlockSpec((1, tg, w), lambda b, t: (b, t, 0))
    const = lambda shape: pl.BlockSpec(shape, lambda b, t: (0, 0))
    return pl.pallas_call(
        _gla_out_kernel,
        grid=(bsz, s // tg),
        in_specs=[act(2 * hk), act(width), act(width), act(hk),
                  const(tril.shape), const(chunk_ind.shape), const((width, width)),
                  act(o_diff.shape[2]), act(d),
                  pl.BlockSpec((1, 1, d), lambda b, t: (b, 0, 2)),
                  const(w_out_bf.shape), const((1, d))],
        out_specs=act(d),
        out_shape=jax.ShapeDtypeStruct((bsz, s, d), F32),
        scratch_shapes=[pltpu.VMEM((hk, GLA_DV), F32)],
        compiler_params=pltpu.CompilerParams(
            dimension_semantics=("arbitrary", "arbitrary"), vmem_limit_bytes=VMEM_LIMIT),
        name="gla_out_proj",
    )(gqk, gv, gza, la, jnp.asarray(tril, BF16), jnp.asarray(chunk_ind, BF16),
      jnp.asarray(mean_bd, BF16), o_diff, x, mod3, w_out_bf, final_gain)


def _diff_lambda(lq1_ref, lk1_ref, lq2_ref, lk2_ref, lam_init):
    a = jnp.sum(lq1_ref[...] * lk1_ref[...], axis=-1, keepdims=True)
    b = jnp.sum(lq2_ref[...] * lk2_ref[...], axis=-1, keepdims=True)
    return jnp.exp(a) - jnp.exp(b) + lam_init


def _attn_spans(nq, tk):
    spans = []
    for qi in range(nq):
        m = ATTN_SPAN_TILES
        tile = [(qi, m * tk * j, m * tk, None) for j in range(qi // m)]
        rest = qi % m
        tile.append((qi, (qi - rest) * tk, (rest + 1) * tk, rest * tk))
        for i, sp in enumerate(tile):
            spans.append(sp + (i == 0, i == len(tile) - 1))
    return spans


def _attn_bias_tables(tq, tk):
    def split(x, terms):
        parts = []
        for _ in range(terms):
            hi = x.astype(jnp.bfloat16)
            parts.append(hi)
            x = x - hi.astype(np.float32)
        return parts

    key = np.arange(ATTN_SPAN_TILES * tk, dtype=np.float32)
    kaug = np.zeros((ATTN_SPAN_TILES * tk, LANES), np.float32)
    kaug[:, 0] = kaug[:, 1] = key % BF16_EXACT_INT
    kaug[:, 2] = kaug[:, 3] = key - key % BF16_EXACT_INT
    kaug[:, 4:7] = 1.0
    slopes = (2.0 ** (-8.0 * np.arange(1, DIFF_HEADS + 1) / DIFF_HEADS) * LOG2E).astype(np.float32)
    qry = np.arange(tq, dtype=np.float32)
    qaug = np.zeros((DIFF_HEADS, tq, LANES), jnp.bfloat16)
    for hh in range(DIFF_HEADS):
        s_hi, s_lo = split(slopes[hh:hh + 1], 2)
        qaug[hh, :, 0] = qaug[hh, :, 2] = s_hi
        qaug[hh, :, 1] = qaug[hh, :, 3] = s_lo
        for lane, part in zip((4, 5, 6), split(-slopes[hh] * qry, 3)):
            qaug[hh, :, lane] = part
    slope_rows = np.broadcast_to(slopes[:, None, None], (DIFF_HEADS, 1, LANES))
    return jnp.asarray(kaug, BF16), jnp.asarray(qaug), jnp.asarray(slope_rows, F32)


def _attn_kernel(q_ref, k_ref, vt_ref, z_ref, kaug_ref, qaug_ref, slope_ref,
                 lq1_ref, lk1_ref, lq2_ref, lk2_ref,
                 o_ref, km_ref, s_ref, mb_ref, acc_ref, *, lam_init):
    tq = ATTN_TQ
    tk = ATTN_TK
    hw = 2 * DIFF_DH
    nq = q_ref.shape[1] // tq
    n_heads = q_ref.shape[2] // hw
    n_buf = ATTN_AHEAD + 1

    first_half = lax.broadcasted_iota(jnp.int32, (1, hw), 1) < DIFF_DH
    map_cols = lax.broadcasted_iota(jnp.int32, (1, 2 * tq), 1) < tq
    r_i = lax.broadcasted_iota(jnp.int32, (hw, LANES), 0)
    c_i = lax.broadcasted_iota(jnp.int32, (hw, LANES), 1)
    half_sel = jnp.where(r_i // DIFF_DH == c_i, 1.0, 0.0).astype(BF16)
    half_rows = (lax.broadcasted_iota(jnp.int32, (8, hw), 1) // DIFF_DH
                 == lax.broadcasted_iota(jnp.int32, (8, hw), 0)).astype(BF16)

    def rel_pos(nk):
        key = lax.broadcasted_iota(jnp.int32, (nk, 2 * tq), 0)
        qry = lax.broadcasted_iota(jnp.int32, (nk, 2 * tq), 1) % tq
        return qry - key

    lam = _diff_lambda(lq1_ref, lk1_ref, lq2_ref, lk2_ref, lam_init)
    spans = _attn_spans(nq, tk)

    class Head:
        def __init__(self, j):
            self.j = j
            self.lanes = slice(j * hw, (j + 1) * hw)
            self.slope = slope_ref[j][:, 0:1]
            k_all = k_ref[0, :, self.lanes]
            zk = jnp.zeros_like(k_all)
            km_ref[j, 0] = jnp.where(first_half, k_all, zk)
            km_ref[j, 1] = jnp.where(first_half, zk, k_all)
            k_norm2 = jnp.max(jnp.dot(k_all * k_all, half_sel, preferred_element_type=F32), axis=0, keepdims=True)
            k_max = jnp.sqrt(k_norm2)
            k_max_row = jnp.where(map_cols, k_max[:, 0:1], k_max[:, 1:2])
            self.bounds = []
            for qi in range(nq):
                q = q_ref[0, qi * tq:(qi + 1) * tq, self.lanes]
                q_half2 = lax.dot_general(half_rows, q * q, (((1,), (1,)), ((), ())), preferred_element_type=F32)
                q_norm2 = jnp.concatenate([q_half2[0:1], q_half2[1:2]], axis=1)
                self.bounds.append(jnp.sqrt(q_norm2) * k_max_row * BOUND_SLACK)

        def span_offset(self, qi, start):
            return -self.slope * float(qi * tq - start)

        def masked_scores(self, t):
            qi, start, nk, diag_off, _, _ = spans[t]
            qrhs = jnp.concatenate([q_ref[0, qi * tq:(qi + 1) * tq, self.lanes], qaug_ref[self.j]], axis=1)
            s_maps = []
            for m in range(2):
                kblk = jnp.concatenate([km_ref[self.j, m, start:start + nk, :], kaug_ref[0:nk, :]], axis=1)
                s_maps.append(lax.dot_general(kblk, qrhs, (((1,), (1,)), ((), ())), preferred_element_type=F32))
            s = jnp.concatenate(s_maps, axis=1)
            if diag_off is not None:
                keep = rel_pos(tk) + (diag_off - (nk - tk)) >= 0
                tail = jnp.where(keep, s[nk - tk:nk], -jnp.inf)
                s = tail if nk == tk else jnp.concatenate([s[0:nk - tk], tail], axis=0)
            return s

        def values_and_ones(self, t):
            _, start, nk, _, _, _ = spans[t]
            vt = vt_ref[0, self.j * DIFF_DV:(self.j + 1) * DIFF_DV, start:start + nk]
            return jnp.concatenate([vt, jnp.ones((SUM_ROWS, nk), BF16)], axis=0)

        def finish(self, qi, acc):
            rows = slice(qi * tq, (qi + 1) * tq)
            o_all = acc[0:DIFF_DV, :] / acc[DIFF_DV:DIFF_DV + 1, :]
            o = (o_all[:, :tq] - lam * o_all[:, tq:]).T
            y = o * lax.rsqrt(jnp.mean(o * o, axis=-1, keepdims=True) + EPS)
            y = y * z_ref[0, rows, self.lanes].astype(F32)
            o_ref[0, rows, self.lanes] = y.astype(o_ref.dtype)

        def exact_path(self):
            def scores(t):
                qi, start, nk, _, _, _ = spans[t]
                s = self.masked_scores(t)
                s_ref[t % n_buf, 0:nk, :] = s
                mb_ref[t % n_buf] = jnp.max(s, axis=0, keepdims=True) + self.span_offset(qi, start)

            def update(t, m_old):
                qi, start, nk, _, first, last = spans[t]
                s = s_ref[t % n_buf, 0:nk, :]
                m_blk = mb_ref[t % n_buf]
                m_new = m_blk if first else jnp.maximum(m_old, m_blk)
                p = jnp.exp2(s - (m_new - self.span_offset(qi, start))).astype(BF16)
                pv = jnp.dot(self.values_and_ones(t), p, preferred_element_type=F32)
                if first:
                    acc_ref[...] = pv
                else:
                    acc_ref[...] = jnp.exp2(m_old - m_new) * acc_ref[...] + pv
                if last:
                    self.finish(qi, acc_ref[...])
                return m_new

            for t in range(min(ATTN_AHEAD, len(spans))):
                scores(t)
            m = None
            for t in range(len(spans)):
                if t + ATTN_AHEAD < len(spans):
                    scores(t + ATTN_AHEAD)
                m = update(t, m)

    heads = [Head(j) for j in range(n_heads)]

    acc = [None] * n_heads
    denom_min = [None] * n_heads
    s_next = [h.masked_scores(0) for h in heads]
    for t, (qi, start, _, _, first, last) in enumerate(spans):
        for j, h in enumerate(heads):
            s = s_next[j]
            if t + 1 < len(spans):
                s_next[j] = h.masked_scores(t + 1)
            p = jnp.exp2(s - (h.bounds[qi] - h.span_offset(qi, start))).astype(BF16)
            pv = jnp.dot(h.values_and_ones(t), p, preferred_element_type=F32)
            acc[j] = pv if first else acc[j] + pv
            if last:
                denom = acc[j][DIFF_DV:DIFF_DV + 1, :]
                denom_min[j] = denom if denom_min[j] is None else jnp.minimum(denom_min[j], denom)
                h.finish(qi, acc[j])

    for j, h in enumerate(heads):
        fast_ok = jnp.min(denom_min[j]) >= FAST_PATH_MIN_DENOM
        pl.when(jnp.logical_not(fast_ok))(h.exact_path)


def _diff_attn(dq, dk, dvt, dz, lq1, lk1, lq2, lk2, lam_init):
    bsz, s, width = dq.shape
    tq = ATTN_TQ
    assert ATTN_TQ == ATTN_TK
    hw = 2 * DIFF_DH
    nh = ATTN_HEADS_PER_STEP
    rowspec = pl.BlockSpec((1, s, nh * hw), lambda b, h: (b, 0, h))
    vtspec = pl.BlockSpec((1, nh * DIFF_DV, s), lambda b, h: (b, h, 0))
    lspec = pl.BlockSpec((1, DIFF_DH), lambda b, h: (0, 0))
    kaug, qaug, slope_rows = _attn_bias_tables(tq, ATTN_TK)
    per_head = lambda rows: pl.BlockSpec((nh, rows, LANES), lambda b, h: (h, 0, 0))
    return pl.pallas_call(
        functools.partial(_attn_kernel, lam_init=lam_init),
        grid=(bsz, DIFF_HEADS // nh),
        in_specs=[rowspec, rowspec, vtspec, rowspec,
                  pl.BlockSpec(kaug.shape, lambda b, h: (0, 0)), per_head(tq), per_head(1),
                  lspec, lspec, lspec, lspec],
        out_specs=rowspec,
        out_shape=jax.ShapeDtypeStruct((bsz, s, width), BF16),
        scratch_shapes=[pltpu.VMEM((nh, 2, s, hw), BF16),
                        pltpu.VMEM((ATTN_AHEAD + 1, ATTN_SPAN_TILES * ATTN_TK, 2 * tq), F32),
                        pltpu.VMEM((ATTN_AHEAD + 1, 1, 2 * tq), F32),
                        pltpu.VMEM((DIFF_DV + SUM_ROWS, 2 * tq), F32)],
        compiler_params=pltpu.CompilerParams(
            dimension_semantics=("arbitrary", "arbitrary"), vmem_limit_bytes=VMEM_LIMIT),
        name="diff_attn",
    )(dq, dk, dvt, dz, kaug, qaug, slope_rows, lq1, lk1, lq2, lk2)


def _w_in_layout():
    hk = GLA_HEADS * GLA_DK
    names = ("gq", "gk", "gv", "gz", "gr", "dq", "dk", "dv", "dz")
    sizes = (hk, hk, SECTION, SECTION, GLA_RANK, SECTION, SECTION, SECTION, SECTION)
    src = dict(zip(names, np.concatenate([[0], np.cumsum(sizes)[:-1]]).tolist()))
    width = dict(zip(names, sizes))
    scale = {"gq": GLA_DK ** -0.5, "dq": DIFF_DH ** -0.5 * LOG2E}
    pieces, dst = [], 0
    for name in ("gq", "gk", "gv", "gz", "dq", "dk", "dz", "dv", "gr"):
        pieces.append((src[name], dst, width[name], scale.get(name)))
        dst += width[name]
    return pieces, dst + RANK_PAD - GLA_RANK


def _w_prep_kernel(wt_ref, o_ref):
    pieces, _ = _w_in_layout()
    for src, dst, width, scale in pieces:
        piece = wt_ref[0, src:src + width, :]
        if scale is not None:
            piece = piece * scale
        if width < LANES:
            piece = jnp.concatenate([piece, jnp.zeros((LANES - width, piece.shape[1]), piece.dtype)], axis=0)
        o_ref[:, dst:dst + piece.shape[0]] = piece.T.astype(o_ref.dtype)


def _prep_w_in(w_in, layer):
    _, d, n_src = w_in.shape
    _, n_dst = _w_in_layout()
    rows = 256
    return pl.pallas_call(
        _w_prep_kernel,
        grid=(d // rows,),
        in_specs=[pl.BlockSpec((1, n_src, rows), lambda i: (layer, 0, i))],
        out_specs=pl.BlockSpec((rows, n_dst), lambda i: (i, 0)),
        out_shape=jax.ShapeDtypeStruct((d, n_dst), BF16),
        compiler_params=pltpu.CompilerParams(dimension_semantics=("arbitrary",), vmem_limit_bytes=VMEM_LIMIT),
        name="w_in_prep",
    )(jnp.swapaxes(w_in, 1, 2))


def kernel(x, c, w_ada, b_ada, norm_gain, w_in, w_gla_gate_up, b_gla_gate, gla_out_gain,
           lambda_q1, lambda_k1, lambda_q2, lambda_k2, diff_out_gain, w_out, final_gain):
    bsz, s, d = x.shape
    depth = w_in.shape[0]
    assert depth == 1, "gla_out_proj applies the final rmsnorm, so exactly one layer is supported"
    for l in range(depth):
        mod = _adaln_mod(c, w_ada, b_ada, l)
        mod3 = mod.reshape(bsz, 1, 3 * d)
        w_in_r = _prep_w_in(w_in, l)
        wup_pad = jnp.pad(w_gla_gate_up[l], ((0, RANK_PAD - GLA_RANK), (0, 0))).astype(BF16)
        lam_init = float(0.8 - 0.6 * np.exp(-0.3 * l))
        gqk, gv, gz, dq, dk, dz, dvt, la = _in_proj(
            x, mod3, norm_gain[l].reshape(1, d), w_in_r, wup_pad, b_gla_gate[l].reshape(1, -1),
            gla_out_gain[l].reshape(1, -1), (diff_out_gain[l] * (1.0 - lam_init)).reshape(1, -1))
        o_diff = _diff_attn(dq, dk, dvt, dz,
                            lambda_q1[l].reshape(1, -1), lambda_k1[l].reshape(1, -1),
                            lambda_q2[l].reshape(1, -1), lambda_k2[l].reshape(1, -1), lam_init)
        x = _gla_out(gqk, gv, gz, la, o_diff, x, mod3, w_out[l].astype(BF16), final_gain.reshape(1, d))
    return x
```

```python
import functools
import math

import jax
import jax.numpy as jnp
import numpy as np
from jax import lax
from jax.experimental import pallas as pl
from jax.experimental.pallas import tpu as pltpu

F32 = jnp.float32
BF16 = jnp.bfloat16

EPS = 1e-6
LOG2E = math.log2(math.e)
GLA_HEADS = 4
GLA_DK = 64
GLA_DV = 128
GLA_RANK = 16
GLA_GATE_NORM = 16.0
GLA_CHUNK = 64
DIFF_HEADS = 4
DIFF_DH = 64
DIFF_DV = 128
LANES = 128
RANK_PAD = LANES
SECTION = 512
SUM_ROWS = 16

ROWS_IN_PROJ = 1024
ROWS_GLA = 1024
GLA_CUMSUM_ROWS = 256
ATTN_TQ = 256
ATTN_TK = 256
ATTN_HEADS_PER_STEP = 1
ATTN_SPAN_TILES = 8
ATTN_AHEAD = 2
BF16_EXACT_INT = 256
BOUND_SLACK = 1.01
FAST_PATH_MIN_DENOM = 2.0 ** -90
VMEM_LIMIT = 48 * 1024 * 1024


def _silu(v):
    return v / (1.0 + jnp.exp(-v))


def _log_sigmoid(v):
    return jnp.minimum(v, 0.0) - jnp.log(1.0 + jnp.exp(-jnp.abs(v)))


def _adaln_kernel(c_ref, w_ref, b_ref, o_ref):
    sc = _silu(c_ref[...]).astype(BF16)
    o_ref[...] = jnp.dot(sc, w_ref[0].astype(BF16), preferred_element_type=F32) + b_ref[0]


def _adaln_mod(c, w_ada, b_ada, layer):
    bsz, d = c.shape
    n = w_ada.shape[2]
    tn = 1024
    return pl.pallas_call(
        _adaln_kernel,
        grid=(n // tn,),
        in_specs=[
            pl.BlockSpec((bsz, d), lambda j: (0, 0)),
            pl.BlockSpec((1, d, tn), lambda j: (layer, 0, j)),
            pl.BlockSpec((1, 1, tn), lambda j: (layer, 0, j)),
        ],
        out_specs=pl.BlockSpec((bsz, tn), lambda j: (0, j)),
        out_shape=jax.ShapeDtypeStruct((bsz, n), F32),
        compiler_params=pltpu.CompilerParams(dimension_semantics=("arbitrary",)),
        name="adaln_mod",
    )(c, w_ada, b_ada.reshape(b_ada.shape[0], 1, n))


def _in_proj_kernel(x_ref, shift_ref, scale_ref, gain_ref, w_ref, wup_ref, bg_ref, ggain_ref, dgain_ref,
                    gqk_ref, gv_ref, gz_ref, dq_ref, dk_ref, dz_ref, dvt_ref, la_ref,
                    wvt_ref):
    n_sec = 7
    dv_sec = 6

    @pl.when((pl.program_id(0) == 0) & (pl.program_id(1) == 0))
    def _():
        wvt_ref[0:SECTION, :] = w_ref[:, dv_sec * SECTION:(dv_sec + 1) * SECTION].T
        wvt_ref[SECTION:SECTION + GLA_RANK, :] = (
            w_ref[:, n_sec * SECTION:n_sec * SECTION + RANK_PAD].T[0:GLA_RANK, :])

    g = gain_ref[...] * (1.0 + scale_ref[0])
    tm = x_ref.shape[1]
    half = tm // 2

    def modulated(rows):
        x = x_ref[0, rows, :]
        rstd = lax.rsqrt(jnp.mean(x * x, axis=-1, keepdims=True) + EPS)
        return (x * rstd * g + shift_ref[0]).astype(BF16)

    h_halves = [modulated(slice(0, half)), modulated(slice(half, tm))]
    for r, h_half in enumerate(h_halves):
        gqk_ref[0, r * half:(r + 1) * half, :] = jnp.dot(
            h_half, w_ref[:, 0:SECTION], preferred_element_type=F32).astype(gqk_ref.dtype)
    h = jnp.concatenate(h_halves, axis=0)

    outs = (gqk_ref, gv_ref, gz_ref, dq_ref, dk_ref, dz_ref)

    def section(j):
        sec = jnp.dot(h, w_ref[:, j * SECTION:(j + 1) * SECTION], preferred_element_type=F32)
        if outs[j] is gz_ref:
            sec = _silu(sec) * ggain_ref[...]
        elif outs[j] is dz_ref:
            sec = _silu(sec) * dgain_ref[...]
        outs[j][0] = sec.astype(outs[j].dtype)

    section(2)
    section(5)

    vt_gr = lax.dot_general(wvt_ref[...], h, (((1,), (1,)), ((), ())), preferred_element_type=F32)
    dvt_ref[0] = vt_gr[0:SECTION].astype(dvt_ref.dtype)
    tm = vt_gr.shape[1]
    gr_t = jnp.concatenate([vt_gr[SECTION:SECTION + GLA_RANK].astype(BF16),
                            jnp.zeros((RANK_PAD - GLA_RANK, tm), BF16)], axis=0)
    logit = lax.dot_general(gr_t, wup_ref[...], (((0,), (0,)), ((), ())),
                            preferred_element_type=F32) + bg_ref[...]
    la_ref[0] = (_log_sigmoid(logit) * (1.0 / GLA_GATE_NORM)).astype(la_ref.dtype)

    for j in (1, 3, 4):
        section(j)


def _in_proj(x, mod3, norm_gain, w_in_r, wup_pad, b_gate, gla_gate_gain, diff_gate_gain):
    bsz, s, d = x.shape
    tm = ROWS_IN_PROJ
    ncol = w_in_r.shape[1]
    hk = GLA_HEADS * GLA_DK
    act = lambda width: pl.BlockSpec((1, tm, width), lambda b, t: (b, t, 0))
    sec = jax.ShapeDtypeStruct((bsz, s, SECTION), BF16)
    out_shapes = (
        sec,
        sec,
        sec,
        sec,
        sec,
        sec,
        jax.ShapeDtypeStruct((bsz, SECTION, s), BF16),
        jax.ShapeDtypeStruct((bsz, s, hk), BF16),
    )
    const = lambda shape: pl.BlockSpec(shape, lambda b, t: (0, 0))
    return pl.pallas_call(
        _in_proj_kernel,
        grid=(bsz, s // tm),
        in_specs=[
            act(d),
            pl.BlockSpec((1, 1, d), lambda b, t: (b, 0, 0)),
            pl.BlockSpec((1, 1, d), lambda b, t: (b, 0, 1)),
            const((1, d)), const((d, ncol)), const((RANK_PAD, hk)), const((1, hk)),
            const((1, SECTION)), const((1, SECTION)),
        ],
        out_specs=(act(SECTION), act(SECTION), act(SECTION), act(SECTION), act(SECTION), act(SECTION),
                   pl.BlockSpec((1, SECTION, tm), lambda b, t: (b, 0, t)), act(hk)),
        out_shape=out_shapes,
        scratch_shapes=[pltpu.VMEM((SECTION + GLA_RANK, d), BF16)],
        compiler_params=pltpu.CompilerParams(
            dimension_semantics=("arbitrary", "arbitrary"), vmem_limit_bytes=VMEM_LIMIT),
        name="in_proj",
    )(x, mod3, mod3, norm_gain, w_in_r, wup_pad, b_gate, gla_gate_gain, diff_gate_gain)


def _gla_out_kernel(qk_ref, v_ref, za_ref, la_ref, tril_ref, chunk_ind_ref, mean_bd_ref,
                    od_ref, x_ref, gate_ref, w_ref, fg_ref, o_ref, state_ref):
    tg = qk_ref.shape[1]
    half = od_ref.shape[2]
    hk = GLA_HEADS * GLA_DK
    c_sz = GLA_CHUNK
    n_ch = tg // c_sz

    @pl.when(pl.program_id(1) == 0)
    def _():
        state_ref[...] = jnp.zeros_like(state_ref)

    la = la_ref[0]
    grp = tril_ref.shape[0]
    b_all = jnp.concatenate(
        [jnp.dot(tril_ref[...], la[g * grp:(g + 1) * grp], preferred_element_type=F32)
         for g in range(tg // grp)], axis=0)
    b_last = jnp.concatenate(
        [jnp.broadcast_to(b_all[(ch + 1) * c_sz - 1:(ch + 1) * c_sz], (c_sz, hk)) for ch in range(n_ch)], axis=0)
    q = qk_ref[0, :, 0:hk]
    k = qk_ref[0, :, hk:2 * hk]
    q_in = q * jnp.exp(b_all).astype(BF16)
    k_in = k * jnp.exp(-b_all).astype(BF16)
    k_st = k * jnp.exp(b_last - b_all).astype(BF16)

    lane_head = lax.broadcasted_iota(jnp.int32, (1, hk), 1) // GLA_DK
    ri = lax.broadcasted_iota(jnp.int32, (GLA_HEADS * c_sz, c_sz), 0) % c_sz
    ci = lax.broadcasted_iota(jnp.int32, (GLA_HEADS * c_sz, c_sz), 1)
    causal = ci <= ri
    dec_all = jnp.concatenate(
        [jnp.exp(lax.dot_general(la[g * grp:(g + 1) * grp], chunk_ind_ref[...], (((0,), (0,)), ((), ())),
                                 preferred_element_type=F32)) for g in range(tg // grp)], axis=1)

    mix_diff = jnp.dot(od_ref[0], w_ref[half:2 * half, :], preferred_element_type=F32)

    chunk_rows = [slice(ch * c_sz, (ch + 1) * c_sz) for ch in range(n_ch)]
    head_rows = [slice(hh * c_sz, (hh + 1) * c_sz) for hh in range(GLA_HEADS)]
    head_cols = [slice(hh * GLA_DV, (hh + 1) * GLA_DV) for hh in range(GLA_HEADS)]
    vs = [v_ref[0, rows, :] for rows in chunk_rows]

    us = []
    for ch, rows in enumerate(chunk_rows):
        kst_t = k_st[rows].T
        us.append(jnp.concatenate(
            [jnp.dot(kst_t[hr], vs[ch][:, vc], preferred_element_type=F32)
             for hr, vc in zip(head_rows, head_cols)], axis=0))
    state = state_ref[...]
    states = []
    for ch in range(n_ch):
        states.append(state.astype(BF16))
        state = dec_all[:, ch * GLA_DV:(ch + 1) * GLA_DV] * state + us[ch]
    state_ref[...] = state
    boths = []
    for ch, rows in enumerate(chunk_rows):
        q_c = q_in[rows]
        qm = jnp.concatenate(
            [jnp.where(lane_head == hh, q_c, jnp.zeros_like(q_c)) for hh in range(GLA_HEADS)], axis=0)
        rhs = jnp.concatenate([states[ch], k_in[rows].T], axis=1)
        boths.append(jnp.dot(qm, rhs, preferred_element_type=F32))
    o_rows = []
    for ch in range(n_ch):
        inter = boths[ch][:, 0:GLA_DV]
        p = jnp.where(causal, boths[ch][:, GLA_DV:GLA_DV + c_sz], 0.0).astype(BF16)
        o_rows.append(jnp.concatenate(
            [jnp.dot(p[hr], vs[ch][:, vc], preferred_element_type=F32) + inter[hr]
             for hr, vc in zip(head_rows, head_cols)], axis=1))

    o = jnp.concatenate(o_rows, axis=0)
    ms = jnp.dot((o * o).astype(BF16), mean_bd_ref[...], preferred_element_type=F32)
    o_gla = (o * lax.rsqrt(ms + EPS) * za_ref[0].astype(F32)).astype(BF16)

    mixw = jnp.dot(o_gla, w_ref[0:half, :], preferred_element_type=F32) + mix_diff
    xn = x_ref[0] + gate_ref[0] * mixw
    o_ref[0] = xn * lax.rsqrt(jnp.mean(xn * xn, axis=-1, keepdims=True) + EPS) * fg_ref[...]


def _gla_out(gqk, gv, gza, la, o_diff, x, mod3, w_out_bf, final_gain):
    bsz, s, d = x.shape
    tg = ROWS_GLA
    hk = GLA_HEADS * GLA_DK
    width = GLA_HEADS * GLA_DV
    grp = GLA_CUMSUM_ROWS
    idx = np.arange(grp)
    tril = ((idx[:, None] // GLA_CHUNK == idx[None, :] // GLA_CHUNK) & (idx[None, :] <= idx[:, None]))
    chunk_ind = idx[:, None] // GLA_CHUNK == np.arange(grp // GLA_CHUNK * GLA_DV)[None, :] // GLA_DV
    col = np.arange(width)
    mean_bd = (col[:, None] // GLA_DV == col[None, :] // GLA_DV) / GLA_DV
    act = lambda w: pl.BlockSpec((1, tg, w), lambda b, t: (b, t, 0))
    const = lambda shape: pl.BlockSpec(shape, lambda b, t: (0, 0))
    return pl.pallas_call(
        _gla_out_kernel,
        grid=(bsz, s // tg),
        in_specs=[act(2 * hk), act(width), act(width), act(hk),
                  const(tril.shape), const(chunk_ind.shape), const((width, width)),
                  act(o_diff.shape[2]), act(d),
                  pl.BlockSpec((1, 1, d), lambda b, t: (b, 0, 2)),
                  const(w_out_bf.shape), const((1, d))],
        out_specs=act(d),
        out_shape=jax.ShapeDtypeStruct((bsz, s, d), F32),
        scratch_shapes=[pltpu.VMEM((hk, GLA_DV), F32)],
        compiler_params=pltpu.CompilerParams(
            dimension_semantics=("arbitrary", "arbitrary"), vmem_limit_bytes=VMEM_LIMIT),
        name="gla_out_proj",
    )(gqk, gv, gza, la, jnp.asarray(tril, BF16), jnp.asarray(chunk_ind, BF16),
      jnp.asarray(mean_bd, BF16), o_diff, x, mod3, w_out_bf, final_gain)


def _diff_lambda(lq1_ref, lk1_ref, lq2_ref, lk2_ref, lam_init):
    a = jnp.sum(lq1_ref[...] * lk1_ref[...], axis=-1, keepdims=True)
    b = jnp.sum(lq2_ref[...] * lk2_ref[...], axis=-1, keepdims=True)
    return jnp.exp(a) - jnp.exp(b) + lam_init


def _attn_spans(nq, tk):
    spans = []
    for qi in range(nq):
        m = ATTN_SPAN_TILES
        tile = [(qi, m * tk * j, m * tk, None) for j in range(qi // m)]
        rest = qi % m
        tile.append((qi, (qi - rest) * tk, (rest + 1) * tk, rest * tk))
        for i, sp in enumerate(tile):
            spans.append(sp + (i == 0, i == len(tile) - 1))
    return spans


def _attn_bias_tables(tq, tk):
    def split(x, terms):
        parts = []
        for _ in range(terms):
            hi = x.astype(jnp.bfloat16)
            parts.append(hi)
            x = x - hi.astype(np.float32)
        return parts

    key = np.arange(ATTN_SPAN_TILES * tk, dtype=np.float32)
    kaug = np.zeros((ATTN_SPAN_TILES * tk, LANES), np.float32)
    kaug[:, 0] = kaug[:, 1] = key % BF16_EXACT_INT
    kaug[:, 2] = kaug[:, 3] = key - key % BF16_EXACT_INT
    kaug[:, 4:7] = 1.0
    slopes = (2.0 ** (-8.0 * np.arange(1, DIFF_HEADS + 1) / DIFF_HEADS) * LOG2E).astype(np.float32)
    qry = np.arange(tq, dtype=np.float32)
    qaug = np.zeros((DIFF_HEADS, tq, LANES), jnp.bfloat16)
    for hh in range(DIFF_HEADS):
        s_hi, s_lo = split(slopes[hh:hh + 1], 2)
        qaug[hh, :, 0] = qaug[hh, :, 2] = s_hi
        qaug[hh, :, 1] = qaug[hh, :, 3] = s_lo
        for lane, part in zip((4, 5, 6), split(-slopes[hh] * qry, 3)):
            qaug[hh, :, lane] = part
    slope_rows = np.broadcast_to(slopes[:, None, None], (DIFF_HEADS, 1, LANES))
    return jnp.asarray(kaug, BF16), jnp.asarray(qaug), jnp.asarray(slope_rows, F32)


def _attn_kernel(q_ref, k_ref, vt_ref, z_ref, kaug_ref, qaug_ref, slope_ref,
                 lq1_ref, lk1_ref, lq2_ref, lk2_ref,
                 o_ref, km_ref, s_ref, mb_ref, acc_ref, *, lam_init):
    tq = ATTN_TQ
    tk = ATTN_TK
    hw = 2 * DIFF_DH
    nq = q_ref.shape[1] // tq
    n_heads = q_ref.shape[2] // hw
    n_buf = ATTN_AHEAD + 1

    first_half = lax.broadcasted_iota(jnp.int32, (1, hw), 1) < DIFF_DH
    map_cols = lax.broadcasted_iota(jnp.int32, (1, 2 * tq), 1) < tq
    r_i = lax.broadcasted_iota(jnp.int32, (hw, LANES), 0)
    c_i = lax.broadcasted_iota(jnp.int32, (hw, LANES), 1)
    half_sel = jnp.where(r_i // DIFF_DH == c_i, 1.0, 0.0).astype(BF16)
    half_rows = (lax.broadcasted_iota(jnp.int32, (8, hw), 1) // DIFF_DH
                 == lax.broadcasted_iota(jnp.int32, (8, hw), 0)).astype(BF16)

    def rel_pos(nk):
        key = lax.broadcasted_iota(jnp.int32, (nk, 2 * tq), 0)
        qry = lax.broadcasted_iota(jnp.int32, (nk, 2 * tq), 1) % tq
        return qry - key

    lam = _diff_lambda(lq1_ref, lk1_ref, lq2_ref, lk2_ref, lam_init)
    spans = _attn_spans(nq, tk)

    class Head:
        def __init__(self, j):
            self.j = j
            self.lanes = slice(j * hw, (j + 1) * hw)
            self.slope = slope_ref[j][:, 0:1]
            k_all = k_ref[0, :, self.lanes]
            zk = jnp.zeros_like(k_all)
            km_ref[j, 0] = jnp.where(first_half, k_all, zk)
            km_ref[j, 1] = jnp.where(first_half, zk, k_all)
            k_norm2 = jnp.max(jnp.dot(k_all * k_all, half_sel, preferred_element_type=F32), axis=0, keepdims=True)
            k_max = jnp.sqrt(k_norm2)
            k_max_row = jnp.where(map_cols, k_max[:, 0:1], k_max[:, 1:2])
            self.bounds = []
            for qi in range(nq):
                q = q_ref[0, qi * tq:(qi + 1) * tq, self.lanes]
                q_half2 = lax.dot_general(half_rows, q * q, (((1,), (1,)), ((), ())), preferred_element_type=F32)
                q_norm2 = jnp.concatenate([q_half2[0:1], q_half2[1:2]], axis=1)
                self.bounds.append(jnp.sqrt(q_norm2) * k_max_row * BOUND_SLACK)

        def span_offset(self, qi, start):
            return -self.slope * float(qi * tq - start)

        def masked_scores(self, t):
            qi, start, nk, diag_off, _, _ = spans[t]
            qrhs = jnp.concatenate([q_ref[0, qi * tq:(qi + 1) * tq, self.lanes], qaug_ref[self.j]], axis=1)
            s_maps = []
            for m in range(2):
                kblk = jnp.concatenate([km_ref[self.j, m, start:start + nk, :], kaug_ref[0:nk, :]], axis=1)
                s_maps.append(lax.dot_general(kblk, qrhs, (((1,), (1,)), ((), ())), preferred_element_type=F32))
            s = jnp.concatenate(s_maps, axis=1)
            if diag_off is not None:
                keep = rel_pos(tk) + (diag_off - (nk - tk)) >= 0
                tail = jnp.where(keep, s[nk - tk:nk], -jnp.inf)
                s = tail if nk == tk else jnp.concatenate([s[0:nk - tk], tail], axis=0)
            return s

        def values_and_ones(self, t):
            _, start, nk, _, _, _ = spans[t]
            vt = vt_ref[0, self.j * DIFF_DV:(self.j + 1) * DIFF_DV, start:start + nk]
            return jnp.concatenate([vt, jnp.ones((SUM_ROWS, nk), BF16)], axis=0)

        def finish(self, qi, acc):
            rows = slice(qi * tq, (qi + 1) * tq)
            o_all = acc[0:DIFF_DV, :] / acc[DIFF_DV:DIFF_DV + 1, :]
            o = (o_all[:, :tq] - lam * o_all[:, tq:]).T
            y = o * lax.rsqrt(jnp.mean(o * o, axis=-1, keepdims=True) + EPS)
            y = y * z_ref[0, rows, self.lanes].astype(F32)
            o_ref[0, rows, self.lanes] = y.astype(o_ref.dtype)

        def exact_path(self):
            def scores(t):
                qi, start, nk, _, _, _ = spans[t]
                s = self.masked_scores(t)
                s_ref[t % n_buf, 0:nk, :] = s
                mb_ref[t % n_buf] = jnp.max(s, axis=0, keepdims=True) + self.span_offset(qi, start)

            def update(t, m_old):
                qi, start, nk, _, first, last = spans[t]
                s = s_ref[t % n_buf, 0:nk, :]
                m_blk = mb_ref[t % n_buf]
                m_new = m_blk if first else jnp.maximum(m_old, m_blk)
                p = jnp.exp2(s - (m_new - self.span_offset(qi, start))).astype(BF16)
                pv = jnp.dot(self.values_and_ones(t), p, preferred_element_type=F32)
                if first:
                    acc_ref[...] = pv
                else:
                    acc_ref[...] = jnp.exp2(m_old - m_new) * acc_ref[...] + pv
                if last:
                    self.finish(qi, acc_ref[...])
                return m_new

            for t in range(min(ATTN_AHEAD, len(spans))):
                scores(t)
            m = None
            for t in range(len(spans)):
                if t + ATTN_AHEAD < len(spans):
                    scores(t + ATTN_AHEAD)
                m = update(t, m)

    heads = [Head(j) for j in range(n_heads)]

    acc = [None] * n_heads
    denom_min = [None] * n_heads
    s_next = [h.masked_scores(0) for h in heads]
    for t, (qi, start, _, _, first, last) in enumerate(spans):
        for j, h in enumerate(heads):
            s = s_next[j]
            if t + 1 < len(spans):
                s_next[j] = h.masked_scores(t + 1)
            p = jnp.exp2(s - (h.bounds[qi] - h.span_offset(qi, start))).astype(BF16)
            pv = jnp.dot(h.values_and_ones(t), p, preferred_element_type=F32)
            acc[j] = pv if first else acc[j] + pv
            if last:
                denom = acc[j][DIFF_DV:DIFF_DV + 1, :]
                denom_min[j] = denom if denom_min[j] is None else jnp.minimum(denom_min[j], denom)
                h.finish(qi, acc[j])

    for j, h in enumerate(heads):
        fast_ok = jnp.min(denom_min[j]) >= FAST_PATH_MIN_DENOM
        pl.when(jnp.logical_not(fast_ok))(h.exact_path)


def _diff_attn(dq, dk, dvt, dz, lq1, lk1, lq2, lk2, lam_init):
    bsz, s, width = dq.shape
    tq = ATTN_TQ
    assert ATTN_TQ == ATTN_TK
    hw = 2 * DIFF_DH
    nh = ATTN_HEADS_PER_STEP
    rowspec = pl.BlockSpec((1, s, nh * hw), lambda b, h: (b, 0, h))
    vtspec = pl.BlockSpec((1, nh * DIFF_DV, s), lambda b, h: (b, h, 0))
    lspec = pl.BlockSpec((1, DIFF_DH), lambda b, h: (0, 0))
    kaug, qaug, slope_rows = _attn_bias_tables(tq, ATTN_TK)
    per_head = lambda rows: pl.BlockSpec((nh, rows, LANES), lambda b, h: (h, 0, 0))
    return pl.pallas_call(
        functools.partial(_attn_kernel, lam_init=lam_init),
        grid=(bsz, DIFF_HEADS // nh),
        in_specs=[rowspec, rowspec, vtspec, rowspec,
                  pl.BlockSpec(kaug.shape, lambda b, h: (0, 0)), per_head(tq), per_head(1),
                  lspec, lspec, lspec, lspec],
        out_specs=rowspec,
        out_shape=jax.ShapeDtypeStruct((bsz, s, width), BF16),
        scratch_shapes=[pltpu.VMEM((nh, 2, s, hw), BF16),
                        pltpu.VMEM((ATTN_AHEAD + 1, ATTN_SPAN_TILES * ATTN_TK, 2 * tq), F32),
                        pltpu.VMEM((ATTN_AHEAD + 1, 1, 2 * tq), F32),
                        pltpu.VMEM((DIFF_DV + SUM_ROWS, 2 * tq), F32)],
        compiler_params=pltpu.CompilerParams(
            dimension_semantics=("arbitrary", "arbitrary"), vmem_limit_bytes=VMEM_LIMIT),
        name="diff_attn",
    )(dq, dk, dvt, dz, kaug, qaug, slope_rows, lq1, lk1, lq2, lk2)


def _w_in_layout():
    hk = GLA_HEADS * GLA_DK
    names = ("gq", "gk", "gv", "gz", "gr", "dq", "dk", "dv", "dz")
    sizes = (hk, hk, SECTION, SECTION, GLA_RANK, SECTION, SECTION, SECTION, SECTION)
    src = dict(zip(names, np.concatenate([[0], np.cumsum(sizes)[:-1]]).tolist()))
    width = dict(zip(names, sizes))
    scale = {"gq": GLA_DK ** -0.5, "dq": DIFF_DH ** -0.5 * LOG2E}
    pieces, dst = [], 0
    for name in ("gq", "gk", "gv", "gz", "dq", "dk", "dz", "dv", "gr"):
        pieces.append((src[name], dst, width[name], scale.get(name)))
        dst += width[name]
    return pieces, dst + RANK_PAD - GLA_RANK


def _w_prep_kernel(wt_ref, o_ref):
    pieces, _ = _w_in_layout()
    for src, dst, width, scale in pieces:
        piece = wt_ref[0, src:src + width, :]
        if scale is not None:
            piece = piece * scale
        if width < LANES:
            piece = jnp.concatenate([piece, jnp.zeros((LANES - width, piece.shape[1]), piece.dtype)], axis=0)
        o_ref[:, dst:dst + piece.shape[0]] = piece.T.astype(o_ref.dtype)


def _prep_w_in(w_in, layer):
    _, d, n_src = w_in.shape
    _, n_dst = _w_in_layout()
    rows = 256
    return pl.pallas_call(
        _w_prep_kernel,
        grid=(d // rows,),
        in_specs=[pl.BlockSpec((1, n_src, rows), lambda i: (layer, 0, i))],
        out_specs=pl.BlockSpec((rows, n_dst), lambda i: (i, 0)),
        out_shape=jax.ShapeDtypeStruct((d, n_dst), BF16),
        compiler_params=pltpu.CompilerParams(dimension_semantics=("arbitrary",), vmem_limit_bytes=VMEM_LIMIT),
        name="w_in_prep",
    )(jnp.swapaxes(w_in, 1, 2))


def kernel(x, c, w_ada, b_ada, norm_gain, w_in, w_gla_gate_up, b_gla_gate, gla_out_gain,
           lambda_q1, lambda_k1, lambda_q2, lambda_k2, diff_out_gain, w_out, final_gain):
    bsz, s, d = x.shape
    depth = w_in.shape[0]
    assert depth == 1, "gla_out_proj applies the final rmsnorm, so exactly one layer is supported"
    for l in range(depth):
        mod = _adaln_mod(c, w_ada, b_ada, l)
        mod3 = mod.reshape(bsz, 1, 3 * d)
        w_in_r = _prep_w_in(w_in, l)
        wup_pad = jnp.pad(w_gla_gate_up[l], ((0, RANK_PAD - GLA_RANK), (0, 0))).astype(BF16)
        lam_init = float(0.8 - 0.6 * np.exp(-0.3 * l))
        gqk, gv, gz, dq, dk, dz, dvt, la = _in_proj(
            x, mod3, norm_gain[l].reshape(1, d), w_in_r, wup_pad, b_gla_gate[l].reshape(1, -1),
            gla_out_gain[l].reshape(1, -1), (diff_out_gain[l] * (1.0 - lam_init)).reshape(1, -1))
        o_diff = _diff_attn(dq, dk, dvt, dz,
                            lambda_q1[l].reshape(1, -1), lambda_k1[l].reshape(1, -1),
                            lambda_q2[l].reshape(1, -1), lambda_k2[l].reshape(1, -1), lam_init)
        x = _gla_out(gqk, gv, gz, la, o_diff, x, mod3, w_out[l].astype(BF16), final_gain.reshape(1, d))
    return x
```

```python
import functools
import math

import jax
import jax.numpy as jnp
import numpy as np
from jax import lax
from jax.experimental import pallas as pl
from jax.experimental.pallas import tpu as pltpu

F32 = jnp.float32
BF16 = jnp.bfloat16

EPS = 1e-6
LOG2E = math.log2(math.e)
GLA_HEADS = 4
GLA_DK = 64
GLA_DV = 128
GLA_RANK = 16
GLA_GATE_NORM = 16.0
GLA_CHUNK = 64
DIFF_HEADS = 4
DIFF_DH = 64
DIFF_DV = 128
LANES = 128
RANK_PAD = LANES
SECTION = 512
SUM_ROWS = 16

ROWS_IN_PROJ = 1024
ROWS_GLA = 512
GLA_SEQS_PER_STEP = 2
GLA_CUMSUM_ROWS = 256
ATTN_TQ = 256
ATTN_TK = 256
ATTN_HEADS_PER_STEP = 2
ATTN_SPAN_TILES = 8
ATTN_AHEAD = 2
BF16_EXACT_INT = 256
BOUND_SLACK = 1.01
FAST_PATH_MIN_DENOM = 2.0 ** -90
VMEM_LIMIT = 48 * 1024 * 1024


def _silu(v):
    return v / (1.0 + jnp.exp(-v))


def _log_sigmoid(v):
    return jnp.minimum(v, 0.0) - jnp.log(1.0 + jnp.exp(-jnp.abs(v)))


def _adaln_kernel(c_ref, w_ref, b_ref, o_ref):
    sc = _silu(c_ref[...]).astype(BF16)
    o_ref[...] = jnp.dot(sc, w_ref[0].astype(BF16), preferred_element_type=F32) + b_ref[0]


def _adaln_mod(c, w_ada, b_ada, layer):
    bsz, d = c.shape
    n = w_ada.shape[2]
    tn = 1024
    return pl.pallas_call(
        _adaln_kernel,
        grid=(n // tn,),
        in_specs=[
            pl.BlockSpec((bsz, d), lambda j: (0, 0)),
            pl.BlockSpec((1, d, tn), lambda j: (layer, 0, j)),
            pl.BlockSpec((1, 1, tn), lambda j: (layer, 0, j)),
        ],
        out_specs=pl.BlockSpec((bsz, tn), lambda j: (0, j)),
        out_shape=jax.ShapeDtypeStruct((bsz, n), F32),
        compiler_params=pltpu.CompilerParams(dimension_semantics=("arbitrary",)),
        name="adaln_mod",
    )(c, w_ada, b_ada.reshape(b_ada.shape[0], 1, n))


def _in_proj_kernel(x_ref, shift_ref, scale_ref, gain_ref, w_ref, wup_ref, bg_ref, ggain_ref, dgain_ref,
                    gqk_ref, gv_ref, gz_ref, dq_ref, dk_ref, dz_ref, dvt_ref, la_ref,
                    wvt_ref):
    n_sec = 7
    dv_sec = 6

    @pl.when((pl.program_id(0) == 0) & (pl.program_id(1) == 0))
    def _():
        wvt_ref[0:SECTION, :] = w_ref[:, dv_sec * SECTION:(dv_sec + 1) * SECTION].T
        wvt_ref[SECTION:SECTION + GLA_RANK, :] = (
            w_ref[:, n_sec * SECTION:n_sec * SECTION + RANK_PAD].T[0:GLA_RANK, :])

    g = gain_ref[...] * (1.0 + scale_ref[0])
    tm = x_ref.shape[1]
    half = tm // 2

    def modulated(rows):
        x = x_ref[0, rows, :]
        rstd = lax.rsqrt(jnp.mean(x * x, axis=-1, keepdims=True) + EPS)
        return (x * rstd * g + shift_ref[0]).astype(BF16)

    h_halves = [modulated(slice(0, half)), modulated(slice(half, tm))]
    for r, h_half in enumerate(h_halves):
        gqk_ref[0, r * half:(r + 1) * half, :] = jnp.dot(
            h_half, w_ref[:, 0:SECTION], preferred_element_type=F32).astype(gqk_ref.dtype)
    h = jnp.concatenate(h_halves, axis=0)

    outs = (gqk_ref, gv_ref, gz_ref, dq_ref, dk_ref, dz_ref)

    def section(j):
        sec = jnp.dot(h, w_ref[:, j * SECTION:(j + 1) * SECTION], preferred_element_type=F32)
        if outs[j] is gz_ref:
            sec = _silu(sec) * ggain_ref[...]
        elif outs[j] is dz_ref:
            sec = _silu(sec) * dgain_ref[...]
        outs[j][0] = sec.astype(outs[j].dtype)

    section(2)
    section(5)

    vt_gr = lax.dot_general(wvt_ref[...], h, (((1,), (1,)), ((), ())), preferred_element_type=F32)
    dvt_ref[0] = vt_gr[0:SECTION].astype(dvt_ref.dtype)
    tm = vt_gr.shape[1]
    gr_t = jnp.concatenate([vt_gr[SECTION:SECTION + GLA_RANK].astype(BF16),
                            jnp.zeros((RANK_PAD - GLA_RANK, tm), BF16)], axis=0)
    logit = lax.dot_general(gr_t, wup_ref[...], (((0,), (0,)), ((), ())),
                            preferred_element_type=F32) + bg_ref[...]
    la_ref[0] = (_log_sigmoid(logit) * (1.0 / GLA_GATE_NORM)).astype(la_ref.dtype)

    for j in (1, 3, 4):
        section(j)


def _in_proj(x, mod3, norm_gain, w_in_r, wup_pad, b_gate, gla_gate_gain, diff_gate_gain):
    bsz, s, d = x.shape
    tm = ROWS_IN_PROJ
    ncol = w_in_r.shape[1]
    hk = GLA_HEADS * GLA_DK
    act = lambda width: pl.BlockSpec((1, tm, width), lambda b, t: (b, t, 0))
    sec = jax.ShapeDtypeStruct((bsz, s, SECTION), BF16)
    out_shapes = (
        sec,
        sec,
        sec,
        sec,
        sec,
        sec,
        jax.ShapeDtypeStruct((bsz, SECTION, s), BF16),
        jax.ShapeDtypeStruct((bsz, s, hk), BF16),
    )
    const = lambda shape: pl.BlockSpec(shape, lambda b, t: (0, 0))
    return pl.pallas_call(
        _in_proj_kernel,
        grid=(bsz, s // tm),
        in_specs=[
            act(d),
            pl.BlockSpec((1, 1, d), lambda b, t: (b, 0, 0)),
            pl.BlockSpec((1, 1, d), lambda b, t: (b, 0, 1)),
            const((1, d)), const((d, ncol)), const((RANK_PAD, hk)), const((1, hk)),
            const((1, SECTION)), const((1, SECTION)),
        ],
        out_specs=(act(SECTION), act(SECTION), act(SECTION), act(SECTION), act(SECTION), act(SECTION),
                   pl.BlockSpec((1, SECTION, tm), lambda b, t: (b, 0, t)), act(hk)),
        out_shape=out_shapes,
        scratch_shapes=[pltpu.VMEM((SECTION + GLA_RANK, d), BF16)],
        compiler_params=pltpu.CompilerParams(
            dimension_semantics=("arbitrary", "arbitrary"), vmem_limit_bytes=VMEM_LIMIT),
        name="in_proj",
    )(x, mod3, mod3, norm_gain, w_in_r, wup_pad, b_gate, gla_gate_gain, diff_gate_gain)


def _gla_out_kernel(qk_ref, v_ref, za_ref, la_ref, tril_ref, chunk_ind_ref, mean_bd_ref,
                    od_ref, x_ref, gate_ref, w_ref, fg_ref, o_ref, state_ref):
    tg = qk_ref.shape[1]
    half = od_ref.shape[2]
    hk = GLA_HEADS * GLA_DK
    c_sz = GLA_CHUNK
    n_ch = tg // c_sz

    @pl.when(pl.program_id(1) == 0)
    def _():
        state_ref[...] = jnp.zeros_like(state_ref)

    grp = tril_ref.shape[0]
    lane_head = lax.broadcasted_iota(jnp.int32, (1, hk), 1) // GLA_DK
    ri = lax.broadcasted_iota(jnp.int32, (GLA_HEADS * c_sz, c_sz), 0) % c_sz
    ci = lax.broadcasted_iota(jnp.int32, (GLA_HEADS * c_sz, c_sz), 1)
    causal = ci <= ri
    chunk_rows = [slice(ch * c_sz, (ch + 1) * c_sz) for ch in range(n_ch)]
    head_rows = [slice(hh * c_sz, (hh + 1) * c_sz) for hh in range(GLA_HEADS)]
    head_cols = [slice(hh * GLA_DV, (hh + 1) * GLA_DV) for hh in range(GLA_HEADS)]

    def sequence(bi):
        la = la_ref[bi]
        b_all = jnp.concatenate(
            [jnp.dot(tril_ref[...], la[g * grp:(g + 1) * grp], preferred_element_type=F32)
             for g in range(tg // grp)], axis=0)
        b_last = jnp.concatenate(
            [jnp.broadcast_to(b_all[(ch + 1) * c_sz - 1:(ch + 1) * c_sz], (c_sz, hk)) for ch in range(n_ch)],
            axis=0)
        q = qk_ref[bi, :, 0:hk]
        k = qk_ref[bi, :, hk:2 * hk]
        q_in = q * jnp.exp(b_all).astype(BF16)
        k_in = k * jnp.exp(-b_all).astype(BF16)
        k_st = k * jnp.exp(b_last - b_all).astype(BF16)
        dec_all = jnp.concatenate(
            [jnp.exp(lax.dot_general(la[g * grp:(g + 1) * grp], chunk_ind_ref[...], (((0,), (0,)), ((), ())),
                                     preferred_element_type=F32)) for g in range(tg // grp)], axis=1)
        mix_diff = jnp.dot(od_ref[bi], w_ref[half:2 * half, :], preferred_element_type=F32)
        vs = [v_ref[bi, rows, :] for rows in chunk_rows]
        yield
        us = []
        for ch, rows in enumerate(chunk_rows):
            kst_t = k_st[rows].T
            us.append(jnp.concatenate(
                [jnp.dot(kst_t[hr], vs[ch][:, vc], preferred_element_type=F32)
                 for hr, vc in zip(head_rows, head_cols)], axis=0))
        state = state_ref[bi]
        states = []
        for ch in range(n_ch):
            states.append(state.astype(BF16))
            state = dec_all[:, ch * GLA_DV:(ch + 1) * GLA_DV] * state + us[ch]
        state_ref[bi] = state
        yield
        boths = []
        for ch, rows in enumerate(chunk_rows):
            q_c = q_in[rows]
            qm = jnp.concatenate(
                [jnp.where(lane_head == hh, q_c, jnp.zeros_like(q_c)) for hh in range(GLA_HEADS)], axis=0)
            rhs = jnp.concatenate([states[ch], k_in[rows].T], axis=1)
            boths.append(jnp.dot(qm, rhs, preferred_element_type=F32))
        yield
        o_rows = []
        for ch in range(n_ch):
            inter = boths[ch][:, 0:GLA_DV]
            p = jnp.where(causal, boths[ch][:, GLA_DV:GLA_DV + c_sz], 0.0).astype(BF16)
            o_rows.append(jnp.concatenate(
                [jnp.dot(p[hr], vs[ch][:, vc], preferred_element_type=F32) + inter[hr]
                 for hr, vc in zip(head_rows, head_cols)], axis=1))
        o = jnp.concatenate(o_rows, axis=0)
        ms = jnp.dot((o * o).astype(BF16), mean_bd_ref[...], preferred_element_type=F32)
        o_gla = (o * lax.rsqrt(ms + EPS) * za_ref[bi].astype(F32)).astype(BF16)
        mixw = jnp.dot(o_gla, w_ref[0:half, :], preferred_element_type=F32) + mix_diff
        xn = x_ref[bi] + gate_ref[bi] * mixw
        o_ref[bi] = xn * lax.rsqrt(jnp.mean(xn * xn, axis=-1, keepdims=True) + EPS) * fg_ref[...]
        yield

    programs = [sequence(bi) for bi in range(qk_ref.shape[0])]
    for _ in range(4):
        for prog in programs:
            next(prog)


def _gla_out(gqk, gv, gza, la, o_diff, x, mod3, w_out_bf, final_gain):
    bsz, s, d = x.shape
    tg = ROWS_GLA
    hk = GLA_HEADS * GLA_DK
    width = GLA_HEADS * GLA_DV
    grp = GLA_CUMSUM_ROWS
    idx = np.arange(grp)
    tril = ((idx[:, None] // GLA_CHUNK == idx[None, :] // GLA_CHUNK) & (idx[None, :] <= idx[:, None]))
    chunk_ind = idx[:, None] // GLA_CHUNK == np.arange(grp // GLA_CHUNK * GLA_DV)[None, :] // GLA_DV
    col = np.arange(width)
    mean_bd = (col[:, None] // GLA_DV == col[None, :] // GLA_DV) / GLA_DV
    nb = GLA_SEQS_PER_STEP
    act = lambda w: pl.BlockSpec((nb, tg, w), lambda b, t: (b, t, 0))
    const = lambda shape: pl.BlockSpec(shape, lambda b, t: (0, 0))
    return pl.pallas_call(
        _gla_out_kernel,
        grid=(bsz // nb, s // tg),
        in_specs=[act(2 * hk), act(width), act(width), act(hk),
                  const(tril.shape), const(chunk_ind.shape), const((width, width)),
                  act(o_diff.shape[2]), act(d),
                  pl.BlockSpec((nb, 1, d), lambda b, t: (b, 0, 2)),
                  const(w_out_bf.shape), const((1, d))],
        out_specs=act(d),
        out_shape=jax.ShapeDtypeStruct((bsz, s, d), F32),
        scratch_shapes=[pltpu.VMEM((nb, hk, GLA_DV), F32)],
        compiler_params=pltpu.CompilerParams(
            dimension_semantics=("arbitrary", "arbitrary"), vmem_limit_bytes=VMEM_LIMIT),
        name="gla_out_proj",
    )(gqk, gv, gza, la, jnp.asarray(tril, BF16), jnp.asarray(chunk_ind, BF16),
      jnp.asarray(mean_bd, BF16), o_diff, x, mod3, w_out_bf, final_gain)


def _diff_lambda(lq1_ref, lk1_ref, lq2_ref, lk2_ref, lam_init):
    a = jnp.sum(lq1_ref[...] * lk1_ref[...], axis=-1, keepdims=True)
    b = jnp.sum(lq2_ref[...] * lk2_ref[...], axis=-1, keepdims=True)
    return jnp.exp(a) - jnp.exp(b) + lam_init


def _attn_spans(nq, tk):
    spans = []
    for qi in range(nq):
        m = ATTN_SPAN_TILES
        tile = [(qi, m * tk * j, m * tk, None) for j in range(qi // m)]
        rest = qi % m
        tile.append((qi, (qi - rest) * tk, (rest + 1) * tk, rest * tk))
        for i, sp in enumerate(tile):
            spans.append(sp + (i == 0, i == len(tile) - 1))
    return spans


def _attn_bias_tables(tq, tk):
    def split(x, terms):
        parts = []
        for _ in range(terms):
            hi = x.astype(jnp.bfloat16)
            parts.append(hi)
            x = x - hi.astype(np.float32)
        return parts

    key = np.arange(ATTN_SPAN_TILES * tk, dtype=np.float32)
    kaug = np.zeros((ATTN_SPAN_TILES * tk, LANES), np.float32)
    kaug[:, 0] = kaug[:, 1] = key % BF16_EXACT_INT
    kaug[:, 2] = kaug[:, 3] = key - key % BF16_EXACT_INT
    kaug[:, 4:7] = 1.0
    slopes = (2.0 ** (-8.0 * np.arange(1, DIFF_HEADS + 1) / DIFF_HEADS) * LOG2E).astype(np.float32)
    qry = np.arange(tq, dtype=np.float32)
    qaug = np.zeros((DIFF_HEADS, tq, LANES), jnp.bfloat16)
    for hh in range(DIFF_HEADS):
        s_hi, s_lo = split(slopes[hh:hh + 1], 2)
        qaug[hh, :, 0] = qaug[hh, :, 2] = s_hi
        qaug[hh, :, 1] = qaug[hh, :, 3] = s_lo
        for lane, part in zip((4, 5, 6), split(-slopes[hh] * qry, 3)):
            qaug[hh, :, lane] = part
    slope_rows = np.broadcast_to(slopes[:, None, None], (DIFF_HEADS, 1, LANES))
    return jnp.asarray(kaug, BF16), jnp.asarray(qaug), jnp.asarray(slope_rows, F32)


def _attn_kernel(q_ref, k_ref, vt_ref, z_ref, kaug_ref, qaug_ref, slope_ref,
                 lq1_ref, lk1_ref, lq2_ref, lk2_ref,
                 o_ref, km_ref, s_ref, mb_ref, acc_ref, *, lam_init):
    tq = ATTN_TQ
    tk = ATTN_TK
    hw = 2 * DIFF_DH
    nq = q_ref.shape[1] // tq
    n_heads = q_ref.shape[2] // hw
    n_buf = ATTN_AHEAD + 1

    first_half = lax.broadcasted_iota(jnp.int32, (1, hw), 1) < DIFF_DH
    map_cols = lax.broadcasted_iota(jnp.int32, (1, 2 * tq), 1) < tq
    r_i = lax.broadcasted_iota(jnp.int32, (hw, LANES), 0)
    c_i = lax.broadcasted_iota(jnp.int32, (hw, LANES), 1)
    half_sel = jnp.where(r_i // DIFF_DH == c_i, 1.0, 0.0).astype(BF16)
    half_rows = (lax.broadcasted_iota(jnp.int32, (8, hw), 1) // DIFF_DH
                 == lax.broadcasted_iota(jnp.int32, (8, hw), 0)).astype(BF16)

    def rel_pos(nk):
        key = lax.broadcasted_iota(jnp.int32, (nk, 2 * tq), 0)
        qry = lax.broadcasted_iota(jnp.int32, (nk, 2 * tq), 1) % tq
        return qry - key

    lam = _diff_lambda(lq1_ref, lk1_ref, lq2_ref, lk2_ref, lam_init)
    spans = _attn_spans(nq, tk)

    class Head:
        def __init__(self, j):
            self.j = j
            self.lanes = slice(j * hw, (j + 1) * hw)
            self.slope = slope_ref[j][:, 0:1]
            k_all = k_ref[0, :, self.lanes]
            zk = jnp.zeros_like(k_all)
            km_ref[j, 0] = jnp.where(first_half, k_all, zk)
            km_ref[j, 1] = jnp.where(first_half, zk, k_all)
            k_norm2 = jnp.max(jnp.dot(k_all * k_all, half_sel, preferred_element_type=F32), axis=0, keepdims=True)
            k_max = jnp.sqrt(k_norm2)
            k_max_row = jnp.where(map_cols, k_max[:, 0:1], k_max[:, 1:2])
            self.bounds = []
            for qi in range(nq):
                q = q_ref[0, qi * tq:(qi + 1) * tq, self.lanes]
                q_half2 = lax.dot_general(half_rows, q * q, (((1,), (1,)), ((), ())), preferred_element_type=F32)
                q_norm2 = jnp.concatenate([q_half2[0:1], q_half2[1:2]], axis=1)
                self.bounds.append(jnp.sqrt(q_norm2) * k_max_row * BOUND_SLACK)

        def span_offset(self, qi, start):
            return -self.slope * float(qi * tq - start)

        def masked_scores(self, t):
            qi, start, nk, diag_off, _, _ = spans[t]
            qrhs = jnp.concatenate([q_ref[0, qi * tq:(qi + 1) * tq, self.lanes], qaug_ref[self.j]], axis=1)
            s_maps = []
            for m in range(2):
                kblk = jnp.concatenate([km_ref[self.j, m, start:start + nk, :], kaug_ref[0:nk, :]], axis=1)
                s_maps.append(lax.dot_general(kblk, qrhs, (((1,), (1,)), ((), ())), preferred_element_type=F32))
            s = jnp.concatenate(s_maps, axis=1)
            if diag_off is not None:
                keep = rel_pos(tk) + (diag_off - (nk - tk)) >= 0
                tail = jnp.where(keep, s[nk - tk:nk], -jnp.inf)
                s = tail if nk == tk else jnp.concatenate([s[0:nk - tk], tail], axis=0)
            return s

        def values_and_ones(self, t):
            _, start, nk, _, _, _ = spans[t]
            vt = vt_ref[0, self.j * DIFF_DV:(self.j + 1) * DIFF_DV, start:start + nk]
            return jnp.concatenate([vt, jnp.ones((SUM_ROWS, nk), BF16)], axis=0)

        def finish(self, qi, acc):
            rows = slice(qi * tq, (qi + 1) * tq)
            o_all = acc[0:DIFF_DV, :] / acc[DIFF_DV:DIFF_DV + 1, :]
            o = (o_all[:, :tq] - lam * o_all[:, tq:]).T
            y = o * lax.rsqrt(jnp.mean(o * o, axis=-1, keepdims=True) + EPS)
            y = y * z_ref[0, rows, self.lanes].astype(F32)
            o_ref[0, rows, self.lanes] = y.astype(o_ref.dtype)

        def exact_path(self):
            def scores(t):
                qi, start, nk, _, _, _ = spans[t]
                s = self.masked_scores(t)
                s_ref[t % n_buf, 0:nk, :] = s
                mb_ref[t % n_buf] = jnp.max(s, axis=0, keepdims=True) + self.span_offset(qi, start)

            def update(t, m_old):
                qi, start, nk, _, first, last = spans[t]
                s = s_ref[t % n_buf, 0:nk, :]
                m_blk = mb_ref[t % n_buf]
                m_new = m_blk if first else jnp.maximum(m_old, m_blk)
                p = jnp.exp2(s - (m_new - self.span_offset(qi, start))).astype(BF16)
                pv = jnp.dot(self.values_and_ones(t), p, preferred_element_type=F32)
                if first:
                    acc_ref[...] = pv
                else:
                    acc_ref[...] = jnp.exp2(m_old - m_new) * acc_ref[...] + pv
                if last:
                    self.finish(qi, acc_ref[...])
                return m_new

            for t in range(min(ATTN_AHEAD, len(spans))):
                scores(t)
            m = None
            for t in range(len(spans)):
                if t + ATTN_AHEAD < len(spans):
                    scores(t + ATTN_AHEAD)
                m = update(t, m)

    heads = [Head(j) for j in range(n_heads)]

    acc = [None] * n_heads
    denom_min = [None] * n_heads
    s_next = [h.masked_scores(0) for h in heads]
    for t, (qi, start, _, _, first, last) in enumerate(spans):
        for j, h in enumerate(heads):
            s = s_next[j]
            if t + 1 < len(spans):
                s_next[j] = h.masked_scores(t + 1)
            p = jnp.exp2(s - (h.bounds[qi] - h.span_offset(qi, start))).astype(BF16)
            pv = jnp.dot(h.values_and_ones(t), p, preferred_element_type=F32)
            acc[j] = pv if first else acc[j] + pv
            if last:
                denom = acc[j][DIFF_DV:DIFF_DV + 1, :]
                denom_min[j] = denom if denom_min[j] is None else jnp.minimum(denom_min[j], denom)
                h.finish(qi, acc[j])

    for j, h in enumerate(heads):
        fast_ok = jnp.min(denom_min[j]) >= FAST_PATH_MIN_DENOM
        pl.when(jnp.logical_not(fast_ok))(h.exact_path)


def _diff_attn(dq, dk, dvt, dz, lq1, lk1, lq2, lk2, lam_init):
    bsz, s, width = dq.shape
    tq = ATTN_TQ
    assert ATTN_TQ == ATTN_TK
    hw = 2 * DIFF_DH
    nh = ATTN_HEADS_PER_STEP
    rowspec = pl.BlockSpec((1, s, nh * hw), lambda b, h: (b, 0, h))
    vtspec = pl.BlockSpec((1, nh * DIFF_DV, s), lambda b, h: (b, h, 0))
    lspec = pl.BlockSpec((1, DIFF_DH), lambda b, h: (0, 0))
    kaug, qaug, slope_rows = _attn_bias_tables(tq, ATTN_TK)
    per_head = lambda rows: pl.BlockSpec((nh, rows, LANES), lambda b, h: (h, 0, 0))
    return pl.pallas_call(
        functools.partial(_attn_kernel, lam_init=lam_init),
        grid=(bsz, DIFF_HEADS // nh),
        in_specs=[rowspec, rowspec, vtspec, rowspec,
                  pl.BlockSpec(kaug.shape, lambda b, h: (0, 0)), per_head(tq), per_head(1),
                  lspec, lspec, lspec, lspec],
        out_specs=rowspec,
        out_shape=jax.ShapeDtypeStruct((bsz, s, width), BF16),
        scratch_shapes=[pltpu.VMEM((nh, 2, s, hw), BF16),
                        pltpu.VMEM((ATTN_AHEAD + 1, ATTN_SPAN_TILES * ATTN_TK, 2 * tq), F32),
                        pltpu.VMEM((ATTN_AHEAD + 1, 1, 2 * tq), F32),
                        pltpu.VMEM((DIFF_DV + SUM_ROWS, 2 * tq), F32)],
        compiler_params=pltpu.CompilerParams(
            dimension_semantics=("arbitrary", "arbitrary"), vmem_limit_bytes=VMEM_LIMIT),
        name="diff_attn",
    )(dq, dk, dvt, dz, kaug, qaug, slope_rows, lq1, lk1, lq2, lk2)


def _w_in_layout():
    hk = GLA_HEADS * GLA_DK
    names = ("gq", "gk", "gv", "gz", "gr", "dq", "dk", "dv", "dz")
    sizes = (hk, hk, SECTION, SECTION, GLA_RANK, SECTION, SECTION, SECTION, SECTION)
    src = dict(zip(names, np.concatenate([[0], np.cumsum(sizes)[:-1]]).tolist()))
    width = dict(zip(names, sizes))
    scale = {"gq": GLA_DK ** -0.5, "dq": DIFF_DH ** -0.5 * LOG2E}
    pieces, dst = [], 0
    for name in ("gq", "gk", "gv", "gz", "dq", "dk", "dz", "dv", "gr"):
        pieces.append((src[name], dst, width[name], scale.get(name)))
        dst += width[name]
    return pieces, dst + RANK_PAD - GLA_RANK


def _w_prep_kernel(wt_ref, o_ref):
    pieces, _ = _w_in_layout()
    for src, dst, width, scale in pieces:
        piece = wt_ref[0, src:src + width, :]
        if scale is not None:
            piece = piece * scale
        if width < LANES:
            piece = jnp.concatenate([piece, jnp.zeros((LANES - width, piece.shape[1]), piece.dtype)], axis=0)
        o_ref[:, dst:dst + piece.shape[0]] = piece.T.astype(o_ref.dtype)


def _prep_w_in(w_in, layer):
    _, d, n_src = w_in.shape
    _, n_dst = _w_in_layout()
    rows = 256
    return pl.pallas_call(
        _w_prep_kernel,
        grid=(d // rows,),
        in_specs=[pl.BlockSpec((1, n_src, rows), lambda i: (layer, 0, i))],
        out_specs=pl.BlockSpec((rows, n_dst), lambda i: (i, 0)),
        out_shape=jax.ShapeDtypeStruct((d, n_dst), BF16),
        compiler_params=pltpu.CompilerParams(dimension_semantics=("arbitrary",), vmem_limit_bytes=VMEM_LIMIT),
        name="w_in_prep",
    )(jnp.swapaxes(w_in, 1, 2))


def kernel(x, c, w_ada, b_ada, norm_gain, w_in, w_gla_gate_up, b_gla_gate, gla_out_gain,
           lambda_q1, lambda_k1, lambda_q2, lambda_k2, diff_out_gain, w_out, final_gain):
    bsz, s, d = x.shape
    depth = w_in.shape[0]
    assert depth == 1, "gla_out_proj applies the final rmsnorm, so exactly one layer is supported"
    for l in range(depth):
        mod = _adaln_mod(c, w_ada, b_ada, l)
        mod3 = mod.reshape(bsz, 1, 3 * d)
        w_in_r = _prep_w_in(w_in, l)
        wup_pad = jnp.pad(w_gla_gate_up[l], ((0, RANK_PAD - GLA_RANK), (0, 0))).astype(BF16)
        lam_init = float(0.8 - 0.6 * np.exp(-0.3 * l))
        gqk, gv, gz, dq, dk, dz, dvt, la = _in_proj(
            x, mod3, norm_gain[l].reshape(1, d), w_in_r, wup_pad, b_gla_gate[l].reshape(1, -1),
            gla_out_gain[l].reshape(1, -1), (diff_out_gain[l] * (1.0 - lam_init)).reshape(1, -1))
        o_diff = _diff_attn(dq, dk, dvt, dz,
                            lambda_q1[l].reshape(1, -1), lambda_k1[l].reshape(1, -1),
                            lambda_q2[l].reshape(1, -1), lambda_k2[l].reshape(1, -1), lam_init)
        x = _gla_out(gqk, gv, gz, la, o_diff, x, mod3, w_out[l].astype(BF16), final_gain.reshape(1, d))
    return x
```
